```python
import jax, jax.numpy as jnp
from jax import lax
import numpy as np

D_MODEL = 2048
BATCH = 4
SEQ = 4096
DEPTH = 2

D_RNN = D_MODEL // 2
RNN_HEADS = 8
RNN_HEAD_DIM = D_RNN // RNN_HEADS
CONV_WIDTH = 4
LRU_C = 8.0
D_SGU = D_MODEL // 2
SGU_GROUPS = 8
SGU_GROUP_DIM = D_SGU // SGU_GROUPS
CHUNK = 128
N_EXPERTS = 16
N_GROUPS = 4
EXPERTS_PER_GROUP = N_EXPERTS // N_GROUPS
TOP_K = 2
D_EXPERT = 1408
EXPERT_BLOCK = 256
D_IN = 2 * D_RNN + 2 * D_SGU + 2 * D_MODEL
SPLITS = (D_RNN, 2 * D_RNN, 2 * D_RNN + D_SGU, 2 * D_RNN + 2 * D_SGU, 2 * D_RNN + 2 * D_SGU + D_MODEL)
LN_EPS = 1e-5
DEEPNORM_ALPHA = (2 * DEPTH) ** 0.25
DEEPNORM_BETA = (8 * DEPTH) ** -0.25

kernel_name = "hybrid_rglru_sgu_grouped_moe_deepnorm"


def layer_norm(x, g, b):
    xf = x.astype(jnp.float32)
    mu = xf.mean(-1, keepdims=True)
    var = jnp.square(xf - mu).mean(-1, keepdims=True)
    y = (xf - mu) * lax.rsqrt(var + LN_EPS) * g.astype(jnp.float32) + b.astype(jnp.float32)
    return y.astype(x.dtype)


def causal_depthwise_conv(x, w, b):
    y = lax.conv_general_dilated(
        x, w[:, None, :].astype(x.dtype), window_strides=(1,),
        padding=[(CONV_WIDTH - 1, 0)], dimension_numbers=("NWC", "WIO", "NWC"),
        feature_group_count=x.shape[-1])
    return y + b


def rg_lru(x, w_a, b_a, w_x, b_x, lam):
    B, S, _ = x.shape
    xh = x.reshape(B, S, RNN_HEADS, RNN_HEAD_DIM)
    r = jax.nn.sigmoid(jnp.einsum("bshi,hij->bshj", xh, w_a) + b_a).reshape(B, S, D_RNN)
    i = jax.nn.sigmoid(jnp.einsum("bshi,hij->bshj", xh, w_x) + b_x).reshape(B, S, D_RNN)
    log_a = -LRU_C * r.astype(jnp.float32) * jax.nn.softplus(-lam.astype(jnp.float32))
    a = jnp.exp(log_a)
    mult = jnp.sqrt(-jnp.expm1(2.0 * log_a))
    u = mult * (i * x).astype(jnp.float32)

    def combine(c1, c2):
        a1, b1 = c1
        a2, b2 = c2
        return a1 * a2, a2 * b1 + b2

    _, h = lax.associative_scan(combine, (a, u), axis=1)
    return h.astype(x.dtype)


def spatial_gating(u, v, ln_g, ln_b, w_s, b_s):
    B, S, _ = v.shape
    v = layer_norm(v, ln_g, ln_b)
    vc = v.reshape(B, S // CHUNK, CHUNK, SGU_GROUPS, SGU_GROUP_DIM)
    causal = jnp.tril(jnp.ones((CHUNK, CHUNK), dtype=bool))
    ws = jnp.where(causal, w_s, 0)
    mixed = jnp.einsum("gts,bcsgd->bctgd", ws, vc) + b_s.T[:, :, None]
    return u * mixed.reshape(B, S, D_SGU)


def hybrid_mixer(x, w_in, conv_w, conv_b, lru_w_a, lru_b_a, lru_w_x, lru_b_x, lru_lambda,
                 w_o_rnn, sgu_ln_g, sgu_ln_b, sgu_w_s, sgu_b_s, w_o_sgu, w_out):
    proj = x @ w_in
    x_rnn, gate_rnn, z_u, z_v, g_a, g_b = jnp.split(proj, SPLITS, axis=-1)
    h = rg_lru(causal_depthwise_conv(x_rnn, conv_w, conv_b), lru_w_a, lru_b_a, lru_w_x, lru_b_x, lru_lambda)
    y_a = (h * jax.nn.gelu(gate_rnn)) @ w_o_rnn
    y_b = spatial_gating(jax.nn.gelu(z_u), jax.nn.gelu(z_v), sgu_ln_g, sgu_ln_b, sgu_w_s, sgu_b_s) @ w_o_sgu
    merged = jax.nn.sigmoid(g_a) * y_a + jax.nn.sigmoid(g_b) * y_b
    return merged @ w_out


def route(x_flat, router_w, router_b):
    T = x_flat.shape[0]
    scores = jax.nn.softmax((x_flat @ router_w).astype(jnp.float32), axis=-1)
    sel = scores + router_b.astype(jnp.float32)
    grp_score = lax.top_k(sel.reshape(T, N_GROUPS, EXPERTS_PER_GROUP), TOP_K)[0].sum(-1)
    best = jnp.argmax(grp_score, axis=-1)
    in_group = (jnp.arange(N_EXPERTS) // EXPERTS_PER_GROUP)[None, :] == best[:, None]
    _, idx = lax.top_k(jnp.where(in_group, sel, -jnp.inf), TOP_K)
    w = jnp.take_along_axis(scores, idx, axis=-1)
    w = w / w.sum(-1, keepdims=True)
    return idx.astype(jnp.int32), w


def moe(x, router_w, router_b, w1, w3, w2):
    B, S, D = x.shape
    T = B * S
    A = T * TOP_K
    xf = x.reshape(T, D)
    idx, gate = route(xf, router_w, router_b)
    flat_e = idx.reshape(A)
    flat_tok = jnp.repeat(jnp.arange(T, dtype=jnp.int32), TOP_K)
    order = jnp.argsort(flat_e)
    e_sorted = flat_e[order]
    tok_sorted = flat_tok[order]
    w_sorted = gate.reshape(A)[order]
    counts = jnp.zeros((N_EXPERTS,), jnp.int32).at[flat_e].add(1)
    start = jnp.cumsum(counts) - counts
    padded = (counts + EXPERT_BLOCK - 1) // EXPERT_BLOCK * EXPERT_BLOCK
    pad_end = jnp.cumsum(padded)
    pad_start = pad_end - padded
    dest = pad_start[e_sorted] + jnp.arange(A, dtype=jnp.int32) - start[e_sorted]
    n_blocks = -(-A // EXPERT_BLOCK) + N_EXPERTS
    P = n_blocks * EXPERT_BLOCK
    buf_tok = jnp.full((P,), T, jnp.int32).at[dest].set(tok_sorted)
    x_ext = jnp.concatenate([xf, jnp.zeros((1, D), xf.dtype)], axis=0)
    xb = x_ext[buf_tok].reshape(n_blocks, EXPERT_BLOCK, D)
    block_e = jnp.minimum(
        jnp.searchsorted(pad_end, jnp.arange(n_blocks, dtype=jnp.int32) * EXPERT_BLOCK, side="right"),
        N_EXPERTS - 1)

    def expert_block(args):
        xblk, e = args
        hdn = jax.nn.silu(xblk @ w1[e]) * (xblk @ w3[e])
        return hdn @ w2[e]

    yb = lax.map(expert_block, (xb, block_e)).reshape(P, D)
    y = jnp.zeros((T, D), x.dtype).at[tok_sorted].add(yb[dest] * w_sorted[:, None].astype(x.dtype))
    return y.reshape(B, S, D)


def setup_inputs(seed: int = 0) -> dict:
    key = jax.random.key(seed)
    ks = jax.random.split(key, 32)
    L, D = DEPTH, D_MODEL
    f32 = jnp.float32

    def nrm(k, shape, scale):
        return jax.random.normal(k, shape, f32) * scale

    u = jax.random.uniform(ks[9], (L, D_RNN), f32, minval=0.9, maxval=0.999)
    s = u ** (1.0 / LRU_C)
    lru_lambda = jnp.log(s) - jnp.log1p(-s)
    return {
        "x": nrm(ks[0], (BATCH, SEQ, D), 1.0),
        "emb_ln_g": 1.0 + nrm(ks[1], (D,), 0.02),
        "emb_ln_b": nrm(ks[2], (D,), 0.02),
        "w_in": nrm(ks[3], (L, D, D_IN), D ** -0.5),
        "conv_w": nrm(ks[4], (L, CONV_WIDTH, D_RNN), CONV_WIDTH ** -0.5),
        "conv_b": nrm(ks[5], (L, D_RNN), 0.02),
        "lru_w_a": nrm(ks[6], (L, RNN_HEADS, RNN_HEAD_DIM, RNN_HEAD_DIM), RNN_HEAD_DIM ** -0.5),
        "lru_b_a": nrm(ks[7], (L, RNN_HEADS, RNN_HEAD_DIM), 0.02),
        "lru_w_x": nrm(ks[8], (L, RNN_HEADS, RNN_HEAD_DIM, RNN_HEAD_DIM), RNN_HEAD_DIM ** -0.5),
        "lru_b_x": nrm(ks[10], (L, RNN_HEADS, RNN_HEAD_DIM), 0.02),
        "lru_lambda": lru_lambda,
        "w_o_rnn": nrm(ks[11], (L, D_RNN, D), D_RNN ** -0.5),
        "sgu_ln_g": 1.0 + nrm(ks[12], (L, D_SGU), 0.02),
        "sgu_ln_b": nrm(ks[13], (L, D_SGU), 0.02),
        "sgu_w_s": nrm(ks[14], (L, SGU_GROUPS, CHUNK, CHUNK), CHUNK ** -0.5),
        "sgu_b_s": 1.0 + nrm(ks[15], (L, SGU_GROUPS, CHUNK), 0.1),
        "w_o_sgu": nrm(ks[16], (L, D_SGU, D), D_SGU ** -0.5),
        "w_out": nrm(ks[17], (L, D, D), D ** -0.5 * DEEPNORM_BETA),
        "ln1_g": 1.0 + nrm(ks[18], (L, D), 0.02),
        "ln1_b": nrm(ks[19], (L, D), 0.02),
        "router_w": nrm(ks[20], (D, N_EXPERTS), D ** -0.5),
        "router_b": nrm(ks[21], (N_EXPERTS,), 0.01),
        "expert_w1": nrm(ks[22], (L, N_EXPERTS, D, D_EXPERT), D ** -0.5),
        "expert_w3": nrm(ks[23], (L, N_EXPERTS, D, D_EXPERT), D ** -0.5),
        "expert_w2": nrm(ks[24], (L, N_EXPERTS, D_EXPERT, D), D_EXPERT ** -0.5 * DEEPNORM_BETA),
        "ln2_g": 1.0 + nrm(ks[25], (L, D), 0.02),
        "ln2_b": nrm(ks[26], (L, D), 0.02),
    }


def reference(x, emb_ln_g, emb_ln_b, w_in, conv_w, conv_b, lru_w_a, lru_b_a, lru_w_x, lru_b_x,
              lru_lambda, w_o_rnn, sgu_ln_g, sgu_ln_b, sgu_w_s, sgu_b_s, w_o_sgu, w_out,
              ln1_g, ln1_b, router_w, router_b, expert_w1, expert_w3, expert_w2, ln2_g, ln2_b):
    x = layer_norm(x, emb_ln_g, emb_ln_b)
    for l in range(DEPTH):
        m = hybrid_mixer(x, w_in[l], conv_w[l], conv_b[l], lru_w_a[l], lru_b_a[l], lru_w_x[l],
                         lru_b_x[l], lru_lambda[l], w_o_rnn[l], sgu_ln_g[l], sgu_ln_b[l],
                         sgu_w_s[l], sgu_b_s[l], w_o_sgu[l], w_out[l])
        x = layer_norm(DEEPNORM_ALPHA * x + m, ln1_g[l], ln1_b[l])
        f = moe(x, router_w, router_b, expert_w1[l], expert_w3[l], expert_w2[l])
        x = layer_norm(DEEPNORM_ALPHA * x + f, ln2_g[l], ln2_b[l])
    return x
```

```python
import functools

import jax
import jax.numpy as jnp
import numpy as np
from jax import lax
from jax.experimental import pallas as pl
from jax.experimental.pallas import tpu as pltpu

F32 = jnp.float32
BF16 = jnp.bfloat16

D_MODEL = 2048
D_RNN = 1024
RNN_HEADS = 8
HEAD_DIM = D_RNN // RNN_HEADS
CONV_WIDTH = 4
LRU_C = 8.0
D_SGU = 1024
SGU_GROUPS = 8
SGU_GROUP_DIM = D_SGU // SGU_GROUPS
CHUNK = 128
N_EXPERTS = 16
N_GROUPS = 4
EXPERTS_PER_GROUP = N_EXPERTS // N_GROUPS
TOP_K = 2
D_EXPERT = 1408
D_IN = 2 * D_RNN + 2 * D_SGU + 2 * D_MODEL
LN_EPS = 1e-5
SQRT_2_OVER_PI = float(np.sqrt(2.0 / np.pi))

LN_ROWS = 512
PROJ_ROWS = 1024
PROJ_COLS = 1024
LRU_ROWS = 512
SGU_ROWS = 512
MIX_ROWS = 256
ROUTER_ROWS = 512
DISPATCH_ROWS = 512
EXPERT_ROWS = 256
COMBINE_ROWS = 512
CONV_PAD = 8
ROW_TILE = 8

MIB = 1024 * 1024


def _params(semantics, vmem_mib):
    return pltpu.CompilerParams(dimension_semantics=semantics,
                                vmem_limit_bytes=vmem_mib * MIB,
                                disable_bounds_checks=True)


def _const_spec(shape):
    zeros = (0,) * len(shape)
    return pl.BlockSpec(shape, lambda *_: zeros, pipeline_mode=pl.Buffered(1))


def _layer_norm(v, g, b):
    mu = jnp.mean(v, axis=-1, keepdims=True)
    d = v - mu
    var = jnp.mean(d * d, axis=-1, keepdims=True)
    return d * lax.rsqrt(var + LN_EPS) * g + b


def _gelu(x):
    return x * (0.5 * (1.0 + jnp.tanh(SQRT_2_OVER_PI * (x + 0.044715 * (x * x * x)))))


def _sigmoid(x):
    return 1.0 / (1.0 + jnp.exp(-x))


def _emb_ln_kernel(x_ref, g_ref, b_ref, of_ref, ob_ref):
    y = _layer_norm(x_ref[...], g_ref[...], b_ref[...])
    of_ref[...] = y
    ob_ref[...] = y.astype(BF16)


def _emb_ln(x, g, b):
    T, D = x.shape
    row = pl.BlockSpec((LN_ROWS, D), lambda i: (i, 0))
    return pl.pallas_call(
        _emb_ln_kernel,
        grid=(T // LN_ROWS,),
        in_specs=[row, _const_spec((1, D)), _const_spec((1, D))],
        out_specs=[row, row],
        out_shape=[jax.ShapeDtypeStruct((T, D), F32), jax.ShapeDtypeStruct((T, D), BF16)],
        compiler_params=_params(("parallel",), 40),
        name="emb_ln",
    )(x, g.reshape(1, D), b.reshape(1, D))


def _proj_kernel(x_ref, w_ref, o_ref):
    acc = jnp.dot(x_ref[...], w_ref[...], preferred_element_type=F32)
    col = pl.program_id(1) * PROJ_COLS
    gelu_from = D_RNN
    sigmoid_from = 2 * D_RNN + 2 * D_SGU

    @pl.when(col < gelu_from)
    def _():
        o_ref[...] = acc.astype(BF16)

    @pl.when(jnp.logical_and(col >= gelu_from, col < sigmoid_from))
    def _():
        o_ref[...] = _gelu(acc).astype(BF16)

    @pl.when(col >= sigmoid_from)
    def _():
        o_ref[...] = _sigmoid(acc).astype(BF16)


def _proj(xb, w_in):
    T, D = xb.shape
    return pl.pallas_call(
        _proj_kernel,
        grid=(T // PROJ_ROWS, D_IN // PROJ_COLS),
        in_specs=[pl.BlockSpec((PROJ_ROWS, D), lambda i, j: (i, 0)),
                  pl.BlockSpec((D, PROJ_COLS), lambda i, j: (0, j))],
        out_specs=pl.BlockSpec((PROJ_ROWS, PROJ_COLS), lambda i, j: (i, j)),
        out_shape=jax.ShapeDtypeStruct((T, D_IN), BF16),
        compiler_params=_params(("parallel", "arbitrary"), 48),
        name="proj",
    )(xb, w_in)


def _lru_kernel(xr_ref, gate_ref, cw_ref, cb_ref, wa_ref, ba_ref, wx_ref, bx_ref, lam_ref,
                o_ref, xpad_ref, hc_ref, a_ref, u_ref, h_ref):
    rows = xr_ref.shape[0]

    @pl.when(pl.program_id(1) == 0)
    def _():
        xpad_ref[0:CONV_PAD, :] = jnp.zeros((CONV_PAD, D_RNN), F32)
        hc_ref[...] = jnp.zeros_like(hc_ref)

    x = xr_ref[...].astype(F32)
    xpad_ref[CONV_PAD:CONV_PAD + rows, :] = x
    xc = cb_ref[...] + cw_ref[CONV_WIDTH - 1:CONV_WIDTH, :] * x
    for k in range(CONV_WIDTH - 1):
        shift = CONV_WIDTH - 1 - k
        xc = xc + cw_ref[k:k + 1, :] * xpad_ref[CONV_PAD - shift:CONV_PAD - shift + rows, :]
    xpad_ref[0:CONV_PAD, :] = x[rows - CONV_PAD:rows, :]

    lam = lam_ref[...]
    sp = jnp.maximum(-lam, 0.0) + jnp.log1p(jnp.exp(-jnp.abs(lam)))
    for h in range(RNN_HEADS):
        cols = slice(h * HEAD_DIM, (h + 1) * HEAD_DIM)
        xh = xc[:, cols]
        xhb = xh.astype(BF16)
        r = _sigmoid(jnp.dot(xhb, wa_ref[h], preferred_element_type=F32) + ba_ref[:, cols])
        gi = _sigmoid(jnp.dot(xhb, wx_ref[h], preferred_element_type=F32) + bx_ref[:, cols])
        log_a = (-LRU_C) * r * sp[:, cols]
        a_ref[:, cols] = jnp.exp(log_a)
        th = jnp.tanh(log_a)
        u_ref[:, cols] = jnp.sqrt(-2.0 * th / (1.0 - th)) * (gi * xh)

    def step(t, h):
        h = a_ref[pl.ds(t, 1), :] * h + u_ref[pl.ds(t, 1), :]
        h_ref[pl.ds(t, 1), :] = h
        return h

    hc_ref[0:1, :] = lax.fori_loop(0, rows, step, hc_ref[0:1, :], unroll=8)
    o_ref[...] = (h_ref[...] * gate_ref[...].astype(F32)).astype(BF16)


def _lru(proj, batch, seq, conv_w, conv_b, w_a, b_a, w_x, b_x, lam):
    T = proj.shape[0]
    tiles = seq // LRU_ROWS
    row = lambda c: pl.BlockSpec((LRU_ROWS, D_RNN), lambda b, s: (b * tiles + s, c))
    vec = _const_spec((1, D_RNN))
    gate_w = _const_spec((RNN_HEADS, HEAD_DIM, HEAD_DIM))
    return pl.pallas_call(
        _lru_kernel,
        grid=(batch, tiles),
        in_specs=[row(0), row(1), _const_spec((CONV_WIDTH, D_RNN)), vec, gate_w, vec, gate_w, vec, vec],
        out_specs=pl.BlockSpec((LRU_ROWS, D_RNN), lambda b, s: (b * tiles + s, 0)),
        out_shape=jax.ShapeDtypeStruct((T, D_RNN), BF16),
        scratch_shapes=[pltpu.VMEM((CONV_PAD + LRU_ROWS, D_RNN), F32),
                        pltpu.VMEM((8, D_RNN), F32),
                        pltpu.VMEM((LRU_ROWS, D_RNN), F32),
                        pltpu.VMEM((LRU_ROWS, D_RNN), F32),
                        pltpu.VMEM((LRU_ROWS, D_RNN), F32)],
        compiler_params=_params(("arbitrary", "arbitrary"), 40),
        name="lru",
    )(proj, proj, conv_w, conv_b.reshape(1, D_RNN), w_a.astype(BF16), b_a.reshape(1, D_RNN),
      w_x.astype(BF16), b_x.reshape(1, D_RNN), lam.reshape(1, D_RNN))


def _sgu_kernel(u_ref, v_ref, g_ref, b_ref, ws_ref, bs_ref, o_ref):
    rows = u_ref.shape[0]
    v = _layer_norm(v_ref[...].astype(F32), g_ref[...], b_ref[...]).astype(BF16)
    t_out = lax.broadcasted_iota(jnp.int32, (CHUNK, CHUNK), 0)
    t_in = lax.broadcasted_iota(jnp.int32, (CHUNK, CHUNK), 1)
    causal = t_in <= t_out
    for g in range(SGU_GROUPS):
        cols = slice(g * SGU_GROUP_DIM, (g + 1) * SGU_GROUP_DIM)
        ws = jnp.where(causal, ws_ref[g], 0.0).astype(BF16)
        bias = bs_ref[:, g:g + 1]
        for c in range(rows // CHUNK):
            rws = slice(c * CHUNK, (c + 1) * CHUNK)
            mixed = jnp.dot(ws, v[rws, cols], preferred_element_type=F32) + bias
            o_ref[rws, cols] = (u_ref[rws, cols].astype(F32) * mixed).astype(BF16)


def _sgu(proj, ln_g, ln_b, w_s, b_s):
    T = proj.shape[0]
    row = lambda c: pl.BlockSpec((SGU_ROWS, D_SGU), lambda i: (i, c))
    return pl.pallas_call(
        _sgu_kernel,
        grid=(T // SGU_ROWS,),
        in_specs=[row(2), row(3), _const_spec((1, D_SGU)), _const_spec((1, D_SGU)),
                  _const_spec((SGU_GROUPS, CHUNK, CHUNK)), _const_spec((CHUNK, SGU_GROUPS))],
        out_specs=pl.BlockSpec((SGU_ROWS, D_SGU), lambda i: (i, 0)),
        out_shape=jax.ShapeDtypeStruct((T, D_SGU), BF16),
        compiler_params=_params(("parallel",), 40),
        name="sgu",
    )(proj, proj, ln_g.reshape(1, D_SGU), ln_b.reshape(1, D_SGU), w_s, b_s.T)


def _mix_out_kernel(alpha, a_ref, b_ref, sga_ref, sgb_ref, x_ref, woa_ref, wob_ref, wout_ref,
                    g_ref, beta_ref, o_ref):
    ya = jnp.dot(a_ref[...], woa_ref[...], preferred_element_type=F32)
    yb = jnp.dot(b_ref[...], wob_ref[...], preferred_element_type=F32)
    merged = sga_ref[...].astype(F32) * ya + sgb_ref[...].astype(F32) * yb
    m = jnp.dot(merged.astype(BF16), wout_ref[...], preferred_element_type=F32)
    o_ref[...] = _layer_norm(alpha * x_ref[...] + m, g_ref[...], beta_ref[...])


def _mix_out(alpha, a, b, proj, x, w_o_rnn, w_o_sgu, w_out, ln_g, ln_b):
    T, D = x.shape
    gates_from = (2 * D_RNN + 2 * D_SGU) // D
    row = lambda w, c: pl.BlockSpec((MIX_ROWS, w), lambda i: (i, c))
    return pl.pallas_call(
        functools.partial(_mix_out_kernel, alpha),
        grid=(T // MIX_ROWS,),
        in_specs=[row(D_RNN, 0), row(D_SGU, 0), row(D, gates_from), row(D, gates_from + 1), row(D, 0),
                  _const_spec((D_RNN, D)), _const_spec((D_SGU, D)), _const_spec((D, D)),
                  _const_spec((1, D)), _const_spec((1, D))],
        out_specs=row(D, 0),
        out_shape=jax.ShapeDtypeStruct((T, D), F32),
        compiler_params=_params(("parallel",), 52),
        name="mix_out",
    )(a, b, proj, proj, x, w_o_rnn, w_o_sgu, w_out, ln_g.reshape(1, D), ln_b.reshape(1, D))


def _first_max(v, row, n):
    m = jnp.max(v, axis=0, keepdims=True)
    idx = jnp.min(jnp.where(v == m, row, float(n)), axis=0, keepdims=True)
    return m, idx


def _router_kernel(x_ref, wt_ref, b_ref, idx_ref, gate_ref, rank_ref, cnt_ref, run_ref):
    rows = x_ref.shape[0]

    @pl.when(pl.program_id(0) == 0)
    def _():
        run_ref[...] = jnp.zeros_like(run_ref)

    logits = lax.dot_general(wt_ref[...], x_ref[...], (((1,), (1,)), ((), ())),
                             precision=lax.Precision.HIGHEST, preferred_element_type=F32)
    e = jnp.exp(logits - jnp.max(logits, axis=0, keepdims=True))
    scores = e / jnp.sum(e, axis=0, keepdims=True)
    sel = scores + b_ref[...]

    row = lax.broadcasted_iota(jnp.int32, (N_EXPERTS, rows), 0).astype(F32)
    grow = lax.broadcasted_iota(jnp.int32, (EXPERTS_PER_GROUP, rows), 0).astype(F32)
    neg_inf = float("-inf")
    best_score = None
    best_group = None
    for g in range(N_GROUPS):
        v = sel[g * EXPERTS_PER_GROUP:(g + 1) * EXPERTS_PER_GROUP, :]
        m1, i1 = _first_max(v, grow, EXPERTS_PER_GROUP)
        m2 = jnp.max(jnp.where(grow == i1, neg_inf, v), axis=0, keepdims=True)
        s = m1 + m2
        if g == 0:
            best_score, best_group = s, jnp.zeros_like(s)
        else:
            better = s > best_score
            best_group = jnp.where(better, float(g), best_group)
            best_score = jnp.where(better, s, best_score)

    lo = best_group * float(EXPERTS_PER_GROUP)
    in_group = jnp.logical_and(row >= lo, row < lo + float(EXPERTS_PER_GROUP))
    masked = jnp.where(in_group, sel, neg_inf)
    _, i1 = _first_max(masked, row, N_EXPERTS)
    pick1 = row == i1
    _, i2 = _first_max(jnp.where(pick1, neg_inf, masked), row, N_EXPERTS)
    pick2 = row == i2
    s1 = jnp.sum(jnp.where(pick1, scores, 0.0), axis=0, keepdims=True)
    s2 = jnp.sum(jnp.where(pick2, scores, 0.0), axis=0, keepdims=True)
    den = s1 + s2

    onehot = jnp.where(jnp.logical_or(pick1, pick2), 1.0, 0.0)
    before = (lax.broadcasted_iota(jnp.int32, (rows, rows), 0)
              < lax.broadcasted_iota(jnp.int32, (rows, rows), 1))
    prefix = jnp.dot(onehot.astype(BF16), jnp.where(before, 1.0, 0.0).astype(BF16),
                     preferred_element_type=F32)
    pos = prefix + run_ref[:, 0:1]
    r1 = jnp.sum(jnp.where(pick1, pos, 0.0), axis=0, keepdims=True)
    r2 = jnp.sum(jnp.where(pick2, pos, 0.0), axis=0, keepdims=True)
    run_ref[...] = run_ref[...] + jnp.sum(onehot, axis=1, keepdims=True)

    idx_ref[0:1, :] = i1.astype(jnp.int32)
    idx_ref[1:2, :] = i2.astype(jnp.int32)
    gate_ref[0:1, :] = s1 / den
    gate_ref[1:2, :] = s2 / den
    rank_ref[0:1, :] = r1.astype(jnp.int32)
    rank_ref[1:2, :] = r2.astype(jnp.int32)
    cnt_ref[...] = run_ref[...].astype(jnp.int32)


def _router(x, router_w, router_b):
    T, D = x.shape
    pair = pl.BlockSpec((TOP_K, ROUTER_ROWS), lambda i: (0, i))
    return pl.pallas_call(
        _router_kernel,
        grid=(T // ROUTER_ROWS,),
        in_specs=[pl.BlockSpec((ROUTER_ROWS, D), lambda i: (i, 0)),
                  _const_spec((N_EXPERTS, D)), _const_spec((N_EXPERTS, 1))],
        out_specs=[pair, pair, pair, pl.BlockSpec((N_EXPERTS, 128), lambda i: (0, 0))],
        out_shape=[jax.ShapeDtypeStruct((TOP_K, T), jnp.int32),
                   jax.ShapeDtypeStruct((TOP_K, T), F32),
                   jax.ShapeDtypeStruct((TOP_K, T), jnp.int32),
                   jax.ShapeDtypeStruct((N_EXPERTS, 128), jnp.int32)],
        scratch_shapes=[pltpu.VMEM((N_EXPERTS, 128), F32)],
        compiler_params=_params(("arbitrary",), 40),
        name="router",
    )(x, router_w.T, router_b.reshape(N_EXPERTS, 1))


def _dispatch_kernel(pad_from_ref, pad_n_ref, n_used_ref, dest_ref, x_hbm, xs_hbm, zero_ref, sem, pad_sem):
    i = pl.program_id(0)
    rows = DISPATCH_ROWS
    n_blocks = xs_hbm.shape[0] // EXPERT_ROWS
    zero_rows = zero_ref.shape[0]

    def tail_copies(j):
        block = n_used_ref[0] + j
        off = pl.multiple_of(block * EXPERT_ROWS, EXPERT_ROWS)
        return block < n_blocks, [
            pltpu.make_async_copy(zero_ref, xs_hbm.at[pl.ds(off + part * zero_rows, zero_rows)], pad_sem)
            for part in range(EXPERT_ROWS // zero_rows)]

    def pad_copy(e, bit):
        n = pad_n_ref[e]
        if bit < ROW_TILE:
            off = pad_from_ref[e] + bit - 1
            return pltpu.make_async_copy(zero_ref.at[pl.ds(0, 1)], xs_hbm.at[pl.ds(off, 1)], pad_sem)
        done = (n & (ROW_TILE - 1)) + (n & ~(2 * bit - 1))
        off = pl.multiple_of(pad_from_ref[e] + done, ROW_TILE)
        return pltpu.make_async_copy(zero_ref.at[pl.ds(0, bit)], xs_hbm.at[pl.ds(off, bit)], pad_sem)

    def pad_needed(e, bit):
        n = pad_n_ref[e]
        if bit < ROW_TILE:
            return bit <= (n & (ROW_TILE - 1))
        return (n & bit) != 0

    bits = list(range(1, ROW_TILE)) + [
        1 << k for k in range(ROW_TILE.bit_length() - 1, EXPERT_ROWS.bit_length() - 1)]

    @pl.when(i == 0)
    def _():
        zero_ref[...] = jnp.zeros_like(zero_ref)
        for e in range(N_EXPERTS):
            for bit in bits:
                @pl.when(pad_needed(e, bit))
                def _():
                    pad_copy(e, bit).start()
        for j in range(N_EXPERTS):
            needed, copies = tail_copies(j)

            @pl.when(needed)
            def _():
                for c in copies:
                    c.start()

    def issue(t, carry):
        src = x_hbm.at[pl.ds(i * rows + t, 1)]
        pltpu.make_async_copy(src, xs_hbm.at[pl.ds(dest_ref[0, t], 1)], sem).start()
        pltpu.make_async_copy(src, xs_hbm.at[pl.ds(dest_ref[0, rows + t], 1)], sem).start()
        return carry

    lax.fori_loop(0, rows, issue, 0, unroll=8)
    for _ in range(TOP_K):
        pltpu.make_async_copy(x_hbm.at[pl.ds(0, rows)], xs_hbm.at[pl.ds(0, rows)], sem).wait()

    @pl.when(i == 0)
    def _():
        for e in range(N_EXPERTS):
            for bit in bits:
                @pl.when(pad_needed(e, bit))
                def _():
                    pad_copy(e, bit).wait()
        for j in range(N_EXPERTS):
            needed, copies = tail_copies(j)

            @pl.when(needed)
            def _():
                for c in copies:
                    c.wait()


def _dispatch(x, dest_tiles, pad_from, pad_n, n_used, n_slots):
    T, D = x.shape
    grid_spec = pltpu.PrefetchScalarGridSpec(
        num_scalar_prefetch=3,
        grid=(T // DISPATCH_ROWS,),
        in_specs=[pl.BlockSpec((None, 1, TOP_K * DISPATCH_ROWS), lambda i, *_: (i, 0, 0),
                               memory_space=pltpu.SMEM),
                  pl.BlockSpec(memory_space=pl.ANY)],
        out_specs=pl.BlockSpec(memory_space=pl.ANY),
        scratch_shapes=[pltpu.VMEM((EXPERT_ROWS // 2, D), F32),
                        pltpu.SemaphoreType.DMA(()), pltpu.SemaphoreType.DMA(())],
    )
    return pl.pallas_call(
        _dispatch_kernel,
        grid_spec=grid_spec,
        out_shape=jax.ShapeDtypeStruct((n_slots, D), F32),
        compiler_params=_params(("arbitrary",), 16),
        name="dispatch",
    )(pad_from, pad_n, n_used, dest_tiles, x)


def _expert_kernel(block_e_ref, n_used_ref, xs_ref, w1_ref, w3_ref, w2_ref, o_ref):
    used = pl.program_id(0) < n_used_ref[0]

    @pl.when(jnp.logical_not(used))
    def _():
        o_ref[...] = jnp.zeros_like(o_ref)

    @pl.when(used)
    def _():
        x = xs_ref[...].astype(BF16)
        h1 = jnp.dot(x, w1_ref[...], preferred_element_type=F32)
        h3 = jnp.dot(x, w3_ref[...], preferred_element_type=F32)
        hidden = (h1 * _sigmoid(h1)) * h3
        o_ref[...] = jnp.dot(hidden.astype(BF16), w2_ref[...], preferred_element_type=F32)


def _experts(xs, block_e, n_used, w1, w3, w2):
    P, D = xs.shape
    n_blocks = P // EXPERT_ROWS
    used_rows = pl.BlockSpec((EXPERT_ROWS, D), lambda i, be, nu: (jnp.minimum(i, nu[0] - 1), 0))
    rows = pl.BlockSpec((EXPERT_ROWS, D), lambda i, be, nu: (i, 0))
    grid_spec = pltpu.PrefetchScalarGridSpec(
        num_scalar_prefetch=2,
        grid=(n_blocks,),
        in_specs=[used_rows,
                  pl.BlockSpec((None, D, D_EXPERT), lambda i, be, nu: (be[i], 0, 0)),
                  pl.BlockSpec((None, D, D_EXPERT), lambda i, be, nu: (be[i], 0, 0)),
                  pl.BlockSpec((None, D_EXPERT, D), lambda i, be, nu: (be[i], 0, 0))],
        out_specs=rows,
    )
    return pl.pallas_call(
        _expert_kernel,
        grid_spec=grid_spec,
        out_shape=jax.ShapeDtypeStruct((P, D), F32),
        compiler_params=_params(("arbitrary",), 56),
        name="experts",
    )(block_e, n_used, xs, w1, w3, w2)


def _combine_kernel(alpha, dest_ref, x_ref, gate_ref, g_ref, b_ref, y_hbm, of_ref, ob_ref, buf_ref, sem):
    rows = COMBINE_ROWS

    def issue(t, carry):
        for k in range(TOP_K):
            pltpu.make_async_copy(y_hbm.at[pl.ds(dest_ref[0, k * rows + t], 1)],
                                  buf_ref.at[k, pl.ds(t, 1)], sem).start()
        return carry

    lax.fori_loop(0, rows, issue, 0, unroll=8)
    for k in range(TOP_K):
        pltpu.make_async_copy(y_hbm.at[pl.ds(0, rows)], buf_ref.at[k], sem).wait()

    f = gate_ref[:, 0:1] * buf_ref[0] + gate_ref[:, 1:2] * buf_ref[1]
    y = _layer_norm(alpha * x_ref[...] + f, g_ref[...], b_ref[...])
    of_ref[...] = y
    ob_ref[...] = y.astype(BF16)


def _combine(alpha, dest_tiles, x, gate_t, ln_g, ln_b, yb):
    T, D = x.shape
    row = pl.BlockSpec((COMBINE_ROWS, D), lambda i: (i, 0))
    return pl.pallas_call(
        functools.partial(_combine_kernel, alpha),
        grid=(T // COMBINE_ROWS,),
        in_specs=[pl.BlockSpec((None, 1, TOP_K * COMBINE_ROWS), lambda i: (i, 0, 0),
                               memory_space=pltpu.SMEM),
                  row, pl.BlockSpec((COMBINE_ROWS, TOP_K), lambda i: (i, 0)),
                  _const_spec((1, D)), _const_spec((1, D)),
                  pl.BlockSpec(memory_space=pl.ANY)],
        out_specs=[row, row],
        out_shape=[jax.ShapeDtypeStruct((T, D), F32), jax.ShapeDtypeStruct((T, D), BF16)],
        scratch_shapes=[pltpu.VMEM((TOP_K, COMBINE_ROWS, D), F32), pltpu.SemaphoreType.DMA(())],
        compiler_params=_params(("arbitrary",), 48),
        name="combine",
    )(dest_tiles, x, gate_t, ln_g.reshape(1, D), ln_b.reshape(1, D), yb)


def _tile_pairs(dest, rows):
    T = dest.shape[1]
    return dest.reshape(TOP_K, T // rows, rows).transpose(1, 0, 2).reshape(T // rows, 1, TOP_K * rows)


def _moe(alpha, x, router_w, router_b, w1, w3, w2, ln_g, ln_b):
    T, D = x.shape
    n_blocks = (T * TOP_K) // EXPERT_ROWS + N_EXPERTS
    idx, gate, rank, cnt = _router(x, router_w, router_b)

    counts = cnt[:, 0]
    padded = (counts + EXPERT_ROWS - 1) // EXPERT_ROWS * EXPERT_ROWS
    pad_end = jnp.cumsum(padded)
    pad_start = pad_end - padded
    n_used = (pad_end[-1] // EXPERT_ROWS).astype(jnp.int32)
    block_start = jnp.minimum(jnp.arange(n_blocks, dtype=jnp.int32), n_used - 1) * EXPERT_ROWS
    block_e = jnp.minimum(jnp.searchsorted(pad_end, block_start, side="right"),
                          N_EXPERTS - 1).astype(jnp.int32)
    expert_ids = jnp.arange(N_EXPERTS, dtype=jnp.int32)[:, None, None]
    dest = jnp.sum(jnp.where(idx[None] == expert_ids, pad_start[:, None, None], 0), axis=0) + rank

    n_used = n_used.reshape(1)
    xs = _dispatch(x, _tile_pairs(dest, DISPATCH_ROWS), (pad_start + counts).astype(jnp.int32),
                   (padded - counts).astype(jnp.int32), n_used, n_blocks * EXPERT_ROWS)
    yb = _experts(xs, block_e, n_used, w1, w3, w2)
    return _combine(alpha, _tile_pairs(dest, COMBINE_ROWS), x, gate.T, ln_g, ln_b, yb)


def kernel(x, emb_ln_g, emb_ln_b, w_in, conv_w, conv_b, lru_w_a, lru_b_a, lru_w_x, lru_b_x, lru_lambda, w_o_rnn, sgu_ln_g, sgu_ln_b, sgu_w_s, sgu_b_s, w_o_sgu, w_out, ln1_g, ln1_b, router_w, router_b, expert_w1, expert_w3, expert_w2, ln2_g, ln2_b):
    batch, seq, D = x.shape
    depth = w_in.shape[0]
    alpha = float((2 * depth) ** 0.25)
    xf, xb = _emb_ln(x.reshape(batch * seq, D), emb_ln_g, emb_ln_b)
    for l in range(depth):
        proj = _proj(xb, w_in[l].astype(BF16))
        a = _lru(proj, batch, seq, conv_w[l], conv_b[l], lru_w_a[l], lru_b_a[l], lru_w_x[l],
                 lru_b_x[l], lru_lambda[l])
        b = _sgu(proj, sgu_ln_g[l], sgu_ln_b[l], sgu_w_s[l], sgu_b_s[l])
        x1 = _mix_out(alpha, a, b, proj, xf, w_o_rnn[l].astype(BF16), w_o_sgu[l].astype(BF16),
                      w_out[l].astype(BF16), ln1_g[l], ln1_b[l])
        xf, xb = _moe(alpha, x1, router_w, router_b, expert_w1[l].astype(BF16),
                      expert_w3[l].astype(BF16), expert_w2[l].astype(BF16), ln2_g[l], ln2_b[l])
    return xf.reshape(batch, seq, D)
```

```python
import functools

import jax
import jax.numpy as jnp
import numpy as np
from jax import lax
from jax.experimental import pallas as pl
from jax.experimental.pallas import tpu as pltpu

F32 = jnp.float32
BF16 = jnp.bfloat16

D_MODEL = 2048
D_RNN = 1024
RNN_HEADS = 8
HEAD_DIM = D_RNN // RNN_HEADS
CONV_WIDTH = 4
LRU_C = 8.0
D_SGU = 1024
SGU_GROUPS = 8
SGU_GROUP_DIM = D_SGU // SGU_GROUPS
CHUNK = 128
N_EXPERTS = 16
N_GROUPS = 4
EXPERTS_PER_GROUP = N_EXPERTS // N_GROUPS
TOP_K = 2
D_EXPERT = 1408
D_IN = 2 * D_RNN + 2 * D_SGU + 2 * D_MODEL
LN_EPS = 1e-5
SQRT_2_OVER_PI = float(np.sqrt(2.0 / np.pi))

LN_ROWS = 512
PROJ_ROWS = 1024
PROJ_COLS = 1024
LRU_ROWS = 512
SGU_ROWS = 512
MIX_ROWS = 256
ROUTER_ROWS = 512
DISPATCH_ROWS = 512
EXPERT_ROWS = 256
COMBINE_ROWS = 512
CONV_PAD = 8
ROW_TILE = 8

MIB = 1024 * 1024


def _params(semantics, vmem_mib):
    return pltpu.CompilerParams(dimension_semantics=semantics,
                                vmem_limit_bytes=vmem_mib * MIB,
                                disable_bounds_checks=True)


def _const_spec(shape):
    zeros = (0,) * len(shape)
    return pl.BlockSpec(shape, lambda *_: zeros, pipeline_mode=pl.Buffered(1))


def _layer_norm(v, g, b):
    mu = jnp.mean(v, axis=-1, keepdims=True)
    d = v - mu
    var = jnp.mean(d * d, axis=-1, keepdims=True)
    return d * lax.rsqrt(var + LN_EPS) * g + b


def _gelu(x):
    return x * (0.5 * (1.0 + jnp.tanh(SQRT_2_OVER_PI * (x + 0.044715 * (x * x * x)))))


def _sigmoid(x):
    return 1.0 / (1.0 + jnp.exp(-x))


def _emb_ln_kernel(x_ref, g_ref, b_ref, of_ref, ob_ref):
    y = _layer_norm(x_ref[...], g_ref[...], b_ref[...])
    of_ref[...] = y
    ob_ref[...] = y.astype(BF16)


def _emb_ln(x, g, b):
    T, D = x.shape
    row = pl.BlockSpec((LN_ROWS, D), lambda i: (i, 0))
    return pl.pallas_call(
        _emb_ln_kernel,
        grid=(T // LN_ROWS,),
        in_specs=[row, _const_spec((1, D)), _const_spec((1, D))],
        out_specs=[row, row],
        out_shape=[jax.ShapeDtypeStruct((T, D), F32), jax.ShapeDtypeStruct((T, D), BF16)],
        compiler_params=_params(("parallel",), 40),
        name="emb_ln",
    )(x, g.reshape(1, D), b.reshape(1, D))


def _proj_kernel(x_ref, w_ref, o_ref):
    acc = jnp.dot(x_ref[...], w_ref[...], preferred_element_type=F32)
    col = pl.program_id(1) * PROJ_COLS
    gelu_from = D_RNN
    sigmoid_from = 2 * D_RNN + 2 * D_SGU

    @pl.when(col < gelu_from)
    def _():
        o_ref[...] = acc.astype(BF16)

    @pl.when(jnp.logical_and(col >= gelu_from, col < sigmoid_from))
    def _():
        o_ref[...] = _gelu(acc).astype(BF16)

    @pl.when(col >= sigmoid_from)
    def _():
        o_ref[...] = _sigmoid(acc).astype(BF16)


def _proj(xb, w_in):
    T, D = xb.shape
    return pl.pallas_call(
        _proj_kernel,
        grid=(T // PROJ_ROWS, D_IN // PROJ_COLS),
        in_specs=[pl.BlockSpec((PROJ_ROWS, D), lambda i, j: (i, 0)),
                  pl.BlockSpec((D, PROJ_COLS), lambda i, j: (0, j))],
        out_specs=pl.BlockSpec((PROJ_ROWS, PROJ_COLS), lambda i, j: (i, j)),
        out_shape=jax.ShapeDtypeStruct((T, D_IN), BF16),
        compiler_params=_params(("parallel", "arbitrary"), 48),
        name="proj",
    )(xb, w_in)


def _lru_kernel(xr_ref, gate_ref, cw_ref, cb_ref, wa_ref, ba_ref, wx_ref, bx_ref, lam_ref,
                o_ref, xpad_ref, hc_ref, a_ref, u_ref, h_ref):
    rows = xr_ref.shape[0]

    @pl.when(pl.program_id(1) == 0)
    def _():
        xpad_ref[0:CONV_PAD, :] = jnp.zeros((CONV_PAD, D_RNN), F32)
        hc_ref[...] = jnp.zeros_like(hc_ref)

    x = xr_ref[...].astype(F32)
    xpad_ref[CONV_PAD:CONV_PAD + rows, :] = x
    xc = cb_ref[...] + cw_ref[CONV_WIDTH - 1:CONV_WIDTH, :] * x
    for k in range(CONV_WIDTH - 1):
        shift = CONV_WIDTH - 1 - k
        xc = xc + cw_ref[k:k + 1, :] * xpad_ref[CONV_PAD - shift:CONV_PAD - shift + rows, :]
    xpad_ref[0:CONV_PAD, :] = x[rows - CONV_PAD:rows, :]

    lam = lam_ref[...]
    sp = jnp.maximum(-lam, 0.0) + jnp.log1p(jnp.exp(-jnp.abs(lam)))
    for h in range(RNN_HEADS):
        cols = slice(h * HEAD_DIM, (h + 1) * HEAD_DIM)
        xh = xc[:, cols]
        xhb = xh.astype(BF16)
        r = _sigmoid(jnp.dot(xhb, wa_ref[h], preferred_element_type=F32) + ba_ref[:, cols])
        gi = _sigmoid(jnp.dot(xhb, wx_ref[h], preferred_element_type=F32) + bx_ref[:, cols])
        log_a = (-LRU_C) * r * sp[:, cols]
        a_ref[:, cols] = jnp.exp(log_a)
        th = jnp.tanh(log_a)
        u_ref[:, cols] = jnp.sqrt(-2.0 * th / (1.0 - th)) * (gi * xh)

    def step(t, h):
        h = a_ref[pl.ds(t, 1), :] * h + u_ref[pl.ds(t, 1), :]
        h_ref[pl.ds(t, 1), :] = h
        return h

    hc_ref[0:1, :] = lax.fori_loop(0, rows, step, hc_ref[0:1, :], unroll=8)
    o_ref[...] = (h_ref[...] * gate_ref[...].astype(F32)).astype(BF16)


def _lru(proj, batch, seq, conv_w, conv_b, w_a, b_a, w_x, b_x, lam):
    T = proj.shape[0]
    tiles = seq // LRU_ROWS
    row = lambda c: pl.BlockSpec((LRU_ROWS, D_RNN), lambda b, s: (b * tiles + s, c))
    vec = _const_spec((1, D_RNN))
    gate_w = _const_spec((RNN_HEADS, HEAD_DIM, HEAD_DIM))
    return pl.pallas_call(
        _lru_kernel,
        grid=(batch, tiles),
        in_specs=[row(0), row(1), _const_spec((CONV_WIDTH, D_RNN)), vec, gate_w, vec, gate_w, vec, vec],
        out_specs=pl.BlockSpec((LRU_ROWS, D_RNN), lambda b, s: (b * tiles + s, 0)),
        out_shape=jax.ShapeDtypeStruct((T, D_RNN), BF16),
        scratch_shapes=[pltpu.VMEM((CONV_PAD + LRU_ROWS, D_RNN), F32),
                        pltpu.VMEM((8, D_RNN), F32),
                        pltpu.VMEM((LRU_ROWS, D_RNN), F32),
                        pltpu.VMEM((LRU_ROWS, D_RNN), F32),
                        pltpu.VMEM((LRU_ROWS, D_RNN), F32)],
        compiler_params=_params(("arbitrary", "arbitrary"), 40),
        name="lru",
    )(proj, proj, conv_w, conv_b.reshape(1, D_RNN), w_a.astype(BF16), b_a.reshape(1, D_RNN),
      w_x.astype(BF16), b_x.reshape(1, D_RNN), lam.reshape(1, D_RNN))


def _sgu_kernel(u_ref, v_ref, g_ref, b_ref, ws_ref, bs_ref, o_ref):
    rows = u_ref.shape[0]
    v = _layer_norm(v_ref[...].astype(F32), g_ref[...], b_ref[...]).astype(BF16)
    t_out = lax.broadcasted_iota(jnp.int32, (CHUNK, CHUNK), 0)
    t_in = lax.broadcasted_iota(jnp.int32, (CHUNK, CHUNK), 1)
    causal = t_in <= t_out
    for g in range(SGU_GROUPS):
        cols = slice(g * SGU_GROUP_DIM, (g + 1) * SGU_GROUP_DIM)
        ws = jnp.where(causal, ws_ref[g], 0.0).astype(BF16)
        bias = bs_ref[:, g:g + 1]
        for c in range(rows // CHUNK):
            rws = slice(c * CHUNK, (c + 1) * CHUNK)
            mixed = jnp.dot(ws, v[rws, cols], preferred_element_type=F32) + bias
            o_ref[rws, cols] = (u_ref[rws, cols].astype(F32) * mixed).astype(BF16)


def _sgu(proj, ln_g, ln_b, w_s, b_s):
    T = proj.shape[0]
    row = lambda c: pl.BlockSpec((SGU_ROWS, D_SGU), lambda i: (i, c))
    return pl.pallas_call(
        _sgu_kernel,
        grid=(T // SGU_ROWS,),
        in_specs=[row(2), row(3), _const_spec((1, D_SGU)), _const_spec((1, D_SGU)),
                  _const_spec((SGU_GROUPS, CHUNK, CHUNK)), _const_spec((CHUNK, SGU_GROUPS))],
        out_specs=pl.BlockSpec((SGU_ROWS, D_SGU), lambda i: (i, 0)),
        out_shape=jax.ShapeDtypeStruct((T, D_SGU), BF16),
        compiler_params=_params(("parallel",), 40),
        name="sgu",
    )(proj, proj, ln_g.reshape(1, D_SGU), ln_b.reshape(1, D_SGU), w_s, b_s.T)


def _mix_out_kernel(alpha, a_ref, b_ref, sga_ref, sgb_ref, x_ref, woa_ref, wob_ref, wout_ref,
                    g_ref, beta_ref, o_ref):
    ya = jnp.dot(a_ref[...], woa_ref[...], preferred_element_type=F32)
    yb = jnp.dot(b_ref[...], wob_ref[...], preferred_element_type=F32)
    merged = sga_ref[...].astype(F32) * ya + sgb_ref[...].astype(F32) * yb
    m = jnp.dot(merged.astype(BF16), wout_ref[...], preferred_element_type=F32)
    o_ref[...] = _layer_norm(alpha * x_ref[...] + m, g_ref[...], beta_ref[...])


def _mix_out(alpha, a, b, proj, x, w_o_rnn, w_o_sgu, w_out, ln_g, ln_b):
    T, D = x.shape
    gates_from = (2 * D_RNN + 2 * D_SGU) // D
    row = lambda w, c: pl.BlockSpec((MIX_ROWS, w), lambda i: (i, c))
    return pl.pallas_call(
        functools.partial(_mix_out_kernel, alpha),
        grid=(T // MIX_ROWS,),
        in_specs=[row(D_RNN, 0), row(D_SGU, 0), row(D, gates_from), row(D, gates_from + 1), row(D, 0),
                  _const_spec((D_RNN, D)), _const_spec((D_SGU, D)), _const_spec((D, D)),
                  _const_spec((1, D)), _const_spec((1, D))],
        out_specs=row(D, 0),
        out_shape=jax.ShapeDtypeStruct((T, D), F32),
        compiler_params=_params(("parallel",), 52),
        name="mix_out",
    )(a, b, proj, proj, x, w_o_rnn, w_o_sgu, w_out, ln_g.reshape(1, D), ln_b.reshape(1, D))


def _first_max(v, row, n):
    m = jnp.max(v, axis=0, keepdims=True)
    idx = jnp.min(jnp.where(v == m, row, float(n)), axis=0, keepdims=True)
    return m, idx


def _router_kernel(x_ref, wt_ref, b_ref, idx_ref, gate_ref, rank_ref, cnt_ref, run_ref):
    rows = x_ref.shape[0]

    @pl.when(pl.program_id(0) == 0)
    def _():
        run_ref[...] = jnp.zeros_like(run_ref)

    logits = lax.dot_general(wt_ref[...], x_ref[...], (((1,), (1,)), ((), ())),
                             precision=lax.Precision.HIGHEST, preferred_element_type=F32)
    e = jnp.exp(logits - jnp.max(logits, axis=0, keepdims=True))
    scores = e / jnp.sum(e, axis=0, keepdims=True)
    sel = scores + b_ref[...]

    row = lax.broadcasted_iota(jnp.int32, (N_EXPERTS, rows), 0).astype(F32)
    grow = lax.broadcasted_iota(jnp.int32, (EXPERTS_PER_GROUP, rows), 0).astype(F32)
    neg_inf = float("-inf")
    best_score = None
    best_group = None
    for g in range(N_GROUPS):
        v = sel[g * EXPERTS_PER_GROUP:(g + 1) * EXPERTS_PER_GROUP, :]
        m1, i1 = _first_max(v, grow, EXPERTS_PER_GROUP)
        m2 = jnp.max(jnp.where(grow == i1, neg_inf, v), axis=0, keepdims=True)
        s = m1 + m2
        if g == 0:
            best_score, best_group = s, jnp.zeros_like(s)
        else:
            better = s > best_score
            best_group = jnp.where(better, float(g), best_group)
            best_score = jnp.where(better, s, best_score)

    lo = best_group * float(EXPERTS_PER_GROUP)
    in_group = jnp.logical_and(row >= lo, row < lo + float(EXPERTS_PER_GROUP))
    masked = jnp.where(in_group, sel, neg_inf)
    _, i1 = _first_max(masked, row, N_EXPERTS)
    pick1 = row == i1
    _, i2 = _first_max(jnp.where(pick1, neg_inf, masked), row, N_EXPERTS)
    pick2 = row == i2
    s1 = jnp.sum(jnp.where(pick1, scores, 0.0), axis=0, keepdims=True)
    s2 = jnp.sum(jnp.where(pick2, scores, 0.0), axis=0, keepdims=True)
    den = s1 + s2

    onehot = jnp.where(jnp.logical_or(pick1, pick2), 1.0, 0.0)
    before = (lax.broadcasted_iota(jnp.int32, (rows, rows), 0)
              < lax.broadcasted_iota(jnp.int32, (rows, rows), 1))
    prefix = jnp.dot(onehot.astype(BF16), jnp.where(before, 1.0, 0.0).astype(BF16),
                     preferred_element_type=F32)
    pos = prefix + run_ref[:, 0:1]
    r1 = jnp.sum(jnp.where(pick1, pos, 0.0), axis=0, keepdims=True)
    r2 = jnp.sum(jnp.where(pick2, pos, 0.0), axis=0, keepdims=True)
    run_ref[...] = run_ref[...] + jnp.sum(onehot, axis=1, keepdims=True)

    idx_ref[0:1, :] = i1.astype(jnp.int32)
    idx_ref[1:2, :] = i2.astype(jnp.int32)
    gate_ref[0:1, :] = s1 / den
    gate_ref[1:2, :] = s2 / den
    rank_ref[0:1, :] = r1.astype(jnp.int32)
    rank_ref[1:2, :] = r2.astype(jnp.int32)
    cnt_ref[...] = run_ref[...].astype(jnp.int32)


def _router(x, router_w, router_b):
    T, D = x.shape
    pair = pl.BlockSpec((TOP_K, ROUTER_ROWS), lambda i: (0, i))
    return pl.pallas_call(
        _router_kernel,
        grid=(T // ROUTER_ROWS,),
        in_specs=[pl.BlockSpec((ROUTER_ROWS, D), lambda i: (i, 0)),
                  _const_spec((N_EXPERTS, D)), _const_spec((N_EXPERTS, 1))],
        out_specs=[pair, pair, pair, pl.BlockSpec((N_EXPERTS, 128), lambda i: (0, 0))],
        out_shape=[jax.ShapeDtypeStruct((TOP_K, T), jnp.int32),
                   jax.ShapeDtypeStruct((TOP_K, T), F32),
                   jax.ShapeDtypeStruct((TOP_K, T), jnp.int32),
                   jax.ShapeDtypeStruct((N_EXPERTS, 128), jnp.int32)],
        scratch_shapes=[pltpu.VMEM((N_EXPERTS, 128), F32)],
        compiler_params=_params(("arbitrary",), 40),
        name="router",
    )(x, router_w.T, router_b.reshape(N_EXPERTS, 1))


def _dispatch_kernel(pad_from_ref, pad_n_ref, n_used_ref, dest_ref, x_ref, xs_hbm, zero_ref, sem, pad_sem):
    i = pl.program_id(0)
    rows = DISPATCH_ROWS
    n_blocks = xs_hbm.shape[0] // EXPERT_ROWS
    zero_rows = zero_ref.shape[0]

    def tail_copies(j):
        block = n_used_ref[0] + j
        off = pl.multiple_of(block * EXPERT_ROWS, EXPERT_ROWS)
        return block < n_blocks, [
            pltpu.make_async_copy(zero_ref, xs_hbm.at[pl.ds(off + part * zero_rows, zero_rows)], pad_sem)
            for part in range(EXPERT_ROWS // zero_rows)]

    def pad_copy(e, bit):
        n = pad_n_ref[e]
        if bit < ROW_TILE:
            off = pad_from_ref[e] + bit - 1
            return pltpu.make_async_copy(zero_ref.at[pl.ds(0, 1)], xs_hbm.at[pl.ds(off, 1)], pad_sem)
        done = (n & (ROW_TILE - 1)) + (n & ~(2 * bit - 1))
        off = pl.multiple_of(pad_from_ref[e] + done, ROW_TILE)
        return pltpu.make_async_copy(zero_ref.at[pl.ds(0, bit)], xs_hbm.at[pl.ds(off, bit)], pad_sem)

    def pad_needed(e, bit):
        n = pad_n_ref[e]
        if bit < ROW_TILE:
            return bit <= (n & (ROW_TILE - 1))
        return (n & bit) != 0

    bits = list(range(1, ROW_TILE)) + [
        1 << k for k in range(ROW_TILE.bit_length() - 1, EXPERT_ROWS.bit_length() - 1)]

    @pl.when(i == 0)
    def _():
        zero_ref[...] = jnp.zeros_like(zero_ref)
        for e in range(N_EXPERTS):
            for bit in bits:
                @pl.when(pad_needed(e, bit))
                def _():
                    pad_copy(e, bit).start()
        for j in range(N_EXPERTS):
            needed, copies = tail_copies(j)

            @pl.when(needed)
            def _():
                for c in copies:
                    c.start()

    def issue(t, carry):
        src = x_ref.at[pl.ds(t, 1)]
        pltpu.make_async_copy(src, xs_hbm.at[pl.ds(dest_ref[0, t], 1)], sem).start()
        pltpu.make_async_copy(src, xs_hbm.at[pl.ds(dest_ref[0, rows + t], 1)], sem).start()
        return carry

    lax.fori_loop(0, rows, issue, 0, unroll=8)
    for _ in range(TOP_K):
        pltpu.make_async_copy(x_ref, xs_hbm.at[pl.ds(0, rows)], sem).wait()

    @pl.when(i == 0)
    def _():
        for e in range(N_EXPERTS):
            for bit in bits:
                @pl.when(pad_needed(e, bit))
                def _():
                    pad_copy(e, bit).wait()
        for j in range(N_EXPERTS):
            needed, copies = tail_copies(j)

            @pl.when(needed)
            def _():
                for c in copies:
                    c.wait()


def _dispatch(x, dest_tiles, pad_from, pad_n, n_used, n_slots):
    T, D = x.shape
    grid_spec = pltpu.PrefetchScalarGridSpec(
        num_scalar_prefetch=3,
        grid=(T // DISPATCH_ROWS,),
        in_specs=[pl.BlockSpec((None, 1, TOP_K * DISPATCH_ROWS), lambda i, *_: (i, 0, 0),
                               memory_space=pltpu.SMEM),
                  pl.BlockSpec((DISPATCH_ROWS, D), lambda i, *_: (i, 0))],
        out_specs=pl.BlockSpec(memory_space=pl.ANY),
        scratch_shapes=[pltpu.VMEM((EXPERT_ROWS // 2, D), F32),
                        pltpu.SemaphoreType.DMA(()), pltpu.SemaphoreType.DMA(())],
    )
    return pl.pallas_call(
        _dispatch_kernel,
        grid_spec=grid_spec,
        out_shape=jax.ShapeDtypeStruct((n_slots, D), F32),
        compiler_params=_params(("arbitrary",), 16),
        name="dispatch",
    )(pad_from, pad_n, n_used, dest_tiles, x)


def _expert_kernel(block_e_ref, n_used_ref, xs_ref, w1_ref, w3_ref, w2_ref, o_ref):
    used = pl.program_id(0) < n_used_ref[0]

    @pl.when(jnp.logical_not(used))
    def _():
        o_ref[...] = jnp.zeros_like(o_ref)

    @pl.when(used)
    def _():
        x = xs_ref[...].astype(BF16)
        h1 = jnp.dot(x, w1_ref[...], preferred_element_type=F32)
        h3 = jnp.dot(x, w3_ref[...], preferred_element_type=F32)
        hidden = (h1 * _sigmoid(h1)) * h3
        o_ref[...] = jnp.dot(hidden.astype(BF16), w2_ref[...], preferred_element_type=F32)


def _experts(xs, block_e, n_used, w1, w3, w2):
    P, D = xs.shape
    n_blocks = P // EXPERT_ROWS
    used_rows = pl.BlockSpec((EXPERT_ROWS, D), lambda i, be, nu: (jnp.minimum(i, nu[0] - 1), 0))
    rows = pl.BlockSpec((EXPERT_ROWS, D), lambda i, be, nu: (i, 0))
    grid_spec = pltpu.PrefetchScalarGridSpec(
        num_scalar_prefetch=2,
        grid=(n_blocks,),
        in_specs=[used_rows,
                  pl.BlockSpec((None, D, D_EXPERT), lambda i, be, nu: (be[i], 0, 0)),
                  pl.BlockSpec((None, D, D_EXPERT), lambda i, be, nu: (be[i], 0, 0)),
                  pl.BlockSpec((None, D_EXPERT, D), lambda i, be, nu: (be[i], 0, 0))],
        out_specs=rows,
    )
    return pl.pallas_call(
        _expert_kernel,
        grid_spec=grid_spec,
        out_shape=jax.ShapeDtypeStruct((P, D), F32),
        compiler_params=_params(("arbitrary",), 56),
        name="experts",
    )(block_e, n_used, xs, w1, w3, w2)


def _combine_kernel(alpha, dest_ref, x_ref, gate_ref, g_ref, b_ref, y_hbm, of_ref, ob_ref, buf_ref, sem):
    rows = COMBINE_ROWS

    def issue(t, carry):
        for k in range(TOP_K):
            pltpu.make_async_copy(y_hbm.at[pl.ds(dest_ref[0, k * rows + t], 1)],
                                  buf_ref.at[k, pl.ds(t, 1)], sem).start()
        return carry

    lax.fori_loop(0, rows, issue, 0, unroll=8)
    for k in range(TOP_K):
        pltpu.make_async_copy(y_hbm.at[pl.ds(0, rows)], buf_ref.at[k], sem).wait()

    f = gate_ref[:, 0:1] * buf_ref[0] + gate_ref[:, 1:2] * buf_ref[1]
    y = _layer_norm(alpha * x_ref[...] + f, g_ref[...], b_ref[...])
    of_ref[...] = y
    ob_ref[...] = y.astype(BF16)


def _combine(alpha, dest_tiles, x, gate_t, ln_g, ln_b, yb):
    T, D = x.shape
    row = pl.BlockSpec((COMBINE_ROWS, D), lambda i: (i, 0))
    return pl.pallas_call(
        functools.partial(_combine_kernel, alpha),
        grid=(T // COMBINE_ROWS,),
        in_specs=[pl.BlockSpec((None, 1, TOP_K * COMBINE_ROWS), lambda i: (i, 0, 0),
                               memory_space=pltpu.SMEM),
                  row, pl.BlockSpec((COMBINE_ROWS, TOP_K), lambda i: (i, 0)),
                  _const_spec((1, D)), _const_spec((1, D)),
                  pl.BlockSpec(memory_space=pl.ANY)],
        out_specs=[row, row],
        out_shape=[jax.ShapeDtypeStruct((T, D), F32), jax.ShapeDtypeStruct((T, D), BF16)],
        scratch_shapes=[pltpu.VMEM((TOP_K, COMBINE_ROWS, D), F32), pltpu.SemaphoreType.DMA(())],
        compiler_params=_params(("arbitrary",), 48),
        name="combine",
    )(dest_tiles, x, gate_t, ln_g.reshape(1, D), ln_b.reshape(1, D), yb)


def _tile_pairs(dest, rows):
    T = dest.shape[1]
    return dest.reshape(TOP_K, T // rows, rows).transpose(1, 0, 2).reshape(T // rows, 1, TOP_K * rows)


def _moe(alpha, x, router_w, router_b, w1, w3, w2, ln_g, ln_b):
    T, D = x.shape
    n_blocks = (T * TOP_K) // EXPERT_ROWS + N_EXPERTS
    idx, gate, rank, cnt = _router(x, router_w, router_b)

    counts = cnt[:, 0]
    padded = (counts + EXPERT_ROWS - 1) // EXPERT_ROWS * EXPERT_ROWS
    pad_end = jnp.cumsum(padded)
    pad_start = pad_end - padded
    n_used = (pad_end[-1] // EXPERT_ROWS).astype(jnp.int32)
    block_start = jnp.minimum(jnp.arange(n_blocks, dtype=jnp.int32), n_used - 1) * EXPERT_ROWS
    block_e = jnp.minimum(jnp.searchsorted(pad_end, block_start, side="right"),
                          N_EXPERTS - 1).astype(jnp.int32)
    expert_ids = jnp.arange(N_EXPERTS, dtype=jnp.int32)[:, None, None]
    dest = jnp.sum(jnp.where(idx[None] == expert_ids, pad_start[:, None, None], 0), axis=0) + rank

    n_used = n_used.reshape(1)
    xs = _dispatch(x, _tile_pairs(dest, DISPATCH_ROWS), (pad_start + counts).astype(jnp.int32),
                   (padded - counts).astype(jnp.int32), n_used, n_blocks * EXPERT_ROWS)
    yb = _experts(xs, block_e, n_used, w1, w3, w2)
    return _combine(alpha, _tile_pairs(dest, COMBINE_ROWS), x, gate.T, ln_g, ln_b, yb)


def kernel(x, emb_ln_g, emb_ln_b, w_in, conv_w, conv_b, lru_w_a, lru_b_a, lru_w_x, lru_b_x, lru_lambda, w_o_rnn, sgu_ln_g, sgu_ln_b, sgu_w_s, sgu_b_s, w_o_sgu, w_out, ln1_g, ln1_b, router_w, router_b, expert_w1, expert_w3, expert_w2, ln2_g, ln2_b):
    batch, seq, D = x.shape
    depth = w_in.shape[0]
    alpha = float((2 * depth) ** 0.25)
    xf, xb = _emb_ln(x.reshape(batch * seq, D), emb_ln_g, emb_ln_b)
    for l in range(depth):
        proj = _proj(xb, w_in[l].astype(BF16))
        a = _lru(proj, batch, seq, conv_w[l], conv_b[l], lru_w_a[l], lru_b_a[l], lru_w_x[l],
                 lru_b_x[l], lru_lambda[l])
        b = _sgu(proj, sgu_ln_g[l], sgu_ln_b[l], sgu_w_s[l], sgu_b_s[l])
        x1 = _mix_out(alpha, a, b, proj, xf, w_o_rnn[l].astype(BF16), w_o_sgu[l].astype(BF16),
                      w_out[l].astype(BF16), ln1_g[l], ln1_b[l])
        xf, xb = _moe(alpha, x1, router_w, router_b, expert_w1[l].astype(BF16),
                      expert_w3[l].astype(BF16), expert_w2[l].astype(BF16), ln2_g[l], ln2_b[l])
    return xf.reshape(batch, seq, D)
```

```python
import functools

import jax
import jax.numpy as jnp
import numpy as np
from jax import lax
from jax.experimental import pallas as pl
from jax.experimental.pallas import tpu as pltpu

F32 = jnp.float32
BF16 = jnp.bfloat16

D_MODEL = 2048
D_RNN = 1024
RNN_HEADS = 8
HEAD_DIM = D_RNN // RNN_HEADS
CONV_WIDTH = 4
LRU_C = 8.0
D_SGU = 1024
SGU_GROUPS = 8
SGU_GROUP_DIM = D_SGU // SGU_GROUPS
CHUNK = 128
N_EXPERTS = 16
N_GROUPS = 4
EXPERTS_PER_GROUP = N_EXPERTS // N_GROUPS
TOP_K = 2
D_EXPERT = 1408
D_IN = 2 * D_RNN + 2 * D_SGU + 2 * D_MODEL
LN_EPS = 1e-5
SQRT_2_OVER_PI = float(np.sqrt(2.0 / np.pi))

LN_ROWS = 512
PROJ_ROWS = 1024
PROJ_COLS = 1024
LRU_ROWS = 512
SGU_ROWS = 512
MIX_ROWS = 256
ROUTER_ROWS = 512
DISPATCH_ROWS = 512
EXPERT_ROWS = 256
COMBINE_ROWS = 512
CONV_PAD = 8
ROW_TILE = 8

MIB = 1024 * 1024


def _params(semantics, vmem_mib):
    return pltpu.CompilerParams(dimension_semantics=semantics,
                                vmem_limit_bytes=vmem_mib * MIB,
                                disable_bounds_checks=True)


def _const_spec(shape):
    zeros = (0,) * len(shape)
    return pl.BlockSpec(shape, lambda *_: zeros, pipeline_mode=pl.Buffered(1))


def _layer_spec(layer, shape):
    zeros = (0,) * len(shape)
    return pl.BlockSpec((None,) + tuple(shape), lambda *_: (layer,) + zeros,
                        pipeline_mode=pl.Buffered(1))


def _layer_norm(v, g, b):
    mu = jnp.mean(v, axis=-1, keepdims=True)
    d = v - mu
    var = jnp.mean(d * d, axis=-1, keepdims=True)
    return d * lax.rsqrt(var + LN_EPS) * g + b


def _gelu(x):
    return x * (0.5 * (1.0 + jnp.tanh(SQRT_2_OVER_PI * (x + 0.044715 * (x * x * x)))))


def _sigmoid(x):
    return 0.5 * jnp.tanh(0.5 * x) + 0.5


def _emb_ln_kernel(x_ref, g_ref, b_ref, of_ref, ob_ref):
    y = _layer_norm(x_ref[...], g_ref[...], b_ref[...])
    of_ref[...] = y
    ob_ref[...] = y.astype(BF16)


def _emb_ln(x, g, b):
    T, D = x.shape
    row = pl.BlockSpec((LN_ROWS, D), lambda i: (i, 0))
    return pl.pallas_call(
        _emb_ln_kernel,
        grid=(T // LN_ROWS,),
        in_specs=[row, _const_spec((1, D)), _const_spec((1, D))],
        out_specs=[row, row],
        out_shape=[jax.ShapeDtypeStruct((T, D), F32), jax.ShapeDtypeStruct((T, D), BF16)],
        compiler_params=_params(("parallel",), 40),
        name="emb_ln",
    )(x, g.reshape(1, D), b.reshape(1, D))


def _proj_kernel(act, x_ref, w_ref, o_ref):
    acc = jnp.dot(x_ref[...], w_ref[...], preferred_element_type=F32)
    o_ref[...] = act(acc).astype(BF16)


def _proj(xb, w_in, layer, col_from, n_cols, act, name):
    T, D = xb.shape
    first = col_from // PROJ_COLS
    return pl.pallas_call(
        functools.partial(_proj_kernel, act),
        grid=(T // PROJ_ROWS, n_cols // PROJ_COLS),
        in_specs=[pl.BlockSpec((PROJ_ROWS, D), lambda i, j: (i, 0)),
                  pl.BlockSpec((None, D, PROJ_COLS), lambda i, j: (layer, 0, first + j))],
        out_specs=pl.BlockSpec((PROJ_ROWS, PROJ_COLS), lambda i, j: (i, j)),
        out_shape=jax.ShapeDtypeStruct((T, n_cols), BF16),
        compiler_params=_params(("parallel", "arbitrary"), 48),
        name=name,
    )(xb, w_in)


def _lru_kernel(xr_ref, gate_ref, cw_ref, cb_ref, wa_ref, ba_ref, wx_ref, bx_ref, lam_ref,
                o_ref, xpad_ref, hc_ref, a_ref, u_ref, h_ref):
    rows = xr_ref.shape[0]

    @pl.when(pl.program_id(1) == 0)
    def _():
        xpad_ref[0:CONV_PAD, :] = jnp.zeros((CONV_PAD, D_RNN), F32)
        hc_ref[...] = jnp.zeros_like(hc_ref)

    x = xr_ref[...].astype(F32)
    xpad_ref[CONV_PAD:CONV_PAD + rows, :] = x
    xc = cb_ref[...] + cw_ref[CONV_WIDTH - 1:CONV_WIDTH, :] * x
    for k in range(CONV_WIDTH - 1):
        shift = CONV_WIDTH - 1 - k
        xc = xc + cw_ref[k:k + 1, :] * xpad_ref[CONV_PAD - shift:CONV_PAD - shift + rows, :]
    xpad_ref[0:CONV_PAD, :] = x[rows - CONV_PAD:rows, :]

    lam = lam_ref[...]
    sp = jnp.maximum(-lam, 0.0) + jnp.log1p(jnp.exp(-jnp.abs(lam)))
    for h in range(RNN_HEADS):
        cols = slice(h * HEAD_DIM, (h + 1) * HEAD_DIM)
        xh = xc[:, cols]
        xhb = xh.astype(BF16)
        r = _sigmoid(jnp.dot(xhb, wa_ref[h], preferred_element_type=F32) + ba_ref[:, cols])
        gi = _sigmoid(jnp.dot(xhb, wx_ref[h], preferred_element_type=F32) + bx_ref[:, cols])
        log_a = (-LRU_C) * r * sp[:, cols]
        a_ref[:, cols] = jnp.exp(log_a)
        th = jnp.tanh(log_a)
        u_ref[:, cols] = jnp.sqrt(-2.0 * th / (1.0 - th)) * (gi * xh)

    def step(t, h):
        h = a_ref[pl.ds(t, 1), :] * h + u_ref[pl.ds(t, 1), :]
        h_ref[pl.ds(t, 1), :] = h
        return h

    hc_ref[0:1, :] = lax.fori_loop(0, rows, step, hc_ref[0:1, :], unroll=8)
    o_ref[...] = (h_ref[...] * gate_ref[...].astype(F32)).astype(BF16)


def _lru(x_rnn, gelu_cols, batch, seq, conv_w, conv_b, w_a, b_a, w_x, b_x, lam):
    T = x_rnn.shape[0]
    tiles = seq // LRU_ROWS
    row = pl.BlockSpec((LRU_ROWS, D_RNN), lambda b, s: (b * tiles + s, 0))
    vec = _const_spec((1, D_RNN))
    gate_w = _const_spec((RNN_HEADS, HEAD_DIM, HEAD_DIM))
    return pl.pallas_call(
        _lru_kernel,
        grid=(batch, tiles),
        in_specs=[row, row, _const_spec((CONV_WIDTH, D_RNN)), vec, gate_w, vec, gate_w, vec, vec],
        out_specs=pl.BlockSpec((LRU_ROWS, D_RNN), lambda b, s: (b * tiles + s, 0)),
        out_shape=jax.ShapeDtypeStruct((T, D_RNN), BF16),
        scratch_shapes=[pltpu.VMEM((CONV_PAD + LRU_ROWS, D_RNN), F32),
                        pltpu.VMEM((8, D_RNN), F32),
                        pltpu.VMEM((LRU_ROWS, D_RNN), F32),
                        pltpu.VMEM((LRU_ROWS, D_RNN), F32),
                        pltpu.VMEM((LRU_ROWS, D_RNN), F32)],
        compiler_params=_params(("arbitrary", "arbitrary"), 40),
        name="lru",
    )(x_rnn, gelu_cols, conv_w, conv_b.reshape(1, D_RNN), w_a.astype(BF16), b_a.reshape(1, D_RNN),
      w_x.astype(BF16), b_x.reshape(1, D_RNN), lam.reshape(1, D_RNN))


def _sgu_kernel(u_ref, v_ref, g_ref, b_ref, ws_ref, bs_ref, o_ref):
    rows = u_ref.shape[0]
    v = _layer_norm(v_ref[...].astype(F32), g_ref[...], b_ref[...]).astype(BF16)
    t_out = lax.broadcasted_iota(jnp.int32, (CHUNK, CHUNK), 0)
    t_in = lax.broadcasted_iota(jnp.int32, (CHUNK, CHUNK), 1)
    causal = t_in <= t_out
    for g in range(SGU_GROUPS):
        cols = slice(g * SGU_GROUP_DIM, (g + 1) * SGU_GROUP_DIM)
        ws = jnp.where(causal, ws_ref[g], 0.0).astype(BF16)
        bias = bs_ref[:, g:g + 1]
        for c in range(rows // CHUNK):
            rws = slice(c * CHUNK, (c + 1) * CHUNK)
            mixed = jnp.dot(ws, v[rws, cols], preferred_element_type=F32) + bias
            o_ref[rws, cols] = (u_ref[rws, cols].astype(F32) * mixed).astype(BF16)


def _sgu(gelu_cols, ln_g, ln_b, w_s, b_s):
    T = gelu_cols.shape[0]
    row = lambda c: pl.BlockSpec((SGU_ROWS, D_SGU), lambda i: (i, c))
    return pl.pallas_call(
        _sgu_kernel,
        grid=(T // SGU_ROWS,),
        in_specs=[row(1), row(2), _const_spec((1, D_SGU)), _const_spec((1, D_SGU)),
                  _const_spec((SGU_GROUPS, CHUNK, CHUNK)), _const_spec((CHUNK, SGU_GROUPS))],
        out_specs=pl.BlockSpec((SGU_ROWS, D_SGU), lambda i: (i, 0)),
        out_shape=jax.ShapeDtypeStruct((T, D_SGU), BF16),
        compiler_params=_params(("parallel",), 40),
        name="sgu",
    )(gelu_cols, gelu_cols, ln_g.reshape(1, D_SGU), ln_b.reshape(1, D_SGU), w_s, b_s.T)


def _mix_out_kernel(alpha, a_ref, b_ref, sga_ref, sgb_ref, x_ref, woa_ref, wob_ref, wout_ref,
                    g_ref, beta_ref, o_ref):
    ya = jnp.dot(a_ref[...], woa_ref[...], preferred_element_type=F32)
    yb = jnp.dot(b_ref[...], wob_ref[...], preferred_element_type=F32)
    merged = sga_ref[...].astype(F32) * ya + sgb_ref[...].astype(F32) * yb
    m = jnp.dot(merged.astype(BF16), wout_ref[...], preferred_element_type=F32)
    o_ref[...] = _layer_norm(alpha * x_ref[...] + m, g_ref[...], beta_ref[...])


def _mix_out(alpha, layer, a, b, sigmoid_cols, x, w_o_rnn, w_o_sgu, w_out, ln_g, ln_b):
    T, D = x.shape
    row = lambda w, c: pl.BlockSpec((MIX_ROWS, w), lambda i: (i, c))
    return pl.pallas_call(
        functools.partial(_mix_out_kernel, alpha),
        grid=(T // MIX_ROWS,),
        in_specs=[row(D_RNN, 0), row(D_SGU, 0), row(D, 0), row(D, 1), row(D, 0),
                  _layer_spec(layer, (D_RNN, D)), _layer_spec(layer, (D_SGU, D)),
                  _layer_spec(layer, (D, D)), _const_spec((1, D)), _const_spec((1, D))],
        out_specs=row(D, 0),
        out_shape=jax.ShapeDtypeStruct((T, D), F32),
        compiler_params=_params(("parallel",), 52),
        name="mix_out",
    )(a, b, sigmoid_cols, sigmoid_cols, x, w_o_rnn, w_o_sgu, w_out,
      ln_g.reshape(1, D), ln_b.reshape(1, D))


def _first_max(v, row, n):
    m = jnp.max(v, axis=0, keepdims=True)
    idx = jnp.min(jnp.where(v == m, row, float(n)), axis=0, keepdims=True)
    return m, idx


def _router_kernel(x_ref, wt_ref, b_ref, idx_ref, gate_ref, rank_ref, cnt_ref, run_ref):
    rows = x_ref.shape[0]

    @pl.when(pl.program_id(0) == 0)
    def _():
        run_ref[...] = jnp.zeros_like(run_ref)

    def nt_dot(w, x):
        return lax.dot_general(w, x, (((1,), (1,)), ((), ())), preferred_element_type=F32)

    x = x_ref[...]
    xh = x.astype(BF16)
    xl = (x - xh.astype(F32)).astype(BF16)
    w = wt_ref[...]
    wh = w.astype(BF16)
    wl = (w - wh.astype(F32)).astype(BF16)
    logits = nt_dot(wh, xh) + (nt_dot(wh, xl) + nt_dot(wl, xh))
    e = jnp.exp(logits - jnp.max(logits, axis=0, keepdims=True))
    scores = e / jnp.sum(e, axis=0, keepdims=True)
    sel = scores + b_ref[...]

    row = lax.broadcasted_iota(jnp.int32, (N_EXPERTS, rows), 0).astype(F32)
    grow = lax.broadcasted_iota(jnp.int32, (EXPERTS_PER_GROUP, rows), 0).astype(F32)
    neg_inf = float("-inf")
    best_score = None
    best_group = None
    for g in range(N_GROUPS):
        v = sel[g * EXPERTS_PER_GROUP:(g + 1) * EXPERTS_PER_GROUP, :]
        m1, i1 = _first_max(v, grow, EXPERTS_PER_GROUP)
        m2 = jnp.max(jnp.where(grow == i1, neg_inf, v), axis=0, keepdims=True)
        s = m1 + m2
        if g == 0:
            best_score, best_group = s, jnp.zeros_like(s)
        else:
            better = s > best_score
            best_group = jnp.where(better, float(g), best_group)
            best_score = jnp.where(better, s, best_score)

    lo = best_group * float(EXPERTS_PER_GROUP)
    in_group = jnp.logical_and(row >= lo, row < lo + float(EXPERTS_PER_GROUP))
    masked = jnp.where(in_group, sel, neg_inf)
    _, i1 = _first_max(masked, row, N_EXPERTS)
    pick1 = row == i1
    _, i2 = _first_max(jnp.where(pick1, neg_inf, masked), row, N_EXPERTS)
    pick2 = row == i2
    s1 = jnp.sum(jnp.where(pick1, scores, 0.0), axis=0, keepdims=True)
    s2 = jnp.sum(jnp.where(pick2, scores, 0.0), axis=0, keepdims=True)
    den = s1 + s2

    onehot = jnp.where(jnp.logical_or(pick1, pick2), 1.0, 0.0)
    before = (lax.broadcasted_iota(jnp.int32, (rows, rows), 0)
              < lax.broadcasted_iota(jnp.int32, (rows, rows), 1))
    prefix = jnp.dot(onehot.astype(BF16), jnp.where(before, 1.0, 0.0).astype(BF16),
                     preferred_element_type=F32)
    pos = prefix + run_ref[:, 0:1]
    r1 = jnp.sum(jnp.where(pick1, pos, 0.0), axis=0, keepdims=True)
    r2 = jnp.sum(jnp.where(pick2, pos, 0.0), axis=0, keepdims=True)
    run_ref[...] = run_ref[...] + jnp.sum(onehot, axis=1, keepdims=True)

    idx_ref[0:1, :] = i1.astype(jnp.int32)
    idx_ref[1:2, :] = i2.astype(jnp.int32)
    gate_ref[0:1, :] = s1 / den
    gate_ref[1:2, :] = s2 / den
    rank_ref[0:1, :] = r1.astype(jnp.int32)
    rank_ref[1:2, :] = r2.astype(jnp.int32)
    cnt_ref[...] = run_ref[...].astype(jnp.int32)


def _router(x, router_w, router_b):
    T, D = x.shape
    pair = pl.BlockSpec((TOP_K, ROUTER_ROWS), lambda i: (0, i))
    return pl.pallas_call(
        _router_kernel,
        grid=(T // ROUTER_ROWS,),
        in_specs=[pl.BlockSpec((ROUTER_ROWS, D), lambda i: (i, 0)),
                  _const_spec((N_EXPERTS, D)), _const_spec((N_EXPERTS, 1))],
        out_specs=[pair, pair, pair, pl.BlockSpec((N_EXPERTS, 128), lambda i: (0, 0))],
        out_shape=[jax.ShapeDtypeStruct((TOP_K, T), jnp.int32),
                   jax.ShapeDtypeStruct((TOP_K, T), F32),
                   jax.ShapeDtypeStruct((TOP_K, T), jnp.int32),
                   jax.ShapeDtypeStruct((N_EXPERTS, 128), jnp.int32)],
        scratch_shapes=[pltpu.VMEM((N_EXPERTS, 128), F32)],
        compiler_params=_params(("arbitrary",), 40),
        name="router",
    )(x, router_w.T, router_b.reshape(N_EXPERTS, 1))


def _dispatch_kernel(pad_from_ref, pad_n_ref, n_used_ref, dest_ref, x_ref, xs_hbm, zero_ref, sem, pad_sem):
    i = pl.program_id(0)
    rows = DISPATCH_ROWS
    n_blocks = xs_hbm.shape[0] // EXPERT_ROWS
    zero_rows = zero_ref.shape[0]

    def tail_copies(j):
        block = n_used_ref[0] + j
        off = pl.multiple_of(block * EXPERT_ROWS, EXPERT_ROWS)
        return block < n_blocks, [
            pltpu.make_async_copy(zero_ref, xs_hbm.at[pl.ds(off + part * zero_rows, zero_rows)], pad_sem)
            for part in range(EXPERT_ROWS // zero_rows)]

    def pad_copy(e, bit):
        n = pad_n_ref[e]
        if bit < ROW_TILE:
            off = pad_from_ref[e] + bit - 1
            return pltpu.make_async_copy(zero_ref.at[pl.ds(0, 1)], xs_hbm.at[pl.ds(off, 1)], pad_sem)
        done = (n & (ROW_TILE - 1)) + (n & ~(2 * bit - 1))
        off = pl.multiple_of(pad_from_ref[e] + done, ROW_TILE)
        return pltpu.make_async_copy(zero_ref.at[pl.ds(0, bit)], xs_hbm.at[pl.ds(off, bit)], pad_sem)

    def pad_needed(e, bit):
        n = pad_n_ref[e]
        if bit < ROW_TILE:
            return bit <= (n & (ROW_TILE - 1))
        return (n & bit) != 0

    bits = list(range(1, ROW_TILE)) + [
        1 << k for k in range(ROW_TILE.bit_length() - 1, EXPERT_ROWS.bit_length() - 1)]

    @pl.when(i == 0)
    def _():
        zero_ref[...] = jnp.zeros_like(zero_ref)
        for e in range(N_EXPERTS):
            for bit in bits:
                @pl.when(pad_needed(e, bit))
                def _():
                    pad_copy(e, bit).start()
        for j in range(N_EXPERTS):
            needed, copies = tail_copies(j)

            @pl.when(needed)
            def _():
                for c in copies:
                    c.start()

    def issue(t, carry):
        src = x_ref.at[pl.ds(t, 1)]
        pltpu.make_async_copy(src, xs_hbm.at[pl.ds(dest_ref[0, t], 1)], sem).start()
        pltpu.make_async_copy(src, xs_hbm.at[pl.ds(dest_ref[0, rows + t], 1)], sem).start()
        return carry

    lax.fori_loop(0, rows, issue, 0, unroll=8)
    for _ in range(TOP_K):
        pltpu.make_async_copy(x_ref, xs_hbm.at[pl.ds(0, rows)], sem).wait()

    @pl.when(i == 0)
    def _():
        for e in range(N_EXPERTS):
            for bit in bits:
                @pl.when(pad_needed(e, bit))
                def _():
                    pad_copy(e, bit).wait()
        for j in range(N_EXPERTS):
            needed, copies = tail_copies(j)

            @pl.when(needed)
            def _():
                for c in copies:
                    c.wait()


def _dispatch(x, dest_tiles, pad_from, pad_n, n_used, n_slots):
    T, D = x.shape
    grid_spec = pltpu.PrefetchScalarGridSpec(
        num_scalar_prefetch=3,
        grid=(T // DISPATCH_ROWS,),
        in_specs=[pl.BlockSpec((None, 1, TOP_K * DISPATCH_ROWS), lambda i, *_: (i, 0, 0),
                               memory_space=pltpu.SMEM),
                  pl.BlockSpec((DISPATCH_ROWS, D), lambda i, *_: (i, 0))],
        out_specs=pl.BlockSpec(memory_space=pl.ANY),
        scratch_shapes=[pltpu.VMEM((EXPERT_ROWS // 2, D), F32),
                        pltpu.SemaphoreType.DMA(()), pltpu.SemaphoreType.DMA(())],
    )
    return pl.pallas_call(
        _dispatch_kernel,
        grid_spec=grid_spec,
        out_shape=jax.ShapeDtypeStruct((n_slots, D), F32),
        compiler_params=_params(("arbitrary",), 16),
        name="dispatch",
    )(pad_from, pad_n, n_used, dest_tiles, x)


def _expert_kernel(block_e_ref, n_used_ref, xs_ref, w1_ref, w3_ref, w2_ref, o_ref):
    used = pl.program_id(0) < n_used_ref[0]

    @pl.when(jnp.logical_not(used))
    def _():
        o_ref[...] = jnp.zeros_like(o_ref)

    @pl.when(used)
    def _():
        x = xs_ref[...].astype(BF16)
        h1 = jnp.dot(x, w1_ref[...], preferred_element_type=F32)
        h3 = jnp.dot(x, w3_ref[...], preferred_element_type=F32)
        hidden = (h1 * _sigmoid(h1)) * h3
        o_ref[...] = jnp.dot(hidden.astype(BF16), w2_ref[...], preferred_element_type=F32)


def _experts(layer, xs, block_e, n_used, w1, w3, w2):
    P, D = xs.shape
    n_blocks = P // EXPERT_ROWS
    used_rows = pl.BlockSpec((EXPERT_ROWS, D),
                             lambda i, be, nu: (jnp.maximum(jnp.minimum(i, nu[0] - 1), 0), 0))
    rows = pl.BlockSpec((EXPERT_ROWS, D), lambda i, be, nu: (i, 0))
    expert_slab = lambda i, be, nu: (layer, be[i], 0, 0)
    grid_spec = pltpu.PrefetchScalarGridSpec(
        num_scalar_prefetch=2,
        grid=(n_blocks,),
        in_specs=[used_rows,
                  pl.BlockSpec((None, None, D, D_EXPERT), expert_slab),
                  pl.BlockSpec((None, None, D, D_EXPERT), expert_slab),
                  pl.BlockSpec((None, None, D_EXPERT, D), expert_slab)],
        out_specs=rows,
    )
    return pl.pallas_call(
        _expert_kernel,
        grid_spec=grid_spec,
        out_shape=jax.ShapeDtypeStruct((P, D), F32),
        compiler_params=_params(("arbitrary",), 56),
        name="experts",
    )(block_e, n_used, xs, w1, w3, w2)


def _combine_kernel(alpha, dest_ref, x_ref, gate_ref, g_ref, b_ref, y_hbm, of_ref, ob_ref, buf_ref, sem):
    rows = COMBINE_ROWS

    def issue(t, carry):
        for k in range(TOP_K):
            pltpu.make_async_copy(y_hbm.at[pl.ds(dest_ref[0, k * rows + t], 1)],
                                  buf_ref.at[k, pl.ds(t, 1)], sem).start()
        return carry

    lax.fori_loop(0, rows, issue, 0, unroll=8)
    for k in range(TOP_K):
        pltpu.make_async_copy(y_hbm.at[pl.ds(0, rows)], buf_ref.at[k], sem).wait()

    f = gate_ref[:, 0:1] * buf_ref[0] + gate_ref[:, 1:2] * buf_ref[1]
    y = _layer_norm(alpha * x_ref[...] + f, g_ref[...], b_ref[...])
    of_ref[...] = y
    ob_ref[...] = y.astype(BF16)


def _combine(alpha, dest_tiles, x, gate_t, ln_g, ln_b, yb):
    T, D = x.shape
    row = pl.BlockSpec((COMBINE_ROWS, D), lambda i: (i, 0))
    return pl.pallas_call(
        functools.partial(_combine_kernel, alpha),
        grid=(T // COMBINE_ROWS,),
        in_specs=[pl.BlockSpec((None, 1, TOP_K * COMBINE_ROWS), lambda i: (i, 0, 0),
                               memory_space=pltpu.SMEM),
                  row, pl.BlockSpec((COMBINE_ROWS, TOP_K), lambda i: (i, 0)),
                  _const_spec((1, D)), _const_spec((1, D)),
                  pl.BlockSpec(memory_space=pl.ANY)],
        out_specs=[row, row],
        out_shape=[jax.ShapeDtypeStruct((T, D), F32), jax.ShapeDtypeStruct((T, D), BF16)],
        scratch_shapes=[pltpu.VMEM((TOP_K, COMBINE_ROWS, D), F32), pltpu.SemaphoreType.DMA(())],
        compiler_params=_params(("arbitrary",), 48),
        name="combine",
    )(dest_tiles, x, gate_t, ln_g.reshape(1, D), ln_b.reshape(1, D), yb)


def _tile_pairs(dest, rows):
    T = dest.shape[1]
    return dest.reshape(TOP_K, T // rows, rows).transpose(1, 0, 2).reshape(T // rows, 1, TOP_K * rows)


def _moe(alpha, layer, x, router_w, router_b, w1, w3, w2, ln_g, ln_b):
    T, D = x.shape
    n_blocks = (T * TOP_K) // EXPERT_ROWS + N_EXPERTS
    idx, gate, rank, cnt = _router(x, router_w, router_b)

    counts = cnt[:, 0]
    padded = (counts + EXPERT_ROWS - 1) // EXPERT_ROWS * EXPERT_ROWS
    pad_end = jnp.cumsum(padded)
    pad_start = pad_end - padded
    n_used = (pad_end[-1] // EXPERT_ROWS).astype(jnp.int32)
    block_start = jnp.minimum(jnp.arange(n_blocks, dtype=jnp.int32), n_used - 1) * EXPERT_ROWS
    block_e = jnp.minimum(jnp.sum(block_start[:, None] >= pad_end[None, :], axis=1),
                          N_EXPERTS - 1).astype(jnp.int32)
    expert_ids = jnp.arange(N_EXPERTS, dtype=jnp.int32)[:, None, None]
    dest = jnp.sum(jnp.where(idx[None] == expert_ids, pad_start[:, None, None], 0), axis=0) + rank

    n_used = n_used.reshape(1)
    xs = _dispatch(x, _tile_pairs(dest, DISPATCH_ROWS), (pad_start + counts).astype(jnp.int32),
                   (padded - counts).astype(jnp.int32), n_used, n_blocks * EXPERT_ROWS)
    yb = _experts(layer, xs, block_e, n_used, w1, w3, w2)
    return _combine(alpha, _tile_pairs(dest, COMBINE_ROWS), x, gate.T, ln_g, ln_b, yb)


def kernel(x, emb_ln_g, emb_ln_b, w_in, conv_w, conv_b, lru_w_a, lru_b_a, lru_w_x, lru_b_x, lru_lambda, w_o_rnn, sgu_ln_g, sgu_ln_b, sgu_w_s, sgu_b_s, w_o_sgu, w_out, ln1_g, ln1_b, router_w, router_b, expert_w1, expert_w3, expert_w2, ln2_g, ln2_b):
    batch, seq, D = x.shape
    depth = w_in.shape[0]
    alpha = float((2 * depth) ** 0.25)
    xf, xb = _emb_ln(x.reshape(batch * seq, D), emb_ln_g, emb_ln_b)
    w_in, w_o_rnn, w_o_sgu, w_out, expert_w1, expert_w3, expert_w2 = (
        w.astype(BF16) for w in (w_in, w_o_rnn, w_o_sgu, w_out, expert_w1, expert_w3, expert_w2))
    gelu_from = D_RNN
    sigmoid_from = 2 * D_RNN + 2 * D_SGU
    for l in range(depth):
        x_rnn = _proj(xb, w_in, l, 0, gelu_from, lambda v: v, "proj_x")
        gelu_cols = _proj(xb, w_in, l, gelu_from, sigmoid_from - gelu_from, _gelu, "proj_gelu")
        sigmoid_cols = _proj(xb, w_in, l, sigmoid_from, D_IN - sigmoid_from, _sigmoid, "proj_sigmoid")
        a = _lru(x_rnn, gelu_cols, batch, seq, conv_w[l], conv_b[l], lru_w_a[l], lru_b_a[l],
                 lru_w_x[l], lru_b_x[l], lru_lambda[l])
        b = _sgu(gelu_cols, sgu_ln_g[l], sgu_ln_b[l], sgu_w_s[l], sgu_b_s[l])
        x1 = _mix_out(alpha, l, a, b, sigmoid_cols, xf, w_o_rnn, w_o_sgu, w_out, ln1_g[l], ln1_b[l])
        xf, xb = _moe(alpha, l, x1, router_w, router_b, expert_w1, expert_w3, expert_w2,
                      ln2_g[l], ln2_b[l])
    return xf.reshape(batch, seq, D)
```

```python
import functools

import jax
import jax.numpy as jnp
import numpy as np
from jax import lax
from jax.experimental import pallas as pl
from jax.experimental.pallas import tpu as pltpu

F32 = jnp.float32
BF16 = jnp.bfloat16

D_MODEL = 2048
D_RNN = 1024
RNN_HEADS = 8
HEAD_DIM = D_RNN // RNN_HEADS
CONV_WIDTH = 4
LRU_C = 8.0
D_SGU = 1024
SGU_GROUPS = 8
SGU_GROUP_DIM = D_SGU // SGU_GROUPS
CHUNK = 128
N_EXPERTS = 16
N_GROUPS = 4
EXPERTS_PER_GROUP = N_EXPERTS // N_GROUPS
TOP_K = 2
D_EXPERT = 1408
D_IN = 2 * D_RNN + 2 * D_SGU + 2 * D_MODEL
LN_EPS = 1e-5
SQRT_2_OVER_PI = float(np.sqrt(2.0 / np.pi))

LN_ROWS = 512
PROJ_ROWS = 1024
PROJ_COLS = 1024
LRU_ROWS = 512
SGU_ROWS = 512
MIX_ROWS = 256
ROUTER_ROWS = 512
DISPATCH_ROWS = 512
EXPERT_ROWS = 256
COMBINE_ROWS = 512
CONV_PAD = 8
ROW_TILE = 8

MIB = 1024 * 1024


def _params(semantics, vmem_mib):
    return pltpu.CompilerParams(dimension_semantics=semantics,
                                vmem_limit_bytes=vmem_mib * MIB,
                                disable_bounds_checks=True)


def _const_spec(shape):
    zeros = (0,) * len(shape)
    return pl.BlockSpec(shape, lambda *_: zeros, pipeline_mode=pl.Buffered(1))


def _layer_spec(layer, shape):
    zeros = (0,) * len(shape)
    return pl.BlockSpec((None,) + tuple(shape), lambda *_: (layer,) + zeros,
                        pipeline_mode=pl.Buffered(1))


def _layer_norm(v, g, b):
    mu = jnp.mean(v, axis=-1, keepdims=True)
    d = v - mu
    var = jnp.mean(d * d, axis=-1, keepdims=True)
    return d * lax.rsqrt(var + LN_EPS) * g + b


def _gelu(x):
    inner = x * (SQRT_2_OVER_PI + (SQRT_2_OVER_PI * 0.044715) * (x * x))
    return x * (0.5 * jnp.tanh(inner) + 0.5)


def _sigmoid(x):
    return 0.5 * jnp.tanh(0.5 * x) + 0.5


def _emb_ln_kernel(x_ref, g_ref, b_ref, of_ref, ob_ref):
    y = _layer_norm(x_ref[...], g_ref[...], b_ref[...])
    of_ref[...] = y
    ob_ref[...] = y.astype(BF16)


def _emb_ln(x, g, b):
    T, D = x.shape
    row = pl.BlockSpec((LN_ROWS, D), lambda i: (i, 0))
    return pl.pallas_call(
        _emb_ln_kernel,
        grid=(T // LN_ROWS,),
        in_specs=[row, _const_spec((1, D)), _const_spec((1, D))],
        out_specs=[row, row],
        out_shape=[jax.ShapeDtypeStruct((T, D), F32), jax.ShapeDtypeStruct((T, D), BF16)],
        compiler_params=_params(("parallel",), 40),
        name="emb_ln",
    )(x, g.reshape(1, D), b.reshape(1, D))


def _cast_specs(stack, layer, n_steps, step_of):
    _, rows, cols = stack.shape
    slab = -(-rows // (n_steps * 16)) * 16
    src = pl.BlockSpec((None, slab, cols), lambda *g: (layer, step_of(*g), 0))
    dst = pl.BlockSpec((slab, cols), lambda *g: (step_of(*g), 0))
    return src, dst, jax.ShapeDtypeStruct((rows, cols), BF16)


def _proj_kernel(act, x_ref, w_ref, *rest):
    o_ref = rest[len(rest) // 2]
    acc = jnp.dot(x_ref[...], w_ref[...], preferred_element_type=F32)
    o_ref[...] = act(acc).astype(BF16)
    for src_ref, dst_ref in zip(rest[:len(rest) // 2], rest[len(rest) // 2 + 1:]):
        dst_ref[...] = src_ref[...].astype(BF16)


def _proj(xb, w_in, layer, col_from, n_cols, act, name, cast=None):
    T, D = xb.shape
    first = col_from // PROJ_COLS
    grid = (T // PROJ_ROWS, n_cols // PROJ_COLS)
    in_specs = [pl.BlockSpec((PROJ_ROWS, D), lambda i, j: (i, 0)),
                pl.BlockSpec((None, D, PROJ_COLS), lambda i, j: (layer, 0, first + j))]
    out_specs = [pl.BlockSpec((PROJ_ROWS, PROJ_COLS), lambda i, j: (i, j))]
    out_shape = [jax.ShapeDtypeStruct((T, n_cols), BF16)]
    operands = [xb, w_in]
    if cast is not None:
        src, dst, shape = _cast_specs(cast, layer, grid[0] * grid[1], lambda i, j: i * grid[1] + j)
        in_specs.append(src)
        out_specs.append(dst)
        out_shape.append(shape)
        operands.append(cast)
    out = pl.pallas_call(
        functools.partial(_proj_kernel, act),
        grid=grid,
        in_specs=in_specs,
        out_specs=out_specs,
        out_shape=out_shape,
        compiler_params=_params(("parallel", "arbitrary"), 48),
        name=name,
    )(*operands)
    return out[0] if cast is None else out


def _lru_kernel(xr_ref, gate_ref, cw_ref, cb_ref, wa_ref, ba_ref, wx_ref, bx_ref, lam_ref,
                o_ref, xpad_ref, hc_ref, a_ref, u_ref, h_ref):
    rows = xr_ref.shape[0]

    @pl.when(pl.program_id(1) == 0)
    def _():
        xpad_ref[0:CONV_PAD, :] = jnp.zeros((CONV_PAD, D_RNN), F32)
        hc_ref[...] = jnp.zeros_like(hc_ref)

    x = xr_ref[...].astype(F32)
    xpad_ref[CONV_PAD:CONV_PAD + rows, :] = x
    xc = cb_ref[...] + cw_ref[CONV_WIDTH - 1:CONV_WIDTH, :] * x
    for k in range(CONV_WIDTH - 1):
        shift = CONV_WIDTH - 1 - k
        xc = xc + cw_ref[k:k + 1, :] * xpad_ref[CONV_PAD - shift:CONV_PAD - shift + rows, :]
    xpad_ref[0:CONV_PAD, :] = x[rows - CONV_PAD:rows, :]

    lam = lam_ref[...]
    sp = jnp.maximum(-lam, 0.0) + jnp.log1p(jnp.exp(-jnp.abs(lam)))
    for h in range(RNN_HEADS):
        cols = slice(h * HEAD_DIM, (h + 1) * HEAD_DIM)
        xh = xc[:, cols]
        xhb = xh.astype(BF16)
        r = _sigmoid(jnp.dot(xhb, wa_ref[h], preferred_element_type=F32) + ba_ref[:, cols])
        gi = _sigmoid(jnp.dot(xhb, wx_ref[h], preferred_element_type=F32) + bx_ref[:, cols])
        log_a = (-LRU_C) * r * sp[:, cols]
        a_ref[:, cols] = jnp.exp(log_a)
        th = jnp.tanh(log_a)
        u_ref[:, cols] = jnp.sqrt(-2.0 * th / (1.0 - th)) * (gi * xh)

    def step(t, h):
        h = a_ref[pl.ds(t, 1), :] * h + u_ref[pl.ds(t, 1), :]
        h_ref[pl.ds(t, 1), :] = h
        return h

    hc_ref[0:1, :] = lax.fori_loop(0, rows, step, hc_ref[0:1, :], unroll=8)
    o_ref[...] = (h_ref[...] * gate_ref[...].astype(F32)).astype(BF16)


def _lru(x_rnn, gelu_cols, batch, seq, conv_w, conv_b, w_a, b_a, w_x, b_x, lam):
    T = x_rnn.shape[0]
    tiles = seq // LRU_ROWS
    row = pl.BlockSpec((LRU_ROWS, D_RNN), lambda b, s: (b * tiles + s, 0))
    vec = _const_spec((1, D_RNN))
    gate_w = _const_spec((RNN_HEADS, HEAD_DIM, HEAD_DIM))
    return pl.pallas_call(
        _lru_kernel,
        grid=(batch, tiles),
        in_specs=[row, row, _const_spec((CONV_WIDTH, D_RNN)), vec, gate_w, vec, gate_w, vec, vec],
        out_specs=pl.BlockSpec((LRU_ROWS, D_RNN), lambda b, s: (b * tiles + s, 0)),
        out_shape=jax.ShapeDtypeStruct((T, D_RNN), BF16),
        scratch_shapes=[pltpu.VMEM((CONV_PAD + LRU_ROWS, D_RNN), F32),
                        pltpu.VMEM((8, D_RNN), F32),
                        pltpu.VMEM((LRU_ROWS, D_RNN), F32),
                        pltpu.VMEM((LRU_ROWS, D_RNN), F32),
                        pltpu.VMEM((LRU_ROWS, D_RNN), F32)],
        compiler_params=_params(("arbitrary", "arbitrary"), 40),
        name="lru",
    )(x_rnn, gelu_cols, conv_w, conv_b.reshape(1, D_RNN), w_a.astype(BF16), b_a.reshape(1, D_RNN),
      w_x.astype(BF16), b_x.reshape(1, D_RNN), lam.reshape(1, D_RNN))


def _sgu_kernel(u_ref, v_ref, g_ref, b_ref, ws_ref, bs_ref, o_ref):
    rows = u_ref.shape[0]
    v = _layer_norm(v_ref[...].astype(F32), g_ref[...], b_ref[...]).astype(BF16)
    t_out = lax.broadcasted_iota(jnp.int32, (CHUNK, CHUNK), 0)
    t_in = lax.broadcasted_iota(jnp.int32, (CHUNK, CHUNK), 1)
    causal = t_in <= t_out
    for g in range(SGU_GROUPS):
        cols = slice(g * SGU_GROUP_DIM, (g + 1) * SGU_GROUP_DIM)
        ws = jnp.where(causal, ws_ref[g], 0.0).astype(BF16)
        bias = bs_ref[:, g:g + 1]
        for c in range(rows // CHUNK):
            rws = slice(c * CHUNK, (c + 1) * CHUNK)
            mixed = jnp.dot(ws, v[rws, cols], preferred_element_type=F32) + bias
            o_ref[rws, cols] = (u_ref[rws, cols].astype(F32) * mixed).astype(BF16)


def _sgu(gelu_cols, ln_g, ln_b, w_s, b_s):
    T = gelu_cols.shape[0]
    row = lambda c: pl.BlockSpec((SGU_ROWS, D_SGU), lambda i: (i, c))
    return pl.pallas_call(
        _sgu_kernel,
        grid=(T // SGU_ROWS,),
        in_specs=[row(1), row(2), _const_spec((1, D_SGU)), _const_spec((1, D_SGU)),
                  _const_spec((SGU_GROUPS, CHUNK, CHUNK)), _const_spec((CHUNK, SGU_GROUPS))],
        out_specs=pl.BlockSpec((SGU_ROWS, D_SGU), lambda i: (i, 0)),
        out_shape=jax.ShapeDtypeStruct((T, D_SGU), BF16),
        compiler_params=_params(("parallel",), 40),
        name="sgu",
    )(gelu_cols, gelu_cols, ln_g.reshape(1, D_SGU), ln_b.reshape(1, D_SGU), w_s, b_s.T)


def _mix_out_kernel(alpha, a_ref, b_ref, sga_ref, sgb_ref, x_ref, woa_ref, wob_ref, wout_ref,
                    g_ref, beta_ref, cast_src_ref, o_ref, cast_dst_ref):
    ya = jnp.dot(a_ref[...], woa_ref[...], preferred_element_type=F32)
    yb = jnp.dot(b_ref[...], wob_ref[...], preferred_element_type=F32)
    merged = sga_ref[...].astype(F32) * ya + sgb_ref[...].astype(F32) * yb
    m = jnp.dot(merged.astype(BF16), wout_ref[...], preferred_element_type=F32)
    o_ref[...] = _layer_norm(alpha * x_ref[...] + m, g_ref[...], beta_ref[...])
    cast_dst_ref[...] = cast_src_ref[...].astype(BF16)


def _mix_out(alpha, layer, a, b, sigmoid_cols, x, w_o_rnn, w_o_sgu, w_out, ln_g, ln_b, cast):
    T, D = x.shape
    n_steps = T // MIX_ROWS
    row = lambda w, c: pl.BlockSpec((MIX_ROWS, w), lambda i: (i, c))
    cast_src, cast_dst, cast_shape = _cast_specs(cast, layer, n_steps, lambda i: i)
    return pl.pallas_call(
        functools.partial(_mix_out_kernel, alpha),
        grid=(n_steps,),
        in_specs=[row(D_RNN, 0), row(D_SGU, 0), row(D, 0), row(D, 1), row(D, 0),
                  _layer_spec(layer, (D_RNN, D)), _layer_spec(layer, (D_SGU, D)),
                  _layer_spec(layer, (D, D)), _const_spec((1, D)), _const_spec((1, D)), cast_src],
        out_specs=[row(D, 0), cast_dst],
        out_shape=[jax.ShapeDtypeStruct((T, D), F32), cast_shape],
        compiler_params=_params(("parallel",), 56),
        name="mix_out",
    )(a, b, sigmoid_cols, sigmoid_cols, x, w_o_rnn, w_o_sgu, w_out,
      ln_g.reshape(1, D), ln_b.reshape(1, D), cast)


def _first_max(v, row, n):
    m = jnp.max(v, axis=0, keepdims=True)
    idx = jnp.min(jnp.where(v == m, row, float(n)), axis=0, keepdims=True)
    return m, idx


def _router_kernel(x_ref, wt_ref, b_ref, idx_ref, gate_ref, rank_ref, cnt_ref, run_ref):
    rows = x_ref.shape[0]

    @pl.when(pl.program_id(0) == 0)
    def _():
        run_ref[...] = jnp.zeros_like(run_ref)

    def nt_dot(w, x):
        return lax.dot_general(w, x, (((1,), (1,)), ((), ())), preferred_element_type=F32)

    x = x_ref[...]
    xh = x.astype(BF16)
    xl = (x - xh.astype(F32)).astype(BF16)
    w = wt_ref[...]
    wh = w.astype(BF16)
    wl = (w - wh.astype(F32)).astype(BF16)
    logits = nt_dot(wh, xh) + (nt_dot(wh, xl) + nt_dot(wl, xh))
    e = jnp.exp(logits - jnp.max(logits, axis=0, keepdims=True))
    scores = e / jnp.sum(e, axis=0, keepdims=True)
    sel = scores + b_ref[...]

    row = lax.broadcasted_iota(jnp.int32, (N_EXPERTS, rows), 0).astype(F32)
    grow = lax.broadcasted_iota(jnp.int32, (EXPERTS_PER_GROUP, rows), 0).astype(F32)
    neg_inf = float("-inf")
    best_score = None
    best_group = None
    for g in range(N_GROUPS):
        v = sel[g * EXPERTS_PER_GROUP:(g + 1) * EXPERTS_PER_GROUP, :]
        m1, i1 = _first_max(v, grow, EXPERTS_PER_GROUP)
        m2 = jnp.max(jnp.where(grow == i1, neg_inf, v), axis=0, keepdims=True)
        s = m1 + m2
        if g == 0:
            best_score, best_group = s, jnp.zeros_like(s)
        else:
            better = s > best_score
            best_group = jnp.where(better, float(g), best_group)
            best_score = jnp.where(better, s, best_score)

    lo = best_group * float(EXPERTS_PER_GROUP)
    in_group = jnp.logical_and(row >= lo, row < lo + float(EXPERTS_PER_GROUP))
    masked = jnp.where(in_group, sel, neg_inf)
    _, i1 = _first_max(masked, row, N_EXPERTS)
    pick1 = row == i1
    _, i2 = _first_max(jnp.where(pick1, neg_inf, masked), row, N_EXPERTS)
    pick2 = row == i2
    s1 = jnp.sum(jnp.where(pick1, scores, 0.0), axis=0, keepdims=True)
    s2 = jnp.sum(jnp.where(pick2, scores, 0.0), axis=0, keepdims=True)
    den = s1 + s2

    onehot = jnp.where(jnp.logical_or(pick1, pick2), 1.0, 0.0)
    before = (lax.broadcasted_iota(jnp.int32, (rows, rows), 0)
              < lax.broadcasted_iota(jnp.int32, (rows, rows), 1))
    prefix = jnp.dot(onehot.astype(BF16), jnp.where(before, 1.0, 0.0).astype(BF16),
                     preferred_element_type=F32)
    pos = prefix + run_ref[:, 0:1]
    r1 = jnp.sum(jnp.where(pick1, pos, 0.0), axis=0, keepdims=True)
    r2 = jnp.sum(jnp.where(pick2, pos, 0.0), axis=0, keepdims=True)
    run_ref[...] = run_ref[...] + jnp.sum(onehot, axis=1, keepdims=True)

    idx_ref[0:1, :] = i1.astype(jnp.int32)
    idx_ref[1:2, :] = i2.astype(jnp.int32)
    gate_ref[0:1, :] = s1 / den
    gate_ref[1:2, :] = s2 / den
    rank_ref[0:1, :] = r1.astype(jnp.int32)
    rank_ref[1:2, :] = r2.astype(jnp.int32)
    cnt_ref[...] = run_ref[...].astype(jnp.int32)


def _router(x, router_w, router_b):
    T, D = x.shape
    pair = pl.BlockSpec((TOP_K, ROUTER_ROWS), lambda i: (0, i))
    return pl.pallas_call(
        _router_kernel,
        grid=(T // ROUTER_ROWS,),
        in_specs=[pl.BlockSpec((ROUTER_ROWS, D), lambda i: (i, 0)),
                  _const_spec((N_EXPERTS, D)), _const_spec((N_EXPERTS, 1))],
        out_specs=[pair, pair, pair, pl.BlockSpec((N_EXPERTS, 128), lambda i: (0, 0))],
        out_shape=[jax.ShapeDtypeStruct((TOP_K, T), jnp.int32),
                   jax.ShapeDtypeStruct((TOP_K, T), F32),
                   jax.ShapeDtypeStruct((TOP_K, T), jnp.int32),
                   jax.ShapeDtypeStruct((N_EXPERTS, 128), jnp.int32)],
        scratch_shapes=[pltpu.VMEM((N_EXPERTS, 128), F32)],
        compiler_params=_params(("arbitrary",), 40),
        name="router",
    )(x, router_w.T, router_b.reshape(N_EXPERTS, 1))


def _dispatch_kernel(pad_from_ref, pad_n_ref, n_used_ref, dest_ref, x_ref, xs_hbm, zero_ref, sem, pad_sem):
    i = pl.program_id(0)
    rows = DISPATCH_ROWS
    n_blocks = xs_hbm.shape[0] // EXPERT_ROWS
    zero_rows = zero_ref.shape[0]

    def tail_copies(j):
        block = n_used_ref[0] + j
        off = pl.multiple_of(block * EXPERT_ROWS, EXPERT_ROWS)
        return block < n_blocks, [
            pltpu.make_async_copy(zero_ref, xs_hbm.at[pl.ds(off + part * zero_rows, zero_rows)], pad_sem)
            for part in range(EXPERT_ROWS // zero_rows)]

    def pad_copy(e, bit):
        n = pad_n_ref[e]
        if bit < ROW_TILE:
            off = pad_from_ref[e] + bit - 1
            return pltpu.make_async_copy(zero_ref.at[pl.ds(0, 1)], xs_hbm.at[pl.ds(off, 1)], pad_sem)
        done = (n & (ROW_TILE - 1)) + (n & ~(2 * bit - 1))
        off = pl.multiple_of(pad_from_ref[e] + done, ROW_TILE)
        return pltpu.make_async_copy(zero_ref.at[pl.ds(0, bit)], xs_hbm.at[pl.ds(off, bit)], pad_sem)

    def pad_needed(e, bit):
        n = pad_n_ref[e]
        if bit < ROW_TILE:
            return bit <= (n & (ROW_TILE - 1))
        return (n & bit) != 0

    bits = list(range(1, ROW_TILE)) + [
        1 << k for k in range(ROW_TILE.bit_length() - 1, EXPERT_ROWS.bit_length() - 1)]

    @pl.when(i == 0)
    def _():
        zero_ref[...] = jnp.zeros_like(zero_ref)
        for e in range(N_EXPERTS):
            for bit in bits:
                @pl.when(pad_needed(e, bit))
                def _():
                    pad_copy(e, bit).start()
        for j in range(N_EXPERTS):
            needed, copies = tail_copies(j)

            @pl.when(needed)
            def _():
                for c in copies:
                    c.start()

    def issue(t, carry):
        src = x_ref.at[pl.ds(t, 1)]
        pltpu.make_async_copy(src, xs_hbm.at[pl.ds(dest_ref[0, t], 1)], sem).start()
        pltpu.make_async_copy(src, xs_hbm.at[pl.ds(dest_ref[0, rows + t], 1)], sem).start()
        return carry

    lax.fori_loop(0, rows, issue, 0, unroll=8)
    for _ in range(TOP_K):
        pltpu.make_async_copy(x_ref, xs_hbm.at[pl.ds(0, rows)], sem).wait()

    @pl.when(i == 0)
    def _():
        for e in range(N_EXPERTS):
            for bit in bits:
                @pl.when(pad_needed(e, bit))
                def _():
                    pad_copy(e, bit).wait()
        for j in range(N_EXPERTS):
            needed, copies = tail_copies(j)

            @pl.when(needed)
            def _():
                for c in copies:
                    c.wait()


def _dispatch(x, dest_tiles, pad_from, pad_n, n_used, n_slots):
    T, D = x.shape
    grid_spec = pltpu.PrefetchScalarGridSpec(
        num_scalar_prefetch=3,
        grid=(T // DISPATCH_ROWS,),
        in_specs=[pl.BlockSpec((None, 1, TOP_K * DISPATCH_ROWS), lambda i, *_: (i, 0, 0),
                               memory_space=pltpu.SMEM),
                  pl.BlockSpec((DISPATCH_ROWS, D), lambda i, *_: (i, 0))],
        out_specs=pl.BlockSpec(memory_space=pl.ANY),
        scratch_shapes=[pltpu.VMEM((EXPERT_ROWS // 2, D), F32),
                        pltpu.SemaphoreType.DMA(()), pltpu.SemaphoreType.DMA(())],
    )
    return pl.pallas_call(
        _dispatch_kernel,
        grid_spec=grid_spec,
        out_shape=jax.ShapeDtypeStruct((n_slots, D), F32),
        compiler_params=_params(("arbitrary",), 16),
        name="dispatch",
    )(pad_from, pad_n, n_used, dest_tiles, x)


def _expert_kernel(block_e_ref, n_used_ref, xs_ref, w1_ref, w3_ref, w2_ref, o_ref):
    used = pl.program_id(0) < n_used_ref[0]

    @pl.when(jnp.logical_not(used))
    def _():
        o_ref[...] = jnp.zeros_like(o_ref)

    @pl.when(used)
    def _():
        x = xs_ref[...].astype(BF16)
        h1 = jnp.dot(x, w1_ref[...], preferred_element_type=F32)
        h3 = jnp.dot(x, w3_ref[...], preferred_element_type=F32)
        hidden = (h1 * _sigmoid(h1)) * h3
        o_ref[...] = jnp.dot(hidden.astype(BF16), w2_ref[...], preferred_element_type=F32)


def _experts(xs, block_e, n_used, w1, w3, w2):
    P, D = xs.shape
    n_blocks = P // EXPERT_ROWS
    used_rows = pl.BlockSpec((EXPERT_ROWS, D),
                             lambda i, be, nu: (jnp.maximum(jnp.minimum(i, nu[0] - 1), 0), 0))
    rows = pl.BlockSpec((EXPERT_ROWS, D), lambda i, be, nu: (i, 0))
    expert_slab = lambda i, be, nu: (be[i], 0, 0)
    grid_spec = pltpu.PrefetchScalarGridSpec(
        num_scalar_prefetch=2,
        grid=(n_blocks,),
        in_specs=[used_rows,
                  pl.BlockSpec((None, D, D_EXPERT), expert_slab),
                  pl.BlockSpec((None, D, D_EXPERT), expert_slab),
                  pl.BlockSpec((None, D_EXPERT, D), expert_slab)],
        out_specs=rows,
    )
    return pl.pallas_call(
        _expert_kernel,
        grid_spec=grid_spec,
        out_shape=jax.ShapeDtypeStruct((P, D), F32),
        compiler_params=_params(("arbitrary",), 56),
        name="experts",
    )(block_e, n_used, xs, w1, w3, w2)


def _combine_kernel(alpha, dest_ref, dest_next_ref, x_ref, gate_ref, g_ref, b_ref, y_hbm,
                    of_ref, ob_ref, buf_ref, sem):
    rows = COMBINE_ROWS
    i = pl.program_id(0)
    slot = i % 2

    def gather(table_ref, into):
        def issue(t, carry):
            for k in range(TOP_K):
                pltpu.make_async_copy(y_hbm.at[pl.ds(table_ref[0, k * rows + t], 1)],
                                      buf_ref.at[into, k, pl.ds(t, 1)], sem.at[into]).start()
            return carry

        lax.fori_loop(0, rows, issue, 0, unroll=8)

    @pl.when(i == 0)
    def _():
        gather(dest_ref, slot)

    @pl.when(i + 1 < pl.num_programs(0))
    def _():
        gather(dest_next_ref, 1 - slot)

    for k in range(TOP_K):
        pltpu.make_async_copy(y_hbm.at[pl.ds(0, rows)], buf_ref.at[slot, k], sem.at[slot]).wait()

    f = gate_ref[:, 0:1] * buf_ref[slot, 0] + gate_ref[:, 1:2] * buf_ref[slot, 1]
    y = _layer_norm(alpha * x_ref[...] + f, g_ref[...], b_ref[...])
    of_ref[...] = y
    ob_ref[...] = y.astype(BF16)


def _combine(alpha, dest_tiles, x, gate_t, ln_g, ln_b, yb):
    T, D = x.shape
    n_tiles = T // COMBINE_ROWS
    row = pl.BlockSpec((COMBINE_ROWS, D), lambda i: (i, 0))
    table = lambda index: pl.BlockSpec((None, 1, TOP_K * COMBINE_ROWS), index, memory_space=pltpu.SMEM)
    return pl.pallas_call(
        functools.partial(_combine_kernel, alpha),
        grid=(n_tiles,),
        in_specs=[table(lambda i: (i, 0, 0)),
                  table(lambda i: (jnp.minimum(i + 1, n_tiles - 1), 0, 0)),
                  row, pl.BlockSpec((COMBINE_ROWS, TOP_K), lambda i: (i, 0)),
                  _const_spec((1, D)), _const_spec((1, D)),
                  pl.BlockSpec(memory_space=pl.ANY)],
        out_specs=[row, row],
        out_shape=[jax.ShapeDtypeStruct((T, D), F32), jax.ShapeDtypeStruct((T, D), BF16)],
        scratch_shapes=[pltpu.VMEM((2, TOP_K, COMBINE_ROWS, D), F32), pltpu.SemaphoreType.DMA((2,))],
        compiler_params=_params(("arbitrary",), 56),
        name="combine",
    )(dest_tiles, dest_tiles, x, gate_t, ln_g.reshape(1, D), ln_b.reshape(1, D), yb)


def _tile_pairs(dest, rows):
    T = dest.shape[1]
    return dest.reshape(TOP_K, T // rows, rows).transpose(1, 0, 2).reshape(T // rows, 1, TOP_K * rows)


def _moe(alpha, x, router_w, router_b, w1, w3, w2, ln_g, ln_b):
    T, D = x.shape
    n_blocks = (T * TOP_K) // EXPERT_ROWS + N_EXPERTS
    idx, gate, rank, cnt = _router(x, router_w, router_b)

    counts = cnt[:, 0]
    padded = (counts + EXPERT_ROWS - 1) // EXPERT_ROWS * EXPERT_ROWS
    pad_end = jnp.cumsum(padded)
    pad_start = pad_end - padded
    n_used = (pad_end[-1] // EXPERT_ROWS).astype(jnp.int32)
    block_start = jnp.minimum(jnp.arange(n_blocks, dtype=jnp.int32), n_used - 1) * EXPERT_ROWS
    block_e = jnp.minimum(jnp.sum(block_start[:, None] >= pad_end[None, :], axis=1),
                          N_EXPERTS - 1).astype(jnp.int32)
    expert_ids = jnp.arange(N_EXPERTS, dtype=jnp.int32)[:, None, None]
    dest = jnp.sum(jnp.where(idx[None] == expert_ids, pad_start[:, None, None], 0), axis=0) + rank

    n_used = n_used.reshape(1)
    xs = _dispatch(x, _tile_pairs(dest, DISPATCH_ROWS), (pad_start + counts).astype(jnp.int32),
                   (padded - counts).astype(jnp.int32), n_used, n_blocks * EXPERT_ROWS)
    yb = _experts(xs, block_e, n_used, w1, w3, w2)
    return _combine(alpha, _tile_pairs(dest, COMBINE_ROWS), x, gate.T, ln_g, ln_b, yb)


def kernel(x, emb_ln_g, emb_ln_b, w_in, conv_w, conv_b, lru_w_a, lru_b_a, lru_w_x, lru_b_x, lru_lambda, w_o_rnn, sgu_ln_g, sgu_ln_b, sgu_w_s, sgu_b_s, w_o_sgu, w_out, ln1_g, ln1_b, router_w, router_b, expert_w1, expert_w3, expert_w2, ln2_g, ln2_b):
    batch, seq, D = x.shape
    depth = w_in.shape[0]
    alpha = float((2 * depth) ** 0.25)
    xf, xb = _emb_ln(x.reshape(batch * seq, D), emb_ln_g, emb_ln_b)
    w_in, w_o_rnn, w_o_sgu, w_out = (w.astype(BF16) for w in (w_in, w_o_rnn, w_o_sgu, w_out))
    rows_of = lambda w: w.reshape(depth, N_EXPERTS * w.shape[2], w.shape[3])
    w1_rows, w3_rows, w2_rows = rows_of(expert_w1), rows_of(expert_w3), rows_of(expert_w2)
    gelu_from = D_RNN
    sigmoid_from = 2 * D_RNN + 2 * D_SGU
    for l in range(depth):
        x_rnn = _proj(xb, w_in, l, 0, gelu_from, lambda v: v, "proj_x")
        gelu_cols, w3 = _proj(xb, w_in, l, gelu_from, sigmoid_from - gelu_from, _gelu, "proj_gelu",
                              cast=w3_rows)
        sigmoid_cols, w1 = _proj(xb, w_in, l, sigmoid_from, D_IN - sigmoid_from, _sigmoid,
                                 "proj_sigmoid", cast=w1_rows)
        a = _lru(x_rnn, gelu_cols, batch, seq, conv_w[l], conv_b[l], lru_w_a[l], lru_b_a[l],
                 lru_w_x[l], lru_b_x[l], lru_lambda[l])
        b = _sgu(gelu_cols, sgu_ln_g[l], sgu_ln_b[l], sgu_w_s[l], sgu_b_s[l])
        x1, w2 = _mix_out(alpha, l, a, b, sigmoid_cols, xf, w_o_rnn, w_o_sgu, w_out,
                          ln1_g[l], ln1_b[l], cast=w2_rows)
        xf, xb = _moe(alpha, x1, router_w, router_b, w1.reshape(expert_w1.shape[1:]),
                      w3.reshape(expert_w3.shape[1:]), w2.reshape(expert_w2.shape[1:]),
                      ln2_g[l], ln2_b[l])
    return xf.reshape(batch, seq, D)
```

```python
import functools

import jax
import jax.numpy as jnp
import numpy as np
from jax import lax
from jax.experimental import pallas as pl
from jax.experimental.pallas import tpu as pltpu

F32 = jnp.float32
BF16 = jnp.bfloat16

D_MODEL = 2048
D_RNN = 1024
RNN_HEADS = 8
HEAD_DIM = D_RNN // RNN_HEADS
CONV_WIDTH = 4
LRU_C = 8.0
D_SGU = 1024
SGU_GROUPS = 8
SGU_GROUP_DIM = D_SGU // SGU_GROUPS
CHUNK = 128
N_EXPERTS = 16
N_GROUPS = 4
EXPERTS_PER_GROUP = N_EXPERTS // N_GROUPS
TOP_K = 2
D_EXPERT = 1408
D_IN = 2 * D_RNN + 2 * D_SGU + 2 * D_MODEL
LN_EPS = 1e-5
SQRT_2_OVER_PI = float(np.sqrt(2.0 / np.pi))

LN_ROWS = 512
PROJ_ROWS = 1024
PROJ_COLS = 1024
LRU_ROWS = 512
SGU_ROWS = 512
MIX_ROWS = 256
ROUTER_ROWS = 512
DISPATCH_ROWS = 512
EXPERT_ROWS = 256
COMBINE_ROWS = 512
CONV_PAD = 8
ROW_TILE = 8

MIB = 1024 * 1024


def _params(semantics, vmem_mib):
    return pltpu.CompilerParams(dimension_semantics=semantics,
                                vmem_limit_bytes=vmem_mib * MIB,
                                disable_bounds_checks=True)


def _const_spec(shape):
    zeros = (0,) * len(shape)
    return pl.BlockSpec(shape, lambda *_: zeros, pipeline_mode=pl.Buffered(1))


def _layer_spec(layer, shape):
    zeros = (0,) * len(shape)
    return pl.BlockSpec((None,) + tuple(shape), lambda *_: (layer,) + zeros,
                        pipeline_mode=pl.Buffered(1))


def _layer_norm(v, g, b):
    mu = jnp.mean(v, axis=-1, keepdims=True)
    d = v - mu
    var = jnp.mean(d * d, axis=-1, keepdims=True)
    return d * lax.rsqrt(var + LN_EPS) * g + b


def _gelu(x):
    inner = x * (SQRT_2_OVER_PI + (SQRT_2_OVER_PI * 0.044715) * (x * x))
    return x * (0.5 * jnp.tanh(inner) + 0.5)


def _sigmoid(x):
    return 0.5 * jnp.tanh(0.5 * x) + 0.5


def _emb_ln_kernel(x_ref, g_ref, b_ref, of_ref, ob_ref):
    y = _layer_norm(x_ref[...], g_ref[...], b_ref[...])
    of_ref[...] = y
    ob_ref[...] = y.astype(BF16)


def _emb_ln(x, g, b):
    T, D = x.shape
    row = pl.BlockSpec((LN_ROWS, D), lambda i: (i, 0))
    return pl.pallas_call(
        _emb_ln_kernel,
        grid=(T // LN_ROWS,),
        in_specs=[row, _const_spec((1, D)), _const_spec((1, D))],
        out_specs=[row, row],
        out_shape=[jax.ShapeDtypeStruct((T, D), F32), jax.ShapeDtypeStruct((T, D), BF16)],
        compiler_params=_params(("parallel",), 40),
        name="emb_ln",
    )(x, g.reshape(1, D), b.reshape(1, D))


def _cast_specs(stack, layer, n_steps, step_of):
    _, rows, cols = stack.shape
    slab = -(-rows // (n_steps * 16)) * 16
    src = pl.BlockSpec((None, slab, cols), lambda *g: (layer, step_of(*g), 0))
    dst = pl.BlockSpec((slab, cols), lambda *g: (step_of(*g), 0))
    return src, dst, jax.ShapeDtypeStruct((rows, cols), BF16)


def _proj_kernel(act, x_ref, w_ref, *rest):
    o_ref = rest[len(rest) // 2]
    acc = jnp.dot(x_ref[...], w_ref[...], preferred_element_type=F32)
    o_ref[...] = act(acc).astype(BF16)
    for src_ref, dst_ref in zip(rest[:len(rest) // 2], rest[len(rest) // 2 + 1:]):
        dst_ref[...] = src_ref[...].astype(BF16)


def _proj(xb, w_in, layer, col_from, n_cols, act, name, cast=None):
    T, D = xb.shape
    first = col_from // PROJ_COLS
    grid = (T // PROJ_ROWS, n_cols // PROJ_COLS)
    in_specs = [pl.BlockSpec((PROJ_ROWS, D), lambda i, j: (i, 0)),
                pl.BlockSpec((None, D, PROJ_COLS), lambda i, j: (layer, 0, first + j))]
    out_specs = [pl.BlockSpec((PROJ_ROWS, PROJ_COLS), lambda i, j: (i, j))]
    out_shape = [jax.ShapeDtypeStruct((T, n_cols), BF16)]
    operands = [xb, w_in]
    if cast is not None:
        src, dst, shape = _cast_specs(cast, layer, grid[0] * grid[1], lambda i, j: i * grid[1] + j)
        in_specs.append(src)
        out_specs.append(dst)
        out_shape.append(shape)
        operands.append(cast)
    out = pl.pallas_call(
        functools.partial(_proj_kernel, act),
        grid=grid,
        in_specs=in_specs,
        out_specs=out_specs,
        out_shape=out_shape,
        compiler_params=_params(("parallel", "arbitrary"), 48),
        name=name,
    )(*operands)
    return out[0] if cast is None else out


def _lru_kernel(xr_ref, gate_ref, cw_ref, cb_ref, wa_ref, ba_ref, wx_ref, bx_ref, lam_ref,
                o_ref, xpad_ref, hc_ref, a_ref, u_ref, h_ref):
    rows = xr_ref.shape[0]

    @pl.when(pl.program_id(1) == 0)
    def _():
        xpad_ref[0:CONV_PAD, :] = jnp.zeros((CONV_PAD, D_RNN), F32)
        hc_ref[...] = jnp.zeros_like(hc_ref)

    x = xr_ref[...].astype(F32)
    xpad_ref[CONV_PAD:CONV_PAD + rows, :] = x
    xc = cb_ref[...] + cw_ref[CONV_WIDTH - 1:CONV_WIDTH, :] * x
    for k in range(CONV_WIDTH - 1):
        shift = CONV_WIDTH - 1 - k
        xc = xc + cw_ref[k:k + 1, :] * xpad_ref[CONV_PAD - shift:CONV_PAD - shift + rows, :]
    xpad_ref[0:CONV_PAD, :] = x[rows - CONV_PAD:rows, :]

    lam = lam_ref[...]
    sp = jnp.maximum(-lam, 0.0) + jnp.log1p(jnp.exp(-jnp.abs(lam)))
    for h in range(RNN_HEADS):
        cols = slice(h * HEAD_DIM, (h + 1) * HEAD_DIM)
        xh = xc[:, cols]
        xhb = xh.astype(BF16)
        r = _sigmoid(jnp.dot(xhb, wa_ref[h], preferred_element_type=F32) + ba_ref[:, cols])
        gi = _sigmoid(jnp.dot(xhb, wx_ref[h], preferred_element_type=F32) + bx_ref[:, cols])
        log_a = (-LRU_C) * r * sp[:, cols]
        a_ref[:, cols] = jnp.exp(log_a)
        th = jnp.tanh(log_a)
        u_ref[:, cols] = jnp.sqrt(-2.0 * th / (1.0 - th)) * (gi * xh)

    def step(t, h):
        h = a_ref[pl.ds(t, 1), :] * h + u_ref[pl.ds(t, 1), :]
        h_ref[pl.ds(t, 1), :] = h
        return h

    hc_ref[0:1, :] = lax.fori_loop(0, rows, step, hc_ref[0:1, :], unroll=8)
    o_ref[...] = (h_ref[...] * gate_ref[...].astype(F32)).astype(BF16)


def _lru(x_rnn, gelu_cols, batch, seq, conv_w, conv_b, w_a, b_a, w_x, b_x, lam):
    T = x_rnn.shape[0]
    tiles = seq // LRU_ROWS
    row = pl.BlockSpec((LRU_ROWS, D_RNN), lambda b, s: (b * tiles + s, 0))
    vec = _const_spec((1, D_RNN))
    gate_w = _const_spec((RNN_HEADS, HEAD_DIM, HEAD_DIM))
    return pl.pallas_call(
        _lru_kernel,
        grid=(batch, tiles),
        in_specs=[row, row, _const_spec((CONV_WIDTH, D_RNN)), vec, gate_w, vec, gate_w, vec, vec],
        out_specs=pl.BlockSpec((LRU_ROWS, D_RNN), lambda b, s: (b * tiles + s, 0)),
        out_shape=jax.ShapeDtypeStruct((T, D_RNN), BF16),
        scratch_shapes=[pltpu.VMEM((CONV_PAD + LRU_ROWS, D_RNN), F32),
                        pltpu.VMEM((8, D_RNN), F32),
                        pltpu.VMEM((LRU_ROWS, D_RNN), F32),
                        pltpu.VMEM((LRU_ROWS, D_RNN), F32),
                        pltpu.VMEM((LRU_ROWS, D_RNN), F32)],
        compiler_params=_params(("arbitrary", "arbitrary"), 40),
        name="lru",
    )(x_rnn, gelu_cols, conv_w, conv_b.reshape(1, D_RNN), w_a.astype(BF16), b_a.reshape(1, D_RNN),
      w_x.astype(BF16), b_x.reshape(1, D_RNN), lam.reshape(1, D_RNN))


def _sgu_kernel(u_ref, v_ref, g_ref, b_ref, ws_ref, bs_ref, o_ref):
    rows = u_ref.shape[0]
    v = _layer_norm(v_ref[...].astype(F32), g_ref[...], b_ref[...]).astype(BF16)
    t_out = lax.broadcasted_iota(jnp.int32, (CHUNK, CHUNK), 0)
    t_in = lax.broadcasted_iota(jnp.int32, (CHUNK, CHUNK), 1)
    causal = t_in <= t_out
    for g in range(SGU_GROUPS):
        cols = slice(g * SGU_GROUP_DIM, (g + 1) * SGU_GROUP_DIM)
        ws = jnp.where(causal, ws_ref[g], 0.0).astype(BF16)
        bias = bs_ref[:, g:g + 1]
        for c in range(rows // CHUNK):
            rws = slice(c * CHUNK, (c + 1) * CHUNK)
            mixed = jnp.dot(ws, v[rws, cols], preferred_element_type=F32) + bias
            o_ref[rws, cols] = (u_ref[rws, cols].astype(F32) * mixed).astype(BF16)


def _sgu(gelu_cols, ln_g, ln_b, w_s, b_s):
    T = gelu_cols.shape[0]
    row = lambda c: pl.BlockSpec((SGU_ROWS, D_SGU), lambda i: (i, c))
    return pl.pallas_call(
        _sgu_kernel,
        grid=(T // SGU_ROWS,),
        in_specs=[row(1), row(2), _const_spec((1, D_SGU)), _const_spec((1, D_SGU)),
                  _const_spec((SGU_GROUPS, CHUNK, CHUNK)), _const_spec((CHUNK, SGU_GROUPS))],
        out_specs=pl.BlockSpec((SGU_ROWS, D_SGU), lambda i: (i, 0)),
        out_shape=jax.ShapeDtypeStruct((T, D_SGU), BF16),
        compiler_params=_params(("parallel",), 40),
        name="sgu",
    )(gelu_cols, gelu_cols, ln_g.reshape(1, D_SGU), ln_b.reshape(1, D_SGU), w_s, b_s.T)


def _mix_out_kernel(alpha, a_ref, b_ref, sga_ref, sgb_ref, x_ref, woa_ref, wob_ref, wout_ref,
                    g_ref, beta_ref, cast_src_ref, o_ref, cast_dst_ref):
    ya = jnp.dot(a_ref[...], woa_ref[...], preferred_element_type=F32)
    yb = jnp.dot(b_ref[...], wob_ref[...], preferred_element_type=F32)
    merged = sga_ref[...].astype(F32) * ya + sgb_ref[...].astype(F32) * yb
    m = jnp.dot(merged.astype(BF16), wout_ref[...], preferred_element_type=F32)
    o_ref[...] = _layer_norm(alpha * x_ref[...] + m, g_ref[...], beta_ref[...])
    cast_dst_ref[...] = cast_src_ref[...].astype(BF16)


def _mix_out(alpha, layer, a, b, sigmoid_cols, x, w_o_rnn, w_o_sgu, w_out, ln_g, ln_b, cast):
    T, D = x.shape
    n_steps = T // MIX_ROWS
    row = lambda w, c: pl.BlockSpec((MIX_ROWS, w), lambda i: (i, c))
    cast_src, cast_dst, cast_shape = _cast_specs(cast, layer, n_steps, lambda i: i)
    return pl.pallas_call(
        functools.partial(_mix_out_kernel, alpha),
        grid=(n_steps,),
        in_specs=[row(D_RNN, 0), row(D_SGU, 0), row(D, 0), row(D, 1), row(D, 0),
                  _layer_spec(layer, (D_RNN, D)), _layer_spec(layer, (D_SGU, D)),
                  _layer_spec(layer, (D, D)), _const_spec((1, D)), _const_spec((1, D)), cast_src],
        out_specs=[row(D, 0), cast_dst],
        out_shape=[jax.ShapeDtypeStruct((T, D), F32), cast_shape],
        compiler_params=_params(("parallel",), 56),
        name="mix_out",
    )(a, b, sigmoid_cols, sigmoid_cols, x, w_o_rnn, w_o_sgu, w_out,
      ln_g.reshape(1, D), ln_b.reshape(1, D), cast)


def _first_max(v, row, n):
    m = jnp.max(v, axis=0, keepdims=True)
    idx = jnp.min(jnp.where(v == m, row, float(n)), axis=0, keepdims=True)
    return m, idx


def _router_kernel(x_ref, wt_ref, b_ref, idx_ref, gate_ref, rank_ref, cnt_ref, run_ref):
    rows = x_ref.shape[0]

    @pl.when(pl.program_id(0) == 0)
    def _():
        run_ref[...] = jnp.zeros_like(run_ref)

    def nt_dot(w, x):
        return lax.dot_general(w, x, (((1,), (1,)), ((), ())), preferred_element_type=F32)

    x = x_ref[...]
    xh = x.astype(BF16)
    xl = (x - xh.astype(F32)).astype(BF16)
    w = wt_ref[...]
    wh = w.astype(BF16)
    wl = (w - wh.astype(F32)).astype(BF16)
    logits = nt_dot(wh, xh) + (nt_dot(wh, xl) + nt_dot(wl, xh))
    e = jnp.exp(logits - jnp.max(logits, axis=0, keepdims=True))
    scores = e / jnp.sum(e, axis=0, keepdims=True)
    sel = scores + b_ref[...]

    row = lax.broadcasted_iota(jnp.int32, (N_EXPERTS, rows), 0).astype(F32)
    grow = lax.broadcasted_iota(jnp.int32, (EXPERTS_PER_GROUP, rows), 0).astype(F32)
    neg_inf = float("-inf")
    best_score = None
    best_group = None
    for g in range(N_GROUPS):
        v = sel[g * EXPERTS_PER_GROUP:(g + 1) * EXPERTS_PER_GROUP, :]
        m1, i1 = _first_max(v, grow, EXPERTS_PER_GROUP)
        m2 = jnp.max(jnp.where(grow == i1, neg_inf, v), axis=0, keepdims=True)
        s = m1 + m2
        if g == 0:
            best_score, best_group = s, jnp.zeros_like(s)
        else:
            better = s > best_score
            best_group = jnp.where(better, float(g), best_group)
            best_score = jnp.where(better, s, best_score)

    lo = best_group * float(EXPERTS_PER_GROUP)
    in_group = jnp.logical_and(row >= lo, row < lo + float(EXPERTS_PER_GROUP))
    masked = jnp.where(in_group, sel, neg_inf)
    _, i1 = _first_max(masked, row, N_EXPERTS)
    pick1 = row == i1
    _, i2 = _first_max(jnp.where(pick1, neg_inf, masked), row, N_EXPERTS)
    pick2 = row == i2
    s1 = jnp.sum(jnp.where(pick1, scores, 0.0), axis=0, keepdims=True)
    s2 = jnp.sum(jnp.where(pick2, scores, 0.0), axis=0, keepdims=True)
    den = s1 + s2

    onehot = jnp.where(jnp.logical_or(pick1, pick2), 1.0, 0.0)
    before = (lax.broadcasted_iota(jnp.int32, (rows, rows), 0)
              < lax.broadcasted_iota(jnp.int32, (rows, rows), 1))
    prefix = jnp.dot(onehot.astype(BF16), jnp.where(before, 1.0, 0.0).astype(BF16),
                     preferred_element_type=F32)
    pos = prefix + run_ref[:, 0:1]
    r1 = jnp.sum(jnp.where(pick1, pos, 0.0), axis=0, keepdims=True)
    r2 = jnp.sum(jnp.where(pick2, pos, 0.0), axis=0, keepdims=True)
    run_ref[...] = run_ref[...] + jnp.sum(onehot, axis=1, keepdims=True)

    idx_ref[0:1, :] = i1.astype(jnp.int32)
    idx_ref[1:2, :] = i2.astype(jnp.int32)
    gate_ref[0:1, :] = s1 / den
    gate_ref[1:2, :] = s2 / den
    rank_ref[0:1, :] = r1.astype(jnp.int32)
    rank_ref[1:2, :] = r2.astype(jnp.int32)
    cnt_ref[...] = run_ref[...].astype(jnp.int32)


def _router(x, router_w, router_b):
    T, D = x.shape
    pair = pl.BlockSpec((TOP_K, ROUTER_ROWS), lambda i: (0, i))
    return pl.pallas_call(
        _router_kernel,
        grid=(T // ROUTER_ROWS,),
        in_specs=[pl.BlockSpec((ROUTER_ROWS, D), lambda i: (i, 0)),
                  _const_spec((N_EXPERTS, D)), _const_spec((N_EXPERTS, 1))],
        out_specs=[pair, pair, pair, pl.BlockSpec((N_EXPERTS, 128), lambda i: (0, 0))],
        out_shape=[jax.ShapeDtypeStruct((TOP_K, T), jnp.int32),
                   jax.ShapeDtypeStruct((TOP_K, T), F32),
                   jax.ShapeDtypeStruct((TOP_K, T), jnp.int32),
                   jax.ShapeDtypeStruct((N_EXPERTS, 128), jnp.int32)],
        scratch_shapes=[pltpu.VMEM((N_EXPERTS, 128), F32)],
        compiler_params=_params(("arbitrary",), 40),
        name="router",
    )(x, router_w.T, router_b.reshape(N_EXPERTS, 1))


def _dispatch_kernel(pad_from_ref, pad_n_ref, n_used_ref, dest_ref, x_ref, xs_hbm, zero_ref, sem, pad_sem):
    i = pl.program_id(0)
    rows = DISPATCH_ROWS
    n_blocks = xs_hbm.shape[0] // EXPERT_ROWS
    zero_rows = zero_ref.shape[0]

    def tail_copies(j):
        block = n_used_ref[0] + j
        off = pl.multiple_of(block * EXPERT_ROWS, EXPERT_ROWS)
        return block < n_blocks, [
            pltpu.make_async_copy(zero_ref, xs_hbm.at[pl.ds(off + part * zero_rows, zero_rows)], pad_sem)
            for part in range(EXPERT_ROWS // zero_rows)]

    def pad_copy(e, bit):
        n = pad_n_ref[e]
        if bit < ROW_TILE:
            off = pad_from_ref[e] + bit - 1
            return pltpu.make_async_copy(zero_ref.at[pl.ds(0, 1)], xs_hbm.at[pl.ds(off, 1)], pad_sem)
        done = (n & (ROW_TILE - 1)) + (n & ~(2 * bit - 1))
        off = pl.multiple_of(pad_from_ref[e] + done, ROW_TILE)
        return pltpu.make_async_copy(zero_ref.at[pl.ds(0, bit)], xs_hbm.at[pl.ds(off, bit)], pad_sem)

    def pad_needed(e, bit):
        n = pad_n_ref[e]
        if bit < ROW_TILE:
            return bit <= (n & (ROW_TILE - 1))
        return (n & bit) != 0

    bits = list(range(1, ROW_TILE)) + [
        1 << k for k in range(ROW_TILE.bit_length() - 1, EXPERT_ROWS.bit_length() - 1)]

    @pl.when(i == 0)
    def _():
        zero_ref[...] = jnp.zeros_like(zero_ref)
        for e in range(N_EXPERTS):
            for bit in bits:
                @pl.when(pad_needed(e, bit))
                def _():
                    pad_copy(e, bit).start()
        for j in range(N_EXPERTS):
            needed, copies = tail_copies(j)

            @pl.when(needed)
            def _():
                for c in copies:
                    c.start()

    for t in range(rows):
        src = x_ref.at[pl.ds(t, 1)]
        pltpu.make_async_copy(src, xs_hbm.at[pl.ds(dest_ref[0, t], 1)], sem).start()
        pltpu.make_async_copy(src, xs_hbm.at[pl.ds(dest_ref[0, rows + t], 1)], sem).start()
    for _ in range(TOP_K):
        pltpu.make_async_copy(x_ref, xs_hbm.at[pl.ds(0, rows)], sem).wait()

    @pl.when(i == 0)
    def _():
        for e in range(N_EXPERTS):
            for bit in bits:
                @pl.when(pad_needed(e, bit))
                def _():
                    pad_copy(e, bit).wait()
        for j in range(N_EXPERTS):
            needed, copies = tail_copies(j)

            @pl.when(needed)
            def _():
                for c in copies:
                    c.wait()


def _dispatch(x, dest_tiles, pad_from, pad_n, n_used, n_slots):
    T, D = x.shape
    grid_spec = pltpu.PrefetchScalarGridSpec(
        num_scalar_prefetch=3,
        grid=(T // DISPATCH_ROWS,),
        in_specs=[pl.BlockSpec((None, 1, TOP_K * DISPATCH_ROWS), lambda i, *_: (i, 0, 0),
                               memory_space=pltpu.SMEM),
                  pl.BlockSpec((DISPATCH_ROWS, D), lambda i, *_: (i, 0))],
        out_specs=pl.BlockSpec(memory_space=pl.ANY),
        scratch_shapes=[pltpu.VMEM((EXPERT_ROWS // 2, D), F32),
                        pltpu.SemaphoreType.DMA(()), pltpu.SemaphoreType.DMA(())],
    )
    return pl.pallas_call(
        _dispatch_kernel,
        grid_spec=grid_spec,
        out_shape=jax.ShapeDtypeStruct((n_slots, D), F32),
        compiler_params=_params(("arbitrary",), 16),
        name="dispatch",
    )(pad_from, pad_n, n_used, dest_tiles, x)


def _expert_kernel(block_e_ref, n_used_ref, xs_ref, w1_ref, w3_ref, w2_ref, o_ref):
    used = pl.program_id(0) < n_used_ref[0]

    @pl.when(jnp.logical_not(used))
    def _():
        o_ref[...] = jnp.zeros_like(o_ref)

    @pl.when(used)
    def _():
        x = xs_ref[...].astype(BF16)
        h1 = jnp.dot(x, w1_ref[...], preferred_element_type=F32)
        h3 = jnp.dot(x, w3_ref[...], preferred_element_type=F32)
        hidden = (h1 * _sigmoid(h1)) * h3
        o_ref[...] = jnp.dot(hidden.astype(BF16), w2_ref[...], preferred_element_type=F32)


def _experts(xs, block_e, n_used, w1, w3, w2):
    P, D = xs.shape
    n_blocks = P // EXPERT_ROWS
    used_rows = pl.BlockSpec((EXPERT_ROWS, D),
                             lambda i, be, nu: (jnp.maximum(jnp.minimum(i, nu[0] - 1), 0), 0))
    rows = pl.BlockSpec((EXPERT_ROWS, D), lambda i, be, nu: (i, 0))
    expert_slab = lambda i, be, nu: (be[i], 0, 0)
    grid_spec = pltpu.PrefetchScalarGridSpec(
        num_scalar_prefetch=2,
        grid=(n_blocks,),
        in_specs=[used_rows,
                  pl.BlockSpec((None, D, D_EXPERT), expert_slab),
                  pl.BlockSpec((None, D, D_EXPERT), expert_slab),
                  pl.BlockSpec((None, D_EXPERT, D), expert_slab)],
        out_specs=rows,
    )
    return pl.pallas_call(
        _expert_kernel,
        grid_spec=grid_spec,
        out_shape=jax.ShapeDtypeStruct((P, D), F32),
        compiler_params=_params(("arbitrary",), 56),
        name="experts",
    )(block_e, n_used, xs, w1, w3, w2)


def _combine_kernel(alpha, dest_ref, dest_next_ref, x_ref, gate_ref, g_ref, b_ref, y_hbm,
                    of_ref, ob_ref, buf_a, buf_b, sem):
    rows = COMBINE_ROWS
    half = rows // 2
    i = pl.program_id(0)

    def copy(table_ref, h, t, k, buf, s):
        return pltpu.make_async_copy(y_hbm.at[pl.ds(table_ref[0, k * rows + h * half + t], 1)],
                                     buf.at[k, pl.ds(t, 1)], sem.at[s])

    def gather(table_ref, h, buf, s):
        for t in range(half):
            for k in range(TOP_K):
                copy(table_ref, h, t, k, buf, s).start()

    def wait(buf, s):
        for k in range(TOP_K):
            pltpu.make_async_copy(y_hbm.at[pl.ds(0, half)], buf.at[k], sem.at[s]).wait()

    def reduce(buf, h):
        r = slice(h * half, (h + 1) * half)
        f = gate_ref[r, 0:1] * buf[0] + gate_ref[r, 1:2] * buf[1]
        y = _layer_norm(alpha * x_ref[r, :] + f, g_ref[...], b_ref[...])
        of_ref[r, :] = y
        ob_ref[r, :] = y.astype(BF16)

    @pl.when(i == 0)
    def _():
        def issue(t, carry):
            for k in range(TOP_K):
                copy(dest_ref, 0, t, k, buf_a, 0).start()
            return carry

        lax.fori_loop(0, half, issue, 0, unroll=8)

    wait(buf_a, 0)
    gather(dest_ref, 1, buf_b, 1)
    reduce(buf_a, 0)
    wait(buf_b, 1)
    gather(dest_next_ref, 0, buf_a, 0)
    reduce(buf_b, 1)

    @pl.when(i == pl.num_programs(0) - 1)
    def _():
        wait(buf_a, 0)


def _combine(alpha, dest_tiles, x, gate_t, ln_g, ln_b, yb):
    T, D = x.shape
    n_tiles = T // COMBINE_ROWS
    row = pl.BlockSpec((COMBINE_ROWS, D), lambda i: (i, 0))
    table = lambda index: pl.BlockSpec((None, 1, TOP_K * COMBINE_ROWS), index, memory_space=pltpu.SMEM)
    half_buf = pltpu.VMEM((TOP_K, COMBINE_ROWS // 2, D), F32)
    return pl.pallas_call(
        functools.partial(_combine_kernel, alpha),
        grid=(n_tiles,),
        in_specs=[table(lambda i: (i, 0, 0)),
                  table(lambda i: (jnp.minimum(i + 1, n_tiles - 1), 0, 0)),
                  row, pl.BlockSpec((COMBINE_ROWS, TOP_K), lambda i: (i, 0)),
                  _const_spec((1, D)), _const_spec((1, D)),
                  pl.BlockSpec(memory_space=pl.ANY)],
        out_specs=[row, row],
        out_shape=[jax.ShapeDtypeStruct((T, D), F32), jax.ShapeDtypeStruct((T, D), BF16)],
        scratch_shapes=[half_buf, half_buf, pltpu.SemaphoreType.DMA((2,))],
        compiler_params=_params(("arbitrary",), 48),
        name="combine",
    )(dest_tiles, dest_tiles, x, gate_t, ln_g.reshape(1, D), ln_b.reshape(1, D), yb)


def _tile_pairs(dest, rows):
    T = dest.shape[1]
    return dest.reshape(TOP_K, T // rows, rows).transpose(1, 0, 2).reshape(T // rows, 1, TOP_K * rows)


def _moe(alpha, x, router_w, router_b, w1, w3, w2, ln_g, ln_b):
    T, D = x.shape
    n_blocks = (T * TOP_K) // EXPERT_ROWS + N_EXPERTS
    idx, gate, rank, cnt = _router(x, router_w, router_b)

    counts = cnt[:, 0]
    padded = (counts + EXPERT_ROWS - 1) // EXPERT_ROWS * EXPERT_ROWS
    pad_end = jnp.cumsum(padded)
    pad_start = pad_end - padded
    n_used = (pad_end[-1] // EXPERT_ROWS).astype(jnp.int32)
    block_start = jnp.minimum(jnp.arange(n_blocks, dtype=jnp.int32), n_used - 1) * EXPERT_ROWS
    block_e = jnp.minimum(jnp.sum(block_start[:, None] >= pad_end[None, :], axis=1),
                          N_EXPERTS - 1).astype(jnp.int32)
    expert_ids = jnp.arange(N_EXPERTS, dtype=jnp.int32)[:, None, None]
    dest = jnp.sum(jnp.where(idx[None] == expert_ids, pad_start[:, None, None], 0), axis=0) + rank

    n_used = n_used.reshape(1)
    xs = _dispatch(x, _tile_pairs(dest, DISPATCH_ROWS), (pad_start + counts).astype(jnp.int32),
                   (padded - counts).astype(jnp.int32), n_used, n_blocks * EXPERT_ROWS)
    yb = _experts(xs, block_e, n_used, w1, w3, w2)
    return _combine(alpha, _tile_pairs(dest, COMBINE_ROWS), x, gate.T, ln_g, ln_b, yb)


def kernel(x, emb_ln_g, emb_ln_b, w_in, conv_w, conv_b, lru_w_a, lru_b_a, lru_w_x, lru_b_x, lru_lambda, w_o_rnn, sgu_ln_g, sgu_ln_b, sgu_w_s, sgu_b_s, w_o_sgu, w_out, ln1_g, ln1_b, router_w, router_b, expert_w1, expert_w3, expert_w2, ln2_g, ln2_b):
    batch, seq, D = x.shape
    depth = w_in.shape[0]
    alpha = float((2 * depth) ** 0.25)
    xf, xb = _emb_ln(x.reshape(batch * seq, D), emb_ln_g, emb_ln_b)
    w_in, w_o_rnn, w_o_sgu, w_out = (w.astype(BF16) for w in (w_in, w_o_rnn, w_o_sgu, w_out))
    rows_of = lambda w: w.reshape(depth, N_EXPERTS * w.shape[2], w.shape[3])
    w1_rows, w3_rows, w2_rows = rows_of(expert_w1), rows_of(expert_w3), rows_of(expert_w2)
    gelu_from = D_RNN
    sigmoid_from = 2 * D_RNN + 2 * D_SGU
    for l in range(depth):
        x_rnn = _proj(xb, w_in, l, 0, gelu_from, lambda v: v, "proj_x")
        gelu_cols, w3 = _proj(xb, w_in, l, gelu_from, sigmoid_from - gelu_from, _gelu, "proj_gelu",
                              cast=w3_rows)
        sigmoid_cols, w1 = _proj(xb, w_in, l, sigmoid_from, D_IN - sigmoid_from, _sigmoid,
                                 "proj_sigmoid", cast=w1_rows)
        a = _lru(x_rnn, gelu_cols, batch, seq, conv_w[l], conv_b[l], lru_w_a[l], lru_b_a[l],
                 lru_w_x[l], lru_b_x[l], lru_lambda[l])
        b = _sgu(gelu_cols, sgu_ln_g[l], sgu_ln_b[l], sgu_w_s[l], sgu_b_s[l])
        x1, w2 = _mix_out(alpha, l, a, b, sigmoid_cols, xf, w_o_rnn, w_o_sgu, w_out,
                          ln1_g[l], ln1_b[l], cast=w2_rows)
        xf, xb = _moe(alpha, x1, router_w, router_b, w1.reshape(expert_w1.shape[1:]),
                      w3.reshape(expert_w3.shape[1:]), w2.reshape(expert_w2.shape[1:]),
                      ln2_g[l], ln2_b[l])
    return xf.reshape(batch, seq, D)
```

```python
import functools

import jax
import jax.numpy as jnp
import numpy as np
from jax import lax
from jax.experimental import pallas as pl
from jax.experimental.pallas import tpu as pltpu

F32 = jnp.float32
BF16 = jnp.bfloat16

D_MODEL = 2048
D_RNN = 1024
RNN_HEADS = 8
HEAD_DIM = D_RNN // RNN_HEADS
CONV_WIDTH = 4
LRU_C = 8.0
D_SGU = 1024
SGU_GROUPS = 8
SGU_GROUP_DIM = D_SGU // SGU_GROUPS
CHUNK = 128
N_EXPERTS = 16
N_GROUPS = 4
EXPERTS_PER_GROUP = N_EXPERTS // N_GROUPS
TOP_K = 2
D_EXPERT = 1408
D_IN = 2 * D_RNN + 2 * D_SGU + 2 * D_MODEL
LN_EPS = 1e-5
SQRT_2_OVER_PI = float(np.sqrt(2.0 / np.pi))

LN_ROWS = 512
PROJ_ROWS = 1024
PROJ_COLS = 1024
LRU_ROWS = 512
SGU_ROWS = 512
MIX_ROWS = 256
ROUTER_ROWS = 512
DISPATCH_ROWS = 512
EXPERT_ROWS = 256
COMBINE_ROWS = 512
CONV_PAD = 8
N_DMA_PRIORITIES = 2
ROW_TILE = 8

MIB = 1024 * 1024


def _params(semantics, vmem_mib):
    return pltpu.CompilerParams(dimension_semantics=semantics,
                                vmem_limit_bytes=vmem_mib * MIB,
                                disable_bounds_checks=True)


def _const_spec(shape):
    zeros = (0,) * len(shape)
    return pl.BlockSpec(shape, lambda *_: zeros, pipeline_mode=pl.Buffered(1))


def _layer_spec(layer, shape):
    zeros = (0,) * len(shape)
    return pl.BlockSpec((None,) + tuple(shape), lambda *_: (layer,) + zeros,
                        pipeline_mode=pl.Buffered(1))


def _layer_norm(v, g, b):
    mu = jnp.mean(v, axis=-1, keepdims=True)
    d = v - mu
    var = jnp.mean(d * d, axis=-1, keepdims=True)
    return d * lax.rsqrt(var + LN_EPS) * g + b


def _gelu(x):
    inner = x * (SQRT_2_OVER_PI + (SQRT_2_OVER_PI * 0.044715) * (x * x))
    return x * (0.5 * jnp.tanh(inner) + 0.5)


def _sigmoid(x):
    return 0.5 * jnp.tanh(0.5 * x) + 0.5


def _emb_ln_kernel(x_ref, g_ref, b_ref, of_ref, ob_ref):
    y = _layer_norm(x_ref[...], g_ref[...], b_ref[...])
    of_ref[...] = y
    ob_ref[...] = y.astype(BF16)


def _emb_ln(x, g, b):
    T, D = x.shape
    row = pl.BlockSpec((LN_ROWS, D), lambda i: (i, 0))
    return pl.pallas_call(
        _emb_ln_kernel,
        grid=(T // LN_ROWS,),
        in_specs=[row, _const_spec((1, D)), _const_spec((1, D))],
        out_specs=[row, row],
        out_shape=[jax.ShapeDtypeStruct((T, D), F32), jax.ShapeDtypeStruct((T, D), BF16)],
        compiler_params=_params(("parallel",), 40),
        name="emb_ln",
    )(x, g.reshape(1, D), b.reshape(1, D))


def _cast_specs(stack, layer, n_steps, step_of):
    _, rows, cols = stack.shape
    slab = -(-rows // (n_steps * 16)) * 16
    src = pl.BlockSpec((None, slab, cols), lambda *g: (layer, step_of(*g), 0))
    dst = pl.BlockSpec((slab, cols), lambda *g: (step_of(*g), 0))
    return src, dst, jax.ShapeDtypeStruct((rows, cols), BF16)


def _proj_kernel(act, x_ref, w_ref, *rest):
    o_ref = rest[len(rest) // 2]
    acc = jnp.dot(x_ref[...], w_ref[...], preferred_element_type=F32)
    o_ref[...] = act(acc).astype(BF16)
    for src_ref, dst_ref in zip(rest[:len(rest) // 2], rest[len(rest) // 2 + 1:]):
        dst_ref[...] = src_ref[...].astype(BF16)


def _proj(xb, w_in, layer, col_from, n_cols, act, name, cast=None):
    T, D = xb.shape
    first = col_from // PROJ_COLS
    grid = (T // PROJ_ROWS, n_cols // PROJ_COLS)
    in_specs = [pl.BlockSpec((PROJ_ROWS, D), lambda i, j: (i, 0)),
                pl.BlockSpec((None, D, PROJ_COLS), lambda i, j: (layer, 0, first + j))]
    out_specs = [pl.BlockSpec((PROJ_ROWS, PROJ_COLS), lambda i, j: (i, j))]
    out_shape = [jax.ShapeDtypeStruct((T, n_cols), BF16)]
    operands = [xb, w_in]
    if cast is not None:
        src, dst, shape = _cast_specs(cast, layer, grid[0] * grid[1], lambda i, j: i * grid[1] + j)
        in_specs.append(src)
        out_specs.append(dst)
        out_shape.append(shape)
        operands.append(cast)
    out = pl.pallas_call(
        functools.partial(_proj_kernel, act),
        grid=grid,
        in_specs=in_specs,
        out_specs=out_specs,
        out_shape=out_shape,
        compiler_params=_params(("parallel", "arbitrary"), 48),
        name=name,
    )(*operands)
    return out[0] if cast is None else out


def _lru_kernel(xr_ref, gate_ref, cw_ref, cb_ref, wa_ref, ba_ref, wx_ref, bx_ref, lam_ref,
                o_ref, xpad_ref, hc_ref, a_ref, u_ref, h_ref):
    rows = xr_ref.shape[0]

    @pl.when(pl.program_id(1) == 0)
    def _():
        xpad_ref[0:CONV_PAD, :] = jnp.zeros((CONV_PAD, D_RNN), F32)
        hc_ref[...] = jnp.zeros_like(hc_ref)

    x = xr_ref[...].astype(F32)
    xpad_ref[CONV_PAD:CONV_PAD + rows, :] = x
    xc = cb_ref[...] + cw_ref[CONV_WIDTH - 1:CONV_WIDTH, :] * x
    for k in range(CONV_WIDTH - 1):
        shift = CONV_WIDTH - 1 - k
        xc = xc + cw_ref[k:k + 1, :] * xpad_ref[CONV_PAD - shift:CONV_PAD - shift + rows, :]
    xpad_ref[0:CONV_PAD, :] = x[rows - CONV_PAD:rows, :]

    lam = lam_ref[...]
    sp = jnp.maximum(-lam, 0.0) + jnp.log1p(jnp.exp(-jnp.abs(lam)))
    for h in range(RNN_HEADS):
        cols = slice(h * HEAD_DIM, (h + 1) * HEAD_DIM)
        xh = xc[:, cols]
        xhb = xh.astype(BF16)
        r = _sigmoid(jnp.dot(xhb, wa_ref[h], preferred_element_type=F32) + ba_ref[:, cols])
        gi = _sigmoid(jnp.dot(xhb, wx_ref[h], preferred_element_type=F32) + bx_ref[:, cols])
        log_a = (-LRU_C) * r * sp[:, cols]
        a_ref[:, cols] = jnp.exp(log_a)
        th = jnp.tanh(log_a)
        u_ref[:, cols] = jnp.sqrt(-2.0 * th / (1.0 - th)) * (gi * xh)

    def step(t, h):
        h = a_ref[pl.ds(t, 1), :] * h + u_ref[pl.ds(t, 1), :]
        h_ref[pl.ds(t, 1), :] = h
        return h

    hc_ref[0:1, :] = lax.fori_loop(0, rows, step, hc_ref[0:1, :], unroll=8)
    o_ref[...] = (h_ref[...] * gate_ref[...].astype(F32)).astype(BF16)


def _lru(x_rnn, gelu_cols, batch, seq, conv_w, conv_b, w_a, b_a, w_x, b_x, lam):
    T = x_rnn.shape[0]
    tiles = seq // LRU_ROWS
    row = pl.BlockSpec((LRU_ROWS, D_RNN), lambda b, s: (b * tiles + s, 0))
    vec = _const_spec((1, D_RNN))
    gate_w = _const_spec((RNN_HEADS, HEAD_DIM, HEAD_DIM))
    return pl.pallas_call(
        _lru_kernel,
        grid=(batch, tiles),
        in_specs=[row, row, _const_spec((CONV_WIDTH, D_RNN)), vec, gate_w, vec, gate_w, vec, vec],
        out_specs=pl.BlockSpec((LRU_ROWS, D_RNN), lambda b, s: (b * tiles + s, 0)),
        out_shape=jax.ShapeDtypeStruct((T, D_RNN), BF16),
        scratch_shapes=[pltpu.VMEM((CONV_PAD + LRU_ROWS, D_RNN), F32),
                        pltpu.VMEM((8, D_RNN), F32),
                        pltpu.VMEM((LRU_ROWS, D_RNN), F32),
                        pltpu.VMEM((LRU_ROWS, D_RNN), F32),
                        pltpu.VMEM((LRU_ROWS, D_RNN), F32)],
        compiler_params=_params(("arbitrary", "arbitrary"), 40),
        name="lru",
    )(x_rnn, gelu_cols, conv_w, conv_b.reshape(1, D_RNN), w_a.astype(BF16), b_a.reshape(1, D_RNN),
      w_x.astype(BF16), b_x.reshape(1, D_RNN), lam.reshape(1, D_RNN))


def _sgu_kernel(u_ref, v_ref, g_ref, b_ref, ws_ref, bs_ref, o_ref):
    rows = u_ref.shape[0]
    v = _layer_norm(v_ref[...].astype(F32), g_ref[...], b_ref[...]).astype(BF16)
    t_out = lax.broadcasted_iota(jnp.int32, (CHUNK, CHUNK), 0)
    t_in = lax.broadcasted_iota(jnp.int32, (CHUNK, CHUNK), 1)
    causal = t_in <= t_out
    for g in range(SGU_GROUPS):
        cols = slice(g * SGU_GROUP_DIM, (g + 1) * SGU_GROUP_DIM)
        ws = jnp.where(causal, ws_ref[g], 0.0).astype(BF16)
        bias = bs_ref[:, g:g + 1]
        for c in range(rows // CHUNK):
            rws = slice(c * CHUNK, (c + 1) * CHUNK)
            mixed = jnp.dot(ws, v[rws, cols], preferred_element_type=F32) + bias
            o_ref[rws, cols] = (u_ref[rws, cols].astype(F32) * mixed).astype(BF16)


def _sgu(gelu_cols, ln_g, ln_b, w_s, b_s):
    T = gelu_cols.shape[0]
    row = lambda c: pl.BlockSpec((SGU_ROWS, D_SGU), lambda i: (i, c))
    return pl.pallas_call(
        _sgu_kernel,
        grid=(T // SGU_ROWS,),
        in_specs=[row(1), row(2), _const_spec((1, D_SGU)), _const_spec((1, D_SGU)),
                  _const_spec((SGU_GROUPS, CHUNK, CHUNK)), _const_spec((CHUNK, SGU_GROUPS))],
        out_specs=pl.BlockSpec((SGU_ROWS, D_SGU), lambda i: (i, 0)),
        out_shape=jax.ShapeDtypeStruct((T, D_SGU), BF16),
        compiler_params=_params(("parallel",), 40),
        name="sgu",
    )(gelu_cols, gelu_cols, ln_g.reshape(1, D_SGU), ln_b.reshape(1, D_SGU), w_s, b_s.T)


def _mix_out_kernel(alpha, a_ref, b_ref, sga_ref, sgb_ref, x_ref, woa_ref, wob_ref, wout_ref,
                    g_ref, beta_ref, cast_src_ref, o_ref, cast_dst_ref):
    ya = jnp.dot(a_ref[...], woa_ref[...], preferred_element_type=F32)
    yb = jnp.dot(b_ref[...], wob_ref[...], preferred_element_type=F32)
    merged = sga_ref[...].astype(F32) * ya + sgb_ref[...].astype(F32) * yb
    m = jnp.dot(merged.astype(BF16), wout_ref[...], preferred_element_type=F32)
    o_ref[...] = _layer_norm(alpha * x_ref[...] + m, g_ref[...], beta_ref[...])
    cast_dst_ref[...] = cast_src_ref[...].astype(BF16)


def _mix_out(alpha, layer, a, b, sigmoid_cols, x, w_o_rnn, w_o_sgu, w_out, ln_g, ln_b, cast):
    T, D = x.shape
    n_steps = T // MIX_ROWS
    row = lambda w, c: pl.BlockSpec((MIX_ROWS, w), lambda i: (i, c))
    cast_src, cast_dst, cast_shape = _cast_specs(cast, layer, n_steps, lambda i: i)
    return pl.pallas_call(
        functools.partial(_mix_out_kernel, alpha),
        grid=(n_steps,),
        in_specs=[row(D_RNN, 0), row(D_SGU, 0), row(D, 0), row(D, 1), row(D, 0),
                  _layer_spec(layer, (D_RNN, D)), _layer_spec(layer, (D_SGU, D)),
                  _layer_spec(layer, (D, D)), _const_spec((1, D)), _const_spec((1, D)), cast_src],
        out_specs=[row(D, 0), cast_dst],
        out_shape=[jax.ShapeDtypeStruct((T, D), F32), cast_shape],
        compiler_params=_params(("parallel",), 56),
        name="mix_out",
    )(a, b, sigmoid_cols, sigmoid_cols, x, w_o_rnn, w_o_sgu, w_out,
      ln_g.reshape(1, D), ln_b.reshape(1, D), cast)


def _first_max(v, row, n):
    m = jnp.max(v, axis=0, keepdims=True)
    idx = jnp.min(jnp.where(v == m, row, float(n)), axis=0, keepdims=True)
    return m, idx


def _router_kernel(x_ref, wt_ref, b_ref, idx_ref, gate_ref, rank_ref, cnt_ref, run_ref):
    rows = x_ref.shape[0]

    @pl.when(pl.program_id(0) == 0)
    def _():
        run_ref[...] = jnp.zeros_like(run_ref)

    def nt_dot(w, x):
        return lax.dot_general(w, x, (((1,), (1,)), ((), ())), preferred_element_type=F32)

    x = x_ref[...]
    xh = x.astype(BF16)
    xl = (x - xh.astype(F32)).astype(BF16)
    w = wt_ref[...]
    wh = w.astype(BF16)
    wl = (w - wh.astype(F32)).astype(BF16)
    logits = nt_dot(wh, xh) + (nt_dot(wh, xl) + nt_dot(wl, xh))
    e = jnp.exp(logits - jnp.max(logits, axis=0, keepdims=True))
    scores = e / jnp.sum(e, axis=0, keepdims=True)
    sel = scores + b_ref[...]

    row = lax.broadcasted_iota(jnp.int32, (N_EXPERTS, rows), 0).astype(F32)
    grow = lax.broadcasted_iota(jnp.int32, (EXPERTS_PER_GROUP, rows), 0).astype(F32)
    neg_inf = float("-inf")
    best_score = None
    best_group = None
    for g in range(N_GROUPS):
        v = sel[g * EXPERTS_PER_GROUP:(g + 1) * EXPERTS_PER_GROUP, :]
        m1, i1 = _first_max(v, grow, EXPERTS_PER_GROUP)
        m2 = jnp.max(jnp.where(grow == i1, neg_inf, v), axis=0, keepdims=True)
        s = m1 + m2
        if g == 0:
            best_score, best_group = s, jnp.zeros_like(s)
        else:
            better = s > best_score
            best_group = jnp.where(better, float(g), best_group)
            best_score = jnp.where(better, s, best_score)

    lo = best_group * float(EXPERTS_PER_GROUP)
    in_group = jnp.logical_and(row >= lo, row < lo + float(EXPERTS_PER_GROUP))
    masked = jnp.where(in_group, sel, neg_inf)
    _, i1 = _first_max(masked, row, N_EXPERTS)
    pick1 = row == i1
    _, i2 = _first_max(jnp.where(pick1, neg_inf, masked), row, N_EXPERTS)
    pick2 = row == i2
    s1 = jnp.sum(jnp.where(pick1, scores, 0.0), axis=0, keepdims=True)
    s2 = jnp.sum(jnp.where(pick2, scores, 0.0), axis=0, keepdims=True)
    den = s1 + s2

    onehot = jnp.where(jnp.logical_or(pick1, pick2), 1.0, 0.0)
    before = (lax.broadcasted_iota(jnp.int32, (rows, rows), 0)
              < lax.broadcasted_iota(jnp.int32, (rows, rows), 1))
    prefix = jnp.dot(onehot.astype(BF16), jnp.where(before, 1.0, 0.0).astype(BF16),
                     preferred_element_type=F32)
    pos = prefix + run_ref[:, 0:1]
    r1 = jnp.sum(jnp.where(pick1, pos, 0.0), axis=0, keepdims=True)
    r2 = jnp.sum(jnp.where(pick2, pos, 0.0), axis=0, keepdims=True)
    run_ref[...] = run_ref[...] + jnp.sum(onehot, axis=1, keepdims=True)

    idx_ref[0:1, :] = i1.astype(jnp.int32)
    idx_ref[1:2, :] = i2.astype(jnp.int32)
    gate_ref[0:1, :] = s1 / den
    gate_ref[1:2, :] = s2 / den
    rank_ref[0:1, :] = r1.astype(jnp.int32)
    rank_ref[1:2, :] = r2.astype(jnp.int32)
    cnt_ref[...] = run_ref[...].astype(jnp.int32)


def _router(x, router_w, router_b):
    T, D = x.shape
    pair = pl.BlockSpec((TOP_K, ROUTER_ROWS), lambda i: (0, i))
    return pl.pallas_call(
        _router_kernel,
        grid=(T // ROUTER_ROWS,),
        in_specs=[pl.BlockSpec((ROUTER_ROWS, D), lambda i: (i, 0)),
                  _const_spec((N_EXPERTS, D)), _const_spec((N_EXPERTS, 1))],
        out_specs=[pair, pair, pair, pl.BlockSpec((N_EXPERTS, 128), lambda i: (0, 0))],
        out_shape=[jax.ShapeDtypeStruct((TOP_K, T), jnp.int32),
                   jax.ShapeDtypeStruct((TOP_K, T), F32),
                   jax.ShapeDtypeStruct((TOP_K, T), jnp.int32),
                   jax.ShapeDtypeStruct((N_EXPERTS, 128), jnp.int32)],
        scratch_shapes=[pltpu.VMEM((N_EXPERTS, 128), F32)],
        compiler_params=_params(("arbitrary",), 40),
        name="router",
    )(x, router_w.T, router_b.reshape(N_EXPERTS, 1))


def _dispatch_kernel(pad_from_ref, pad_n_ref, n_used_ref, dest_ref, x_ref, xs_hbm, zero_ref, sem, pad_sem):
    i = pl.program_id(0)
    rows = DISPATCH_ROWS
    n_blocks = xs_hbm.shape[0] // EXPERT_ROWS
    zero_rows = zero_ref.shape[0]

    def tail_copies(j):
        block = n_used_ref[0] + j
        off = pl.multiple_of(block * EXPERT_ROWS, EXPERT_ROWS)
        return block < n_blocks, [
            pltpu.make_async_copy(zero_ref, xs_hbm.at[pl.ds(off + part * zero_rows, zero_rows)], pad_sem)
            for part in range(EXPERT_ROWS // zero_rows)]

    def pad_copy(e, bit):
        n = pad_n_ref[e]
        if bit < ROW_TILE:
            off = pad_from_ref[e] + bit - 1
            return pltpu.make_async_copy(zero_ref.at[pl.ds(0, 1)], xs_hbm.at[pl.ds(off, 1)], pad_sem)
        done = (n & (ROW_TILE - 1)) + (n & ~(2 * bit - 1))
        off = pl.multiple_of(pad_from_ref[e] + done, ROW_TILE)
        return pltpu.make_async_copy(zero_ref.at[pl.ds(0, bit)], xs_hbm.at[pl.ds(off, bit)], pad_sem)

    def pad_needed(e, bit):
        n = pad_n_ref[e]
        if bit < ROW_TILE:
            return bit <= (n & (ROW_TILE - 1))
        return (n & bit) != 0

    bits = list(range(1, ROW_TILE)) + [
        1 << k for k in range(ROW_TILE.bit_length() - 1, EXPERT_ROWS.bit_length() - 1)]

    @pl.when(i == 0)
    def _():
        zero_ref[...] = jnp.zeros_like(zero_ref)
        for e in range(N_EXPERTS):
            for bit in bits:
                @pl.when(pad_needed(e, bit))
                def _():
                    pad_copy(e, bit).start()
        for j in range(N_EXPERTS):
            needed, copies = tail_copies(j)

            @pl.when(needed)
            def _():
                for c in copies:
                    c.start()

    for t in range(rows):
        src = x_ref.at[pl.ds(t, 1)]
        for k in range(TOP_K):
            pltpu.make_async_copy(src, xs_hbm.at[pl.ds(dest_ref[0, k * rows + t], 1)],
                                  sem).start(priority=k % N_DMA_PRIORITIES)
    for _ in range(TOP_K):
        pltpu.make_async_copy(x_ref, xs_hbm.at[pl.ds(0, rows)], sem).wait()

    @pl.when(i == 0)
    def _():
        for e in range(N_EXPERTS):
            for bit in bits:
                @pl.when(pad_needed(e, bit))
                def _():
                    pad_copy(e, bit).wait()
        for j in range(N_EXPERTS):
            needed, copies = tail_copies(j)

            @pl.when(needed)
            def _():
                for c in copies:
                    c.wait()


def _dispatch(x, dest_tiles, pad_from, pad_n, n_used, n_slots):
    T, D = x.shape
    grid_spec = pltpu.PrefetchScalarGridSpec(
        num_scalar_prefetch=3,
        grid=(T // DISPATCH_ROWS,),
        in_specs=[pl.BlockSpec((None, 1, TOP_K * DISPATCH_ROWS), lambda i, *_: (i, 0, 0),
                               memory_space=pltpu.SMEM),
                  pl.BlockSpec((DISPATCH_ROWS, D), lambda i, *_: (i, 0))],
        out_specs=pl.BlockSpec(memory_space=pl.ANY),
        scratch_shapes=[pltpu.VMEM((EXPERT_ROWS // 2, D), F32),
                        pltpu.SemaphoreType.DMA(()), pltpu.SemaphoreType.DMA(())],
    )
    return pl.pallas_call(
        _dispatch_kernel,
        grid_spec=grid_spec,
        out_shape=jax.ShapeDtypeStruct((n_slots, D), F32),
        compiler_params=_params(("arbitrary",), 16),
        name="dispatch",
    )(pad_from, pad_n, n_used, dest_tiles, x)


def _expert_kernel(block_e_ref, n_used_ref, xs_ref, w1_ref, w3_ref, w2_ref, o_ref):
    used = pl.program_id(0) < n_used_ref[0]

    @pl.when(jnp.logical_not(used))
    def _():
        o_ref[...] = jnp.zeros_like(o_ref)

    @pl.when(used)
    def _():
        x = xs_ref[...].astype(BF16)
        h1 = jnp.dot(x, w1_ref[...], preferred_element_type=F32)
        h3 = jnp.dot(x, w3_ref[...], preferred_element_type=F32)
        hidden = (h1 * _sigmoid(h1)) * h3
        o_ref[...] = jnp.dot(hidden.astype(BF16), w2_ref[...], preferred_element_type=F32)


def _experts(xs, block_e, n_used, w1, w3, w2):
    P, D = xs.shape
    n_blocks = P // EXPERT_ROWS
    used_rows = pl.BlockSpec((EXPERT_ROWS, D),
                             lambda i, be, nu: (jnp.maximum(jnp.minimum(i, nu[0] - 1), 0), 0))
    rows = pl.BlockSpec((EXPERT_ROWS, D), lambda i, be, nu: (i, 0))
    expert_slab = lambda i, be, nu: (be[i], 0, 0)
    grid_spec = pltpu.PrefetchScalarGridSpec(
        num_scalar_prefetch=2,
        grid=(n_blocks,),
        in_specs=[used_rows,
                  pl.BlockSpec((None, D, D_EXPERT), expert_slab),
                  pl.BlockSpec((None, D, D_EXPERT), expert_slab),
                  pl.BlockSpec((None, D_EXPERT, D), expert_slab)],
        out_specs=rows,
    )
    return pl.pallas_call(
        _expert_kernel,
        grid_spec=grid_spec,
        out_shape=jax.ShapeDtypeStruct((P, D), F32),
        compiler_params=_params(("arbitrary",), 56),
        name="experts",
    )(block_e, n_used, xs, w1, w3, w2)


def _combine_kernel(alpha, dest_ref, dest_next_ref, x_ref, gate_ref, g_ref, b_ref, y_hbm,
                    of_ref, ob_ref, buf_a, buf_b, sem):
    rows = COMBINE_ROWS
    half = rows // 2
    i = pl.program_id(0)

    def copy(table_ref, h, t, k, buf, s):
        return pltpu.make_async_copy(y_hbm.at[pl.ds(table_ref[0, k * rows + h * half + t], 1)],
                                     buf.at[k, pl.ds(t, 1)], sem.at[s])

    def gather(table_ref, h, buf, s):
        for t in range(half):
            for k in range(TOP_K):
                copy(table_ref, h, t, k, buf, s).start(priority=k % N_DMA_PRIORITIES)

    def wait(buf, s):
        for k in range(TOP_K):
            pltpu.make_async_copy(y_hbm.at[pl.ds(0, half)], buf.at[k], sem.at[s]).wait()

    def reduce(buf, h):
        r = slice(h * half, (h + 1) * half)
        f = gate_ref[r, 0:1] * buf[0] + gate_ref[r, 1:2] * buf[1]
        y = _layer_norm(alpha * x_ref[r, :] + f, g_ref[...], b_ref[...])
        of_ref[r, :] = y
        ob_ref[r, :] = y.astype(BF16)

    @pl.when(i == 0)
    def _():
        def issue(t, carry):
            for k in range(TOP_K):
                copy(dest_ref, 0, t, k, buf_a, 0).start()
            return carry

        lax.fori_loop(0, half, issue, 0, unroll=8)

    wait(buf_a, 0)
    gather(dest_ref, 1, buf_b, 1)
    reduce(buf_a, 0)
    wait(buf_b, 1)
    gather(dest_next_ref, 0, buf_a, 0)
    reduce(buf_b, 1)

    @pl.when(i == pl.num_programs(0) - 1)
    def _():
        wait(buf_a, 0)


def _combine(alpha, dest_tiles, x, gate_t, ln_g, ln_b, yb):
    T, D = x.shape
    n_tiles = T // COMBINE_ROWS
    row = pl.BlockSpec((COMBINE_ROWS, D), lambda i: (i, 0))
    table = lambda index: pl.BlockSpec((None, 1, TOP_K * COMBINE_ROWS), index, memory_space=pltpu.SMEM)
    half_buf = pltpu.VMEM((TOP_K, COMBINE_ROWS // 2, D), F32)
    return pl.pallas_call(
        functools.partial(_combine_kernel, alpha),
        grid=(n_tiles,),
        in_specs=[table(lambda i: (i, 0, 0)),
                  table(lambda i: (jnp.minimum(i + 1, n_tiles - 1), 0, 0)),
                  row, pl.BlockSpec((COMBINE_ROWS, TOP_K), lambda i: (i, 0)),
                  _const_spec((1, D)), _const_spec((1, D)),
                  pl.BlockSpec(memory_space=pl.ANY)],
        out_specs=[row, row],
        out_shape=[jax.ShapeDtypeStruct((T, D), F32), jax.ShapeDtypeStruct((T, D), BF16)],
        scratch_shapes=[half_buf, half_buf, pltpu.SemaphoreType.DMA((2,))],
        compiler_params=_params(("arbitrary",), 48),
        name="combine",
    )(dest_tiles, dest_tiles, x, gate_t, ln_g.reshape(1, D), ln_b.reshape(1, D), yb)


def _tile_pairs(dest, rows):
    T = dest.shape[1]
    return dest.reshape(TOP_K, T // rows, rows).transpose(1, 0, 2).reshape(T // rows, 1, TOP_K * rows)


def _moe(alpha, x, router_w, router_b, w1, w3, w2, ln_g, ln_b):
    T, D = x.shape
    n_blocks = (T * TOP_K) // EXPERT_ROWS + N_EXPERTS
    idx, gate, rank, cnt = _router(x, router_w, router_b)

    counts = cnt[:, 0]
    padded = (counts + EXPERT_ROWS - 1) // EXPERT_ROWS * EXPERT_ROWS
    pad_end = jnp.cumsum(padded)
    pad_start = pad_end - padded
    n_used = (pad_end[-1] // EXPERT_ROWS).astype(jnp.int32)
    block_start = jnp.minimum(jnp.arange(n_blocks, dtype=jnp.int32), n_used - 1) * EXPERT_ROWS
    block_e = jnp.minimum(jnp.sum(block_start[:, None] >= pad_end[None, :], axis=1),
                          N_EXPERTS - 1).astype(jnp.int32)
    expert_ids = jnp.arange(N_EXPERTS, dtype=jnp.int32)[:, None, None]
    dest = jnp.sum(jnp.where(idx[None] == expert_ids, pad_start[:, None, None], 0), axis=0) + rank

    n_used = n_used.reshape(1)
    xs = _dispatch(x, _tile_pairs(dest, DISPATCH_ROWS), (pad_start + counts).astype(jnp.int32),
                   (padded - counts).astype(jnp.int32), n_used, n_blocks * EXPERT_ROWS)
    yb = _experts(xs, block_e, n_used, w1, w3, w2)
    return _combine(alpha, _tile_pairs(dest, COMBINE_ROWS), x, gate.T, ln_g, ln_b, yb)


def kernel(x, emb_ln_g, emb_ln_b, w_in, conv_w, conv_b, lru_w_a, lru_b_a, lru_w_x, lru_b_x, lru_lambda, w_o_rnn, sgu_ln_g, sgu_ln_b, sgu_w_s, sgu_b_s, w_o_sgu, w_out, ln1_g, ln1_b, router_w, router_b, expert_w1, expert_w3, expert_w2, ln2_g, ln2_b):
    batch, seq, D = x.shape
    depth = w_in.shape[0]
    alpha = float((2 * depth) ** 0.25)
    xf, xb = _emb_ln(x.reshape(batch * seq, D), emb_ln_g, emb_ln_b)
    w_in, w_o_rnn, w_o_sgu, w_out = (w.astype(BF16) for w in (w_in, w_o_rnn, w_o_sgu, w_out))
    rows_of = lambda w: w.reshape(depth, N_EXPERTS * w.shape[2], w.shape[3])
    w1_rows, w3_rows, w2_rows = rows_of(expert_w1), rows_of(expert_w3), rows_of(expert_w2)
    gelu_from = D_RNN
    sigmoid_from = 2 * D_RNN + 2 * D_SGU
    for l in range(depth):
        x_rnn = _proj(xb, w_in, l, 0, gelu_from, lambda v: v, "proj_x")
        gelu_cols, w3 = _proj(xb, w_in, l, gelu_from, sigmoid_from - gelu_from, _gelu, "proj_gelu",
                              cast=w3_rows)
        sigmoid_cols, w1 = _proj(xb, w_in, l, sigmoid_from, D_IN - sigmoid_from, _sigmoid,
                                 "proj_sigmoid", cast=w1_rows)
        a = _lru(x_rnn, gelu_cols, batch, seq, conv_w[l], conv_b[l], lru_w_a[l], lru_b_a[l],
                 lru_w_x[l], lru_b_x[l], lru_lambda[l])
        b = _sgu(gelu_cols, sgu_ln_g[l], sgu_ln_b[l], sgu_w_s[l], sgu_b_s[l])
        x1, w2 = _mix_out(alpha, l, a, b, sigmoid_cols, xf, w_o_rnn, w_o_sgu, w_out,
                          ln1_g[l], ln1_b[l], cast=w2_rows)
        xf, xb = _moe(alpha, x1, router_w, router_b, w1.reshape(expert_w1.shape[1:]),
                      w3.reshape(expert_w3.shape[1:]), w2.reshape(expert_w2.shape[1:]),
                      ln2_g[l], ln2_b[l])
    return xf.reshape(batch, seq, D)
```

```python
import functools

import jax
import jax.numpy as jnp
import numpy as np
from jax import lax
from jax.experimental import pallas as pl
from jax.experimental.pallas import tpu as pltpu

F32 = jnp.float32
BF16 = jnp.bfloat16

D_MODEL = 2048
D_RNN = 1024
RNN_HEADS = 8
HEAD_DIM = D_RNN // RNN_HEADS
CONV_WIDTH = 4
LRU_C = 8.0
D_SGU = 1024
SGU_GROUPS = 8
SGU_GROUP_DIM = D_SGU // SGU_GROUPS
CHUNK = 128
N_EXPERTS = 16
N_GROUPS = 4
EXPERTS_PER_GROUP = N_EXPERTS // N_GROUPS
TOP_K = 2
D_EXPERT = 1408
D_IN = 2 * D_RNN + 2 * D_SGU + 2 * D_MODEL
LN_EPS = 1e-5
SQRT_2_OVER_PI = float(np.sqrt(2.0 / np.pi))

LN_ROWS = 512
PROJ_ROWS = 1024
PROJ_COLS = 1024
LRU_ROWS = 512
SGU_ROWS = 512
MIX_ROWS = 256
ROUTER_ROWS = 512
DISPATCH_ROWS = 512
EXPERT_ROWS = 256
COMBINE_ROWS = 512
CONV_PAD = 8
N_DMA_PRIORITIES = 2
ROW_TILE = 8

MIB = 1024 * 1024


def _params(semantics, vmem_mib):
    return pltpu.CompilerParams(dimension_semantics=semantics,
                                vmem_limit_bytes=vmem_mib * MIB,
                                disable_bounds_checks=True)


def _const_spec(shape):
    zeros = (0,) * len(shape)
    return pl.BlockSpec(shape, lambda *_: zeros, pipeline_mode=pl.Buffered(1))


def _layer_spec(layer, shape):
    zeros = (0,) * len(shape)
    return pl.BlockSpec((None,) + tuple(shape), lambda *_: (layer,) + zeros,
                        pipeline_mode=pl.Buffered(1))


def _layer_norm(v, g, b):
    mu = jnp.mean(v, axis=-1, keepdims=True)
    d = v - mu
    var = jnp.mean(d * d, axis=-1, keepdims=True)
    return d * lax.rsqrt(var + LN_EPS) * g + b


def _gelu(x):
    inner = x * (SQRT_2_OVER_PI + (SQRT_2_OVER_PI * 0.044715) * (x * x))
    return x * (0.5 * jnp.tanh(inner) + 0.5)


def _pack_halves(y):
    n = y.shape[1] // 2
    bits = lambda v: lax.bitcast_convert_type(v.astype(BF16).astype(F32), jnp.uint32)
    return (bits(y[:, :n]) >> 16) | bits(y[:, n:])


def _unpack_halves(p):
    lo = lax.bitcast_convert_type(p << 16, F32)
    hi = lax.bitcast_convert_type(p & jnp.uint32(0xFFFF0000), F32)
    return lo, hi


def _sigmoid(x):
    return 0.5 * jnp.tanh(0.5 * x) + 0.5


def _emb_ln_kernel(x_ref, g_ref, b_ref, of_ref, ob_ref):
    y = _layer_norm(x_ref[...], g_ref[...], b_ref[...])
    of_ref[...] = y
    ob_ref[...] = y.astype(BF16)


def _emb_ln(x, g, b):
    T, D = x.shape
    row = pl.BlockSpec((LN_ROWS, D), lambda i: (i, 0))
    return pl.pallas_call(
        _emb_ln_kernel,
        grid=(T // LN_ROWS,),
        in_specs=[row, _const_spec((1, D)), _const_spec((1, D))],
        out_specs=[row, row],
        out_shape=[jax.ShapeDtypeStruct((T, D), F32), jax.ShapeDtypeStruct((T, D), BF16)],
        compiler_params=_params(("parallel",), 40),
        name="emb_ln",
    )(x, g.reshape(1, D), b.reshape(1, D))


def _cast_specs(stack, layer, n_steps, step_of):
    _, rows, cols = stack.shape
    slab = -(-rows // (n_steps * 16)) * 16
    src = pl.BlockSpec((None, slab, cols), lambda *g: (layer, step_of(*g), 0))
    dst = pl.BlockSpec((slab, cols), lambda *g: (step_of(*g), 0))
    return src, dst, jax.ShapeDtypeStruct((rows, cols), BF16)


def _proj_kernel(act, x_ref, w_ref, *rest):
    o_ref = rest[len(rest) // 2]
    acc = jnp.dot(x_ref[...], w_ref[...], preferred_element_type=F32)
    o_ref[...] = act(acc).astype(BF16)
    for src_ref, dst_ref in zip(rest[:len(rest) // 2], rest[len(rest) // 2 + 1:]):
        dst_ref[...] = src_ref[...].astype(BF16)


def _proj(xb, w_in, layer, col_from, n_cols, act, name, cast=None):
    T, D = xb.shape
    first = col_from // PROJ_COLS
    grid = (T // PROJ_ROWS, n_cols // PROJ_COLS)
    in_specs = [pl.BlockSpec((PROJ_ROWS, D), lambda i, j: (i, 0)),
                pl.BlockSpec((None, D, PROJ_COLS), lambda i, j: (layer, 0, first + j))]
    out_specs = [pl.BlockSpec((PROJ_ROWS, PROJ_COLS), lambda i, j: (i, j))]
    out_shape = [jax.ShapeDtypeStruct((T, n_cols), BF16)]
    operands = [xb, w_in]
    if cast is not None:
        src, dst, shape = _cast_specs(cast, layer, grid[0] * grid[1], lambda i, j: i * grid[1] + j)
        in_specs.append(src)
        out_specs.append(dst)
        out_shape.append(shape)
        operands.append(cast)
    out = pl.pallas_call(
        functools.partial(_proj_kernel, act),
        grid=grid,
        in_specs=in_specs,
        out_specs=out_specs,
        out_shape=out_shape,
        compiler_params=_params(("parallel", "arbitrary"), 48),
        name=name,
    )(*operands)
    return out[0] if cast is None else out


def _lru_kernel(xr_ref, gate_ref, cw_ref, cb_ref, wa_ref, ba_ref, wx_ref, bx_ref, lam_ref,
                o_ref, xpad_ref, hc_ref, a_ref, u_ref, h_ref):
    rows = xr_ref.shape[0]

    @pl.when(pl.program_id(1) == 0)
    def _():
        xpad_ref[0:CONV_PAD, :] = jnp.zeros((CONV_PAD, D_RNN), F32)
        hc_ref[...] = jnp.zeros_like(hc_ref)

    x = xr_ref[...].astype(F32)
    xpad_ref[CONV_PAD:CONV_PAD + rows, :] = x
    xc = cb_ref[...] + cw_ref[CONV_WIDTH - 1:CONV_WIDTH, :] * x
    for k in range(CONV_WIDTH - 1):
        shift = CONV_WIDTH - 1 - k
        xc = xc + cw_ref[k:k + 1, :] * xpad_ref[CONV_PAD - shift:CONV_PAD - shift + rows, :]
    xpad_ref[0:CONV_PAD, :] = x[rows - CONV_PAD:rows, :]

    lam = lam_ref[...]
    sp = jnp.maximum(-lam, 0.0) + jnp.log1p(jnp.exp(-jnp.abs(lam)))
    for h in range(RNN_HEADS):
        cols = slice(h * HEAD_DIM, (h + 1) * HEAD_DIM)
        xh = xc[:, cols]
        xhb = xh.astype(BF16)
        r = _sigmoid(jnp.dot(xhb, wa_ref[h], preferred_element_type=F32) + ba_ref[:, cols])
        gi = _sigmoid(jnp.dot(xhb, wx_ref[h], preferred_element_type=F32) + bx_ref[:, cols])
        log_a = (-LRU_C) * r * sp[:, cols]
        a_ref[:, cols] = jnp.exp(log_a)
        th = jnp.tanh(log_a)
        u_ref[:, cols] = jnp.sqrt(-2.0 * th / (1.0 - th)) * (gi * xh)

    def step(t, h):
        h = a_ref[pl.ds(t, 1), :] * h + u_ref[pl.ds(t, 1), :]
        h_ref[pl.ds(t, 1), :] = h
        return h

    hc_ref[0:1, :] = lax.fori_loop(0, rows, step, hc_ref[0:1, :], unroll=8)
    o_ref[...] = (h_ref[...] * gate_ref[...].astype(F32)).astype(BF16)


def _lru(x_rnn, gelu_cols, batch, seq, conv_w, conv_b, w_a, b_a, w_x, b_x, lam):
    T = x_rnn.shape[0]
    tiles = seq // LRU_ROWS
    row = pl.BlockSpec((LRU_ROWS, D_RNN), lambda b, s: (b * tiles + s, 0))
    vec = _const_spec((1, D_RNN))
    gate_w = _const_spec((RNN_HEADS, HEAD_DIM, HEAD_DIM))
    return pl.pallas_call(
        _lru_kernel,
        grid=(batch, tiles),
        in_specs=[row, row, _const_spec((CONV_WIDTH, D_RNN)), vec, gate_w, vec, gate_w, vec, vec],
        out_specs=pl.BlockSpec((LRU_ROWS, D_RNN), lambda b, s: (b * tiles + s, 0)),
        out_shape=jax.ShapeDtypeStruct((T, D_RNN), BF16),
        scratch_shapes=[pltpu.VMEM((CONV_PAD + LRU_ROWS, D_RNN), F32),
                        pltpu.VMEM((8, D_RNN), F32),
                        pltpu.VMEM((LRU_ROWS, D_RNN), F32),
                        pltpu.VMEM((LRU_ROWS, D_RNN), F32),
                        pltpu.VMEM((LRU_ROWS, D_RNN), F32)],
        compiler_params=_params(("arbitrary", "arbitrary"), 40),
        name="lru",
    )(x_rnn, gelu_cols, conv_w, conv_b.reshape(1, D_RNN), w_a.astype(BF16), b_a.reshape(1, D_RNN),
      w_x.astype(BF16), b_x.reshape(1, D_RNN), lam.reshape(1, D_RNN))


def _sgu_kernel(u_ref, v_ref, g_ref, b_ref, ws_ref, bs_ref, o_ref):
    rows = u_ref.shape[0]
    v = _layer_norm(v_ref[...].astype(F32), g_ref[...], b_ref[...]).astype(BF16)
    t_out = lax.broadcasted_iota(jnp.int32, (CHUNK, CHUNK), 0)
    t_in = lax.broadcasted_iota(jnp.int32, (CHUNK, CHUNK), 1)
    causal = t_in <= t_out
    for g in range(SGU_GROUPS):
        cols = slice(g * SGU_GROUP_DIM, (g + 1) * SGU_GROUP_DIM)
        ws = jnp.where(causal, ws_ref[g], 0.0).astype(BF16)
        bias = bs_ref[:, g:g + 1]
        for c in range(rows // CHUNK):
            rws = slice(c * CHUNK, (c + 1) * CHUNK)
            mixed = jnp.dot(ws, v[rws, cols], preferred_element_type=F32) + bias
            o_ref[rws, cols] = (u_ref[rws, cols].astype(F32) * mixed).astype(BF16)


def _sgu(gelu_cols, ln_g, ln_b, w_s, b_s):
    T = gelu_cols.shape[0]
    row = lambda c: pl.BlockSpec((SGU_ROWS, D_SGU), lambda i: (i, c))
    return pl.pallas_call(
        _sgu_kernel,
        grid=(T // SGU_ROWS,),
        in_specs=[row(1), row(2), _const_spec((1, D_SGU)), _const_spec((1, D_SGU)),
                  _const_spec((SGU_GROUPS, CHUNK, CHUNK)), _const_spec((CHUNK, SGU_GROUPS))],
        out_specs=pl.BlockSpec((SGU_ROWS, D_SGU), lambda i: (i, 0)),
        out_shape=jax.ShapeDtypeStruct((T, D_SGU), BF16),
        compiler_params=_params(("parallel",), 40),
        name="sgu",
    )(gelu_cols, gelu_cols, ln_g.reshape(1, D_SGU), ln_b.reshape(1, D_SGU), w_s, b_s.T)


def _mix_out_kernel(alpha, a_ref, b_ref, sga_ref, sgb_ref, x_ref, woa_ref, wob_ref, wout_ref,
                    g_ref, beta_ref, cast_src_ref, o_ref, packed_ref, cast_dst_ref):
    ya = jnp.dot(a_ref[...], woa_ref[...], preferred_element_type=F32)
    yb = jnp.dot(b_ref[...], wob_ref[...], preferred_element_type=F32)
    merged = sga_ref[...].astype(F32) * ya + sgb_ref[...].astype(F32) * yb
    m = jnp.dot(merged.astype(BF16), wout_ref[...], preferred_element_type=F32)
    y = _layer_norm(alpha * x_ref[...] + m, g_ref[...], beta_ref[...])
    o_ref[...] = y
    packed_ref[...] = _pack_halves(y)
    cast_dst_ref[...] = cast_src_ref[...].astype(BF16)


def _mix_out(alpha, layer, a, b, sigmoid_cols, x, w_o_rnn, w_o_sgu, w_out, ln_g, ln_b, cast):
    T, D = x.shape
    n_steps = T // MIX_ROWS
    row = lambda w, c: pl.BlockSpec((MIX_ROWS, w), lambda i: (i, c))
    cast_src, cast_dst, cast_shape = _cast_specs(cast, layer, n_steps, lambda i: i)
    return pl.pallas_call(
        functools.partial(_mix_out_kernel, alpha),
        grid=(n_steps,),
        in_specs=[row(D_RNN, 0), row(D_SGU, 0), row(D, 0), row(D, 1), row(D, 0),
                  _layer_spec(layer, (D_RNN, D)), _layer_spec(layer, (D_SGU, D)),
                  _layer_spec(layer, (D, D)), _const_spec((1, D)), _const_spec((1, D)), cast_src],
        out_specs=[row(D, 0), row(D // 2, 0), cast_dst],
        out_shape=[jax.ShapeDtypeStruct((T, D), F32), jax.ShapeDtypeStruct((T, D // 2), jnp.uint32),
                   cast_shape],
        compiler_params=_params(("parallel",), 56),
        name="mix_out",
    )(a, b, sigmoid_cols, sigmoid_cols, x, w_o_rnn, w_o_sgu, w_out,
      ln_g.reshape(1, D), ln_b.reshape(1, D), cast)


def _first_max(v, row, n):
    m = jnp.max(v, axis=0, keepdims=True)
    idx = jnp.min(jnp.where(v == m, row, float(n)), axis=0, keepdims=True)
    return m, idx


def _router_kernel(x_ref, wt_ref, b_ref, idx_ref, gate_ref, rank_ref, cnt_ref, run_ref):
    rows = x_ref.shape[0]

    @pl.when(pl.program_id(0) == 0)
    def _():
        run_ref[...] = jnp.zeros_like(run_ref)

    def nt_dot(w, x):
        return lax.dot_general(w, x, (((1,), (1,)), ((), ())), preferred_element_type=F32)

    x = x_ref[...]
    xh = x.astype(BF16)
    xl = (x - xh.astype(F32)).astype(BF16)
    w = wt_ref[...]
    wh = w.astype(BF16)
    wl = (w - wh.astype(F32)).astype(BF16)
    logits = nt_dot(wh, xh) + (nt_dot(wh, xl) + nt_dot(wl, xh))
    e = jnp.exp(logits - jnp.max(logits, axis=0, keepdims=True))
    scores = e / jnp.sum(e, axis=0, keepdims=True)
    sel = scores + b_ref[...]

    row = lax.broadcasted_iota(jnp.int32, (N_EXPERTS, rows), 0).astype(F32)
    grow = lax.broadcasted_iota(jnp.int32, (EXPERTS_PER_GROUP, rows), 0).astype(F32)
    neg_inf = float("-inf")
    best_score = None
    best_group = None
    for g in range(N_GROUPS):
        v = sel[g * EXPERTS_PER_GROUP:(g + 1) * EXPERTS_PER_GROUP, :]
        m1, i1 = _first_max(v, grow, EXPERTS_PER_GROUP)
        m2 = jnp.max(jnp.where(grow == i1, neg_inf, v), axis=0, keepdims=True)
        s = m1 + m2
        if g == 0:
            best_score, best_group = s, jnp.zeros_like(s)
        else:
            better = s > best_score
            best_group = jnp.where(better, float(g), best_group)
            best_score = jnp.where(better, s, best_score)

    lo = best_group * float(EXPERTS_PER_GROUP)
    in_group = jnp.logical_and(row >= lo, row < lo + float(EXPERTS_PER_GROUP))
    masked = jnp.where(in_group, sel, neg_inf)
    _, i1 = _first_max(masked, row, N_EXPERTS)
    pick1 = row == i1
    _, i2 = _first_max(jnp.where(pick1, neg_inf, masked), row, N_EXPERTS)
    pick2 = row == i2
    s1 = jnp.sum(jnp.where(pick1, scores, 0.0), axis=0, keepdims=True)
    s2 = jnp.sum(jnp.where(pick2, scores, 0.0), axis=0, keepdims=True)
    den = s1 + s2

    onehot = jnp.where(jnp.logical_or(pick1, pick2), 1.0, 0.0)
    before = (lax.broadcasted_iota(jnp.int32, (rows, rows), 0)
              < lax.broadcasted_iota(jnp.int32, (rows, rows), 1))
    prefix = jnp.dot(onehot.astype(BF16), jnp.where(before, 1.0, 0.0).astype(BF16),
                     preferred_element_type=F32)
    pos = prefix + run_ref[:, 0:1]
    r1 = jnp.sum(jnp.where(pick1, pos, 0.0), axis=0, keepdims=True)
    r2 = jnp.sum(jnp.where(pick2, pos, 0.0), axis=0, keepdims=True)
    run_ref[...] = run_ref[...] + jnp.sum(onehot, axis=1, keepdims=True)

    idx_ref[0:1, :] = i1.astype(jnp.int32)
    idx_ref[1:2, :] = i2.astype(jnp.int32)
    gate_ref[0:1, :] = s1 / den
    gate_ref[1:2, :] = s2 / den
    rank_ref[0:1, :] = r1.astype(jnp.int32)
    rank_ref[1:2, :] = r2.astype(jnp.int32)
    cnt_ref[...] = run_ref[...].astype(jnp.int32)


def _router(x, router_w, router_b):
    T, D = x.shape
    pair = pl.BlockSpec((TOP_K, ROUTER_ROWS), lambda i: (0, i))
    return pl.pallas_call(
        _router_kernel,
        grid=(T // ROUTER_ROWS,),
        in_specs=[pl.BlockSpec((ROUTER_ROWS, D), lambda i: (i, 0)),
                  _const_spec((N_EXPERTS, D)), _const_spec((N_EXPERTS, 1))],
        out_specs=[pair, pair, pair, pl.BlockSpec((N_EXPERTS, 128), lambda i: (0, 0))],
        out_shape=[jax.ShapeDtypeStruct((TOP_K, T), jnp.int32),
                   jax.ShapeDtypeStruct((TOP_K, T), F32),
                   jax.ShapeDtypeStruct((TOP_K, T), jnp.int32),
                   jax.ShapeDtypeStruct((N_EXPERTS, 128), jnp.int32)],
        scratch_shapes=[pltpu.VMEM((N_EXPERTS, 128), F32)],
        compiler_params=_params(("arbitrary",), 40),
        name="router",
    )(x, router_w.T, router_b.reshape(N_EXPERTS, 1))


def _dispatch_kernel(pad_from_ref, pad_n_ref, n_used_ref, dest_ref, x_ref, xs_hbm, zero_ref, sem, pad_sem):
    i = pl.program_id(0)
    rows = DISPATCH_ROWS
    n_blocks = xs_hbm.shape[0] // EXPERT_ROWS
    zero_rows = zero_ref.shape[0]

    def tail_copies(j):
        block = n_used_ref[0] + j
        off = pl.multiple_of(block * EXPERT_ROWS, EXPERT_ROWS)
        return block < n_blocks, [
            pltpu.make_async_copy(zero_ref, xs_hbm.at[pl.ds(off + part * zero_rows, zero_rows)], pad_sem)
            for part in range(EXPERT_ROWS // zero_rows)]

    def pad_copy(e, bit):
        n = pad_n_ref[e]
        if bit < ROW_TILE:
            off = pad_from_ref[e] + bit - 1
            return pltpu.make_async_copy(zero_ref.at[pl.ds(0, 1)], xs_hbm.at[pl.ds(off, 1)], pad_sem)
        done = (n & (ROW_TILE - 1)) + (n & ~(2 * bit - 1))
        off = pl.multiple_of(pad_from_ref[e] + done, ROW_TILE)
        return pltpu.make_async_copy(zero_ref.at[pl.ds(0, bit)], xs_hbm.at[pl.ds(off, bit)], pad_sem)

    def pad_needed(e, bit):
        n = pad_n_ref[e]
        if bit < ROW_TILE:
            return bit <= (n & (ROW_TILE - 1))
        return (n & bit) != 0

    bits = list(range(1, ROW_TILE)) + [
        1 << k for k in range(ROW_TILE.bit_length() - 1, EXPERT_ROWS.bit_length() - 1)]

    @pl.when(i == 0)
    def _():
        zero_ref[...] = jnp.zeros_like(zero_ref)
        for e in range(N_EXPERTS):
            for bit in bits:
                @pl.when(pad_needed(e, bit))
                def _():
                    pad_copy(e, bit).start()
        for j in range(N_EXPERTS):
            needed, copies = tail_copies(j)

            @pl.when(needed)
            def _():
                for c in copies:
                    c.start()

    for t in range(rows):
        src = x_ref.at[pl.ds(t, 1)]
        for k in range(TOP_K):
            pltpu.make_async_copy(src, xs_hbm.at[pl.ds(dest_ref[0, k * rows + t], 1)],
                                  sem).start(priority=k % N_DMA_PRIORITIES)
    for _ in range(TOP_K):
        pltpu.make_async_copy(x_ref, xs_hbm.at[pl.ds(0, rows)], sem).wait()

    @pl.when(i == 0)
    def _():
        for e in range(N_EXPERTS):
            for bit in bits:
                @pl.when(pad_needed(e, bit))
                def _():
                    pad_copy(e, bit).wait()
        for j in range(N_EXPERTS):
            needed, copies = tail_copies(j)

            @pl.when(needed)
            def _():
                for c in copies:
                    c.wait()


def _dispatch(x, dest_tiles, pad_from, pad_n, n_used, n_slots):
    T, D = x.shape
    grid_spec = pltpu.PrefetchScalarGridSpec(
        num_scalar_prefetch=3,
        grid=(T // DISPATCH_ROWS,),
        in_specs=[pl.BlockSpec((None, 1, TOP_K * DISPATCH_ROWS), lambda i, *_: (i, 0, 0),
                               memory_space=pltpu.SMEM),
                  pl.BlockSpec((DISPATCH_ROWS, D), lambda i, *_: (i, 0))],
        out_specs=pl.BlockSpec(memory_space=pl.ANY),
        scratch_shapes=[pltpu.VMEM((EXPERT_ROWS // 2, D), x.dtype),
                        pltpu.SemaphoreType.DMA(()), pltpu.SemaphoreType.DMA(())],
    )
    return pl.pallas_call(
        _dispatch_kernel,
        grid_spec=grid_spec,
        out_shape=jax.ShapeDtypeStruct((n_slots, D), x.dtype),
        compiler_params=_params(("arbitrary",), 16),
        name="dispatch",
    )(pad_from, pad_n, n_used, dest_tiles, x)


def _expert_kernel(block_e_ref, n_used_ref, xs_ref, w1_ref, w3_ref, w2_ref, o_ref):
    used = pl.program_id(0) < n_used_ref[0]

    @pl.when(jnp.logical_not(used))
    def _():
        o_ref[...] = jnp.zeros_like(o_ref)

    @pl.when(used)
    def _():
        lo, hi = _unpack_halves(xs_ref[...])
        x = jnp.concatenate([lo.astype(BF16), hi.astype(BF16)], axis=1)
        h1 = jnp.dot(x, w1_ref[...], preferred_element_type=F32)
        h3 = jnp.dot(x, w3_ref[...], preferred_element_type=F32)
        hidden = (h1 * _sigmoid(h1)) * h3
        out = jnp.dot(hidden.astype(BF16), w2_ref[...], preferred_element_type=F32)
        o_ref[...] = _pack_halves(out)


def _experts(xs, block_e, n_used, w1, w3, w2):
    P, packed = xs.shape
    D = 2 * packed
    n_blocks = P // EXPERT_ROWS
    used_rows = pl.BlockSpec((EXPERT_ROWS, packed),
                             lambda i, be, nu: (jnp.maximum(jnp.minimum(i, nu[0] - 1), 0), 0))
    rows = pl.BlockSpec((EXPERT_ROWS, packed), lambda i, be, nu: (i, 0))
    expert_slab = lambda i, be, nu: (be[i], 0, 0)
    grid_spec = pltpu.PrefetchScalarGridSpec(
        num_scalar_prefetch=2,
        grid=(n_blocks,),
        in_specs=[used_rows,
                  pl.BlockSpec((None, D, D_EXPERT), expert_slab),
                  pl.BlockSpec((None, D, D_EXPERT), expert_slab),
                  pl.BlockSpec((None, D_EXPERT, D), expert_slab)],
        out_specs=rows,
    )
    return pl.pallas_call(
        _expert_kernel,
        grid_spec=grid_spec,
        out_shape=jax.ShapeDtypeStruct((P, packed), jnp.uint32),
        compiler_params=_params(("arbitrary",), 56),
        name="experts",
    )(block_e, n_used, xs, w1, w3, w2)


def _combine_kernel(alpha, dest_ref, dest_next_ref, x_ref, gate_ref, g_ref, b_ref, y_hbm,
                    of_ref, ob_ref, buf_a, buf_b, sem):
    rows = COMBINE_ROWS
    half = rows // 2
    i = pl.program_id(0)

    def copy(table_ref, h, t, k, buf, s):
        return pltpu.make_async_copy(y_hbm.at[pl.ds(table_ref[0, k * rows + h * half + t], 1)],
                                     buf.at[k, pl.ds(t, 1)], sem.at[s])

    def gather(table_ref, h, buf, s):
        for t in range(half):
            for k in range(TOP_K):
                copy(table_ref, h, t, k, buf, s).start(priority=k % N_DMA_PRIORITIES)

    def wait(buf, s):
        for k in range(TOP_K):
            pltpu.make_async_copy(y_hbm.at[pl.ds(0, half)], buf.at[k], sem.at[s]).wait()

    def reduce(buf, h):
        r = slice(h * half, (h + 1) * half)
        lo0, hi0 = _unpack_halves(buf[0])
        lo1, hi1 = _unpack_halves(buf[1])
        g0 = gate_ref[r, 0:1]
        g1 = gate_ref[r, 1:2]
        f = jnp.concatenate([g0 * lo0 + g1 * lo1, g0 * hi0 + g1 * hi1], axis=1)
        y = _layer_norm(alpha * x_ref[r, :] + f, g_ref[...], b_ref[...])
        of_ref[r, :] = y
        ob_ref[r, :] = y.astype(BF16)

    @pl.when(i == 0)
    def _():
        def issue(t, carry):
            for k in range(TOP_K):
                copy(dest_ref, 0, t, k, buf_a, 0).start()
            return carry

        lax.fori_loop(0, half, issue, 0, unroll=8)

    wait(buf_a, 0)
    gather(dest_ref, 1, buf_b, 1)
    reduce(buf_a, 0)
    wait(buf_b, 1)
    gather(dest_next_ref, 0, buf_a, 0)
    reduce(buf_b, 1)

    @pl.when(i == pl.num_programs(0) - 1)
    def _():
        wait(buf_a, 0)


def _combine(alpha, dest_tiles, x, gate_t, ln_g, ln_b, yb):
    T, D = x.shape
    n_tiles = T // COMBINE_ROWS
    row = pl.BlockSpec((COMBINE_ROWS, D), lambda i: (i, 0))
    table = lambda index: pl.BlockSpec((None, 1, TOP_K * COMBINE_ROWS), index, memory_space=pltpu.SMEM)
    half_buf = pltpu.VMEM((TOP_K, COMBINE_ROWS // 2, yb.shape[1]), yb.dtype)
    return pl.pallas_call(
        functools.partial(_combine_kernel, alpha),
        grid=(n_tiles,),
        in_specs=[table(lambda i: (i, 0, 0)),
                  table(lambda i: (jnp.minimum(i + 1, n_tiles - 1), 0, 0)),
                  row, pl.BlockSpec((COMBINE_ROWS, TOP_K), lambda i: (i, 0)),
                  _const_spec((1, D)), _const_spec((1, D)),
                  pl.BlockSpec(memory_space=pl.ANY)],
        out_specs=[row, row],
        out_shape=[jax.ShapeDtypeStruct((T, D), F32), jax.ShapeDtypeStruct((T, D), BF16)],
        scratch_shapes=[half_buf, half_buf, pltpu.SemaphoreType.DMA((2,))],
        compiler_params=_params(("arbitrary",), 48),
        name="combine",
    )(dest_tiles, dest_tiles, x, gate_t, ln_g.reshape(1, D), ln_b.reshape(1, D), yb)


def _tile_pairs(dest, rows):
    T = dest.shape[1]
    return dest.reshape(TOP_K, T // rows, rows).transpose(1, 0, 2).reshape(T // rows, 1, TOP_K * rows)


def _moe(alpha, x, x_packed, router_w, router_b, w1, w3, w2, ln_g, ln_b):
    T, D = x.shape
    n_blocks = (T * TOP_K) // EXPERT_ROWS + N_EXPERTS
    idx, gate, rank, cnt = _router(x, router_w, router_b)

    counts = cnt[:, 0]
    padded = (counts + EXPERT_ROWS - 1) // EXPERT_ROWS * EXPERT_ROWS
    pad_end = jnp.cumsum(padded)
    pad_start = pad_end - padded
    n_used = (pad_end[-1] // EXPERT_ROWS).astype(jnp.int32)
    block_start = jnp.minimum(jnp.arange(n_blocks, dtype=jnp.int32), n_used - 1) * EXPERT_ROWS
    block_e = jnp.minimum(jnp.sum(block_start[:, None] >= pad_end[None, :], axis=1),
                          N_EXPERTS - 1).astype(jnp.int32)
    expert_ids = jnp.arange(N_EXPERTS, dtype=jnp.int32)[:, None, None]
    dest = jnp.sum(jnp.where(idx[None] == expert_ids, pad_start[:, None, None], 0), axis=0) + rank

    n_used = n_used.reshape(1)
    xs = _dispatch(x_packed, _tile_pairs(dest, DISPATCH_ROWS), (pad_start + counts).astype(jnp.int32),
                   (padded - counts).astype(jnp.int32), n_used, n_blocks * EXPERT_ROWS)
    yb = _experts(xs, block_e, n_used, w1, w3, w2)
    return _combine(alpha, _tile_pairs(dest, COMBINE_ROWS), x, gate.T, ln_g, ln_b, yb)


def kernel(x, emb_ln_g, emb_ln_b, w_in, conv_w, conv_b, lru_w_a, lru_b_a, lru_w_x, lru_b_x, lru_lambda, w_o_rnn, sgu_ln_g, sgu_ln_b, sgu_w_s, sgu_b_s, w_o_sgu, w_out, ln1_g, ln1_b, router_w, router_b, expert_w1, expert_w3, expert_w2, ln2_g, ln2_b):
    batch, seq, D = x.shape
    depth = w_in.shape[0]
    alpha = float((2 * depth) ** 0.25)
    xf, xb = _emb_ln(x.reshape(batch * seq, D), emb_ln_g, emb_ln_b)
    w_in, w_o_rnn, w_o_sgu, w_out = (w.astype(BF16) for w in (w_in, w_o_rnn, w_o_sgu, w_out))
    rows_of = lambda w: w.reshape(depth, N_EXPERTS * w.shape[2], w.shape[3])
    w1_rows, w3_rows, w2_rows = rows_of(expert_w1), rows_of(expert_w3), rows_of(expert_w2)
    gelu_from = D_RNN
    sigmoid_from = 2 * D_RNN + 2 * D_SGU
    for l in range(depth):
        x_rnn = _proj(xb, w_in, l, 0, gelu_from, lambda v: v, "proj_x")
        gelu_cols, w3 = _proj(xb, w_in, l, gelu_from, sigmoid_from - gelu_from, _gelu, "proj_gelu",
                              cast=w3_rows)
        sigmoid_cols, w1 = _proj(xb, w_in, l, sigmoid_from, D_IN - sigmoid_from, _sigmoid,
                                 "proj_sigmoid", cast=w1_rows)
        a = _lru(x_rnn, gelu_cols, batch, seq, conv_w[l], conv_b[l], lru_w_a[l], lru_b_a[l],
                 lru_w_x[l], lru_b_x[l], lru_lambda[l])
        b = _sgu(gelu_cols, sgu_ln_g[l], sgu_ln_b[l], sgu_w_s[l], sgu_b_s[l])
        x1, x1_packed, w2 = _mix_out(alpha, l, a, b, sigmoid_cols, xf, w_o_rnn, w_o_sgu, w_out,
                                     ln1_g[l], ln1_b[l], cast=w2_rows)
        xf, xb = _moe(alpha, x1, x1_packed, router_w, router_b, w1.reshape(expert_w1.shape[1:]),
                      w3.reshape(expert_w3.shape[1:]), w2.reshape(expert_w2.shape[1:]),
                      ln2_g[l], ln2_b[l])
    return xf.reshape(batch, seq, D)
```

```python
import functools

import jax
import jax.numpy as jnp
import numpy as np
from jax import lax
from jax.experimental import pallas as pl
from jax.experimental.pallas import tpu as pltpu

F32 = jnp.float32
BF16 = jnp.bfloat16

D_MODEL = 2048
D_RNN = 1024
RNN_HEADS = 8
HEAD_DIM = D_RNN // RNN_HEADS
CONV_WIDTH = 4
LRU_C = 8.0
D_SGU = 1024
SGU_GROUPS = 8
SGU_GROUP_DIM = D_SGU // SGU_GROUPS
CHUNK = 128
N_EXPERTS = 16
N_GROUPS = 4
EXPERTS_PER_GROUP = N_EXPERTS // N_GROUPS
TOP_K = 2
D_EXPERT = 1408
D_IN = 2 * D_RNN + 2 * D_SGU + 2 * D_MODEL
LN_EPS = 1e-5
SQRT_2_OVER_PI = float(np.sqrt(2.0 / np.pi))

LN_ROWS = 512
PROJ_ROWS = 1024
PROJ_COLS = 1024
LRU_ROWS = 512
MIX_ROWS = 256
ROUTER_ROWS = 512
DISPATCH_ROWS = 512
EXPERT_ROWS = 256
COMBINE_ROWS = 512
CONV_PAD = 8
N_DMA_PRIORITIES = 2
ROW_TILE = 8

MIB = 1024 * 1024


def _params(semantics, vmem_mib):
    return pltpu.CompilerParams(dimension_semantics=semantics,
                                vmem_limit_bytes=vmem_mib * MIB,
                                disable_bounds_checks=True)


def _const_spec(shape):
    zeros = (0,) * len(shape)
    return pl.BlockSpec(shape, lambda *_: zeros, pipeline_mode=pl.Buffered(1))


def _layer_spec(layer, shape):
    zeros = (0,) * len(shape)
    return pl.BlockSpec((None,) + tuple(shape), lambda *_: (layer,) + zeros,
                        pipeline_mode=pl.Buffered(1))


def _layer_norm(v, g, b):
    mu = jnp.mean(v, axis=-1, keepdims=True)
    d = v - mu
    var = jnp.mean(d * d, axis=-1, keepdims=True)
    return d * lax.rsqrt(var + LN_EPS) * g + b


def _gelu(x):
    inner = x * (SQRT_2_OVER_PI + (SQRT_2_OVER_PI * 0.044715) * (x * x))
    return x * (0.5 * jnp.tanh(inner) + 0.5)


def _pack_halves(y):
    n = y.shape[1] // 2
    bits = lambda v: lax.bitcast_convert_type(v.astype(BF16).astype(F32), jnp.uint32)
    return (bits(y[:, :n]) >> 16) | bits(y[:, n:])


def _unpack_halves(p):
    lo = lax.bitcast_convert_type(p << 16, F32)
    hi = lax.bitcast_convert_type(p & jnp.uint32(0xFFFF0000), F32)
    return lo, hi


def _sigmoid(x):
    return 0.5 * jnp.tanh(0.5 * x) + 0.5


def _emb_ln_kernel(x_ref, g_ref, b_ref, ob_ref):
    ob_ref[...] = _layer_norm(x_ref[...], g_ref[...], b_ref[...]).astype(BF16)


def _emb_ln(x, g, b):
    T, D = x.shape
    row = pl.BlockSpec((LN_ROWS, D), lambda i: (i, 0))
    return pl.pallas_call(
        _emb_ln_kernel,
        grid=(T // LN_ROWS,),
        in_specs=[row, _const_spec((1, D)), _const_spec((1, D))],
        out_specs=row,
        out_shape=jax.ShapeDtypeStruct((T, D), BF16),
        compiler_params=_params(("parallel",), 40),
        name="emb_ln",
    )(x, g.reshape(1, D), b.reshape(1, D))


def _cast_specs(stack, layer, n_steps, step_of):
    _, rows, cols = stack.shape
    slab = -(-rows // (n_steps * 16)) * 16
    src = pl.BlockSpec((None, slab, cols), lambda *g: (layer, step_of(*g), 0))
    dst = pl.BlockSpec((slab, cols), lambda *g: (step_of(*g), 0))
    return src, dst, jax.ShapeDtypeStruct((rows, cols), BF16)


def _proj_kernel(act, x_ref, w_ref, *rest):
    o_ref = rest[len(rest) // 2]
    acc = jnp.dot(x_ref[...], w_ref[...], preferred_element_type=F32)
    o_ref[...] = act(acc).astype(BF16)
    for src_ref, dst_ref in zip(rest[:len(rest) // 2], rest[len(rest) // 2 + 1:]):
        dst_ref[...] = src_ref[...].astype(BF16)


def _proj(xb, w_in, layer, col_from, n_cols, act, name, cast=None):
    T, D = xb.shape
    first = col_from // PROJ_COLS
    grid = (T // PROJ_ROWS, n_cols // PROJ_COLS)
    in_specs = [pl.BlockSpec((PROJ_ROWS, D), lambda i, j: (i, 0)),
                pl.BlockSpec((None, D, PROJ_COLS), lambda i, j: (layer, 0, first + j))]
    out_specs = [pl.BlockSpec((PROJ_ROWS, PROJ_COLS), lambda i, j: (i, j))]
    out_shape = [jax.ShapeDtypeStruct((T, n_cols), BF16)]
    operands = [xb, w_in]
    if cast is not None:
        src, dst, shape = _cast_specs(cast, layer, grid[0] * grid[1], lambda i, j: i * grid[1] + j)
        in_specs.append(src)
        out_specs.append(dst)
        out_shape.append(shape)
        operands.append(cast)
    out = pl.pallas_call(
        functools.partial(_proj_kernel, act),
        grid=grid,
        in_specs=in_specs,
        out_specs=out_specs,
        out_shape=out_shape,
        compiler_params=_params(("parallel", "arbitrary"), 48),
        name=name,
    )(*operands)
    return out[0] if cast is None else out


def _lru_kernel(xr_ref, gate_ref, cw_ref, cb_ref, wa_ref, ba_ref, wx_ref, bx_ref, lam_ref,
                o_ref, xpad_ref, hc_ref, a_ref, u_ref, h_ref):
    rows = xr_ref.shape[0]

    @pl.when(pl.program_id(1) == 0)
    def _():
        xpad_ref[0:CONV_PAD, :] = jnp.zeros((CONV_PAD, D_RNN), F32)
        hc_ref[...] = jnp.zeros_like(hc_ref)

    x = xr_ref[...].astype(F32)
    xpad_ref[CONV_PAD:CONV_PAD + rows, :] = x
    xc = cb_ref[...] + cw_ref[CONV_WIDTH - 1:CONV_WIDTH, :] * x
    for k in range(CONV_WIDTH - 1):
        shift = CONV_WIDTH - 1 - k
        xc = xc + cw_ref[k:k + 1, :] * xpad_ref[CONV_PAD - shift:CONV_PAD - shift + rows, :]
    xpad_ref[0:CONV_PAD, :] = x[rows - CONV_PAD:rows, :]

    lam = lam_ref[...]
    sp = jnp.maximum(-lam, 0.0) + jnp.log1p(jnp.exp(-jnp.abs(lam)))
    for h in range(RNN_HEADS):
        cols = slice(h * HEAD_DIM, (h + 1) * HEAD_DIM)
        xh = xc[:, cols]
        xhb = xh.astype(BF16)
        r = _sigmoid(jnp.dot(xhb, wa_ref[h], preferred_element_type=F32) + ba_ref[:, cols])
        gi = _sigmoid(jnp.dot(xhb, wx_ref[h], preferred_element_type=F32) + bx_ref[:, cols])
        log_a = (-LRU_C) * r * sp[:, cols]
        a_ref[:, cols] = jnp.exp(log_a)
        th = jnp.tanh(log_a)
        u_ref[:, cols] = jnp.sqrt(-2.0 * th / (1.0 - th)) * (gi * xh)

    def step(t, h):
        h = a_ref[pl.ds(t, 1), :] * h + u_ref[pl.ds(t, 1), :]
        h_ref[pl.ds(t, 1), :] = h
        return h

    hc_ref[0:1, :] = lax.fori_loop(0, rows, step, hc_ref[0:1, :], unroll=8)
    o_ref[...] = (h_ref[...] * gate_ref[...].astype(F32)).astype(BF16)


def _lru(x_rnn, gelu_cols, batch, seq, conv_w, conv_b, w_a, b_a, w_x, b_x, lam):
    T = x_rnn.shape[0]
    tiles = seq // LRU_ROWS
    row = pl.BlockSpec((LRU_ROWS, D_RNN), lambda b, s: (b * tiles + s, 0))
    vec = _const_spec((1, D_RNN))
    gate_w = _const_spec((RNN_HEADS, HEAD_DIM, HEAD_DIM))
    return pl.pallas_call(
        _lru_kernel,
        grid=(batch, tiles),
        in_specs=[row, row, _const_spec((CONV_WIDTH, D_RNN)), vec, gate_w, vec, gate_w, vec, vec],
        out_specs=pl.BlockSpec((LRU_ROWS, D_RNN), lambda b, s: (b * tiles + s, 0)),
        out_shape=jax.ShapeDtypeStruct((T, D_RNN), BF16),
        scratch_shapes=[pltpu.VMEM((CONV_PAD + LRU_ROWS, D_RNN), F32),
                        pltpu.VMEM((8, D_RNN), F32),
                        pltpu.VMEM((LRU_ROWS, D_RNN), F32),
                        pltpu.VMEM((LRU_ROWS, D_RNN), F32),
                        pltpu.VMEM((LRU_ROWS, D_RNN), F32)],
        compiler_params=_params(("arbitrary", "arbitrary"), 40),
        name="lru",
    )(x_rnn, gelu_cols, conv_w, conv_b.reshape(1, D_RNN), w_a.astype(BF16), b_a.reshape(1, D_RNN),
      w_x.astype(BF16), b_x.reshape(1, D_RNN), lam.reshape(1, D_RNN))


def _spatial_gating(u_ref, v_ref, g_ref, b_ref, ws_ref, bs_ref, o_ref):
    rows = u_ref.shape[0]
    v = _layer_norm(v_ref[...].astype(F32), g_ref[...], b_ref[...]).astype(BF16)
    t_out = lax.broadcasted_iota(jnp.int32, (CHUNK, CHUNK), 0)
    t_in = lax.broadcasted_iota(jnp.int32, (CHUNK, CHUNK), 1)
    causal = t_in <= t_out
    for g in range(SGU_GROUPS):
        cols = slice(g * SGU_GROUP_DIM, (g + 1) * SGU_GROUP_DIM)
        ws = jnp.where(causal, ws_ref[g], 0.0).astype(BF16)
        bias = bs_ref[:, g:g + 1]
        for c in range(rows // CHUNK):
            rws = slice(c * CHUNK, (c + 1) * CHUNK)
            mixed = jnp.dot(ws, v[rws, cols], preferred_element_type=F32) + bias
            o_ref[rws, cols] = (u_ref[rws, cols].astype(F32) * mixed).astype(BF16)


def _mix_out_kernel(alpha, x_is_raw, a_ref, u_ref, v_ref, sga_ref, sgb_ref, x_ref, emb_g_ref,
                    emb_b_ref, sgu_g_ref, sgu_b_ref, ws_ref, bs_ref, woa_ref, wob_ref, wout_ref,
                    g_ref, beta_ref, cast_src_ref, o_ref, packed_ref, cast_dst_ref, b_ref):
    _spatial_gating(u_ref, v_ref, sgu_g_ref, sgu_b_ref, ws_ref, bs_ref, b_ref)
    ya = jnp.dot(a_ref[...], woa_ref[...], preferred_element_type=F32)
    yb = jnp.dot(b_ref[...], wob_ref[...], preferred_element_type=F32)
    merged = sga_ref[...].astype(F32) * ya + sgb_ref[...].astype(F32) * yb
    m = jnp.dot(merged.astype(BF16), wout_ref[...], preferred_element_type=F32)
    x = x_ref[...]
    if x_is_raw:
        x = _layer_norm(x, emb_g_ref[...], emb_b_ref[...])
    y = _layer_norm(alpha * x + m, g_ref[...], beta_ref[...])
    o_ref[...] = y
    packed_ref[...] = _pack_halves(y)
    cast_dst_ref[...] = cast_src_ref[...].astype(BF16)


def _mix_out(alpha, layer, a, gelu_cols, sigmoid_cols, x, x_is_raw, emb_ln_g, emb_ln_b,
             sgu_ln_g, sgu_ln_b, w_s, b_s, w_o_rnn, w_o_sgu, w_out, ln_g, ln_b, cast):
    T, D = x.shape
    n_steps = T // MIX_ROWS
    row = lambda w, c: pl.BlockSpec((MIX_ROWS, w), lambda i: (i, c))
    cast_src, cast_dst, cast_shape = _cast_specs(cast, layer, n_steps, lambda i: i)
    return pl.pallas_call(
        functools.partial(_mix_out_kernel, alpha, x_is_raw),
        grid=(n_steps,),
        in_specs=[row(D_RNN, 0), row(D_SGU, 1), row(D_SGU, 2), row(D, 0), row(D, 1), row(D, 0),
                  _const_spec((1, D)), _const_spec((1, D)),
                  _const_spec((1, D_SGU)), _const_spec((1, D_SGU)),
                  _const_spec((SGU_GROUPS, CHUNK, CHUNK)), _const_spec((CHUNK, SGU_GROUPS)),
                  _layer_spec(layer, (D_RNN, D)), _layer_spec(layer, (D_SGU, D)),
                  _layer_spec(layer, (D, D)), _const_spec((1, D)), _const_spec((1, D)), cast_src],
        out_specs=[row(D, 0), row(D // 2, 0), cast_dst],
        out_shape=[jax.ShapeDtypeStruct((T, D), F32), jax.ShapeDtypeStruct((T, D // 2), jnp.uint32),
                   cast_shape],
        scratch_shapes=[pltpu.VMEM((MIX_ROWS, D_SGU), BF16)],
        compiler_params=_params(("parallel",), 56),
        name="mix_out",
    )(a, gelu_cols, gelu_cols, sigmoid_cols, sigmoid_cols, x,
      emb_ln_g.reshape(1, D), emb_ln_b.reshape(1, D),
      sgu_ln_g.reshape(1, D_SGU), sgu_ln_b.reshape(1, D_SGU), w_s, b_s.T,
      w_o_rnn, w_o_sgu, w_out, ln_g.reshape(1, D), ln_b.reshape(1, D), cast)


def _first_max(v, row, n):
    m = jnp.max(v, axis=0, keepdims=True)
    idx = jnp.min(jnp.where(v == m, row, float(n)), axis=0, keepdims=True)
    return m, idx


def _router_kernel(x_ref, wt_ref, b_ref, idx_ref, gate_ref, rank_ref, cnt_ref, run_ref):
    rows = x_ref.shape[0]

    @pl.when(pl.program_id(0) == 0)
    def _():
        run_ref[...] = jnp.zeros_like(run_ref)

    def nt_dot(w, x):
        return lax.dot_general(w, x, (((1,), (1,)), ((), ())), preferred_element_type=F32)

    x = x_ref[...]
    xh = x.astype(BF16)
    xl = (x - xh.astype(F32)).astype(BF16)
    w = wt_ref[...]
    wh = w.astype(BF16)
    wl = (w - wh.astype(F32)).astype(BF16)
    logits = nt_dot(wh, xh) + (nt_dot(wh, xl) + nt_dot(wl, xh))
    e = jnp.exp(logits - jnp.max(logits, axis=0, keepdims=True))
    scores = e / jnp.sum(e, axis=0, keepdims=True)
    sel = scores + b_ref[...]

    row = lax.broadcasted_iota(jnp.int32, (N_EXPERTS, rows), 0).astype(F32)
    grow = lax.broadcasted_iota(jnp.int32, (EXPERTS_PER_GROUP, rows), 0).astype(F32)
    neg_inf = float("-inf")
    best_score = None
    best_group = None
    for g in range(N_GROUPS):
        v = sel[g * EXPERTS_PER_GROUP:(g + 1) * EXPERTS_PER_GROUP, :]
        m1, i1 = _first_max(v, grow, EXPERTS_PER_GROUP)
        m2 = jnp.max(jnp.where(grow == i1, neg_inf, v), axis=0, keepdims=True)
        s = m1 + m2
        if g == 0:
            best_score, best_group = s, jnp.zeros_like(s)
        else:
            better = s > best_score
            best_group = jnp.where(better, float(g), best_group)
            best_score = jnp.where(better, s, best_score)

    lo = best_group * float(EXPERTS_PER_GROUP)
    in_group = jnp.logical_and(row >= lo, row < lo + float(EXPERTS_PER_GROUP))
    masked = jnp.where(in_group, sel, neg_inf)
    _, i1 = _first_max(masked, row, N_EXPERTS)
    pick1 = row == i1
    _, i2 = _first_max(jnp.where(pick1, neg_inf, masked), row, N_EXPERTS)
    pick2 = row == i2
    s1 = jnp.sum(jnp.where(pick1, scores, 0.0), axis=0, keepdims=True)
    s2 = jnp.sum(jnp.where(pick2, scores, 0.0), axis=0, keepdims=True)
    den = s1 + s2

    onehot = jnp.where(jnp.logical_or(pick1, pick2), 1.0, 0.0)
    before = (lax.broadcasted_iota(jnp.int32, (rows, rows), 0)
              < lax.broadcasted_iota(jnp.int32, (rows, rows), 1))
    prefix = jnp.dot(onehot.astype(BF16), jnp.where(before, 1.0, 0.0).astype(BF16),
                     preferred_element_type=F32)
    pos = prefix + run_ref[:, 0:1]
    r1 = jnp.sum(jnp.where(pick1, pos, 0.0), axis=0, keepdims=True)
    r2 = jnp.sum(jnp.where(pick2, pos, 0.0), axis=0, keepdims=True)
    run_ref[...] = run_ref[...] + jnp.sum(onehot, axis=1, keepdims=True)

    idx_ref[0:1, :] = i1.astype(jnp.int32)
    idx_ref[1:2, :] = i2.astype(jnp.int32)
    gate_ref[0:1, :] = s1 / den
    gate_ref[1:2, :] = s2 / den
    rank_ref[0:1, :] = r1.astype(jnp.int32)
    rank_ref[1:2, :] = r2.astype(jnp.int32)
    cnt_ref[...] = run_ref[...].astype(jnp.int32)


def _router(x, router_w, router_b):
    T, D = x.shape
    pair = pl.BlockSpec((TOP_K, ROUTER_ROWS), lambda i: (0, i))
    return pl.pallas_call(
        _router_kernel,
        grid=(T // ROUTER_ROWS,),
        in_specs=[pl.BlockSpec((ROUTER_ROWS, D), lambda i: (i, 0)),
                  _const_spec((N_EXPERTS, D)), _const_spec((N_EXPERTS, 1))],
        out_specs=[pair, pair, pair, pl.BlockSpec((N_EXPERTS, 128), lambda i: (0, 0))],
        out_shape=[jax.ShapeDtypeStruct((TOP_K, T), jnp.int32),
                   jax.ShapeDtypeStruct((TOP_K, T), F32),
                   jax.ShapeDtypeStruct((TOP_K, T), jnp.int32),
                   jax.ShapeDtypeStruct((N_EXPERTS, 128), jnp.int32)],
        scratch_shapes=[pltpu.VMEM((N_EXPERTS, 128), F32)],
        compiler_params=_params(("arbitrary",), 40),
        name="router",
    )(x, router_w.T, router_b.reshape(N_EXPERTS, 1))


def _dispatch_kernel(pad_from_ref, pad_n_ref, n_used_ref, dest_ref, x_ref, xs_hbm, zero_ref, sem, pad_sem):
    i = pl.program_id(0)
    rows = DISPATCH_ROWS
    n_blocks = xs_hbm.shape[0] // EXPERT_ROWS
    zero_rows = zero_ref.shape[0]

    def tail_copies(j):
        block = n_used_ref[0] + j
        off = pl.multiple_of(block * EXPERT_ROWS, EXPERT_ROWS)
        return block < n_blocks, [
            pltpu.make_async_copy(zero_ref, xs_hbm.at[pl.ds(off + part * zero_rows, zero_rows)], pad_sem)
            for part in range(EXPERT_ROWS // zero_rows)]

    def pad_copy(e, bit):
        n = pad_n_ref[e]
        if bit < ROW_TILE:
            off = pad_from_ref[e] + bit - 1
            return pltpu.make_async_copy(zero_ref.at[pl.ds(0, 1)], xs_hbm.at[pl.ds(off, 1)], pad_sem)
        done = (n & (ROW_TILE - 1)) + (n & ~(2 * bit - 1))
        off = pl.multiple_of(pad_from_ref[e] + done, ROW_TILE)
        return pltpu.make_async_copy(zero_ref.at[pl.ds(0, bit)], xs_hbm.at[pl.ds(off, bit)], pad_sem)

    def pad_needed(e, bit):
        n = pad_n_ref[e]
        if bit < ROW_TILE:
            return bit <= (n & (ROW_TILE - 1))
        return (n & bit) != 0

    bits = list(range(1, ROW_TILE)) + [
        1 << k for k in range(ROW_TILE.bit_length() - 1, EXPERT_ROWS.bit_length() - 1)]

    @pl.when(i == 0)
    def _():
        zero_ref[...] = jnp.zeros_like(zero_ref)
        for e in range(N_EXPERTS):
            for bit in bits:
                @pl.when(pad_needed(e, bit))
                def _():
                    pad_copy(e, bit).start()
        for j in range(N_EXPERTS):
            needed, copies = tail_copies(j)

            @pl.when(needed)
            def _():
                for c in copies:
                    c.start()

    for t in range(rows):
        src = x_ref.at[pl.ds(t, 1)]
        for k in range(TOP_K):
            pltpu.make_async_copy(src, xs_hbm.at[pl.ds(dest_ref[0, k * rows + t], 1)],
                                  sem).start(priority=k % N_DMA_PRIORITIES)
    for _ in range(TOP_K):
        pltpu.make_async_copy(x_ref, xs_hbm.at[pl.ds(0, rows)], sem).wait()

    @pl.when(i == 0)
    def _():
        for e in range(N_EXPERTS):
            for bit in bits:
                @pl.when(pad_needed(e, bit))
                def _():
                    pad_copy(e, bit).wait()
        for j in range(N_EXPERTS):
            needed, copies = tail_copies(j)

            @pl.when(needed)
            def _():
                for c in copies:
                    c.wait()


def _dispatch(x, dest_tiles, pad_from, pad_n, n_used, n_slots):
    T, D = x.shape
    grid_spec = pltpu.PrefetchScalarGridSpec(
        num_scalar_prefetch=3,
        grid=(T // DISPATCH_ROWS,),
        in_specs=[pl.BlockSpec((None, 1, TOP_K * DISPATCH_ROWS), lambda i, *_: (i, 0, 0),
                               memory_space=pltpu.SMEM),
                  pl.BlockSpec((DISPATCH_ROWS, D), lambda i, *_: (i, 0))],
        out_specs=pl.BlockSpec(memory_space=pl.ANY),
        scratch_shapes=[pltpu.VMEM((EXPERT_ROWS // 2, D), x.dtype),
                        pltpu.SemaphoreType.DMA(()), pltpu.SemaphoreType.DMA(())],
    )
    return pl.pallas_call(
        _dispatch_kernel,
        grid_spec=grid_spec,
        out_shape=jax.ShapeDtypeStruct((n_slots, D), x.dtype),
        compiler_params=_params(("arbitrary",), 16),
        name="dispatch",
    )(pad_from, pad_n, n_used, dest_tiles, x)


def _expert_kernel(block_e_ref, n_used_ref, xs_ref, w1_ref, w3_ref, w2_ref, o_ref):
    used = pl.program_id(0) < n_used_ref[0]

    @pl.when(jnp.logical_not(used))
    def _():
        o_ref[...] = jnp.zeros_like(o_ref)

    @pl.when(used)
    def _():
        lo, hi = _unpack_halves(xs_ref[...])
        x = jnp.concatenate([lo.astype(BF16), hi.astype(BF16)], axis=1)
        h1 = jnp.dot(x, w1_ref[...], preferred_element_type=F32)
        h3 = jnp.dot(x, w3_ref[...], preferred_element_type=F32)
        hidden = (h1 * _sigmoid(h1)) * h3
        out = jnp.dot(hidden.astype(BF16), w2_ref[...], preferred_element_type=F32)
        o_ref[...] = _pack_halves(out)


def _experts(xs, block_e, n_used, w1, w3, w2):
    P, packed = xs.shape
    D = 2 * packed
    n_blocks = P // EXPERT_ROWS
    used_rows = pl.BlockSpec((EXPERT_ROWS, packed),
                             lambda i, be, nu: (jnp.maximum(jnp.minimum(i, nu[0] - 1), 0), 0))
    rows = pl.BlockSpec((EXPERT_ROWS, packed), lambda i, be, nu: (i, 0))
    expert_slab = lambda i, be, nu: (be[i], 0, 0)
    grid_spec = pltpu.PrefetchScalarGridSpec(
        num_scalar_prefetch=2,
        grid=(n_blocks,),
        in_specs=[used_rows,
                  pl.BlockSpec((None, D, D_EXPERT), expert_slab),
                  pl.BlockSpec((None, D, D_EXPERT), expert_slab),
                  pl.BlockSpec((None, D_EXPERT, D), expert_slab)],
        out_specs=rows,
    )
    return pl.pallas_call(
        _expert_kernel,
        grid_spec=grid_spec,
        out_shape=jax.ShapeDtypeStruct((P, packed), jnp.uint32),
        compiler_params=_params(("arbitrary",), 56),
        name="experts",
    )(block_e, n_used, xs, w1, w3, w2)


def _combine_kernel(alpha, dest_ref, dest_next_ref, x_ref, gate_ref, g_ref, b_ref, y_hbm,
                    of_ref, ob_ref, buf_a, buf_b, sem):
    rows = COMBINE_ROWS
    half = rows // 2
    i = pl.program_id(0)

    def copy(table_ref, h, t, k, buf, s):
        return pltpu.make_async_copy(y_hbm.at[pl.ds(table_ref[0, k * rows + h * half + t], 1)],
                                     buf.at[k, pl.ds(t, 1)], sem.at[s])

    def gather(table_ref, h, buf, s):
        for t in range(half):
            for k in range(TOP_K):
                copy(table_ref, h, t, k, buf, s).start(priority=k % N_DMA_PRIORITIES)

    def wait(buf, s):
        for k in range(TOP_K):
            pltpu.make_async_copy(y_hbm.at[pl.ds(0, half)], buf.at[k], sem.at[s]).wait()

    def reduce(buf, h):
        r = slice(h * half, (h + 1) * half)
        lo0, hi0 = _unpack_halves(buf[0])
        lo1, hi1 = _unpack_halves(buf[1])
        g0 = gate_ref[r, 0:1]
        g1 = gate_ref[r, 1:2]
        f = jnp.concatenate([g0 * lo0 + g1 * lo1, g0 * hi0 + g1 * hi1], axis=1)
        y = _layer_norm(alpha * x_ref[r, :] + f, g_ref[...], b_ref[...])
        of_ref[r, :] = y
        ob_ref[r, :] = y.astype(BF16)

    @pl.when(i == 0)
    def _():
        def issue(t, carry):
            for k in range(TOP_K):
                copy(dest_ref, 0, t, k, buf_a, 0).start()
            return carry

        lax.fori_loop(0, half, issue, 0, unroll=8)

    wait(buf_a, 0)
    gather(dest_ref, 1, buf_b, 1)
    reduce(buf_a, 0)
    wait(buf_b, 1)
    gather(dest_next_ref, 0, buf_a, 0)
    reduce(buf_b, 1)

    @pl.when(i == pl.num_programs(0) - 1)
    def _():
        wait(buf_a, 0)


def _combine(alpha, dest_tiles, x, gate_t, ln_g, ln_b, yb):
    T, D = x.shape
    n_tiles = T // COMBINE_ROWS
    row = pl.BlockSpec((COMBINE_ROWS, D), lambda i: (i, 0))
    table = lambda index: pl.BlockSpec((None, 1, TOP_K * COMBINE_ROWS), index, memory_space=pltpu.SMEM)
    half_buf = pltpu.VMEM((TOP_K, COMBINE_ROWS // 2, yb.shape[1]), yb.dtype)
    return pl.pallas_call(
        functools.partial(_combine_kernel, alpha),
        grid=(n_tiles,),
        in_specs=[table(lambda i: (i, 0, 0)),
                  table(lambda i: (jnp.minimum(i + 1, n_tiles - 1), 0, 0)),
                  row, pl.BlockSpec((COMBINE_ROWS, TOP_K), lambda i: (i, 0)),
                  _const_spec((1, D)), _const_spec((1, D)),
                  pl.BlockSpec(memory_space=pl.ANY)],
        out_specs=[row, row],
        out_shape=[jax.ShapeDtypeStruct((T, D), F32), jax.ShapeDtypeStruct((T, D), BF16)],
        scratch_shapes=[half_buf, half_buf, pltpu.SemaphoreType.DMA((2,))],
        compiler_params=_params(("arbitrary",), 48),
        name="combine",
    )(dest_tiles, dest_tiles, x, gate_t, ln_g.reshape(1, D), ln_b.reshape(1, D), yb)


def _tile_pairs(dest, rows):
    T = dest.shape[1]
    return dest.reshape(TOP_K, T // rows, rows).transpose(1, 0, 2).reshape(T // rows, 1, TOP_K * rows)


def _moe(alpha, x, x_packed, router_w, router_b, w1, w3, w2, ln_g, ln_b):
    T, D = x.shape
    n_blocks = (T * TOP_K) // EXPERT_ROWS + N_EXPERTS
    idx, gate, rank, cnt = _router(x, router_w, router_b)

    counts = cnt[:, 0]
    padded = (counts + EXPERT_ROWS - 1) // EXPERT_ROWS * EXPERT_ROWS
    pad_end = jnp.cumsum(padded)
    pad_start = pad_end - padded
    n_used = (pad_end[-1] // EXPERT_ROWS).astype(jnp.int32)
    block_start = jnp.minimum(jnp.arange(n_blocks, dtype=jnp.int32), n_used - 1) * EXPERT_ROWS
    block_e = jnp.minimum(jnp.sum(block_start[:, None] >= pad_end[None, :], axis=1),
                          N_EXPERTS - 1).astype(jnp.int32)
    expert_ids = jnp.arange(N_EXPERTS, dtype=jnp.int32)[:, None, None]
    dest = jnp.sum(jnp.where(idx[None] == expert_ids, pad_start[:, None, None], 0), axis=0) + rank

    n_used = n_used.reshape(1)
    xs = _dispatch(x_packed, _tile_pairs(dest, DISPATCH_ROWS), (pad_start + counts).astype(jnp.int32),
                   (padded - counts).astype(jnp.int32), n_used, n_blocks * EXPERT_ROWS)
    yb = _experts(xs, block_e, n_used, w1, w3, w2)
    return _combine(alpha, _tile_pairs(dest, COMBINE_ROWS), x, gate.T, ln_g, ln_b, yb)


def kernel(x, emb_ln_g, emb_ln_b, w_in, conv_w, conv_b, lru_w_a, lru_b_a, lru_w_x, lru_b_x, lru_lambda, w_o_rnn, sgu_ln_g, sgu_ln_b, sgu_w_s, sgu_b_s, w_o_sgu, w_out, ln1_g, ln1_b, router_w, router_b, expert_w1, expert_w3, expert_w2, ln2_g, ln2_b):
    batch, seq, D = x.shape
    depth = w_in.shape[0]
    alpha = float((2 * depth) ** 0.25)
    xf = x.reshape(batch * seq, D)
    xb = _emb_ln(xf, emb_ln_g, emb_ln_b)
    w_in, w_o_rnn, w_o_sgu, w_out = (w.astype(BF16) for w in (w_in, w_o_rnn, w_o_sgu, w_out))
    rows_of = lambda w: w.reshape(depth, N_EXPERTS * w.shape[2], w.shape[3])
    w1_rows, w3_rows, w2_rows = rows_of(expert_w1), rows_of(expert_w3), rows_of(expert_w2)
    gelu_from = D_RNN
    sigmoid_from = 2 * D_RNN + 2 * D_SGU
    for l in range(depth):
        x_rnn = _proj(xb, w_in, l, 0, gelu_from, lambda v: v, "proj_x")
        gelu_cols, w3 = _proj(xb, w_in, l, gelu_from, sigmoid_from - gelu_from, _gelu, "proj_gelu",
                              cast=w3_rows)
        sigmoid_cols, w1 = _proj(xb, w_in, l, sigmoid_from, D_IN - sigmoid_from, _sigmoid,
                                 "proj_sigmoid", cast=w1_rows)
        a = _lru(x_rnn, gelu_cols, batch, seq, conv_w[l], conv_b[l], lru_w_a[l], lru_b_a[l],
                 lru_w_x[l], lru_b_x[l], lru_lambda[l])
        x1, x1_packed, w2 = _mix_out(alpha, l, a, gelu_cols, sigmoid_cols, xf, l == 0, emb_ln_g,
                                     emb_ln_b, sgu_ln_g[l], sgu_ln_b[l], sgu_w_s[l], sgu_b_s[l],
                                     w_o_rnn, w_o_sgu, w_out, ln1_g[l], ln1_b[l], cast=w2_rows)
        xf, xb = _moe(alpha, x1, x1_packed, router_w, router_b, w1.reshape(expert_w1.shape[1:]),
                      w3.reshape(expert_w3.shape[1:]), w2.reshape(expert_w2.shape[1:]),
                      ln2_g[l], ln2_b[l])
    return xf.reshape(batch, seq, D)
```

```python
import functools

import jax
import jax.numpy as jnp
import numpy as np
from jax import lax
from jax.experimental import pallas as pl
from jax.experimental.pallas import tpu as pltpu

F32 = jnp.float32
BF16 = jnp.bfloat16

D_MODEL = 2048
D_RNN = 1024
RNN_HEADS = 8
HEAD_DIM = D_RNN // RNN_HEADS
CONV_WIDTH = 4
LRU_C = 8.0
D_SGU = 1024
SGU_GROUPS = 8
SGU_GROUP_DIM = D_SGU // SGU_GROUPS
CHUNK = 128
N_EXPERTS = 16
N_GROUPS = 4
EXPERTS_PER_GROUP = N_EXPERTS // N_GROUPS
TOP_K = 2
D_EXPERT = 1408
D_IN = 2 * D_RNN + 2 * D_SGU + 2 * D_MODEL
LN_EPS = 1e-5
SQRT_2_OVER_PI = float(np.sqrt(2.0 / np.pi))

LN_ROWS = 512
PROJ_ROWS = 1024
PROJ_COLS = 1024
MIX_ROWS = 256
ROUTER_ROWS = 512
DISPATCH_ROWS = 512
EXPERT_ROWS = 256
COMBINE_ROWS = 512
CONV_PAD = 8
N_DMA_PRIORITIES = 2
ROW_TILE = 8

MIB = 1024 * 1024


def _params(semantics, vmem_mib):
    return pltpu.CompilerParams(dimension_semantics=semantics,
                                vmem_limit_bytes=vmem_mib * MIB,
                                disable_bounds_checks=True)


def _const_spec(shape):
    zeros = (0,) * len(shape)
    return pl.BlockSpec(shape, lambda *_: zeros, pipeline_mode=pl.Buffered(1))


def _layer_spec(layer, shape):
    zeros = (0,) * len(shape)
    return pl.BlockSpec((None,) + tuple(shape), lambda *_: (layer,) + zeros,
                        pipeline_mode=pl.Buffered(1))


def _layer_norm(v, g, b):
    mu = jnp.mean(v, axis=-1, keepdims=True)
    d = v - mu
    var = jnp.mean(d * d, axis=-1, keepdims=True)
    return d * lax.rsqrt(var + LN_EPS) * g + b


def _gelu(x):
    inner = x * (SQRT_2_OVER_PI + (SQRT_2_OVER_PI * 0.044715) * (x * x))
    return x * (0.5 * jnp.tanh(inner) + 0.5)


def _pack_halves(y):
    n = y.shape[1] // 2
    bits = lambda v: lax.bitcast_convert_type(v.astype(BF16).astype(F32), jnp.uint32)
    return (bits(y[:, :n]) >> 16) | bits(y[:, n:])


def _unpack_halves(p):
    lo = lax.bitcast_convert_type(p << 16, F32)
    hi = lax.bitcast_convert_type(p & jnp.uint32(0xFFFF0000), F32)
    return lo, hi


def _sigmoid(x):
    return 0.5 * jnp.tanh(0.5 * x) + 0.5


def _emb_ln_kernel(x_ref, g_ref, b_ref, ob_ref):
    ob_ref[...] = _layer_norm(x_ref[...], g_ref[...], b_ref[...]).astype(BF16)


def _emb_ln(x, g, b):
    T, D = x.shape
    row = pl.BlockSpec((LN_ROWS, D), lambda i: (i, 0))
    return pl.pallas_call(
        _emb_ln_kernel,
        grid=(T // LN_ROWS,),
        in_specs=[row, _const_spec((1, D)), _const_spec((1, D))],
        out_specs=row,
        out_shape=jax.ShapeDtypeStruct((T, D), BF16),
        compiler_params=_params(("parallel",), 40),
        name="emb_ln",
    )(x, g.reshape(1, D), b.reshape(1, D))


def _cast_specs(stack, layer, n_steps, step_of):
    _, rows, cols = stack.shape
    slab = -(-rows // (n_steps * 16)) * 16
    src = pl.BlockSpec((None, slab, cols), lambda *g: (layer, step_of(*g), 0))
    dst = pl.BlockSpec((slab, cols), lambda *g: (step_of(*g), 0))
    return src, dst, jax.ShapeDtypeStruct((rows, cols), BF16)


def _proj_kernel(act, x_ref, w_ref, *rest):
    o_ref = rest[len(rest) // 2]
    acc = jnp.dot(x_ref[...], w_ref[...], preferred_element_type=F32)
    o_ref[...] = act(acc).astype(BF16)
    for src_ref, dst_ref in zip(rest[:len(rest) // 2], rest[len(rest) // 2 + 1:]):
        dst_ref[...] = src_ref[...].astype(BF16)


def _proj(xb, w_in, layer, col_from, n_cols, act, name, cast=None):
    T, D = xb.shape
    first = col_from // PROJ_COLS
    grid = (T // PROJ_ROWS, n_cols // PROJ_COLS)
    in_specs = [pl.BlockSpec((PROJ_ROWS, D), lambda i, j: (i, 0)),
                pl.BlockSpec((None, D, PROJ_COLS), lambda i, j: (layer, 0, first + j))]
    out_specs = [pl.BlockSpec((PROJ_ROWS, PROJ_COLS), lambda i, j: (i, j))]
    out_shape = [jax.ShapeDtypeStruct((T, n_cols), BF16)]
    operands = [xb, w_in]
    if cast is not None:
        src, dst, shape = _cast_specs(cast, layer, grid[0] * grid[1], lambda i, j: i * grid[1] + j)
        in_specs.append(src)
        out_specs.append(dst)
        out_shape.append(shape)
        operands.append(cast)
    out = pl.pallas_call(
        functools.partial(_proj_kernel, act),
        grid=grid,
        in_specs=in_specs,
        out_specs=out_specs,
        out_shape=out_shape,
        compiler_params=_params(("parallel", "arbitrary"), 48),
        name=name,
    )(*operands)
    return out[0] if cast is None else out


def _lru_gates(first, xr_ref, cw_ref, cb_ref, wa_ref, ba_ref, wx_ref, bx_ref, lam_ref,
               xpad_ref, hc_ref, a_ref, u_ref):
    rows = xr_ref.shape[0]

    @pl.when(first)
    def _():
        xpad_ref[0:CONV_PAD, :] = jnp.zeros((CONV_PAD, D_RNN), F32)
        hc_ref[...] = jnp.zeros_like(hc_ref)

    x = xr_ref[...].astype(F32)
    xpad_ref[CONV_PAD:CONV_PAD + rows, :] = x
    xc = cb_ref[...] + cw_ref[CONV_WIDTH - 1:CONV_WIDTH, :] * x
    for k in range(CONV_WIDTH - 1):
        shift = CONV_WIDTH - 1 - k
        xc = xc + cw_ref[k:k + 1, :] * xpad_ref[CONV_PAD - shift:CONV_PAD - shift + rows, :]
    xpad_ref[0:CONV_PAD, :] = x[rows - CONV_PAD:rows, :]

    lam = lam_ref[...]
    sp = jnp.maximum(-lam, 0.0) + jnp.log1p(jnp.exp(-jnp.abs(lam)))
    for h in range(RNN_HEADS):
        cols = slice(h * HEAD_DIM, (h + 1) * HEAD_DIM)
        xh = xc[:, cols]
        xhb = xh.astype(BF16)
        r = _sigmoid(jnp.dot(xhb, wa_ref[h], preferred_element_type=F32) + ba_ref[:, cols])
        gi = _sigmoid(jnp.dot(xhb, wx_ref[h], preferred_element_type=F32) + bx_ref[:, cols])
        log_a = (-LRU_C) * r * sp[:, cols]
        a_ref[:, cols] = jnp.exp(log_a)
        th = jnp.tanh(log_a)
        u_ref[:, cols] = jnp.sqrt(-2.0 * th / (1.0 - th)) * (gi * xh)


def _lru_scan(gate_ref, o_ref, hc_ref, a_ref, u_ref, h_ref):
    def step(t, h):
        h = a_ref[pl.ds(t, 1), :] * h + u_ref[pl.ds(t, 1), :]
        h_ref[pl.ds(t, 1), :] = h
        return h

    hc_ref[0:1, :] = lax.fori_loop(0, a_ref.shape[0], step, hc_ref[0:1, :], unroll=8)
    o_ref[...] = (h_ref[...] * gate_ref[...].astype(F32)).astype(BF16)


def _proj_lru_kernel(act, tiles_per_seq, x_ref, w_ref, cast_src_ref, xr_ref, gate_ref, cw_ref, cb_ref,
                     wa_ref, ba_ref, wx_ref, bx_ref, lam_ref, o_ref, cast_dst_ref, a_out_ref,
                     xpad_ref, hc_ref, a_ref, u_ref, h_ref):
    step = pl.program_id(0) * pl.num_programs(1) + pl.program_id(1)
    _lru_gates(step % tiles_per_seq == 0, xr_ref, cw_ref, cb_ref, wa_ref, ba_ref, wx_ref, bx_ref,
               lam_ref, xpad_ref, hc_ref, a_ref, u_ref)
    acc = jnp.dot(x_ref[...], w_ref[...], preferred_element_type=F32)
    o_ref[...] = act(acc).astype(BF16)
    cast_dst_ref[...] = cast_src_ref[...].astype(BF16)
    _lru_scan(gate_ref, a_out_ref, hc_ref, a_ref, u_ref, h_ref)


def _proj_lru(xb, w_in, layer, col_from, n_cols, act, name, cast, x_rnn, gelu_cols, seq,
              conv_w, conv_b, w_a, b_a, w_x, b_x, lam):
    T, D = xb.shape
    first = col_from // PROJ_COLS
    grid = (T // PROJ_ROWS, n_cols // PROJ_COLS)
    n_steps = grid[0] * grid[1]
    step_of = lambda i, j: i * grid[1] + j
    lru_rows = T // n_steps
    src, dst, cast_shape = _cast_specs(cast, layer, n_steps, step_of)
    tile = pl.BlockSpec((lru_rows, D_RNN), lambda i, j: (step_of(i, j), 0))
    vec = _const_spec((1, D_RNN))
    gate_w = _const_spec((RNN_HEADS, HEAD_DIM, HEAD_DIM))
    tile_f32 = pltpu.VMEM((lru_rows, D_RNN), F32)
    return pl.pallas_call(
        functools.partial(_proj_lru_kernel, act, seq // lru_rows),
        grid=grid,
        in_specs=[pl.BlockSpec((PROJ_ROWS, D), lambda i, j: (i, 0)),
                  pl.BlockSpec((None, D, PROJ_COLS), lambda i, j: (layer, 0, first + j)),
                  src, tile, tile, _const_spec((CONV_WIDTH, D_RNN)), vec, gate_w, vec, gate_w, vec, vec],
        out_specs=[pl.BlockSpec((PROJ_ROWS, PROJ_COLS), lambda i, j: (i, j)), dst, tile],
        out_shape=[jax.ShapeDtypeStruct((T, n_cols), BF16), cast_shape,
                   jax.ShapeDtypeStruct((T, D_RNN), BF16)],
        scratch_shapes=[pltpu.VMEM((CONV_PAD + lru_rows, D_RNN), F32), pltpu.VMEM((8, D_RNN), F32),
                        tile_f32, tile_f32, tile_f32],
        compiler_params=_params(("arbitrary", "arbitrary"), 52),
        name=name,
    )(xb, w_in, cast, x_rnn, gelu_cols, conv_w, conv_b.reshape(1, D_RNN), w_a.astype(BF16),
      b_a.reshape(1, D_RNN), w_x.astype(BF16), b_x.reshape(1, D_RNN), lam.reshape(1, D_RNN))


def _spatial_gating(u_ref, v_ref, g_ref, b_ref, ws_ref, bs_ref, o_ref):
    rows = u_ref.shape[0]
    v = _layer_norm(v_ref[...].astype(F32), g_ref[...], b_ref[...]).astype(BF16)
    t_out = lax.broadcasted_iota(jnp.int32, (CHUNK, CHUNK), 0)
    t_in = lax.broadcasted_iota(jnp.int32, (CHUNK, CHUNK), 1)
    causal = t_in <= t_out
    for g in range(SGU_GROUPS):
        cols = slice(g * SGU_GROUP_DIM, (g + 1) * SGU_GROUP_DIM)
        ws = jnp.where(causal, ws_ref[g], 0.0).astype(BF16)
        bias = bs_ref[:, g:g + 1]
        for c in range(rows // CHUNK):
            rws = slice(c * CHUNK, (c + 1) * CHUNK)
            mixed = jnp.dot(ws, v[rws, cols], preferred_element_type=F32) + bias
            o_ref[rws, cols] = (u_ref[rws, cols].astype(F32) * mixed).astype(BF16)


def _mix_out_kernel(alpha, x_is_raw, a_ref, u_ref, v_ref, sga_ref, sgb_ref, x_ref, emb_g_ref,
                    emb_b_ref, sgu_g_ref, sgu_b_ref, ws_ref, bs_ref, woa_ref, wob_ref, wout_ref,
                    g_ref, beta_ref, cast_src_ref, o_ref, packed_ref, cast_dst_ref, b_ref):
    _spatial_gating(u_ref, v_ref, sgu_g_ref, sgu_b_ref, ws_ref, bs_ref, b_ref)
    ya = jnp.dot(a_ref[...], woa_ref[...], preferred_element_type=F32)
    yb = jnp.dot(b_ref[...], wob_ref[...], preferred_element_type=F32)
    merged = sga_ref[...].astype(F32) * ya + sgb_ref[...].astype(F32) * yb
    m = jnp.dot(merged.astype(BF16), wout_ref[...], preferred_element_type=F32)
    x = x_ref[...]
    if x_is_raw:
        x = _layer_norm(x, emb_g_ref[...], emb_b_ref[...])
    y = _layer_norm(alpha * x + m, g_ref[...], beta_ref[...])
    o_ref[...] = y
    packed_ref[...] = _pack_halves(y)
    cast_dst_ref[...] = cast_src_ref[...].astype(BF16)


def _mix_out(alpha, layer, a, gelu_cols, sigmoid_cols, x, x_is_raw, emb_ln_g, emb_ln_b,
             sgu_ln_g, sgu_ln_b, w_s, b_s, w_o_rnn, w_o_sgu, w_out, ln_g, ln_b, cast):
    T, D = x.shape
    n_steps = T // MIX_ROWS
    row = lambda w, c: pl.BlockSpec((MIX_ROWS, w), lambda i: (i, c))
    cast_src, cast_dst, cast_shape = _cast_specs(cast, layer, n_steps, lambda i: i)
    return pl.pallas_call(
        functools.partial(_mix_out_kernel, alpha, x_is_raw),
        grid=(n_steps,),
        in_specs=[row(D_RNN, 0), row(D_SGU, 1), row(D_SGU, 2), row(D, 0), row(D, 1), row(D, 0),
                  _const_spec((1, D)), _const_spec((1, D)),
                  _const_spec((1, D_SGU)), _const_spec((1, D_SGU)),
                  _const_spec((SGU_GROUPS, CHUNK, CHUNK)), _const_spec((CHUNK, SGU_GROUPS)),
                  _layer_spec(layer, (D_RNN, D)), _layer_spec(layer, (D_SGU, D)),
                  _layer_spec(layer, (D, D)), _const_spec((1, D)), _const_spec((1, D)), cast_src],
        out_specs=[row(D, 0), row(D // 2, 0), cast_dst],
        out_shape=[jax.ShapeDtypeStruct((T, D), F32), jax.ShapeDtypeStruct((T, D // 2), jnp.uint32),
                   cast_shape],
        scratch_shapes=[pltpu.VMEM((MIX_ROWS, D_SGU), BF16)],
        compiler_params=_params(("parallel",), 56),
        name="mix_out",
    )(a, gelu_cols, gelu_cols, sigmoid_cols, sigmoid_cols, x,
      emb_ln_g.reshape(1, D), emb_ln_b.reshape(1, D),
      sgu_ln_g.reshape(1, D_SGU), sgu_ln_b.reshape(1, D_SGU), w_s, b_s.T,
      w_o_rnn, w_o_sgu, w_out, ln_g.reshape(1, D), ln_b.reshape(1, D), cast)


def _first_max(v, row, n):
    m = jnp.max(v, axis=0, keepdims=True)
    idx = jnp.min(jnp.where(v == m, row, float(n)), axis=0, keepdims=True)
    return m, idx


def _router_kernel(x_ref, wt_ref, b_ref, idx_ref, gate_ref, rank_ref, cnt_ref, run_ref):
    rows = x_ref.shape[0]

    @pl.when(pl.program_id(0) == 0)
    def _():
        run_ref[...] = jnp.zeros_like(run_ref)

    def nt_dot(w, x):
        return lax.dot_general(w, x, (((1,), (1,)), ((), ())), preferred_element_type=F32)

    x = x_ref[...]
    xh = x.astype(BF16)
    xl = (x - xh.astype(F32)).astype(BF16)
    w = wt_ref[...]
    wh = w.astype(BF16)
    wl = (w - wh.astype(F32)).astype(BF16)
    logits = nt_dot(wh, xh) + (nt_dot(wh, xl) + nt_dot(wl, xh))
    e = jnp.exp(logits - jnp.max(logits, axis=0, keepdims=True))
    scores = e / jnp.sum(e, axis=0, keepdims=True)
    sel = scores + b_ref[...]

    row = lax.broadcasted_iota(jnp.int32, (N_EXPERTS, rows), 0).astype(F32)
    grow = lax.broadcasted_iota(jnp.int32, (EXPERTS_PER_GROUP, rows), 0).astype(F32)
    neg_inf = float("-inf")
    best_score = None
    best_group = None
    for g in range(N_GROUPS):
        v = sel[g * EXPERTS_PER_GROUP:(g + 1) * EXPERTS_PER_GROUP, :]
        m1, i1 = _first_max(v, grow, EXPERTS_PER_GROUP)
        m2 = jnp.max(jnp.where(grow == i1, neg_inf, v), axis=0, keepdims=True)
        s = m1 + m2
        if g == 0:
            best_score, best_group = s, jnp.zeros_like(s)
        else:
            better = s > best_score
            best_group = jnp.where(better, float(g), best_group)
            best_score = jnp.where(better, s, best_score)

    lo = best_group * float(EXPERTS_PER_GROUP)
    in_group = jnp.logical_and(row >= lo, row < lo + float(EXPERTS_PER_GROUP))
    masked = jnp.where(in_group, sel, neg_inf)
    _, i1 = _first_max(masked, row, N_EXPERTS)
    pick1 = row == i1
    _, i2 = _first_max(jnp.where(pick1, neg_inf, masked), row, N_EXPERTS)
    pick2 = row == i2
    s1 = jnp.sum(jnp.where(pick1, scores, 0.0), axis=0, keepdims=True)
    s2 = jnp.sum(jnp.where(pick2, scores, 0.0), axis=0, keepdims=True)
    den = s1 + s2

    onehot = jnp.where(jnp.logical_or(pick1, pick2), 1.0, 0.0)
    before = (lax.broadcasted_iota(jnp.int32, (rows, rows), 0)
              < lax.broadcasted_iota(jnp.int32, (rows, rows), 1))
    prefix = jnp.dot(onehot.astype(BF16), jnp.where(before, 1.0, 0.0).astype(BF16),
                     preferred_element_type=F32)
    pos = prefix + run_ref[:, 0:1]
    r1 = jnp.sum(jnp.where(pick1, pos, 0.0), axis=0, keepdims=True)
    r2 = jnp.sum(jnp.where(pick2, pos, 0.0), axis=0, keepdims=True)
    run_ref[...] = run_ref[...] + jnp.sum(onehot, axis=1, keepdims=True)

    idx_ref[0:1, :] = i1.astype(jnp.int32)
    idx_ref[1:2, :] = i2.astype(jnp.int32)
    gate_ref[0:1, :] = s1 / den
    gate_ref[1:2, :] = s2 / den
    rank_ref[0:1, :] = r1.astype(jnp.int32)
    rank_ref[1:2, :] = r2.astype(jnp.int32)
    cnt_ref[...] = run_ref[...].astype(jnp.int32)


def _router(x, router_w, router_b):
    T, D = x.shape
    pair = pl.BlockSpec((TOP_K, ROUTER_ROWS), lambda i: (0, i))
    return pl.pallas_call(
        _router_kernel,
        grid=(T // ROUTER_ROWS,),
        in_specs=[pl.BlockSpec((ROUTER_ROWS, D), lambda i: (i, 0)),
                  _const_spec((N_EXPERTS, D)), _const_spec((N_EXPERTS, 1))],
        out_specs=[pair, pair, pair, pl.BlockSpec((N_EXPERTS, 128), lambda i: (0, 0))],
        out_shape=[jax.ShapeDtypeStruct((TOP_K, T), jnp.int32),
                   jax.ShapeDtypeStruct((TOP_K, T), F32),
                   jax.ShapeDtypeStruct((TOP_K, T), jnp.int32),
                   jax.ShapeDtypeStruct((N_EXPERTS, 128), jnp.int32)],
        scratch_shapes=[pltpu.VMEM((N_EXPERTS, 128), F32)],
        compiler_params=_params(("arbitrary",), 40),
        name="router",
    )(x, router_w.T, router_b.reshape(N_EXPERTS, 1))


def _dispatch_kernel(pad_from_ref, pad_n_ref, n_used_ref, dest_ref, x_ref, xs_hbm, zero_ref, sem, pad_sem):
    i = pl.program_id(0)
    rows = DISPATCH_ROWS
    n_blocks = xs_hbm.shape[0] // EXPERT_ROWS
    zero_rows = zero_ref.shape[0]

    def tail_copies(j):
        block = n_used_ref[0] + j
        off = pl.multiple_of(block * EXPERT_ROWS, EXPERT_ROWS)
        return block < n_blocks, [
            pltpu.make_async_copy(zero_ref, xs_hbm.at[pl.ds(off + part * zero_rows, zero_rows)], pad_sem)
            for part in range(EXPERT_ROWS // zero_rows)]

    def pad_copy(e, bit):
        n = pad_n_ref[e]
        if bit < ROW_TILE:
            off = pad_from_ref[e] + bit - 1
            return pltpu.make_async_copy(zero_ref.at[pl.ds(0, 1)], xs_hbm.at[pl.ds(off, 1)], pad_sem)
        done = (n & (ROW_TILE - 1)) + (n & ~(2 * bit - 1))
        off = pl.multiple_of(pad_from_ref[e] + done, ROW_TILE)
        return pltpu.make_async_copy(zero_ref.at[pl.ds(0, bit)], xs_hbm.at[pl.ds(off, bit)], pad_sem)

    def pad_needed(e, bit):
        n = pad_n_ref[e]
        if bit < ROW_TILE:
            return bit <= (n & (ROW_TILE - 1))
        return (n & bit) != 0

    bits = list(range(1, ROW_TILE)) + [
        1 << k for k in range(ROW_TILE.bit_length() - 1, EXPERT_ROWS.bit_length() - 1)]

    @pl.when(i == 0)
    def _():
        zero_ref[...] = jnp.zeros_like(zero_ref)
        for e in range(N_EXPERTS):
            for bit in bits:
                @pl.when(pad_needed(e, bit))
                def _():
                    pad_copy(e, bit).start()
        for j in range(N_EXPERTS):
            needed, copies = tail_copies(j)

            @pl.when(needed)
            def _():
                for c in copies:
                    c.start()

    for t in range(rows):
        src = x_ref.at[pl.ds(t, 1)]
        for k in range(TOP_K):
            pltpu.make_async_copy(src, xs_hbm.at[pl.ds(dest_ref[0, k * rows + t], 1)],
                                  sem).start(priority=k % N_DMA_PRIORITIES)
    for _ in range(TOP_K):
        pltpu.make_async_copy(x_ref, xs_hbm.at[pl.ds(0, rows)], sem).wait()

    @pl.when(i == 0)
    def _():
        for e in range(N_EXPERTS):
            for bit in bits:
                @pl.when(pad_needed(e, bit))
                def _():
                    pad_copy(e, bit).wait()
        for j in range(N_EXPERTS):
            needed, copies = tail_copies(j)

            @pl.when(needed)
            def _():
                for c in copies:
                    c.wait()


def _dispatch(x, dest_tiles, pad_from, pad_n, n_used, n_slots):
    T, D = x.shape
    grid_spec = pltpu.PrefetchScalarGridSpec(
        num_scalar_prefetch=3,
        grid=(T // DISPATCH_ROWS,),
        in_specs=[pl.BlockSpec((None, 1, TOP_K * DISPATCH_ROWS), lambda i, *_: (i, 0, 0),
                               memory_space=pltpu.SMEM),
                  pl.BlockSpec((DISPATCH_ROWS, D), lambda i, *_: (i, 0))],
        out_specs=pl.BlockSpec(memory_space=pl.ANY),
        scratch_shapes=[pltpu.VMEM((EXPERT_ROWS // 2, D), x.dtype),
                        pltpu.SemaphoreType.DMA(()), pltpu.SemaphoreType.DMA(())],
    )
    return pl.pallas_call(
        _dispatch_kernel,
        grid_spec=grid_spec,
        out_shape=jax.ShapeDtypeStruct((n_slots, D), x.dtype),
        compiler_params=_params(("arbitrary",), 16),
        name="dispatch",
    )(pad_from, pad_n, n_used, dest_tiles, x)


def _expert_kernel(block_e_ref, n_used_ref, xs_ref, w1_ref, w3_ref, w2_ref, o_ref):
    used = pl.program_id(0) < n_used_ref[0]

    @pl.when(jnp.logical_not(used))
    def _():
        o_ref[...] = jnp.zeros_like(o_ref)

    @pl.when(used)
    def _():
        lo, hi = _unpack_halves(xs_ref[...])
        x = jnp.concatenate([lo.astype(BF16), hi.astype(BF16)], axis=1)
        h1 = jnp.dot(x, w1_ref[...], preferred_element_type=F32)
        h3 = jnp.dot(x, w3_ref[...], preferred_element_type=F32)
        hidden = (h1 * _sigmoid(h1)) * h3
        out = jnp.dot(hidden.astype(BF16), w2_ref[...], preferred_element_type=F32)
        o_ref[...] = _pack_halves(out)


def _experts(xs, block_e, n_used, w1, w3, w2):
    P, packed = xs.shape
    D = 2 * packed
    n_blocks = P // EXPERT_ROWS
    used_rows = pl.BlockSpec((EXPERT_ROWS, packed),
                             lambda i, be, nu: (jnp.maximum(jnp.minimum(i, nu[0] - 1), 0), 0))
    rows = pl.BlockSpec((EXPERT_ROWS, packed), lambda i, be, nu: (i, 0))
    expert_slab = lambda i, be, nu: (be[i], 0, 0)
    grid_spec = pltpu.PrefetchScalarGridSpec(
        num_scalar_prefetch=2,
        grid=(n_blocks,),
        in_specs=[used_rows,
                  pl.BlockSpec((None, D, D_EXPERT), expert_slab),
                  pl.BlockSpec((None, D, D_EXPERT), expert_slab),
                  pl.BlockSpec((None, D_EXPERT, D), expert_slab)],
        out_specs=rows,
    )
    return pl.pallas_call(
        _expert_kernel,
        grid_spec=grid_spec,
        out_shape=jax.ShapeDtypeStruct((P, packed), jnp.uint32),
        compiler_params=_params(("arbitrary",), 56),
        name="experts",
    )(block_e, n_used, xs, w1, w3, w2)


def _combine_kernel(alpha, dest_ref, dest_next_ref, x_ref, gate_ref, g_ref, b_ref, y_hbm,
                    of_ref, ob_ref, buf_a, buf_b, sem):
    rows = COMBINE_ROWS
    half = rows // 2
    i = pl.program_id(0)

    def copy(table_ref, h, t, k, buf, s):
        return pltpu.make_async_copy(y_hbm.at[pl.ds(table_ref[0, k * rows + h * half + t], 1)],
                                     buf.at[k, pl.ds(t, 1)], sem.at[s])

    def gather(table_ref, h, buf, s):
        for t in range(half):
            for k in range(TOP_K):
                copy(table_ref, h, t, k, buf, s).start(priority=k % N_DMA_PRIORITIES)

    def wait(buf, s):
        for k in range(TOP_K):
            pltpu.make_async_copy(y_hbm.at[pl.ds(0, half)], buf.at[k], sem.at[s]).wait()

    def reduce(buf, h):
        r = slice(h * half, (h + 1) * half)
        lo0, hi0 = _unpack_halves(buf[0])
        lo1, hi1 = _unpack_halves(buf[1])
        g0 = gate_ref[r, 0:1]
        g1 = gate_ref[r, 1:2]
        f = jnp.concatenate([g0 * lo0 + g1 * lo1, g0 * hi0 + g1 * hi1], axis=1)
        y = _layer_norm(alpha * x_ref[r, :] + f, g_ref[...], b_ref[...])
        of_ref[r, :] = y
        ob_ref[r, :] = y.astype(BF16)

    @pl.when(i == 0)
    def _():
        def issue(t, carry):
            for k in range(TOP_K):
                copy(dest_ref, 0, t, k, buf_a, 0).start()
            return carry

        lax.fori_loop(0, half, issue, 0, unroll=8)

    wait(buf_a, 0)
    gather(dest_ref, 1, buf_b, 1)
    reduce(buf_a, 0)
    wait(buf_b, 1)
    gather(dest_next_ref, 0, buf_a, 0)
    reduce(buf_b, 1)

    @pl.when(i == pl.num_programs(0) - 1)
    def _():
        wait(buf_a, 0)


def _combine(alpha, dest_tiles, x, gate_t, ln_g, ln_b, yb):
    T, D = x.shape
    n_tiles = T // COMBINE_ROWS
    row = pl.BlockSpec((COMBINE_ROWS, D), lambda i: (i, 0))
    table = lambda index: pl.BlockSpec((None, 1, TOP_K * COMBINE_ROWS), index, memory_space=pltpu.SMEM)
    half_buf = pltpu.VMEM((TOP_K, COMBINE_ROWS // 2, yb.shape[1]), yb.dtype)
    return pl.pallas_call(
        functools.partial(_combine_kernel, alpha),
        grid=(n_tiles,),
        in_specs=[table(lambda i: (i, 0, 0)),
                  table(lambda i: (jnp.minimum(i + 1, n_tiles - 1), 0, 0)),
                  row, pl.BlockSpec((COMBINE_ROWS, TOP_K), lambda i: (i, 0)),
                  _const_spec((1, D)), _const_spec((1, D)),
                  pl.BlockSpec(memory_space=pl.ANY)],
        out_specs=[row, row],
        out_shape=[jax.ShapeDtypeStruct((T, D), F32), jax.ShapeDtypeStruct((T, D), BF16)],
        scratch_shapes=[half_buf, half_buf, pltpu.SemaphoreType.DMA((2,))],
        compiler_params=_params(("arbitrary",), 48),
        name="combine",
    )(dest_tiles, dest_tiles, x, gate_t, ln_g.reshape(1, D), ln_b.reshape(1, D), yb)


def _tile_pairs(dest, rows):
    T = dest.shape[1]
    return dest.reshape(TOP_K, T // rows, rows).transpose(1, 0, 2).reshape(T // rows, 1, TOP_K * rows)


def _moe(alpha, x, x_packed, router_w, router_b, w1, w3, w2, ln_g, ln_b):
    T, D = x.shape
    n_blocks = (T * TOP_K) // EXPERT_ROWS + N_EXPERTS
    idx, gate, rank, cnt = _router(x, router_w, router_b)

    counts = cnt[:, 0]
    padded = (counts + EXPERT_ROWS - 1) // EXPERT_ROWS * EXPERT_ROWS
    pad_end = jnp.cumsum(padded)
    pad_start = pad_end - padded
    n_used = (pad_end[-1] // EXPERT_ROWS).astype(jnp.int32)
    block_start = jnp.minimum(jnp.arange(n_blocks, dtype=jnp.int32), n_used - 1) * EXPERT_ROWS
    block_e = jnp.minimum(jnp.sum(block_start[:, None] >= pad_end[None, :], axis=1),
                          N_EXPERTS - 1).astype(jnp.int32)
    expert_ids = jnp.arange(N_EXPERTS, dtype=jnp.int32)[:, None, None]
    dest = jnp.sum(jnp.where(idx[None] == expert_ids, pad_start[:, None, None], 0), axis=0) + rank

    n_used = n_used.reshape(1)
    xs = _dispatch(x_packed, _tile_pairs(dest, DISPATCH_ROWS), (pad_start + counts).astype(jnp.int32),
                   (padded - counts).astype(jnp.int32), n_used, n_blocks * EXPERT_ROWS)
    yb = _experts(xs, block_e, n_used, w1, w3, w2)
    return _combine(alpha, _tile_pairs(dest, COMBINE_ROWS), x, gate.T, ln_g, ln_b, yb)


def kernel(x, emb_ln_g, emb_ln_b, w_in, conv_w, conv_b, lru_w_a, lru_b_a, lru_w_x, lru_b_x, lru_lambda, w_o_rnn, sgu_ln_g, sgu_ln_b, sgu_w_s, sgu_b_s, w_o_sgu, w_out, ln1_g, ln1_b, router_w, router_b, expert_w1, expert_w3, expert_w2, ln2_g, ln2_b):
    batch, seq, D = x.shape
    depth = w_in.shape[0]
    alpha = float((2 * depth) ** 0.25)
    xf = x.reshape(batch * seq, D)
    xb = _emb_ln(xf, emb_ln_g, emb_ln_b)
    w_in, w_o_rnn, w_o_sgu, w_out = (w.astype(BF16) for w in (w_in, w_o_rnn, w_o_sgu, w_out))
    rows_of = lambda w: w.reshape(depth, N_EXPERTS * w.shape[2], w.shape[3])
    w1_rows, w3_rows, w2_rows = rows_of(expert_w1), rows_of(expert_w3), rows_of(expert_w2)
    gelu_from = D_RNN
    sigmoid_from = 2 * D_RNN + 2 * D_SGU
    for l in range(depth):
        x_rnn = _proj(xb, w_in, l, 0, gelu_from, lambda v: v, "proj_x")
        gelu_cols, w3 = _proj(xb, w_in, l, gelu_from, sigmoid_from - gelu_from, _gelu, "proj_gelu",
                              cast=w3_rows)
        sigmoid_cols, w1, a = _proj_lru(
            xb, w_in, l, sigmoid_from, D_IN - sigmoid_from, _sigmoid, "proj_sigmoid_lru", w1_rows,
            x_rnn, gelu_cols, seq, conv_w[l], conv_b[l], lru_w_a[l], lru_b_a[l], lru_w_x[l],
            lru_b_x[l], lru_lambda[l])
        x1, x1_packed, w2 = _mix_out(alpha, l, a, gelu_cols, sigmoid_cols, xf, l == 0, emb_ln_g,
                                     emb_ln_b, sgu_ln_g[l], sgu_ln_b[l], sgu_w_s[l], sgu_b_s[l],
                                     w_o_rnn, w_o_sgu, w_out, ln1_g[l], ln1_b[l], cast=w2_rows)
        xf, xb = _moe(alpha, x1, x1_packed, router_w, router_b, w1.reshape(expert_w1.shape[1:]),
                      w3.reshape(expert_w3.shape[1:]), w2.reshape(expert_w2.shape[1:]),
                      ln2_g[l], ln2_b[l])
    return xf.reshape(batch, seq, D)
```

```python
import functools

import jax
import jax.numpy as jnp
import numpy as np
from jax import lax
from jax.experimental import pallas as pl
from jax.experimental.pallas import tpu as pltpu

F32 = jnp.float32
BF16 = jnp.bfloat16

D_MODEL = 2048
D_RNN = 1024
RNN_HEADS = 8
HEAD_DIM = D_RNN // RNN_HEADS
CONV_WIDTH = 4
LRU_C = 8.0
D_SGU = 1024
SGU_GROUPS = 8
SGU_GROUP_DIM = D_SGU // SGU_GROUPS
CHUNK = 128
N_EXPERTS = 16
N_GROUPS = 4
EXPERTS_PER_GROUP = N_EXPERTS // N_GROUPS
TOP_K = 2
D_EXPERT = 1408
D_IN = 2 * D_RNN + 2 * D_SGU + 2 * D_MODEL
LN_EPS = 1e-5
SQRT_2_OVER_PI = float(np.sqrt(2.0 / np.pi))

LN_ROWS = 512
PROJ_ROWS = 1024
PROJ_COLS = 1024
MIX_ROWS = 256
ROUTER_ROWS = 1024
DISPATCH_ROWS = 512
EXPERT_ROWS = 256
COMBINE_ROWS = 512
CONV_PAD = 8
N_DMA_PRIORITIES = 2
ROW_TILE = 8

MIB = 1024 * 1024


def _params(semantics, vmem_mib):
    return pltpu.CompilerParams(dimension_semantics=semantics,
                                vmem_limit_bytes=vmem_mib * MIB,
                                disable_bounds_checks=True)


def _const_spec(shape):
    zeros = (0,) * len(shape)
    return pl.BlockSpec(shape, lambda *_: zeros, pipeline_mode=pl.Buffered(1))


def _layer_spec(layer, shape):
    zeros = (0,) * len(shape)
    return pl.BlockSpec((None,) + tuple(shape), lambda *_: (layer,) + zeros,
                        pipeline_mode=pl.Buffered(1))


def _layer_norm(v, g, b):
    mu = jnp.mean(v, axis=-1, keepdims=True)
    d = v - mu
    var = jnp.mean(d * d, axis=-1, keepdims=True)
    return d * lax.rsqrt(var + LN_EPS) * g + b


def _gelu(x):
    inner = x * (SQRT_2_OVER_PI + (SQRT_2_OVER_PI * 0.044715) * (x * x))
    return x * (0.5 * jnp.tanh(inner) + 0.5)


def _pack_halves(y):
    n = y.shape[1] // 2
    bits = lambda v: lax.bitcast_convert_type(v.astype(BF16).astype(F32), jnp.uint32)
    return (bits(y[:, :n]) >> 16) | bits(y[:, n:])


def _unpack_halves(p):
    lo = lax.bitcast_convert_type(p << 16, F32)
    hi = lax.bitcast_convert_type(p & jnp.uint32(0xFFFF0000), F32)
    return lo, hi


def _sigmoid(x):
    return 0.5 * jnp.tanh(0.5 * x) + 0.5


def _emb_ln_kernel(x_ref, g_ref, b_ref, ob_ref):
    ob_ref[...] = _layer_norm(x_ref[...], g_ref[...], b_ref[...]).astype(BF16)


def _emb_ln(x, g, b):
    T, D = x.shape
    row = pl.BlockSpec((LN_ROWS, D), lambda i: (i, 0))
    return pl.pallas_call(
        _emb_ln_kernel,
        grid=(T // LN_ROWS,),
        in_specs=[row, _const_spec((1, D)), _const_spec((1, D))],
        out_specs=row,
        out_shape=jax.ShapeDtypeStruct((T, D), BF16),
        compiler_params=_params(("parallel",), 40),
        name="emb_ln",
    )(x, g.reshape(1, D), b.reshape(1, D))


def _cast_specs(stack, layer, n_steps, step_of):
    _, rows, cols = stack.shape
    slab = -(-rows // (n_steps * 16)) * 16
    last = -(-rows // slab) - 1
    src = pl.BlockSpec((None, slab, cols), lambda *g: (layer, jnp.minimum(step_of(*g), last), 0))
    dst = pl.BlockSpec((slab, cols), lambda *g: (jnp.minimum(step_of(*g), last), 0))
    return src, dst, jax.ShapeDtypeStruct((rows, cols), BF16)


def _proj_kernel(act, x_ref, w_ref, *rest):
    o_ref = rest[len(rest) // 2]
    acc = jnp.dot(x_ref[...], w_ref[...], preferred_element_type=F32)
    o_ref[...] = act(acc).astype(BF16)
    for src_ref, dst_ref in zip(rest[:len(rest) // 2], rest[len(rest) // 2 + 1:]):
        dst_ref[...] = src_ref[...].astype(BF16)


def _proj(xb, w_in, layer, col_from, n_cols, act, name, cast=None):
    T, D = xb.shape
    first = col_from // PROJ_COLS
    grid = (T // PROJ_ROWS, n_cols // PROJ_COLS)
    in_specs = [pl.BlockSpec((PROJ_ROWS, D), lambda i, j: (i, 0)),
                pl.BlockSpec((None, D, PROJ_COLS), lambda i, j: (layer, 0, first + j))]
    out_specs = [pl.BlockSpec((PROJ_ROWS, PROJ_COLS), lambda i, j: (i, j))]
    out_shape = [jax.ShapeDtypeStruct((T, n_cols), BF16)]
    operands = [xb, w_in]
    if cast is not None:
        src, dst, shape = _cast_specs(*cast, grid[0] * grid[1], lambda i, j: i * grid[1] + j)
        in_specs.append(src)
        out_specs.append(dst)
        out_shape.append(shape)
        operands.append(cast[0])
    out = pl.pallas_call(
        functools.partial(_proj_kernel, act),
        grid=grid,
        in_specs=in_specs,
        out_specs=out_specs,
        out_shape=out_shape,
        compiler_params=_params(("parallel", "arbitrary"), 48),
        name=name,
    )(*operands)
    return out[0] if cast is None else out


def _lru_gates(first, xr_ref, cw_ref, cb_ref, wa_ref, ba_ref, wx_ref, bx_ref, lam_ref,
               xpad_ref, hc_ref, a_ref, u_ref):
    rows = xr_ref.shape[0]

    @pl.when(first)
    def _():
        xpad_ref[0:CONV_PAD, :] = jnp.zeros((CONV_PAD, D_RNN), F32)
        hc_ref[...] = jnp.zeros_like(hc_ref)

    x = xr_ref[...].astype(F32)
    xpad_ref[CONV_PAD:CONV_PAD + rows, :] = x
    xc = cb_ref[...] + cw_ref[CONV_WIDTH - 1:CONV_WIDTH, :] * x
    for k in range(CONV_WIDTH - 1):
        shift = CONV_WIDTH - 1 - k
        xc = xc + cw_ref[k:k + 1, :] * xpad_ref[CONV_PAD - shift:CONV_PAD - shift + rows, :]
    xpad_ref[0:CONV_PAD, :] = x[rows - CONV_PAD:rows, :]

    lam = lam_ref[...]
    sp = jnp.maximum(-lam, 0.0) + jnp.log1p(jnp.exp(-jnp.abs(lam)))
    for h in range(RNN_HEADS):
        cols = slice(h * HEAD_DIM, (h + 1) * HEAD_DIM)
        xh = xc[:, cols]
        xhb = xh.astype(BF16)
        r = _sigmoid(jnp.dot(xhb, wa_ref[h], preferred_element_type=F32) + ba_ref[:, cols])
        gi = _sigmoid(jnp.dot(xhb, wx_ref[h], preferred_element_type=F32) + bx_ref[:, cols])
        log_a = (-LRU_C) * r * sp[:, cols]
        a_ref[:, cols] = jnp.exp(log_a)
        th = jnp.tanh(log_a)
        u_ref[:, cols] = jnp.sqrt(-2.0 * th / (1.0 - th)) * (gi * xh)


def _lru_scan(gate_ref, o_ref, hc_ref, a_ref, u_ref, h_ref):
    def step(t, h):
        h = a_ref[pl.ds(t, 1), :] * h + u_ref[pl.ds(t, 1), :]
        h_ref[pl.ds(t, 1), :] = h
        return h

    hc_ref[0:1, :] = lax.fori_loop(0, a_ref.shape[0], step, hc_ref[0:1, :], unroll=8)
    o_ref[...] = (h_ref[...] * gate_ref[...].astype(F32)).astype(BF16)


def _proj_lru_kernel(act, tiles_per_seq, x_ref, w_ref, cast_src_ref, xr_ref, gate_ref, cw_ref, cb_ref,
                     wa_ref, ba_ref, wx_ref, bx_ref, lam_ref, o_ref, cast_dst_ref, a_out_ref,
                     xpad_ref, hc_ref, a_ref, u_ref, h_ref):
    step = pl.program_id(0) * pl.num_programs(1) + pl.program_id(1)
    _lru_gates(step % tiles_per_seq == 0, xr_ref, cw_ref, cb_ref, wa_ref, ba_ref, wx_ref, bx_ref,
               lam_ref, xpad_ref, hc_ref, a_ref, u_ref)
    acc = jnp.dot(x_ref[...], w_ref[...], preferred_element_type=F32)
    o_ref[...] = act(acc).astype(BF16)
    cast_dst_ref[...] = cast_src_ref[...].astype(BF16)
    _lru_scan(gate_ref, a_out_ref, hc_ref, a_ref, u_ref, h_ref)


def _proj_lru(xb, w_in, layer, col_from, n_cols, act, name, cast, x_rnn, gelu_cols, seq,
              conv_w, conv_b, w_a, b_a, w_x, b_x, lam):
    T, D = xb.shape
    first = col_from // PROJ_COLS
    grid = (T // PROJ_ROWS, n_cols // PROJ_COLS)
    n_steps = grid[0] * grid[1]
    step_of = lambda i, j: i * grid[1] + j
    lru_rows = T // n_steps
    src, dst, cast_shape = _cast_specs(*cast, n_steps, step_of)
    tile = pl.BlockSpec((lru_rows, D_RNN), lambda i, j: (step_of(i, j), 0))
    vec = _const_spec((1, D_RNN))
    gate_w = _const_spec((RNN_HEADS, HEAD_DIM, HEAD_DIM))
    tile_f32 = pltpu.VMEM((lru_rows, D_RNN), F32)
    return pl.pallas_call(
        functools.partial(_proj_lru_kernel, act, seq // lru_rows),
        grid=grid,
        in_specs=[pl.BlockSpec((PROJ_ROWS, D), lambda i, j: (i, 0)),
                  pl.BlockSpec((None, D, PROJ_COLS), lambda i, j: (layer, 0, first + j)),
                  src, tile, tile, _const_spec((CONV_WIDTH, D_RNN)), vec, gate_w, vec, gate_w, vec, vec],
        out_specs=[pl.BlockSpec((PROJ_ROWS, PROJ_COLS), lambda i, j: (i, j)), dst, tile],
        out_shape=[jax.ShapeDtypeStruct((T, n_cols), BF16), cast_shape,
                   jax.ShapeDtypeStruct((T, D_RNN), BF16)],
        scratch_shapes=[pltpu.VMEM((CONV_PAD + lru_rows, D_RNN), F32), pltpu.VMEM((8, D_RNN), F32),
                        tile_f32, tile_f32, tile_f32],
        compiler_params=_params(("arbitrary", "arbitrary"), 52),
        name=name,
    )(xb, w_in, cast[0], x_rnn, gelu_cols, conv_w, conv_b.reshape(1, D_RNN), w_a.astype(BF16),
      b_a.reshape(1, D_RNN), w_x.astype(BF16), b_x.reshape(1, D_RNN), lam.reshape(1, D_RNN))


def _spatial_gating(u_ref, v_ref, g_ref, b_ref, ws_ref, bs_ref, o_ref):
    rows = u_ref.shape[0]
    v = _layer_norm(v_ref[...].astype(F32), g_ref[...], b_ref[...]).astype(BF16)
    t_out = lax.broadcasted_iota(jnp.int32, (CHUNK, CHUNK), 0)
    t_in = lax.broadcasted_iota(jnp.int32, (CHUNK, CHUNK), 1)
    causal = t_in <= t_out
    for g in range(SGU_GROUPS):
        cols = slice(g * SGU_GROUP_DIM, (g + 1) * SGU_GROUP_DIM)
        ws = jnp.where(causal, ws_ref[g], 0.0).astype(BF16)
        bias = bs_ref[:, g:g + 1]
        for c in range(rows // CHUNK):
            rws = slice(c * CHUNK, (c + 1) * CHUNK)
            mixed = jnp.dot(ws, v[rws, cols], preferred_element_type=F32) + bias
            o_ref[rws, cols] = (u_ref[rws, cols].astype(F32) * mixed).astype(BF16)


def _mix_out_kernel(alpha, x_is_raw, a_ref, u_ref, v_ref, sga_ref, sgb_ref, x_ref, emb_g_ref,
                    emb_b_ref, sgu_g_ref, sgu_b_ref, ws_ref, bs_ref, woa_ref, wob_ref, wout_ref,
                    g_ref, beta_ref, cast_src_ref, o_ref, packed_ref, cast_dst_ref, b_ref):
    _spatial_gating(u_ref, v_ref, sgu_g_ref, sgu_b_ref, ws_ref, bs_ref, b_ref)
    ya = jnp.dot(a_ref[...], woa_ref[...], preferred_element_type=F32)
    yb = jnp.dot(b_ref[...], wob_ref[...], preferred_element_type=F32)
    merged = sga_ref[...].astype(F32) * ya + sgb_ref[...].astype(F32) * yb
    m = jnp.dot(merged.astype(BF16), wout_ref[...], preferred_element_type=F32)
    x = x_ref[...]
    if x_is_raw:
        x = _layer_norm(x, emb_g_ref[...], emb_b_ref[...])
    y = _layer_norm(alpha * x + m, g_ref[...], beta_ref[...])
    o_ref[...] = y
    packed_ref[...] = _pack_halves(y)
    cast_dst_ref[...] = cast_src_ref[...].astype(BF16)


def _mix_out(alpha, layer, a, gelu_cols, sigmoid_cols, x, x_is_raw, emb_ln_g, emb_ln_b,
             sgu_ln_g, sgu_ln_b, w_s, b_s, w_o_rnn, w_o_sgu, w_out, ln_g, ln_b, cast):
    T, D = x.shape
    n_steps = T // MIX_ROWS
    row = lambda w, c: pl.BlockSpec((MIX_ROWS, w), lambda i: (i, c))
    cast_src, cast_dst, cast_shape = _cast_specs(*cast, n_steps, lambda i: i)
    return pl.pallas_call(
        functools.partial(_mix_out_kernel, alpha, x_is_raw),
        grid=(n_steps,),
        in_specs=[row(D_RNN, 0), row(D_SGU, 1), row(D_SGU, 2), row(D, 0), row(D, 1), row(D, 0),
                  _const_spec((1, D)), _const_spec((1, D)),
                  _const_spec((1, D_SGU)), _const_spec((1, D_SGU)),
                  _const_spec((SGU_GROUPS, CHUNK, CHUNK)), _const_spec((CHUNK, SGU_GROUPS)),
                  _layer_spec(layer, (D_RNN, D)), _layer_spec(layer, (D_SGU, D)),
                  _layer_spec(layer, (D, D)), _const_spec((1, D)), _const_spec((1, D)), cast_src],
        out_specs=[row(D, 0), row(D // 2, 0), cast_dst],
        out_shape=[jax.ShapeDtypeStruct((T, D), F32), jax.ShapeDtypeStruct((T, D // 2), jnp.uint32),
                   cast_shape],
        scratch_shapes=[pltpu.VMEM((MIX_ROWS, D_SGU), BF16)],
        compiler_params=_params(("parallel",), 56),
        name="mix_out",
    )(a, gelu_cols, gelu_cols, sigmoid_cols, sigmoid_cols, x,
      emb_ln_g.reshape(1, D), emb_ln_b.reshape(1, D),
      sgu_ln_g.reshape(1, D_SGU), sgu_ln_b.reshape(1, D_SGU), w_s, b_s.T,
      w_o_rnn, w_o_sgu, w_out, ln_g.reshape(1, D), ln_b.reshape(1, D), cast[0])


def _first_max(v, row, n):
    m = jnp.max(v, axis=0, keepdims=True)
    idx = jnp.min(jnp.where(v == m, row, float(n)), axis=0, keepdims=True)
    return m, idx


def _router_kernel(x_ref, wt_ref, b_ref, idx_ref, gate_ref, rank_ref, cnt_ref, run_ref):
    rows = x_ref.shape[0]

    @pl.when(pl.program_id(0) == 0)
    def _():
        run_ref[...] = jnp.zeros_like(run_ref)

    def nt_dot(w, x):
        return lax.dot_general(w, x, (((1,), (1,)), ((), ())), preferred_element_type=F32)

    x = x_ref[...]
    xh = x.astype(BF16)
    xl = (x - xh.astype(F32)).astype(BF16)
    w = wt_ref[...]
    wh = w.astype(BF16)
    wl = (w - wh.astype(F32)).astype(BF16)
    logits = nt_dot(wh, xh) + (nt_dot(wh, xl) + nt_dot(wl, xh))
    e = jnp.exp(logits - jnp.max(logits, axis=0, keepdims=True))
    scores = e / jnp.sum(e, axis=0, keepdims=True)
    sel = scores + b_ref[...]

    row = lax.broadcasted_iota(jnp.int32, (N_EXPERTS, rows), 0).astype(F32)
    grow = lax.broadcasted_iota(jnp.int32, (EXPERTS_PER_GROUP, rows), 0).astype(F32)
    neg_inf = float("-inf")
    best_score = None
    best_group = None
    for g in range(N_GROUPS):
        v = sel[g * EXPERTS_PER_GROUP:(g + 1) * EXPERTS_PER_GROUP, :]
        m1, i1 = _first_max(v, grow, EXPERTS_PER_GROUP)
        m2 = jnp.max(jnp.where(grow == i1, neg_inf, v), axis=0, keepdims=True)
        s = m1 + m2
        if g == 0:
            best_score, best_group = s, jnp.zeros_like(s)
        else:
            better = s > best_score
            best_group = jnp.where(better, float(g), best_group)
            best_score = jnp.where(better, s, best_score)

    lo = best_group * float(EXPERTS_PER_GROUP)
    in_group = jnp.logical_and(row >= lo, row < lo + float(EXPERTS_PER_GROUP))
    masked = jnp.where(in_group, sel, neg_inf)
    _, i1 = _first_max(masked, row, N_EXPERTS)
    pick1 = row == i1
    _, i2 = _first_max(jnp.where(pick1, neg_inf, masked), row, N_EXPERTS)
    pick2 = row == i2
    s1 = jnp.sum(jnp.where(pick1, scores, 0.0), axis=0, keepdims=True)
    s2 = jnp.sum(jnp.where(pick2, scores, 0.0), axis=0, keepdims=True)
    den = s1 + s2

    onehot = jnp.where(jnp.logical_or(pick1, pick2), 1.0, 0.0)
    before = (lax.broadcasted_iota(jnp.int32, (rows, rows), 0)
              < lax.broadcasted_iota(jnp.int32, (rows, rows), 1))
    prefix = jnp.dot(onehot.astype(BF16), jnp.where(before, 1.0, 0.0).astype(BF16),
                     preferred_element_type=F32)
    pos = prefix + run_ref[:, 0:1]
    r1 = jnp.sum(jnp.where(pick1, pos, 0.0), axis=0, keepdims=True)
    r2 = jnp.sum(jnp.where(pick2, pos, 0.0), axis=0, keepdims=True)
    run_ref[...] = run_ref[...] + jnp.sum(onehot, axis=1, keepdims=True)

    idx_ref[0:1, :] = i1.astype(jnp.int32)
    idx_ref[1:2, :] = i2.astype(jnp.int32)
    gate_ref[0:1, :] = s1 / den
    gate_ref[1:2, :] = s2 / den
    rank_ref[0:1, :] = r1.astype(jnp.int32)
    rank_ref[1:2, :] = r2.astype(jnp.int32)
    cnt_ref[...] = run_ref[...].astype(jnp.int32)


def _router(x, router_w, router_b):
    T, D = x.shape
    pair = pl.BlockSpec((TOP_K, ROUTER_ROWS), lambda i: (0, i))
    return pl.pallas_call(
        _router_kernel,
        grid=(T // ROUTER_ROWS,),
        in_specs=[pl.BlockSpec((ROUTER_ROWS, D), lambda i: (i, 0)),
                  _const_spec((N_EXPERTS, D)), _const_spec((N_EXPERTS, 1))],
        out_specs=[pair, pair, pair, pl.BlockSpec((N_EXPERTS, 128), lambda i: (0, 0))],
        out_shape=[jax.ShapeDtypeStruct((TOP_K, T), jnp.int32),
                   jax.ShapeDtypeStruct((TOP_K, T), F32),
                   jax.ShapeDtypeStruct((TOP_K, T), jnp.int32),
                   jax.ShapeDtypeStruct((N_EXPERTS, 128), jnp.int32)],
        scratch_shapes=[pltpu.VMEM((N_EXPERTS, 128), F32)],
        compiler_params=_params(("arbitrary",), 40),
        name="router",
    )(x, router_w.T, router_b.reshape(N_EXPERTS, 1))


def _dispatch_kernel(pad_from_ref, pad_n_ref, n_used_ref, dest_ref, x_ref, xs_hbm, zero_ref, sem, pad_sem):
    i = pl.program_id(0)
    rows = DISPATCH_ROWS
    n_blocks = xs_hbm.shape[0] // EXPERT_ROWS
    zero_rows = zero_ref.shape[0]

    def tail_copies(j):
        block = n_used_ref[0] + j
        off = pl.multiple_of(block * EXPERT_ROWS, EXPERT_ROWS)
        return block < n_blocks, [
            pltpu.make_async_copy(zero_ref, xs_hbm.at[pl.ds(off + part * zero_rows, zero_rows)], pad_sem)
            for part in range(EXPERT_ROWS // zero_rows)]

    def pad_copy(e, bit):
        n = pad_n_ref[e]
        if bit < ROW_TILE:
            off = pad_from_ref[e] + bit - 1
            return pltpu.make_async_copy(zero_ref.at[pl.ds(0, 1)], xs_hbm.at[pl.ds(off, 1)], pad_sem)
        done = (n & (ROW_TILE - 1)) + (n & ~(2 * bit - 1))
        off = pl.multiple_of(pad_from_ref[e] + done, ROW_TILE)
        return pltpu.make_async_copy(zero_ref.at[pl.ds(0, bit)], xs_hbm.at[pl.ds(off, bit)], pad_sem)

    def pad_needed(e, bit):
        n = pad_n_ref[e]
        if bit < ROW_TILE:
            return bit <= (n & (ROW_TILE - 1))
        return (n & bit) != 0

    bits = list(range(1, ROW_TILE)) + [
        1 << k for k in range(ROW_TILE.bit_length() - 1, EXPERT_ROWS.bit_length() - 1)]

    @pl.when(i == 0)
    def _():
        zero_ref[...] = jnp.zeros_like(zero_ref)
        for e in range(N_EXPERTS):
            for bit in bits:
                @pl.when(pad_needed(e, bit))
                def _():
                    pad_copy(e, bit).start()
        for j in range(N_EXPERTS):
            needed, copies = tail_copies(j)

            @pl.when(needed)
            def _():
                for c in copies:
                    c.start()

    for t in range(rows):
        src = x_ref.at[pl.ds(t, 1)]
        for k in range(TOP_K):
            pltpu.make_async_copy(src, xs_hbm.at[pl.ds(dest_ref[0, k * rows + t], 1)],
                                  sem).start(priority=k % N_DMA_PRIORITIES)
    for _ in range(TOP_K):
        pltpu.make_async_copy(x_ref, xs_hbm.at[pl.ds(0, rows)], sem).wait()

    @pl.when(i == 0)
    def _():
        for e in range(N_EXPERTS):
            for bit in bits:
                @pl.when(pad_needed(e, bit))
                def _():
                    pad_copy(e, bit).wait()
        for j in range(N_EXPERTS):
            needed, copies = tail_copies(j)

            @pl.when(needed)
            def _():
                for c in copies:
                    c.wait()


def _dispatch(x, dest_tiles, pad_from, pad_n, n_used, n_slots):
    T, D = x.shape
    grid_spec = pltpu.PrefetchScalarGridSpec(
        num_scalar_prefetch=3,
        grid=(T // DISPATCH_ROWS,),
        in_specs=[pl.BlockSpec((None, 1, TOP_K * DISPATCH_ROWS), lambda i, *_: (i, 0, 0),
                               memory_space=pltpu.SMEM),
                  pl.BlockSpec((DISPATCH_ROWS, D), lambda i, *_: (i, 0))],
        out_specs=pl.BlockSpec(memory_space=pl.ANY),
        scratch_shapes=[pltpu.VMEM((EXPERT_ROWS // 2, D), x.dtype),
                        pltpu.SemaphoreType.DMA(()), pltpu.SemaphoreType.DMA(())],
    )
    return pl.pallas_call(
        _dispatch_kernel,
        grid_spec=grid_spec,
        out_shape=jax.ShapeDtypeStruct((n_slots, D), x.dtype),
        compiler_params=_params(("arbitrary",), 16),
        name="dispatch",
    )(pad_from, pad_n, n_used, dest_tiles, x)


def _expert_kernel(block_e_ref, n_used_ref, xs_ref, w1_ref, w3_ref, w2_ref, *rest):
    n_casts = len(rest) // 2
    o_ref = rest[n_casts]
    used = pl.program_id(0) < n_used_ref[0]

    @pl.when(jnp.logical_not(used))
    def _():
        o_ref[...] = jnp.zeros_like(o_ref)

    @pl.when(used)
    def _():
        lo, hi = _unpack_halves(xs_ref[...])
        x = jnp.concatenate([lo.astype(BF16), hi.astype(BF16)], axis=1)
        h1 = jnp.dot(x, w1_ref[...], preferred_element_type=F32)
        h3 = jnp.dot(x, w3_ref[...], preferred_element_type=F32)
        hidden = (h1 * _sigmoid(h1)) * h3
        out = jnp.dot(hidden.astype(BF16), w2_ref[...], preferred_element_type=F32)
        o_ref[...] = _pack_halves(out)
        for src_ref, dst_ref in zip(rest[:n_casts], rest[n_casts + 1:]):
            dst_ref[...] = src_ref[...].astype(BF16)


def _experts(xs, block_e, n_used, w1, w3, w2, casts=()):
    P, packed = xs.shape
    D = 2 * packed
    n_blocks = P // EXPERT_ROWS
    used_rows = pl.BlockSpec((EXPERT_ROWS, packed),
                             lambda i, be, nu: (jnp.maximum(jnp.minimum(i, nu[0] - 1), 0), 0))
    rows = pl.BlockSpec((EXPERT_ROWS, packed), lambda i, be, nu: (i, 0))
    expert_slab = lambda i, be, nu: (be[i], 0, 0)
    cast_specs = [_cast_specs(stack, layer, n_blocks - N_EXPERTS, lambda i, be, nu: i)
                  for stack, layer in casts]
    grid_spec = pltpu.PrefetchScalarGridSpec(
        num_scalar_prefetch=2,
        grid=(n_blocks,),
        in_specs=[used_rows,
                  pl.BlockSpec((None, D, D_EXPERT), expert_slab),
                  pl.BlockSpec((None, D, D_EXPERT), expert_slab),
                  pl.BlockSpec((None, D_EXPERT, D), expert_slab)] + [s[0] for s in cast_specs],
        out_specs=[rows] + [s[1] for s in cast_specs],
    )
    out = pl.pallas_call(
        _expert_kernel,
        grid_spec=grid_spec,
        out_shape=[jax.ShapeDtypeStruct((P, packed), jnp.uint32)] + [s[2] for s in cast_specs],
        compiler_params=_params(("arbitrary",), 56),
        name="experts",
    )(block_e, n_used, xs, w1, w3, w2, *[stack for stack, _ in casts])
    return out[0], out[1:]


def _combine_kernel(alpha, dest_ref, dest_next_ref, x_ref, gate_ref, g_ref, b_ref, y_hbm,
                    of_ref, ob_ref, buf_a, buf_b, sem):
    rows = COMBINE_ROWS
    half = rows // 2
    i = pl.program_id(0)

    def copy(table_ref, h, t, k, buf, s):
        return pltpu.make_async_copy(y_hbm.at[pl.ds(table_ref[0, k * rows + h * half + t], 1)],
                                     buf.at[k, pl.ds(t, 1)], sem.at[s])

    def gather(table_ref, h, buf, s):
        for t in range(half):
            for k in range(TOP_K):
                copy(table_ref, h, t, k, buf, s).start(priority=k % N_DMA_PRIORITIES)

    def wait(buf, s):
        for k in range(TOP_K):
            pltpu.make_async_copy(y_hbm.at[pl.ds(0, half)], buf.at[k], sem.at[s]).wait()

    def reduce(buf, h):
        r = slice(h * half, (h + 1) * half)
        lo0, hi0 = _unpack_halves(buf[0])
        lo1, hi1 = _unpack_halves(buf[1])
        g0 = gate_ref[r, 0:1]
        g1 = gate_ref[r, 1:2]
        f = jnp.concatenate([g0 * lo0 + g1 * lo1, g0 * hi0 + g1 * hi1], axis=1)
        y = _layer_norm(alpha * x_ref[r, :] + f, g_ref[...], b_ref[...])
        of_ref[r, :] = y
        ob_ref[r, :] = y.astype(BF16)

    @pl.when(i == 0)
    def _():
        def issue(t, carry):
            for k in range(TOP_K):
                copy(dest_ref, 0, t, k, buf_a, 0).start()
            return carry

        lax.fori_loop(0, half, issue, 0, unroll=8)

    wait(buf_a, 0)
    gather(dest_ref, 1, buf_b, 1)
    reduce(buf_a, 0)
    wait(buf_b, 1)
    gather(dest_next_ref, 0, buf_a, 0)
    reduce(buf_b, 1)

    @pl.when(i == pl.num_programs(0) - 1)
    def _():
        wait(buf_a, 0)


def _combine(alpha, dest_tiles, x, gate_t, ln_g, ln_b, yb):
    T, D = x.shape
    n_tiles = T // COMBINE_ROWS
    row = pl.BlockSpec((COMBINE_ROWS, D), lambda i: (i, 0))
    table = lambda index: pl.BlockSpec((None, 1, TOP_K * COMBINE_ROWS), index, memory_space=pltpu.SMEM)
    half_buf = pltpu.VMEM((TOP_K, COMBINE_ROWS // 2, yb.shape[1]), yb.dtype)
    return pl.pallas_call(
        functools.partial(_combine_kernel, alpha),
        grid=(n_tiles,),
        in_specs=[table(lambda i: (i, 0, 0)),
                  table(lambda i: (jnp.minimum(i + 1, n_tiles - 1), 0, 0)),
                  row, pl.BlockSpec((COMBINE_ROWS, TOP_K), lambda i: (i, 0)),
                  _const_spec((1, D)), _const_spec((1, D)),
                  pl.BlockSpec(memory_space=pl.ANY)],
        out_specs=[row, row],
        out_shape=[jax.ShapeDtypeStruct((T, D), F32), jax.ShapeDtypeStruct((T, D), BF16)],
        scratch_shapes=[half_buf, half_buf, pltpu.SemaphoreType.DMA((2,))],
        compiler_params=_params(("arbitrary",), 48),
        name="combine",
    )(dest_tiles, dest_tiles, x, gate_t, ln_g.reshape(1, D), ln_b.reshape(1, D), yb)


def _tile_pairs(dest, rows):
    T = dest.shape[1]
    return dest.reshape(TOP_K, T // rows, rows).transpose(1, 0, 2).reshape(T // rows, 1, TOP_K * rows)


def _moe(alpha, x, x_packed, router_w, router_b, w1, w3, w2, ln_g, ln_b, casts=()):
    T, D = x.shape
    n_blocks = (T * TOP_K) // EXPERT_ROWS + N_EXPERTS
    idx, gate, rank, cnt = _router(x, router_w, router_b)

    counts = cnt[:, 0]
    padded = (counts + EXPERT_ROWS - 1) // EXPERT_ROWS * EXPERT_ROWS
    pad_end = jnp.cumsum(padded)
    pad_start = pad_end - padded
    n_used = (pad_end[-1] // EXPERT_ROWS).astype(jnp.int32)
    block_start = jnp.minimum(jnp.arange(n_blocks, dtype=jnp.int32), n_used - 1) * EXPERT_ROWS
    block_e = jnp.minimum(jnp.sum(block_start[:, None] >= pad_end[None, :], axis=1),
                          N_EXPERTS - 1).astype(jnp.int32)
    expert_ids = jnp.arange(N_EXPERTS, dtype=jnp.int32)[:, None, None]
    dest = jnp.sum(jnp.where(idx[None] == expert_ids, pad_start[:, None, None], 0), axis=0) + rank

    n_used = n_used.reshape(1)
    xs = _dispatch(x_packed, _tile_pairs(dest, DISPATCH_ROWS), (pad_start + counts).astype(jnp.int32),
                   (padded - counts).astype(jnp.int32), n_used, n_blocks * EXPERT_ROWS)
    yb, casted = _experts(xs, block_e, n_used, w1, w3, w2, casts)
    xf, xb = _combine(alpha, _tile_pairs(dest, COMBINE_ROWS), x, gate.T, ln_g, ln_b, yb)
    return xf, xb, casted


def kernel(x, emb_ln_g, emb_ln_b, w_in, conv_w, conv_b, lru_w_a, lru_b_a, lru_w_x, lru_b_x, lru_lambda, w_o_rnn, sgu_ln_g, sgu_ln_b, sgu_w_s, sgu_b_s, w_o_sgu, w_out, ln1_g, ln1_b, router_w, router_b, expert_w1, expert_w3, expert_w2, ln2_g, ln2_b):
    batch, seq, D = x.shape
    depth = w_in.shape[0]
    alpha = float((2 * depth) ** 0.25)
    xf = x.reshape(batch * seq, D)
    xb = _emb_ln(xf, emb_ln_g, emb_ln_b)
    mixer_f32 = (w_in, w_o_rnn, w_o_sgu, w_out)
    w_in, w_o_rnn, w_o_sgu, w_out = (w[0:1].astype(BF16) for w in mixer_f32)
    rows_of = lambda w: w.reshape(depth, N_EXPERTS * w.shape[2], w.shape[3])
    w1_rows, w3_rows, w2_rows = rows_of(expert_w1), rows_of(expert_w3), rows_of(expert_w2)
    gelu_from = D_RNN
    sigmoid_from = 2 * D_RNN + 2 * D_SGU
    for l in range(depth):
        x_rnn = _proj(xb, w_in, 0, 0, gelu_from, lambda v: v, "proj_x")
        gelu_cols, w3 = _proj(xb, w_in, 0, gelu_from, sigmoid_from - gelu_from, _gelu, "proj_gelu",
                              cast=(w3_rows, l))
        sigmoid_cols, w1, a = _proj_lru(
            xb, w_in, 0, sigmoid_from, D_IN - sigmoid_from, _sigmoid, "proj_sigmoid_lru",
            (w1_rows, l), x_rnn, gelu_cols, seq, conv_w[l], conv_b[l], lru_w_a[l], lru_b_a[l],
            lru_w_x[l], lru_b_x[l], lru_lambda[l])
        x1, x1_packed, w2 = _mix_out(alpha, 0, a, gelu_cols, sigmoid_cols, xf, l == 0, emb_ln_g,
                                     emb_ln_b, sgu_ln_g[l], sgu_ln_b[l], sgu_w_s[l], sgu_b_s[l],
                                     w_o_rnn, w_o_sgu, w_out, ln1_g[l], ln1_b[l], cast=(w2_rows, l))
        next_mixer = [(w, l + 1) for w in mixer_f32] if l + 1 < depth else []
        xf, xb, casted = _moe(alpha, x1, x1_packed, router_w, router_b,
                              w1.reshape(expert_w1.shape[1:]), w3.reshape(expert_w3.shape[1:]),
                              w2.reshape(expert_w2.shape[1:]), ln2_g[l], ln2_b[l], casts=next_mixer)
        if casted:
            w_in, w_o_rnn, w_o_sgu, w_out = (w[None] for w in casted)
    return xf.reshape(batch, seq, D)
```

```python
import functools

import jax
import jax.numpy as jnp
import numpy as np
from jax import lax
from jax.experimental import pallas as pl
from jax.experimental.pallas import tpu as pltpu

F32 = jnp.float32
BF16 = jnp.bfloat16

D_MODEL = 2048
D_RNN = 1024
RNN_HEADS = 8
HEAD_DIM = D_RNN // RNN_HEADS
CONV_WIDTH = 4
LRU_C = 8.0
D_SGU = 1024
SGU_GROUPS = 8
SGU_GROUP_DIM = D_SGU // SGU_GROUPS
CHUNK = 128
N_EXPERTS = 16
N_GROUPS = 4
EXPERTS_PER_GROUP = N_EXPERTS // N_GROUPS
TOP_K = 2
D_EXPERT = 1408
D_IN = 2 * D_RNN + 2 * D_SGU + 2 * D_MODEL
LN_EPS = 1e-5
SQRT_2_OVER_PI = float(np.sqrt(2.0 / np.pi))

LN_ROWS = 512
PROJ_ROWS = 1024
PROJ_COLS = 1024
MIX_ROWS = 256
ROUTER_ROWS = 1024
DISPATCH_ROWS = 512
EXPERT_ROWS = 256
COMBINE_ROWS = 512
CONV_PAD = 8
N_DMA_PRIORITIES = 2
ROW_TILE = 8

MIB = 1024 * 1024


def _params(semantics, vmem_mib):
    return pltpu.CompilerParams(dimension_semantics=semantics,
                                vmem_limit_bytes=vmem_mib * MIB,
                                disable_bounds_checks=True)


def _const_spec(shape):
    zeros = (0,) * len(shape)
    return pl.BlockSpec(shape, lambda *_: zeros, pipeline_mode=pl.Buffered(1))


def _layer_spec(layer, shape):
    zeros = (0,) * len(shape)
    return pl.BlockSpec((None,) + tuple(shape), lambda *_: (layer,) + zeros,
                        pipeline_mode=pl.Buffered(1))


def _layer_norm(v, g, b):
    mu = jnp.mean(v, axis=-1, keepdims=True)
    d = v - mu
    var = jnp.mean(d * d, axis=-1, keepdims=True)
    return d * lax.rsqrt(var + LN_EPS) * g + b


def _gelu(x):
    inner = x * (SQRT_2_OVER_PI + (SQRT_2_OVER_PI * 0.044715) * (x * x))
    return x * (0.5 * jnp.tanh(inner) + 0.5)


def _pack_halves(y):
    n = y.shape[1] // 2
    bits = lambda v: lax.bitcast_convert_type(v.astype(BF16).astype(F32), jnp.uint32)
    return (bits(y[:, :n]) >> 16) | bits(y[:, n:])


def _unpack_halves(p):
    lo = lax.bitcast_convert_type(p << 16, F32)
    hi = lax.bitcast_convert_type(p & jnp.uint32(0xFFFF0000), F32)
    return lo, hi


def _sigmoid(x):
    return 0.5 * jnp.tanh(0.5 * x) + 0.5


def _emb_ln_kernel(x_ref, g_ref, b_ref, ob_ref):
    ob_ref[...] = _layer_norm(x_ref[...], g_ref[...], b_ref[...]).astype(BF16)


def _emb_ln(x, g, b):
    T, D = x.shape
    row = pl.BlockSpec((LN_ROWS, D), lambda i: (i, 0))
    return pl.pallas_call(
        _emb_ln_kernel,
        grid=(T // LN_ROWS,),
        in_specs=[row, _const_spec((1, D)), _const_spec((1, D))],
        out_specs=row,
        out_shape=jax.ShapeDtypeStruct((T, D), BF16),
        compiler_params=_params(("parallel",), 40),
        name="emb_ln",
    )(x, g.reshape(1, D), b.reshape(1, D))


def _cast_specs(stack, layer, n_steps, step_of):
    _, rows, cols = stack.shape
    slab = -(-rows // (n_steps * 16)) * 16
    last = -(-rows // slab) - 1
    src = pl.BlockSpec((None, slab, cols), lambda *g: (layer, jnp.minimum(step_of(*g), last), 0))
    dst = pl.BlockSpec((slab, cols), lambda *g: (jnp.minimum(step_of(*g), last), 0))
    return src, dst, jax.ShapeDtypeStruct((rows, cols), BF16)


def _proj_kernel(act, x_ref, w_ref, *rest):
    o_ref = rest[len(rest) // 2]
    acc = jnp.dot(x_ref[...], w_ref[...], preferred_element_type=F32)
    o_ref[...] = act(acc).astype(BF16)
    for src_ref, dst_ref in zip(rest[:len(rest) // 2], rest[len(rest) // 2 + 1:]):
        dst_ref[...] = src_ref[...].astype(BF16)


def _proj(xb, w_in, layer, col_from, n_cols, act, name, cast=None):
    T, D = xb.shape
    first = col_from // PROJ_COLS
    grid = (T // PROJ_ROWS, n_cols // PROJ_COLS)
    in_specs = [pl.BlockSpec((PROJ_ROWS, D), lambda i, j: (i, 0)),
                pl.BlockSpec((None, D, PROJ_COLS), lambda i, j: (layer, 0, first + j))]
    out_specs = [pl.BlockSpec((PROJ_ROWS, PROJ_COLS), lambda i, j: (i, j))]
    out_shape = [jax.ShapeDtypeStruct((T, n_cols), BF16)]
    operands = [xb, w_in]
    if cast is not None:
        src, dst, shape = _cast_specs(*cast, grid[0] * grid[1], lambda i, j: i * grid[1] + j)
        in_specs.append(src)
        out_specs.append(dst)
        out_shape.append(shape)
        operands.append(cast[0])
    out = pl.pallas_call(
        functools.partial(_proj_kernel, act),
        grid=grid,
        in_specs=in_specs,
        out_specs=out_specs,
        out_shape=out_shape,
        compiler_params=_params(("parallel", "arbitrary"), 48),
        name=name,
    )(*operands)
    return out[0] if cast is None else out


def _lru_gates(first, xr_ref, cw_ref, cb_ref, wa_ref, ba_ref, wx_ref, bx_ref, lam_ref,
               xpad_ref, hc_ref, a_ref, u_ref):
    rows = xr_ref.shape[0]

    @pl.when(first)
    def _():
        xpad_ref[0:CONV_PAD, :] = jnp.zeros((CONV_PAD, D_RNN), F32)
        hc_ref[...] = jnp.zeros_like(hc_ref)

    x = xr_ref[...].astype(F32)
    xpad_ref[CONV_PAD:CONV_PAD + rows, :] = x
    xc = cb_ref[...] + cw_ref[CONV_WIDTH - 1:CONV_WIDTH, :] * x
    for k in range(CONV_WIDTH - 1):
        shift = CONV_WIDTH - 1 - k
        xc = xc + cw_ref[k:k + 1, :] * xpad_ref[CONV_PAD - shift:CONV_PAD - shift + rows, :]
    xpad_ref[0:CONV_PAD, :] = x[rows - CONV_PAD:rows, :]

    lam = lam_ref[...]
    sp = jnp.maximum(-lam, 0.0) + jnp.log1p(jnp.exp(-jnp.abs(lam)))
    for h in range(RNN_HEADS):
        cols = slice(h * HEAD_DIM, (h + 1) * HEAD_DIM)
        xh = xc[:, cols]
        xhb = xh.astype(BF16)
        r = _sigmoid(jnp.dot(xhb, wa_ref[h], preferred_element_type=F32) + ba_ref[:, cols])
        gi = _sigmoid(jnp.dot(xhb, wx_ref[h], preferred_element_type=F32) + bx_ref[:, cols])
        log_a = (-LRU_C) * r * sp[:, cols]
        a_ref[:, cols] = jnp.exp(log_a)
        th = jnp.tanh(log_a)
        u_ref[:, cols] = jnp.sqrt(-2.0 * th / (1.0 - th)) * (gi * xh)


def _lru_scan(gate_ref, o_ref, hc_ref, a_ref, u_ref, h_ref):
    def step(t, h):
        h = a_ref[pl.ds(t, 1), :] * h + u_ref[pl.ds(t, 1), :]
        h_ref[pl.ds(t, 1), :] = h
        return h

    hc_ref[0:1, :] = lax.fori_loop(0, a_ref.shape[0], step, hc_ref[0:1, :], unroll=8)
    o_ref[...] = (h_ref[...] * gate_ref[...].astype(F32)).astype(BF16)


def _proj_lru_kernel(act, tiles_per_seq, x_ref, w_ref, cast_src_ref, xr_ref, gate_ref, cw_ref, cb_ref,
                     wa_ref, ba_ref, wx_ref, bx_ref, lam_ref, o_ref, cast_dst_ref, a_out_ref,
                     xpad_ref, hc_ref, a_ref, u_ref, h_ref):
    step = pl.program_id(0) * pl.num_programs(1) + pl.program_id(1)
    _lru_gates(step % tiles_per_seq == 0, xr_ref, cw_ref, cb_ref, wa_ref, ba_ref, wx_ref, bx_ref,
               lam_ref, xpad_ref, hc_ref, a_ref, u_ref)
    acc = jnp.dot(x_ref[...], w_ref[...], preferred_element_type=F32)
    o_ref[...] = act(acc).astype(BF16)
    cast_dst_ref[...] = cast_src_ref[...].astype(BF16)
    _lru_scan(gate_ref, a_out_ref, hc_ref, a_ref, u_ref, h_ref)


def _proj_lru(xb, w_in, layer, col_from, n_cols, act, name, cast, x_rnn, gelu_cols, seq,
              conv_w, conv_b, w_a, b_a, w_x, b_x, lam):
    T, D = xb.shape
    first = col_from // PROJ_COLS
    grid = (T // PROJ_ROWS, n_cols // PROJ_COLS)
    n_steps = grid[0] * grid[1]
    step_of = lambda i, j: i * grid[1] + j
    lru_rows = T // n_steps
    src, dst, cast_shape = _cast_specs(*cast, n_steps, step_of)
    tile = pl.BlockSpec((lru_rows, D_RNN), lambda i, j: (step_of(i, j), 0))
    vec = _const_spec((1, D_RNN))
    gate_w = _const_spec((RNN_HEADS, HEAD_DIM, HEAD_DIM))
    tile_f32 = pltpu.VMEM((lru_rows, D_RNN), F32)
    return pl.pallas_call(
        functools.partial(_proj_lru_kernel, act, seq // lru_rows),
        grid=grid,
        in_specs=[pl.BlockSpec((PROJ_ROWS, D), lambda i, j: (i, 0)),
                  pl.BlockSpec((None, D, PROJ_COLS), lambda i, j: (layer, 0, first + j)),
                  src, tile, tile, _const_spec((CONV_WIDTH, D_RNN)), vec, gate_w, vec, gate_w, vec, vec],
        out_specs=[pl.BlockSpec((PROJ_ROWS, PROJ_COLS), lambda i, j: (i, j)), dst, tile],
        out_shape=[jax.ShapeDtypeStruct((T, n_cols), BF16), cast_shape,
                   jax.ShapeDtypeStruct((T, D_RNN), BF16)],
        scratch_shapes=[pltpu.VMEM((CONV_PAD + lru_rows, D_RNN), F32), pltpu.VMEM((8, D_RNN), F32),
                        tile_f32, tile_f32, tile_f32],
        compiler_params=_params(("arbitrary", "arbitrary"), 52),
        name=name,
    )(xb, w_in, cast[0], x_rnn, gelu_cols, conv_w, conv_b.reshape(1, D_RNN), w_a.astype(BF16),
      b_a.reshape(1, D_RNN), w_x.astype(BF16), b_x.reshape(1, D_RNN), lam.reshape(1, D_RNN))


def _spatial_gating(u_ref, v_ref, g_ref, b_ref, ws_ref, bs_ref, o_ref):
    rows = u_ref.shape[0]
    v = _layer_norm(v_ref[...].astype(F32), g_ref[...], b_ref[...]).astype(BF16)
    t_out = lax.broadcasted_iota(jnp.int32, (CHUNK, CHUNK), 0)
    t_in = lax.broadcasted_iota(jnp.int32, (CHUNK, CHUNK), 1)
    causal = t_in <= t_out
    for g in range(SGU_GROUPS):
        cols = slice(g * SGU_GROUP_DIM, (g + 1) * SGU_GROUP_DIM)
        ws = jnp.where(causal, ws_ref[g], 0.0).astype(BF16)
        bias = bs_ref[:, g:g + 1]
        for c in range(rows // CHUNK):
            rws = slice(c * CHUNK, (c + 1) * CHUNK)
            mixed = jnp.dot(ws, v[rws, cols], preferred_element_type=F32) + bias
            o_ref[rws, cols] = (u_ref[rws, cols].astype(F32) * mixed).astype(BF16)


def _mix_out_kernel(alpha, x_is_raw, a_ref, u_ref, v_ref, sga_ref, sgb_ref, x_ref, emb_g_ref,
                    emb_b_ref, sgu_g_ref, sgu_b_ref, ws_ref, bs_ref, woa_ref, wob_ref, wout_ref,
                    g_ref, beta_ref, cast_src_ref, o_ref, packed_ref, cast_dst_ref, b_ref):
    _spatial_gating(u_ref, v_ref, sgu_g_ref, sgu_b_ref, ws_ref, bs_ref, b_ref)
    ya = jnp.dot(a_ref[...], woa_ref[...], preferred_element_type=F32)
    yb = jnp.dot(b_ref[...], wob_ref[...], preferred_element_type=F32)
    merged = sga_ref[...].astype(F32) * ya + sgb_ref[...].astype(F32) * yb
    m = jnp.dot(merged.astype(BF16), wout_ref[...], preferred_element_type=F32)
    x = x_ref[...]
    if x_is_raw:
        x = _layer_norm(x, emb_g_ref[...], emb_b_ref[...])
    y = _layer_norm(alpha * x + m, g_ref[...], beta_ref[...])
    o_ref[...] = y
    packed_ref[...] = _pack_halves(y)
    cast_dst_ref[...] = cast_src_ref[...].astype(BF16)


def _mix_out(alpha, layer, a, gelu_cols, sigmoid_cols, x, x_is_raw, emb_ln_g, emb_ln_b,
             sgu_ln_g, sgu_ln_b, w_s, b_s, w_o_rnn, w_o_sgu, w_out, ln_g, ln_b, cast):
    T, D = x.shape
    n_steps = T // MIX_ROWS
    row = lambda w, c: pl.BlockSpec((MIX_ROWS, w), lambda i: (i, c))
    cast_src, cast_dst, cast_shape = _cast_specs(*cast, n_steps, lambda i: i)
    return pl.pallas_call(
        functools.partial(_mix_out_kernel, alpha, x_is_raw),
        grid=(n_steps,),
        in_specs=[row(D_RNN, 0), row(D_SGU, 1), row(D_SGU, 2), row(D, 0), row(D, 1), row(D, 0),
                  _const_spec((1, D)), _const_spec((1, D)),
                  _const_spec((1, D_SGU)), _const_spec((1, D_SGU)),
                  _const_spec((SGU_GROUPS, CHUNK, CHUNK)), _const_spec((CHUNK, SGU_GROUPS)),
                  _layer_spec(layer, (D_RNN, D)), _layer_spec(layer, (D_SGU, D)),
                  _layer_spec(layer, (D, D)), _const_spec((1, D)), _const_spec((1, D)), cast_src],
        out_specs=[row(D, 0), row(D // 2, 0), cast_dst],
        out_shape=[jax.ShapeDtypeStruct((T, D), F32), jax.ShapeDtypeStruct((T, D // 2), jnp.uint32),
                   cast_shape],
        scratch_shapes=[pltpu.VMEM((MIX_ROWS, D_SGU), BF16)],
        compiler_params=_params(("parallel",), 56),
        name="mix_out",
    )(a, gelu_cols, gelu_cols, sigmoid_cols, sigmoid_cols, x,
      emb_ln_g.reshape(1, D), emb_ln_b.reshape(1, D),
      sgu_ln_g.reshape(1, D_SGU), sgu_ln_b.reshape(1, D_SGU), w_s, b_s.T,
      w_o_rnn, w_o_sgu, w_out, ln_g.reshape(1, D), ln_b.reshape(1, D), cast[0])


def _first_max(v, row, n):
    m = jnp.max(v, axis=0, keepdims=True)
    idx = jnp.min(jnp.where(v == m, row, float(n)), axis=0, keepdims=True)
    return m, idx


def _router_kernel(x_ref, wt_ref, b_ref, idx_ref, gate_ref, rank_ref, cnt_ref, run_ref):
    rows = x_ref.shape[0]

    @pl.when(pl.program_id(0) == 0)
    def _():
        run_ref[...] = jnp.zeros_like(run_ref)

    def nt_dot(w, x):
        return lax.dot_general(w, x, (((1,), (1,)), ((), ())), preferred_element_type=F32)

    x = x_ref[...]
    xh = x.astype(BF16)
    xl = (x - xh.astype(F32)).astype(BF16)
    w = wt_ref[...]
    wh = w.astype(BF16)
    wl = (w - wh.astype(F32)).astype(BF16)
    logits = nt_dot(wh, xh) + (nt_dot(wh, xl) + nt_dot(wl, xh))
    e = jnp.exp(logits - jnp.max(logits, axis=0, keepdims=True))
    scores = e / jnp.sum(e, axis=0, keepdims=True)
    sel = scores + b_ref[...]

    row = lax.broadcasted_iota(jnp.int32, (N_EXPERTS, rows), 0).astype(F32)
    grow = lax.broadcasted_iota(jnp.int32, (EXPERTS_PER_GROUP, rows), 0).astype(F32)
    neg_inf = float("-inf")
    best_score = None
    best_group = None
    for g in range(N_GROUPS):
        v = sel[g * EXPERTS_PER_GROUP:(g + 1) * EXPERTS_PER_GROUP, :]
        m1, i1 = _first_max(v, grow, EXPERTS_PER_GROUP)
        m2 = jnp.max(jnp.where(grow == i1, neg_inf, v), axis=0, keepdims=True)
        s = m1 + m2
        if g == 0:
            best_score, best_group = s, jnp.zeros_like(s)
        else:
            better = s > best_score
            best_group = jnp.where(better, float(g), best_group)
            best_score = jnp.where(better, s, best_score)

    lo = best_group * float(EXPERTS_PER_GROUP)
    in_group = jnp.logical_and(row >= lo, row < lo + float(EXPERTS_PER_GROUP))
    masked = jnp.where(in_group, sel, neg_inf)
    _, i1 = _first_max(masked, row, N_EXPERTS)
    pick1 = row == i1
    _, i2 = _first_max(jnp.where(pick1, neg_inf, masked), row, N_EXPERTS)
    pick2 = row == i2
    s1 = jnp.sum(jnp.where(pick1, scores, 0.0), axis=0, keepdims=True)
    s2 = jnp.sum(jnp.where(pick2, scores, 0.0), axis=0, keepdims=True)
    den = s1 + s2

    onehot = jnp.where(jnp.logical_or(pick1, pick2), 1.0, 0.0)
    before = (lax.broadcasted_iota(jnp.int32, (rows, rows), 0)
              < lax.broadcasted_iota(jnp.int32, (rows, rows), 1))
    prefix = jnp.dot(onehot.astype(BF16), jnp.where(before, 1.0, 0.0).astype(BF16),
                     preferred_element_type=F32)
    pos = prefix + run_ref[:, 0:1]
    r1 = jnp.sum(jnp.where(pick1, pos, 0.0), axis=0, keepdims=True)
    r2 = jnp.sum(jnp.where(pick2, pos, 0.0), axis=0, keepdims=True)
    run_ref[...] = run_ref[...] + jnp.sum(onehot, axis=1, keepdims=True)

    idx_ref[0:1, :] = i1.astype(jnp.int32)
    idx_ref[1:2, :] = i2.astype(jnp.int32)
    gate_ref[0:1, :] = s1 / den
    gate_ref[1:2, :] = s2 / den
    rank_ref[0:1, :] = r1.astype(jnp.int32)
    rank_ref[1:2, :] = r2.astype(jnp.int32)
    cnt_ref[...] = run_ref[...].astype(jnp.int32)


def _router(x, router_w, router_b):
    T, D = x.shape
    pair = pl.BlockSpec((TOP_K, ROUTER_ROWS), lambda i: (0, i))
    return pl.pallas_call(
        _router_kernel,
        grid=(T // ROUTER_ROWS,),
        in_specs=[pl.BlockSpec((ROUTER_ROWS, D), lambda i: (i, 0)),
                  _const_spec((N_EXPERTS, D)), _const_spec((N_EXPERTS, 1))],
        out_specs=[pair, pair, pair, pl.BlockSpec((N_EXPERTS, 128), lambda i: (0, 0))],
        out_shape=[jax.ShapeDtypeStruct((TOP_K, T), jnp.int32),
                   jax.ShapeDtypeStruct((TOP_K, T), F32),
                   jax.ShapeDtypeStruct((TOP_K, T), jnp.int32),
                   jax.ShapeDtypeStruct((N_EXPERTS, 128), jnp.int32)],
        scratch_shapes=[pltpu.VMEM((N_EXPERTS, 128), F32)],
        compiler_params=_params(("arbitrary",), 40),
        name="router",
    )(x, router_w.T, router_b.reshape(N_EXPERTS, 1))


def _dispatch_kernel(pad_from_ref, pad_n_ref, n_used_ref, dest_ref, x_ref, xs_hbm, zero_ref, sem, pad_sem):
    i = pl.program_id(0)
    rows = DISPATCH_ROWS
    n_blocks = xs_hbm.shape[0] // EXPERT_ROWS
    zero_rows = zero_ref.shape[0]

    def tail_copies(j):
        block = n_used_ref[0] + j
        off = pl.multiple_of(block * EXPERT_ROWS, EXPERT_ROWS)
        return block < n_blocks, [
            pltpu.make_async_copy(zero_ref, xs_hbm.at[pl.ds(off + part * zero_rows, zero_rows)], pad_sem)
            for part in range(EXPERT_ROWS // zero_rows)]

    def pad_copy(e, bit):
        n = pad_n_ref[e]
        if bit < ROW_TILE:
            off = pad_from_ref[e] + bit - 1
            return pltpu.make_async_copy(zero_ref.at[pl.ds(0, 1)], xs_hbm.at[pl.ds(off, 1)], pad_sem)
        done = (n & (ROW_TILE - 1)) + (n & ~(2 * bit - 1))
        off = pl.multiple_of(pad_from_ref[e] + done, ROW_TILE)
        return pltpu.make_async_copy(zero_ref.at[pl.ds(0, bit)], xs_hbm.at[pl.ds(off, bit)], pad_sem)

    def pad_needed(e, bit):
        n = pad_n_ref[e]
        if bit < ROW_TILE:
            return bit <= (n & (ROW_TILE - 1))
        return (n & bit) != 0

    bits = list(range(1, ROW_TILE)) + [
        1 << k for k in range(ROW_TILE.bit_length() - 1, EXPERT_ROWS.bit_length() - 1)]

    @pl.when(i == 0)
    def _():
        zero_ref[...] = jnp.zeros_like(zero_ref)
        for e in range(N_EXPERTS):
            for bit in bits:
                @pl.when(pad_needed(e, bit))
                def _():
                    pad_copy(e, bit).start()
        for j in range(N_EXPERTS):
            needed, copies = tail_copies(j)

            @pl.when(needed)
            def _():
                for c in copies:
                    c.start()

    for t in range(rows):
        src = x_ref.at[pl.ds(t, 1)]
        for k in range(TOP_K):
            pltpu.make_async_copy(src, xs_hbm.at[pl.ds(dest_ref[0, k * rows + t], 1)],
                                  sem).start(priority=k % N_DMA_PRIORITIES)
    for _ in range(TOP_K):
        pltpu.make_async_copy(x_ref, xs_hbm.at[pl.ds(0, rows)], sem).wait()

    @pl.when(i == 0)
    def _():
        for e in range(N_EXPERTS):
            for bit in bits:
                @pl.when(pad_needed(e, bit))
                def _():
                    pad_copy(e, bit).wait()
        for j in range(N_EXPERTS):
            needed, copies = tail_copies(j)

            @pl.when(needed)
            def _():
                for c in copies:
                    c.wait()


def _dispatch(x, dest_tiles, pad_from, pad_n, n_used, n_slots):
    T, D = x.shape
    grid_spec = pltpu.PrefetchScalarGridSpec(
        num_scalar_prefetch=3,
        grid=(T // DISPATCH_ROWS,),
        in_specs=[pl.BlockSpec((None, 1, TOP_K * DISPATCH_ROWS), lambda i, *_: (i, 0, 0),
                               memory_space=pltpu.SMEM),
                  pl.BlockSpec((DISPATCH_ROWS, D), lambda i, *_: (i, 0))],
        out_specs=pl.BlockSpec(memory_space=pl.ANY),
        scratch_shapes=[pltpu.VMEM((EXPERT_ROWS // 2, D), x.dtype),
                        pltpu.SemaphoreType.DMA(()), pltpu.SemaphoreType.DMA(())],
    )
    return pl.pallas_call(
        _dispatch_kernel,
        grid_spec=grid_spec,
        out_shape=jax.ShapeDtypeStruct((n_slots, D), x.dtype),
        compiler_params=_params(("arbitrary",), 16),
        name="dispatch",
    )(pad_from, pad_n, n_used, dest_tiles, x)


def _expert_kernel(block_e_ref, n_used_ref, xs_ref, w1_ref, w3_ref, w2_ref, *rest):
    n_casts = len(rest) // 2
    o_ref = rest[n_casts]
    used = pl.program_id(0) < n_used_ref[0]

    @pl.when(jnp.logical_not(used))
    def _():
        o_ref[...] = jnp.zeros_like(o_ref)

    @pl.when(used)
    def _():
        lo, hi = _unpack_halves(xs_ref[...])
        x = jnp.concatenate([lo.astype(BF16), hi.astype(BF16)], axis=1)
        h1 = jnp.dot(x, w1_ref[...], preferred_element_type=F32)
        h3 = jnp.dot(x, w3_ref[...], preferred_element_type=F32)
        hidden = (h1 * _sigmoid(h1)) * h3
        out = jnp.dot(hidden.astype(BF16), w2_ref[...], preferred_element_type=F32)
        o_ref[...] = _pack_halves(out)
        for src_ref, dst_ref in zip(rest[:n_casts], rest[n_casts + 1:]):
            dst_ref[...] = src_ref[...].astype(BF16)


def _experts(xs, block_e, n_used, w1, w3, w2, casts=()):
    P, packed = xs.shape
    D = 2 * packed
    n_blocks = P // EXPERT_ROWS
    used_rows = pl.BlockSpec((EXPERT_ROWS, packed),
                             lambda i, be, nu: (jnp.maximum(jnp.minimum(i, nu[0] - 1), 0), 0))
    rows = pl.BlockSpec((EXPERT_ROWS, packed), lambda i, be, nu: (i, 0))
    expert_slab = lambda i, be, nu: (be[i], 0, 0)
    cast_specs = [_cast_specs(stack, layer, n_blocks - N_EXPERTS, lambda i, be, nu: i)
                  for stack, layer in casts]
    grid_spec = pltpu.PrefetchScalarGridSpec(
        num_scalar_prefetch=2,
        grid=(n_blocks,),
        in_specs=[used_rows,
                  pl.BlockSpec((None, D, D_EXPERT), expert_slab),
                  pl.BlockSpec((None, D, D_EXPERT), expert_slab),
                  pl.BlockSpec((None, D_EXPERT, D), expert_slab)] + [s[0] for s in cast_specs],
        out_specs=[rows] + [s[1] for s in cast_specs],
    )
    out = pl.pallas_call(
        _expert_kernel,
        grid_spec=grid_spec,
        out_shape=[jax.ShapeDtypeStruct((P, packed), jnp.uint32)] + [s[2] for s in cast_specs],
        compiler_params=_params(("arbitrary",), 56),
        name="experts",
    )(block_e, n_used, xs, w1, w3, w2, *[stack for stack, _ in casts])
    return out[0], out[1:]


def _combine_kernel(alpha, project, dest_ref, dest_next_ref, x_ref, gate_ref, g_ref, b_ref, y_hbm,
                    *rest):
    if project:
        w_ref, of_ref, ob_ref, oxr_ref, buf_a, buf_b, sem = rest
    else:
        of_ref, ob_ref, buf_a, buf_b, sem = rest
    rows = COMBINE_ROWS
    half = rows // 2
    i = pl.program_id(0)

    def copy(table_ref, h, t, k, buf, s):
        return pltpu.make_async_copy(y_hbm.at[pl.ds(table_ref[0, k * rows + h * half + t], 1)],
                                     buf.at[k, pl.ds(t, 1)], sem.at[s])

    def gather(table_ref, h, buf, s):
        for t in range(half):
            for k in range(TOP_K):
                copy(table_ref, h, t, k, buf, s).start(priority=k % N_DMA_PRIORITIES)

    def wait(buf, s):
        for k in range(TOP_K):
            pltpu.make_async_copy(y_hbm.at[pl.ds(0, half)], buf.at[k], sem.at[s]).wait()

    def reduce(buf, h):
        r = slice(h * half, (h + 1) * half)
        lo0, hi0 = _unpack_halves(buf[0])
        lo1, hi1 = _unpack_halves(buf[1])
        g0 = gate_ref[r, 0:1]
        g1 = gate_ref[r, 1:2]
        f = jnp.concatenate([g0 * lo0 + g1 * lo1, g0 * hi0 + g1 * hi1], axis=1)
        y = _layer_norm(alpha * x_ref[r, :] + f, g_ref[...], b_ref[...])
        of_ref[r, :] = y
        yb = y.astype(BF16)
        ob_ref[r, :] = yb
        if project:
            oxr_ref[r, :] = jnp.dot(yb, w_ref[...], preferred_element_type=F32).astype(BF16)

    @pl.when(i == 0)
    def _():
        def issue(t, carry):
            for k in range(TOP_K):
                copy(dest_ref, 0, t, k, buf_a, 0).start()
            return carry

        lax.fori_loop(0, half, issue, 0, unroll=8)

    wait(buf_a, 0)
    gather(dest_ref, 1, buf_b, 1)
    reduce(buf_a, 0)
    wait(buf_b, 1)
    gather(dest_next_ref, 0, buf_a, 0)
    reduce(buf_b, 1)

    @pl.when(i == pl.num_programs(0) - 1)
    def _():
        wait(buf_a, 0)


def _combine(alpha, dest_tiles, x, gate_t, ln_g, ln_b, yb, next_w_in=None):
    T, D = x.shape
    n_tiles = T // COMBINE_ROWS
    row = lambda w: pl.BlockSpec((COMBINE_ROWS, w), lambda i: (i, 0))
    table = lambda index: pl.BlockSpec((None, 1, TOP_K * COMBINE_ROWS), index, memory_space=pltpu.SMEM)
    half_buf = pltpu.VMEM((TOP_K, COMBINE_ROWS // 2, yb.shape[1]), yb.dtype)
    project = next_w_in is not None
    in_specs = [table(lambda i: (i, 0, 0)),
                table(lambda i: (jnp.minimum(i + 1, n_tiles - 1), 0, 0)),
                row(D), pl.BlockSpec((COMBINE_ROWS, TOP_K), lambda i: (i, 0)),
                _const_spec((1, D)), _const_spec((1, D)),
                pl.BlockSpec(memory_space=pl.ANY)]
    out_specs = [row(D), row(D)]
    out_shape = [jax.ShapeDtypeStruct((T, D), F32), jax.ShapeDtypeStruct((T, D), BF16)]
    operands = [dest_tiles, dest_tiles, x, gate_t, ln_g.reshape(1, D), ln_b.reshape(1, D), yb]
    if project:
        in_specs.append(_const_spec((D, D_RNN)))
        out_specs.append(row(D_RNN))
        out_shape.append(jax.ShapeDtypeStruct((T, D_RNN), BF16))
        operands.append(next_w_in)
    return pl.pallas_call(
        functools.partial(_combine_kernel, alpha, project),
        grid=(n_tiles,),
        in_specs=in_specs,
        out_specs=out_specs,
        out_shape=out_shape,
        scratch_shapes=[half_buf, half_buf, pltpu.SemaphoreType.DMA((2,))],
        compiler_params=_params(("arbitrary",), 52),
        name="combine",
    )(*operands)


def _tile_pairs(dest, rows):
    T = dest.shape[1]
    return dest.reshape(TOP_K, T // rows, rows).transpose(1, 0, 2).reshape(T // rows, 1, TOP_K * rows)


def _moe(alpha, x, x_packed, router_w, router_b, w1, w3, w2, ln_g, ln_b, casts=()):
    T, D = x.shape
    n_blocks = (T * TOP_K) // EXPERT_ROWS + N_EXPERTS
    idx, gate, rank, cnt = _router(x, router_w, router_b)

    counts = cnt[:, 0]
    padded = (counts + EXPERT_ROWS - 1) // EXPERT_ROWS * EXPERT_ROWS
    pad_end = jnp.cumsum(padded)
    pad_start = pad_end - padded
    n_used = (pad_end[-1] // EXPERT_ROWS).astype(jnp.int32)
    block_start = jnp.minimum(jnp.arange(n_blocks, dtype=jnp.int32), n_used - 1) * EXPERT_ROWS
    block_e = jnp.minimum(jnp.sum(block_start[:, None] >= pad_end[None, :], axis=1),
                          N_EXPERTS - 1).astype(jnp.int32)
    expert_ids = jnp.arange(N_EXPERTS, dtype=jnp.int32)[:, None, None]
    dest = jnp.sum(jnp.where(idx[None] == expert_ids, pad_start[:, None, None], 0), axis=0) + rank

    n_used = n_used.reshape(1)
    xs = _dispatch(x_packed, _tile_pairs(dest, DISPATCH_ROWS), (pad_start + counts).astype(jnp.int32),
                   (padded - counts).astype(jnp.int32), n_used, n_blocks * EXPERT_ROWS)
    yb, casted = _experts(xs, block_e, n_used, w1, w3, w2, casts)
    new = _combine(alpha, _tile_pairs(dest, COMBINE_ROWS), x, gate.T, ln_g, ln_b, yb,
                   next_w_in=casted[0] if casted else None)
    return new, casted


def kernel(x, emb_ln_g, emb_ln_b, w_in, conv_w, conv_b, lru_w_a, lru_b_a, lru_w_x, lru_b_x, lru_lambda, w_o_rnn, sgu_ln_g, sgu_ln_b, sgu_w_s, sgu_b_s, w_o_sgu, w_out, ln1_g, ln1_b, router_w, router_b, expert_w1, expert_w3, expert_w2, ln2_g, ln2_b):
    batch, seq, D = x.shape
    depth = w_in.shape[0]
    alpha = float((2 * depth) ** 0.25)
    xf = x.reshape(batch * seq, D)
    xb = _emb_ln(xf, emb_ln_g, emb_ln_b)
    mixer_f32 = (w_in, w_o_rnn, w_o_sgu, w_out)
    w_in, w_o_rnn, w_o_sgu, w_out = (w[0:1].astype(BF16) for w in mixer_f32)
    rows_of = lambda w: w.reshape(depth, N_EXPERTS * w.shape[2], w.shape[3])
    w1_rows, w3_rows, w2_rows = rows_of(expert_w1), rows_of(expert_w3), rows_of(expert_w2)
    gelu_from = D_RNN
    sigmoid_from = 2 * D_RNN + 2 * D_SGU
    x_rnn = None
    for l in range(depth):
        if x_rnn is None:
            x_rnn = _proj(xb, w_in, 0, 0, gelu_from, lambda v: v, "proj_x")
        gelu_cols, w3 = _proj(xb, w_in, 0, gelu_from, sigmoid_from - gelu_from, _gelu, "proj_gelu",
                              cast=(w3_rows, l))
        sigmoid_cols, w1, a = _proj_lru(
            xb, w_in, 0, sigmoid_from, D_IN - sigmoid_from, _sigmoid, "proj_sigmoid_lru",
            (w1_rows, l), x_rnn, gelu_cols, seq, conv_w[l], conv_b[l], lru_w_a[l], lru_b_a[l],
            lru_w_x[l], lru_b_x[l], lru_lambda[l])
        x1, x1_packed, w2 = _mix_out(alpha, 0, a, gelu_cols, sigmoid_cols, xf, l == 0, emb_ln_g,
                                     emb_ln_b, sgu_ln_g[l], sgu_ln_b[l], sgu_w_s[l], sgu_b_s[l],
                                     w_o_rnn, w_o_sgu, w_out, ln1_g[l], ln1_b[l], cast=(w2_rows, l))
        next_mixer = [(w, l + 1) for w in mixer_f32] if l + 1 < depth else []
        (xf, xb, *x_rnn), casted = _moe(
            alpha, x1, x1_packed, router_w, router_b, w1.reshape(expert_w1.shape[1:]),
            w3.reshape(expert_w3.shape[1:]), w2.reshape(expert_w2.shape[1:]), ln2_g[l], ln2_b[l],
            casts=next_mixer)
        x_rnn = x_rnn[0] if x_rnn else None
        if casted:
            w_in, w_o_rnn, w_o_sgu, w_out = (w[None] for w in casted)
    return xf.reshape(batch, seq, D)
```

```python
import functools

import jax
import jax.numpy as jnp
import numpy as np
from jax import lax
from jax.experimental import pallas as pl
from jax.experimental.pallas import tpu as pltpu

F32 = jnp.float32
BF16 = jnp.bfloat16

D_MODEL = 2048
D_RNN = 1024
RNN_HEADS = 8
HEAD_DIM = D_RNN // RNN_HEADS
CONV_WIDTH = 4
LRU_C = 8.0
D_SGU = 1024
SGU_GROUPS = 8
SGU_GROUP_DIM = D_SGU // SGU_GROUPS
CHUNK = 128
N_EXPERTS = 16
N_GROUPS = 4
EXPERTS_PER_GROUP = N_EXPERTS // N_GROUPS
TOP_K = 2
D_EXPERT = 1408
D_IN = 2 * D_RNN + 2 * D_SGU + 2 * D_MODEL
LN_EPS = 1e-5
SQRT_2_OVER_PI = float(np.sqrt(2.0 / np.pi))

LN_ROWS = 512
PROJ_ROWS = 1024
PROJ_COLS = 1024
MIX_ROWS = 256
ROUTER_ROWS = 1024
DISPATCH_ROWS = 512
EXPERT_ROWS = 256
COMBINE_ROWS = 512
CONV_PAD = 8
N_DMA_PRIORITIES = 2
ROW_TILE = 8

MIB = 1024 * 1024


def _params(semantics, vmem_mib):
    return pltpu.CompilerParams(dimension_semantics=semantics,
                                vmem_limit_bytes=vmem_mib * MIB,
                                disable_bounds_checks=True)


def _const_spec(shape):
    zeros = (0,) * len(shape)
    return pl.BlockSpec(shape, lambda *_: zeros, pipeline_mode=pl.Buffered(1))


def _layer_spec(layer, shape):
    zeros = (0,) * len(shape)
    return pl.BlockSpec((None,) + tuple(shape), lambda *_: (layer,) + zeros,
                        pipeline_mode=pl.Buffered(1))


def _layer_norm(v, g, b):
    mu = jnp.mean(v, axis=-1, keepdims=True)
    d = v - mu
    var = jnp.mean(d * d, axis=-1, keepdims=True)
    return d * lax.rsqrt(var + LN_EPS) * g + b


def _gelu(x):
    inner = x * (SQRT_2_OVER_PI + (SQRT_2_OVER_PI * 0.044715) * (x * x))
    return x * (0.5 * jnp.tanh(inner) + 0.5)


def _pack_halves(y):
    n = y.shape[1] // 2
    bits = lambda v: lax.bitcast_convert_type(v.astype(BF16).astype(F32), jnp.uint32)
    return (bits(y[:, :n]) >> 16) | bits(y[:, n:])


def _unpack_halves(p):
    lo = lax.bitcast_convert_type(p << 16, F32)
    hi = lax.bitcast_convert_type(p & jnp.uint32(0xFFFF0000), F32)
    return lo, hi


def _sigmoid(x):
    return 0.5 * jnp.tanh(0.5 * x) + 0.5


def _emb_ln_kernel(x_ref, g_ref, b_ref, w_ref, ob_ref, oxr_ref):
    xb = _layer_norm(x_ref[...], g_ref[...], b_ref[...]).astype(BF16)
    ob_ref[...] = xb
    oxr_ref[...] = jnp.dot(xb, w_ref[...], preferred_element_type=F32).astype(BF16)


def _emb_ln(x, g, b, w_in):
    T, D = x.shape
    row = lambda w: pl.BlockSpec((LN_ROWS, w), lambda i: (i, 0))
    return pl.pallas_call(
        _emb_ln_kernel,
        grid=(T // LN_ROWS,),
        in_specs=[row(D), _const_spec((1, D)), _const_spec((1, D)), _layer_spec(0, (D, D_RNN))],
        out_specs=[row(D), row(D_RNN)],
        out_shape=[jax.ShapeDtypeStruct((T, D), BF16), jax.ShapeDtypeStruct((T, D_RNN), BF16)],
        compiler_params=_params(("parallel",), 40),
        name="emb_ln",
    )(x, g.reshape(1, D), b.reshape(1, D), w_in)


def _cast_specs(stack, layer, n_steps, step_of):
    _, rows, cols = stack.shape
    slab = -(-rows // (n_steps * 16)) * 16
    last = -(-rows // slab) - 1
    src = pl.BlockSpec((None, slab, cols), lambda *g: (layer, jnp.minimum(step_of(*g), last), 0))
    dst = pl.BlockSpec((slab, cols), lambda *g: (jnp.minimum(step_of(*g), last), 0))
    return src, dst, jax.ShapeDtypeStruct((rows, cols), BF16)


def _proj_kernel(act, x_ref, w_ref, cast_src_ref, o_ref, cast_dst_ref):
    acc = jnp.dot(x_ref[...], w_ref[...], preferred_element_type=F32)
    o_ref[...] = act(acc).astype(BF16)
    cast_dst_ref[...] = cast_src_ref[...].astype(BF16)


def _proj(xb, w_in, layer, col_from, n_cols, act, name, cast):
    T, D = xb.shape
    first = col_from // PROJ_COLS
    grid = (T // PROJ_ROWS, n_cols // PROJ_COLS)
    src, dst, cast_shape = _cast_specs(*cast, grid[0] * grid[1], lambda i, j: i * grid[1] + j)
    return pl.pallas_call(
        functools.partial(_proj_kernel, act),
        grid=grid,
        in_specs=[pl.BlockSpec((PROJ_ROWS, D), lambda i, j: (i, 0)),
                  pl.BlockSpec((None, D, PROJ_COLS), lambda i, j: (layer, 0, first + j)), src],
        out_specs=[pl.BlockSpec((PROJ_ROWS, PROJ_COLS), lambda i, j: (i, j)), dst],
        out_shape=[jax.ShapeDtypeStruct((T, n_cols), BF16), cast_shape],
        compiler_params=_params(("parallel", "arbitrary"), 48),
        name=name,
    )(xb, w_in, cast[0])


def _lru_gates(first, xr_ref, cw_ref, cb_ref, wa_ref, ba_ref, wx_ref, bx_ref, lam_ref,
               xpad_ref, hc_ref, a_ref, u_ref):
    rows = xr_ref.shape[0]

    @pl.when(first)
    def _():
        xpad_ref[0:CONV_PAD, :] = jnp.zeros((CONV_PAD, D_RNN), F32)
        hc_ref[...] = jnp.zeros_like(hc_ref)

    x = xr_ref[...].astype(F32)
    xpad_ref[CONV_PAD:CONV_PAD + rows, :] = x
    xc = cb_ref[...] + cw_ref[CONV_WIDTH - 1:CONV_WIDTH, :] * x
    for k in range(CONV_WIDTH - 1):
        shift = CONV_WIDTH - 1 - k
        xc = xc + cw_ref[k:k + 1, :] * xpad_ref[CONV_PAD - shift:CONV_PAD - shift + rows, :]
    xpad_ref[0:CONV_PAD, :] = x[rows - CONV_PAD:rows, :]

    lam = lam_ref[...]
    sp = jnp.maximum(-lam, 0.0) + jnp.log1p(jnp.exp(-jnp.abs(lam)))
    for h in range(RNN_HEADS):
        cols = slice(h * HEAD_DIM, (h + 1) * HEAD_DIM)
        xh = xc[:, cols]
        xhb = xh.astype(BF16)
        r = _sigmoid(jnp.dot(xhb, wa_ref[h], preferred_element_type=F32) + ba_ref[:, cols])
        gi = _sigmoid(jnp.dot(xhb, wx_ref[h], preferred_element_type=F32) + bx_ref[:, cols])
        log_a = (-LRU_C) * r * sp[:, cols]
        a_ref[:, cols] = jnp.exp(log_a)
        th = jnp.tanh(log_a)
        u_ref[:, cols] = jnp.sqrt(-2.0 * th / (1.0 - th)) * (gi * xh)


def _lru_scan(gate_ref, o_ref, hc_ref, a_ref, u_ref, h_ref):
    def step(t, h):
        h = a_ref[pl.ds(t, 1), :] * h + u_ref[pl.ds(t, 1), :]
        h_ref[pl.ds(t, 1), :] = h
        return h

    hc_ref[0:1, :] = lax.fori_loop(0, a_ref.shape[0], step, hc_ref[0:1, :], unroll=8)
    o_ref[...] = (h_ref[...] * gate_ref[...].astype(F32)).astype(BF16)


def _proj_lru_kernel(act, tiles_per_seq, x_ref, w_ref, cast_src_ref, xr_ref, gate_ref, cw_ref, cb_ref,
                     wa_ref, ba_ref, wx_ref, bx_ref, lam_ref, o_ref, cast_dst_ref, a_out_ref,
                     xpad_ref, hc_ref, a_ref, u_ref, h_ref):
    step = pl.program_id(0) * pl.num_programs(1) + pl.program_id(1)
    _lru_gates(step % tiles_per_seq == 0, xr_ref, cw_ref, cb_ref, wa_ref, ba_ref, wx_ref, bx_ref,
               lam_ref, xpad_ref, hc_ref, a_ref, u_ref)
    acc = jnp.dot(x_ref[...], w_ref[...], preferred_element_type=F32)
    o_ref[...] = act(acc).astype(BF16)
    cast_dst_ref[...] = cast_src_ref[...].astype(BF16)
    _lru_scan(gate_ref, a_out_ref, hc_ref, a_ref, u_ref, h_ref)


def _proj_lru(xb, w_in, layer, col_from, n_cols, act, name, cast, x_rnn, gelu_cols, seq,
              conv_w, conv_b, w_a, b_a, w_x, b_x, lam):
    T, D = xb.shape
    first = col_from // PROJ_COLS
    grid = (T // PROJ_ROWS, n_cols // PROJ_COLS)
    n_steps = grid[0] * grid[1]
    step_of = lambda i, j: i * grid[1] + j
    lru_rows = T // n_steps
    src, dst, cast_shape = _cast_specs(*cast, n_steps, step_of)
    tile = pl.BlockSpec((lru_rows, D_RNN), lambda i, j: (step_of(i, j), 0))
    vec = _const_spec((1, D_RNN))
    gate_w = _const_spec((RNN_HEADS, HEAD_DIM, HEAD_DIM))
    tile_f32 = pltpu.VMEM((lru_rows, D_RNN), F32)
    return pl.pallas_call(
        functools.partial(_proj_lru_kernel, act, seq // lru_rows),
        grid=grid,
        in_specs=[pl.BlockSpec((PROJ_ROWS, D), lambda i, j: (i, 0)),
                  pl.BlockSpec((None, D, PROJ_COLS), lambda i, j: (layer, 0, first + j)),
                  src, tile, tile, _const_spec((CONV_WIDTH, D_RNN)), vec, gate_w, vec, gate_w, vec, vec],
        out_specs=[pl.BlockSpec((PROJ_ROWS, PROJ_COLS), lambda i, j: (i, j)), dst, tile],
        out_shape=[jax.ShapeDtypeStruct((T, n_cols), BF16), cast_shape,
                   jax.ShapeDtypeStruct((T, D_RNN), BF16)],
        scratch_shapes=[pltpu.VMEM((CONV_PAD + lru_rows, D_RNN), F32), pltpu.VMEM((8, D_RNN), F32),
                        tile_f32, tile_f32, tile_f32],
        compiler_params=_params(("arbitrary", "arbitrary"), 52),
        name=name,
    )(xb, w_in, cast[0], x_rnn, gelu_cols, conv_w, conv_b.reshape(1, D_RNN), w_a.astype(BF16),
      b_a.reshape(1, D_RNN), w_x.astype(BF16), b_x.reshape(1, D_RNN), lam.reshape(1, D_RNN))


def _spatial_gating(u_ref, v_ref, g_ref, b_ref, ws_ref, bs_ref, o_ref):
    rows = u_ref.shape[0]
    v = _layer_norm(v_ref[...].astype(F32), g_ref[...], b_ref[...]).astype(BF16)
    t_out = lax.broadcasted_iota(jnp.int32, (CHUNK, CHUNK), 0)
    t_in = lax.broadcasted_iota(jnp.int32, (CHUNK, CHUNK), 1)
    causal = t_in <= t_out
    for g in range(SGU_GROUPS):
        cols = slice(g * SGU_GROUP_DIM, (g + 1) * SGU_GROUP_DIM)
        ws = jnp.where(causal, ws_ref[g], 0.0).astype(BF16)
        bias = bs_ref[:, g:g + 1]
        for c in range(rows // CHUNK):
            rws = slice(c * CHUNK, (c + 1) * CHUNK)
            mixed = jnp.dot(ws, v[rws, cols], preferred_element_type=F32) + bias
            o_ref[rws, cols] = (u_ref[rws, cols].astype(F32) * mixed).astype(BF16)


def _mix_out_kernel(alpha, x_is_raw, a_ref, u_ref, v_ref, sga_ref, sgb_ref, x_ref, emb_g_ref,
                    emb_b_ref, sgu_g_ref, sgu_b_ref, ws_ref, bs_ref, woa_ref, wob_ref, wout_ref,
                    g_ref, beta_ref, cast_src_ref, o_ref, packed_ref, cast_dst_ref, b_ref):
    _spatial_gating(u_ref, v_ref, sgu_g_ref, sgu_b_ref, ws_ref, bs_ref, b_ref)
    ya = jnp.dot(a_ref[...], woa_ref[...], preferred_element_type=F32)
    yb = jnp.dot(b_ref[...], wob_ref[...], preferred_element_type=F32)
    merged = sga_ref[...].astype(F32) * ya + sgb_ref[...].astype(F32) * yb
    m = jnp.dot(merged.astype(BF16), wout_ref[...], preferred_element_type=F32)
    x = x_ref[...]
    if x_is_raw:
        x = _layer_norm(x, emb_g_ref[...], emb_b_ref[...])
    y = _layer_norm(alpha * x + m, g_ref[...], beta_ref[...])
    o_ref[...] = y
    packed_ref[...] = _pack_halves(y)
    cast_dst_ref[...] = cast_src_ref[...].astype(BF16)


def _mix_out(alpha, layer, a, gelu_cols, sigmoid_cols, x, x_is_raw, emb_ln_g, emb_ln_b,
             sgu_ln_g, sgu_ln_b, w_s, b_s, w_o_rnn, w_o_sgu, w_out, ln_g, ln_b, cast):
    T, D = x.shape
    n_steps = T // MIX_ROWS
    row = lambda w, c: pl.BlockSpec((MIX_ROWS, w), lambda i: (i, c))
    cast_src, cast_dst, cast_shape = _cast_specs(*cast, n_steps, lambda i: i)
    return pl.pallas_call(
        functools.partial(_mix_out_kernel, alpha, x_is_raw),
        grid=(n_steps,),
        in_specs=[row(D_RNN, 0), row(D_SGU, 1), row(D_SGU, 2), row(D, 0), row(D, 1), row(D, 0),
                  _const_spec((1, D)), _const_spec((1, D)),
                  _const_spec((1, D_SGU)), _const_spec((1, D_SGU)),
                  _const_spec((SGU_GROUPS, CHUNK, CHUNK)), _const_spec((CHUNK, SGU_GROUPS)),
                  _layer_spec(layer, (D_RNN, D)), _layer_spec(layer, (D_SGU, D)),
                  _layer_spec(layer, (D, D)), _const_spec((1, D)), _const_spec((1, D)), cast_src],
        out_specs=[row(D, 0), row(D // 2, 0), cast_dst],
        out_shape=[jax.ShapeDtypeStruct((T, D), F32), jax.ShapeDtypeStruct((T, D // 2), jnp.uint32),
                   cast_shape],
        scratch_shapes=[pltpu.VMEM((MIX_ROWS, D_SGU), BF16)],
        compiler_params=_params(("parallel",), 56),
        name="mix_out",
    )(a, gelu_cols, gelu_cols, sigmoid_cols, sigmoid_cols, x,
      emb_ln_g.reshape(1, D), emb_ln_b.reshape(1, D),
      sgu_ln_g.reshape(1, D_SGU), sgu_ln_b.reshape(1, D_SGU), w_s, b_s.T,
      w_o_rnn, w_o_sgu, w_out, ln_g.reshape(1, D), ln_b.reshape(1, D), cast[0])


def _first_max(v, row, n):
    m = jnp.max(v, axis=0, keepdims=True)
    idx = jnp.min(jnp.where(v == m, row, float(n)), axis=0, keepdims=True)
    return m, idx


def _router_kernel(x_ref, wt_ref, b_ref, idx_ref, gate_ref, rank_ref, cnt_ref, run_ref):
    rows = x_ref.shape[0]

    @pl.when(pl.program_id(0) == 0)
    def _():
        run_ref[...] = jnp.zeros_like(run_ref)

    def nt_dot(w, x):
        return lax.dot_general(w, x, (((1,), (1,)), ((), ())), preferred_element_type=F32)

    x = x_ref[...]
    xh = x.astype(BF16)
    xl = (x - xh.astype(F32)).astype(BF16)
    w = wt_ref[...]
    wh = w.astype(BF16)
    wl = (w - wh.astype(F32)).astype(BF16)
    logits = nt_dot(wh, xh) + (nt_dot(wh, xl) + nt_dot(wl, xh))
    e = jnp.exp(logits - jnp.max(logits, axis=0, keepdims=True))
    scores = e / jnp.sum(e, axis=0, keepdims=True)
    sel = scores + b_ref[...]

    row = lax.broadcasted_iota(jnp.int32, (N_EXPERTS, rows), 0).astype(F32)
    grow = lax.broadcasted_iota(jnp.int32, (EXPERTS_PER_GROUP, rows), 0).astype(F32)
    neg_inf = float("-inf")
    best_score = None
    best_group = None
    for g in range(N_GROUPS):
        v = sel[g * EXPERTS_PER_GROUP:(g + 1) * EXPERTS_PER_GROUP, :]
        m1, i1 = _first_max(v, grow, EXPERTS_PER_GROUP)
        m2 = jnp.max(jnp.where(grow == i1, neg_inf, v), axis=0, keepdims=True)
        s = m1 + m2
        if g == 0:
            best_score, best_group = s, jnp.zeros_like(s)
        else:
            better = s > best_score
            best_group = jnp.where(better, float(g), best_group)
            best_score = jnp.where(better, s, best_score)

    lo = best_group * float(EXPERTS_PER_GROUP)
    in_group = jnp.logical_and(row >= lo, row < lo + float(EXPERTS_PER_GROUP))
    masked = jnp.where(in_group, sel, neg_inf)
    _, i1 = _first_max(masked, row, N_EXPERTS)
    pick1 = row == i1
    _, i2 = _first_max(jnp.where(pick1, neg_inf, masked), row, N_EXPERTS)
    pick2 = row == i2
    s1 = jnp.sum(jnp.where(pick1, scores, 0.0), axis=0, keepdims=True)
    s2 = jnp.sum(jnp.where(pick2, scores, 0.0), axis=0, keepdims=True)
    den = s1 + s2

    onehot = jnp.where(jnp.logical_or(pick1, pick2), 1.0, 0.0)
    before = (lax.broadcasted_iota(jnp.int32, (rows, rows), 0)
              < lax.broadcasted_iota(jnp.int32, (rows, rows), 1))
    prefix = jnp.dot(onehot.astype(BF16), jnp.where(before, 1.0, 0.0).astype(BF16),
                     preferred_element_type=F32)
    pos = prefix + run_ref[:, 0:1]
    r1 = jnp.sum(jnp.where(pick1, pos, 0.0), axis=0, keepdims=True)
    r2 = jnp.sum(jnp.where(pick2, pos, 0.0), axis=0, keepdims=True)
    run_ref[...] = run_ref[...] + jnp.sum(onehot, axis=1, keepdims=True)

    idx_ref[0:1, :] = i1.astype(jnp.int32)
    idx_ref[1:2, :] = i2.astype(jnp.int32)
    gate_ref[0:1, :] = s1 / den
    gate_ref[1:2, :] = s2 / den
    rank_ref[0:1, :] = r1.astype(jnp.int32)
    rank_ref[1:2, :] = r2.astype(jnp.int32)
    cnt_ref[...] = run_ref[...].astype(jnp.int32)


def _router(x, router_w, router_b):
    T, D = x.shape
    pair = pl.BlockSpec((TOP_K, ROUTER_ROWS), lambda i: (0, i))
    return pl.pallas_call(
        _router_kernel,
        grid=(T // ROUTER_ROWS,),
        in_specs=[pl.BlockSpec((ROUTER_ROWS, D), lambda i: (i, 0)),
                  _const_spec((N_EXPERTS, D)), _const_spec((N_EXPERTS, 1))],
        out_specs=[pair, pair, pair, pl.BlockSpec((N_EXPERTS, 128), lambda i: (0, 0))],
        out_shape=[jax.ShapeDtypeStruct((TOP_K, T), jnp.int32),
                   jax.ShapeDtypeStruct((TOP_K, T), F32),
                   jax.ShapeDtypeStruct((TOP_K, T), jnp.int32),
                   jax.ShapeDtypeStruct((N_EXPERTS, 128), jnp.int32)],
        scratch_shapes=[pltpu.VMEM((N_EXPERTS, 128), F32)],
        compiler_params=_params(("arbitrary",), 40),
        name="router",
    )(x, router_w.T, router_b.reshape(N_EXPERTS, 1))


def _dispatch_kernel(pad_from_ref, pad_n_ref, n_used_ref, dest_ref, x_ref, xs_hbm, zero_ref, sem, pad_sem):
    i = pl.program_id(0)
    rows = DISPATCH_ROWS
    n_blocks = xs_hbm.shape[0] // EXPERT_ROWS
    zero_rows = zero_ref.shape[0]

    def tail_copies(j):
        block = n_used_ref[0] + j
        off = pl.multiple_of(block * EXPERT_ROWS, EXPERT_ROWS)
        return block < n_blocks, [
            pltpu.make_async_copy(zero_ref, xs_hbm.at[pl.ds(off + part * zero_rows, zero_rows)], pad_sem)
            for part in range(EXPERT_ROWS // zero_rows)]

    def pad_copy(e, bit):
        n = pad_n_ref[e]
        if bit < ROW_TILE:
            off = pad_from_ref[e] + bit - 1
            return pltpu.make_async_copy(zero_ref.at[pl.ds(0, 1)], xs_hbm.at[pl.ds(off, 1)], pad_sem)
        done = (n & (ROW_TILE - 1)) + (n & ~(2 * bit - 1))
        off = pl.multiple_of(pad_from_ref[e] + done, ROW_TILE)
        return pltpu.make_async_copy(zero_ref.at[pl.ds(0, bit)], xs_hbm.at[pl.ds(off, bit)], pad_sem)

    def pad_needed(e, bit):
        n = pad_n_ref[e]
        if bit < ROW_TILE:
            return bit <= (n & (ROW_TILE - 1))
        return (n & bit) != 0

    bits = list(range(1, ROW_TILE)) + [
        1 << k for k in range(ROW_TILE.bit_length() - 1, EXPERT_ROWS.bit_length() - 1)]

    @pl.when(i == 0)
    def _():
        zero_ref[...] = jnp.zeros_like(zero_ref)
        for e in range(N_EXPERTS):
            for bit in bits:
                @pl.when(pad_needed(e, bit))
                def _():
                    pad_copy(e, bit).start()
        for j in range(N_EXPERTS):
            needed, copies = tail_copies(j)

            @pl.when(needed)
            def _():
                for c in copies:
                    c.start()

    for t in range(rows):
        src = x_ref.at[pl.ds(t, 1)]
        for k in range(TOP_K):
            pltpu.make_async_copy(src, xs_hbm.at[pl.ds(dest_ref[0, k * rows + t], 1)],
                                  sem).start(priority=k % N_DMA_PRIORITIES)
    for _ in range(TOP_K):
        pltpu.make_async_copy(x_ref, xs_hbm.at[pl.ds(0, rows)], sem).wait()

    @pl.when(i == 0)
    def _():
        for e in range(N_EXPERTS):
            for bit in bits:
                @pl.when(pad_needed(e, bit))
                def _():
                    pad_copy(e, bit).wait()
        for j in range(N_EXPERTS):
            needed, copies = tail_copies(j)

            @pl.when(needed)
            def _():
                for c in copies:
                    c.wait()


def _dispatch(x, dest_tiles, pad_from, pad_n, n_used, n_slots):
    T, D = x.shape
    grid_spec = pltpu.PrefetchScalarGridSpec(
        num_scalar_prefetch=3,
        grid=(T // DISPATCH_ROWS,),
        in_specs=[pl.BlockSpec((None, 1, TOP_K * DISPATCH_ROWS), lambda i, *_: (i, 0, 0),
                               memory_space=pltpu.SMEM),
                  pl.BlockSpec((DISPATCH_ROWS, D), lambda i, *_: (i, 0))],
        out_specs=pl.BlockSpec(memory_space=pl.ANY),
        scratch_shapes=[pltpu.VMEM((EXPERT_ROWS // 2, D), x.dtype),
                        pltpu.SemaphoreType.DMA(()), pltpu.SemaphoreType.DMA(())],
    )
    return pl.pallas_call(
        _dispatch_kernel,
        grid_spec=grid_spec,
        out_shape=jax.ShapeDtypeStruct((n_slots, D), x.dtype),
        compiler_params=_params(("arbitrary",), 16),
        name="dispatch",
    )(pad_from, pad_n, n_used, dest_tiles, x)


def _expert_kernel(block_e_ref, n_used_ref, xs_ref, w1_ref, w3_ref, w2_ref, *rest):
    n_casts = len(rest) // 2
    o_ref = rest[n_casts]
    used = pl.program_id(0) < n_used_ref[0]

    @pl.when(jnp.logical_not(used))
    def _():
        o_ref[...] = jnp.zeros_like(o_ref)

    @pl.when(used)
    def _():
        lo, hi = _unpack_halves(xs_ref[...])
        x = jnp.concatenate([lo.astype(BF16), hi.astype(BF16)], axis=1)
        h1 = jnp.dot(x, w1_ref[...], preferred_element_type=F32)
        h3 = jnp.dot(x, w3_ref[...], preferred_element_type=F32)
        hidden = (h1 * _sigmoid(h1)) * h3
        out = jnp.dot(hidden.astype(BF16), w2_ref[...], preferred_element_type=F32)
        o_ref[...] = _pack_halves(out)
        for src_ref, dst_ref in zip(rest[:n_casts], rest[n_casts + 1:]):
            dst_ref[...] = src_ref[...].astype(BF16)


def _experts(xs, block_e, n_used, w1, w3, w2, casts=()):
    P, packed = xs.shape
    D = 2 * packed
    n_blocks = P // EXPERT_ROWS
    used_rows = pl.BlockSpec((EXPERT_ROWS, packed),
                             lambda i, be, nu: (jnp.maximum(jnp.minimum(i, nu[0] - 1), 0), 0))
    rows = pl.BlockSpec((EXPERT_ROWS, packed), lambda i, be, nu: (i, 0))
    expert_slab = lambda i, be, nu: (be[i], 0, 0)
    cast_specs = [_cast_specs(stack, layer, n_blocks - N_EXPERTS, lambda i, be, nu: i)
                  for stack, layer in casts]
    grid_spec = pltpu.PrefetchScalarGridSpec(
        num_scalar_prefetch=2,
        grid=(n_blocks,),
        in_specs=[used_rows,
                  pl.BlockSpec((None, D, D_EXPERT), expert_slab),
                  pl.BlockSpec((None, D, D_EXPERT), expert_slab),
                  pl.BlockSpec((None, D_EXPERT, D), expert_slab)] + [s[0] for s in cast_specs],
        out_specs=[rows] + [s[1] for s in cast_specs],
    )
    out = pl.pallas_call(
        _expert_kernel,
        grid_spec=grid_spec,
        out_shape=[jax.ShapeDtypeStruct((P, packed), jnp.uint32)] + [s[2] for s in cast_specs],
        compiler_params=_params(("arbitrary",), 56),
        name="experts",
    )(block_e, n_used, xs, w1, w3, w2, *[stack for stack, _ in casts])
    return out[0], out[1:]


def _combine_kernel(alpha, project, dest_ref, dest_next_ref, x_ref, gate_ref, g_ref, b_ref, y_hbm,
                    *rest):
    if project:
        w_ref, of_ref, ob_ref, oxr_ref, buf_a, buf_b, sem = rest
    else:
        of_ref, ob_ref, buf_a, buf_b, sem = rest
    rows = COMBINE_ROWS
    half = rows // 2
    i = pl.program_id(0)

    def copy(table_ref, h, t, k, buf, s):
        return pltpu.make_async_copy(y_hbm.at[pl.ds(table_ref[0, k * rows + h * half + t], 1)],
                                     buf.at[k, pl.ds(t, 1)], sem.at[s])

    def gather(table_ref, h, buf, s):
        for t in range(half):
            for k in range(TOP_K):
                copy(table_ref, h, t, k, buf, s).start(priority=k % N_DMA_PRIORITIES)

    def wait(buf, s):
        for k in range(TOP_K):
            pltpu.make_async_copy(y_hbm.at[pl.ds(0, half)], buf.at[k], sem.at[s]).wait()

    def reduce(buf, h):
        r = slice(h * half, (h + 1) * half)
        lo0, hi0 = _unpack_halves(buf[0])
        lo1, hi1 = _unpack_halves(buf[1])
        g0 = gate_ref[r, 0:1]
        g1 = gate_ref[r, 1:2]
        f = jnp.concatenate([g0 * lo0 + g1 * lo1, g0 * hi0 + g1 * hi1], axis=1)
        y = _layer_norm(alpha * x_ref[r, :] + f, g_ref[...], b_ref[...])
        of_ref[r, :] = y
        yb = y.astype(BF16)
        ob_ref[r, :] = yb
        if project:
            oxr_ref[r, :] = jnp.dot(yb, w_ref[...], preferred_element_type=F32).astype(BF16)

    @pl.when(i == 0)
    def _():
        def issue(t, carry):
            for k in range(TOP_K):
                copy(dest_ref, 0, t, k, buf_a, 0).start()
            return carry

        lax.fori_loop(0, half, issue, 0, unroll=8)

    wait(buf_a, 0)
    gather(dest_ref, 1, buf_b, 1)
    reduce(buf_a, 0)
    wait(buf_b, 1)
    gather(dest_next_ref, 0, buf_a, 0)
    reduce(buf_b, 1)

    @pl.when(i == pl.num_programs(0) - 1)
    def _():
        wait(buf_a, 0)


def _combine(alpha, dest_tiles, x, gate_t, ln_g, ln_b, yb, next_w_in=None):
    T, D = x.shape
    n_tiles = T // COMBINE_ROWS
    row = lambda w: pl.BlockSpec((COMBINE_ROWS, w), lambda i: (i, 0))
    table = lambda index: pl.BlockSpec((None, 1, TOP_K * COMBINE_ROWS), index, memory_space=pltpu.SMEM)
    half_buf = pltpu.VMEM((TOP_K, COMBINE_ROWS // 2, yb.shape[1]), yb.dtype)
    project = next_w_in is not None
    in_specs = [table(lambda i: (i, 0, 0)),
                table(lambda i: (jnp.minimum(i + 1, n_tiles - 1), 0, 0)),
                row(D), pl.BlockSpec((COMBINE_ROWS, TOP_K), lambda i: (i, 0)),
                _const_spec((1, D)), _const_spec((1, D)),
                pl.BlockSpec(memory_space=pl.ANY)]
    out_specs = [row(D), row(D)]
    out_shape = [jax.ShapeDtypeStruct((T, D), F32), jax.ShapeDtypeStruct((T, D), BF16)]
    operands = [dest_tiles, dest_tiles, x, gate_t, ln_g.reshape(1, D), ln_b.reshape(1, D), yb]
    if project:
        in_specs.append(_const_spec((D, D_RNN)))
        out_specs.append(row(D_RNN))
        out_shape.append(jax.ShapeDtypeStruct((T, D_RNN), BF16))
        operands.append(next_w_in)
    return pl.pallas_call(
        functools.partial(_combine_kernel, alpha, project),
        grid=(n_tiles,),
        in_specs=in_specs,
        out_specs=out_specs,
        out_shape=out_shape,
        scratch_shapes=[half_buf, half_buf, pltpu.SemaphoreType.DMA((2,))],
        compiler_params=_params(("arbitrary",), 52),
        name="combine",
    )(*operands)


def _tile_pairs(dest, rows):
    T = dest.shape[1]
    return dest.reshape(TOP_K, T // rows, rows).transpose(1, 0, 2).reshape(T // rows, 1, TOP_K * rows)


def _moe(alpha, x, x_packed, router_w, router_b, w1, w3, w2, ln_g, ln_b, casts=()):
    T, D = x.shape
    n_blocks = (T * TOP_K) // EXPERT_ROWS + N_EXPERTS
    idx, gate, rank, cnt = _router(x, router_w, router_b)

    counts = cnt[:, 0]
    padded = (counts + EXPERT_ROWS - 1) // EXPERT_ROWS * EXPERT_ROWS
    pad_end = jnp.cumsum(padded)
    pad_start = pad_end - padded
    n_used = (pad_end[-1] // EXPERT_ROWS).astype(jnp.int32)
    block_start = jnp.minimum(jnp.arange(n_blocks, dtype=jnp.int32), n_used - 1) * EXPERT_ROWS
    block_e = jnp.minimum(jnp.sum(block_start[:, None] >= pad_end[None, :], axis=1),
                          N_EXPERTS - 1).astype(jnp.int32)
    expert_ids = jnp.arange(N_EXPERTS, dtype=jnp.int32)[:, None, None]
    dest = jnp.sum(jnp.where(idx[None] == expert_ids, pad_start[:, None, None], 0), axis=0) + rank

    n_used = n_used.reshape(1)
    xs = _dispatch(x_packed, _tile_pairs(dest, DISPATCH_ROWS), (pad_start + counts).astype(jnp.int32),
                   (padded - counts).astype(jnp.int32), n_used, n_blocks * EXPERT_ROWS)
    yb, casted = _experts(xs, block_e, n_used, w1, w3, w2, casts)
    new = _combine(alpha, _tile_pairs(dest, COMBINE_ROWS), x, gate.T, ln_g, ln_b, yb,
                   next_w_in=casted[0] if casted else None)
    return new, casted


def kernel(x, emb_ln_g, emb_ln_b, w_in, conv_w, conv_b, lru_w_a, lru_b_a, lru_w_x, lru_b_x, lru_lambda, w_o_rnn, sgu_ln_g, sgu_ln_b, sgu_w_s, sgu_b_s, w_o_sgu, w_out, ln1_g, ln1_b, router_w, router_b, expert_w1, expert_w3, expert_w2, ln2_g, ln2_b):
    batch, seq, D = x.shape
    depth = w_in.shape[0]
    alpha = float((2 * depth) ** 0.25)
    xf = x.reshape(batch * seq, D)
    mixer_f32 = (w_in, w_o_rnn, w_o_sgu, w_out)
    w_in, w_o_rnn, w_o_sgu, w_out = (w[0:1].astype(BF16) for w in mixer_f32)
    xb, x_rnn = _emb_ln(xf, emb_ln_g, emb_ln_b, w_in)
    rows_of = lambda w: w.reshape(depth, N_EXPERTS * w.shape[2], w.shape[3])
    w1_rows, w3_rows, w2_rows = rows_of(expert_w1), rows_of(expert_w3), rows_of(expert_w2)
    gelu_from = D_RNN
    sigmoid_from = 2 * D_RNN + 2 * D_SGU
    for l in range(depth):
        gelu_cols, w3 = _proj(xb, w_in, 0, gelu_from, sigmoid_from - gelu_from, _gelu, "proj_gelu",
                              cast=(w3_rows, l))
        sigmoid_cols, w1, a = _proj_lru(
            xb, w_in, 0, sigmoid_from, D_IN - sigmoid_from, _sigmoid, "proj_sigmoid_lru",
            (w1_rows, l), x_rnn, gelu_cols, seq, conv_w[l], conv_b[l], lru_w_a[l], lru_b_a[l],
            lru_w_x[l], lru_b_x[l], lru_lambda[l])
        x1, x1_packed, w2 = _mix_out(alpha, 0, a, gelu_cols, sigmoid_cols, xf, l == 0, emb_ln_g,
                                     emb_ln_b, sgu_ln_g[l], sgu_ln_b[l], sgu_w_s[l], sgu_b_s[l],
                                     w_o_rnn, w_o_sgu, w_out, ln1_g[l], ln1_b[l], cast=(w2_rows, l))
        next_mixer = [(w, l + 1) for w in mixer_f32] if l + 1 < depth else []
        (xf, xb, *x_rnn), casted = _moe(
            alpha, x1, x1_packed, router_w, router_b, w1.reshape(expert_w1.shape[1:]),
            w3.reshape(expert_w3.shape[1:]), w2.reshape(expert_w2.shape[1:]), ln2_g[l], ln2_b[l],
            casts=next_mixer)
        x_rnn = x_rnn[0] if x_rnn else None
        if casted:
            w_in, w_o_rnn, w_o_sgu, w_out = (w[None] for w in casted)
    return xf.reshape(batch, seq, D)
```

```python
import functools

import jax
import jax.numpy as jnp
import numpy as np
from jax import lax
from jax.experimental import pallas as pl
from jax.experimental.pallas import tpu as pltpu

F32 = jnp.float32
BF16 = jnp.bfloat16

D_MODEL = 2048
D_RNN = 1024
RNN_HEADS = 8
HEAD_DIM = D_RNN // RNN_HEADS
CONV_WIDTH = 4
LRU_C = 8.0
D_SGU = 1024
SGU_GROUPS = 8
SGU_GROUP_DIM = D_SGU // SGU_GROUPS
CHUNK = 128
N_EXPERTS = 16
N_GROUPS = 4
EXPERTS_PER_GROUP = N_EXPERTS // N_GROUPS
TOP_K = 2
D_EXPERT = 1408
D_IN = 2 * D_RNN + 2 * D_SGU + 2 * D_MODEL
LN_EPS = 1e-5
SQRT_2_OVER_PI = float(np.sqrt(2.0 / np.pi))

LN_ROWS = 512
PROJ_ROWS = 1024
PROJ_COLS = 1024
MIX_ROWS = 256
ROUTER_ROWS = 1024
DISPATCH_ROWS = 512
EXPERT_ROWS = 256
COMBINE_ROWS = 512
CONV_PAD = 8
N_DMA_PRIORITIES = 2
ROW_TILE = 8

MIB = 1024 * 1024


def _params(semantics, vmem_mib):
    return pltpu.CompilerParams(dimension_semantics=semantics,
                                vmem_limit_bytes=vmem_mib * MIB,
                                disable_bounds_checks=True)


def _const_spec(shape):
    zeros = (0,) * len(shape)
    return pl.BlockSpec(shape, lambda *_: zeros, pipeline_mode=pl.Buffered(1))


def _layer_spec(layer, shape):
    zeros = (0,) * len(shape)
    return pl.BlockSpec((None,) + tuple(shape), lambda *_: (layer,) + zeros,
                        pipeline_mode=pl.Buffered(1))


def _layer_norm(v, g, b):
    mu = jnp.mean(v, axis=-1, keepdims=True)
    d = v - mu
    var = jnp.mean(d * d, axis=-1, keepdims=True)
    return d * lax.rsqrt(var + LN_EPS) * g + b


def _gelu(x):
    inner = x * (SQRT_2_OVER_PI + (SQRT_2_OVER_PI * 0.044715) * (x * x))
    return x * (0.5 * jnp.tanh(inner) + 0.5)


def _pack_halves(y):
    n = y.shape[1] // 2
    bits = lambda v: lax.bitcast_convert_type(v.astype(BF16).astype(F32), jnp.uint32)
    return (bits(y[:, :n]) >> 16) | bits(y[:, n:])


def _unpack_halves(p):
    lo = lax.bitcast_convert_type(p << 16, F32)
    hi = lax.bitcast_convert_type(p & jnp.uint32(0xFFFF0000), F32)
    return lo, hi


def _sigmoid(x):
    return 0.5 * jnp.tanh(0.5 * x) + 0.5


def _project_branch_a(xb, w_ref, oxr_ref, ogate_ref, rows=slice(None)):
    acc = jnp.dot(xb, w_ref[...], preferred_element_type=F32)
    oxr_ref[rows, :] = acc[:, :D_RNN].astype(BF16)
    ogate_ref[rows, :] = _gelu(acc[:, D_RNN:]).astype(BF16)


def _emb_ln_kernel(x_ref, g_ref, b_ref, w_ref, ob_ref, oxr_ref, ogate_ref):
    xb = _layer_norm(x_ref[...], g_ref[...], b_ref[...]).astype(BF16)
    ob_ref[...] = xb
    _project_branch_a(xb, w_ref, oxr_ref, ogate_ref)


def _emb_ln(x, g, b, w_in):
    T, D = x.shape
    row = lambda w: pl.BlockSpec((LN_ROWS, w), lambda i: (i, 0))
    half = jax.ShapeDtypeStruct((T, D_RNN), BF16)
    return pl.pallas_call(
        _emb_ln_kernel,
        grid=(T // LN_ROWS,),
        in_specs=[row(D), _const_spec((1, D)), _const_spec((1, D)), _layer_spec(0, (D, 2 * D_RNN))],
        out_specs=[row(D), row(D_RNN), row(D_RNN)],
        out_shape=[jax.ShapeDtypeStruct((T, D), BF16), half, half],
        compiler_params=_params(("parallel",), 40),
        name="emb_ln",
    )(x, g.reshape(1, D), b.reshape(1, D), w_in)


def _cast_specs(stack, layer, n_steps, step_of):
    _, rows, cols = stack.shape
    slab = -(-rows // (n_steps * 16)) * 16
    last = -(-rows // slab) - 1
    src = pl.BlockSpec((None, slab, cols), lambda *g: (layer, jnp.minimum(step_of(*g), last), 0))
    dst = pl.BlockSpec((slab, cols), lambda *g: (jnp.minimum(step_of(*g), last), 0))
    return src, dst, jax.ShapeDtypeStruct((rows, cols), BF16)


def _proj_kernel(act, x_ref, w_ref, cast_src_ref, o_ref, cast_dst_ref):
    acc = jnp.dot(x_ref[...], w_ref[...], preferred_element_type=F32)
    o_ref[...] = act(acc).astype(BF16)
    cast_dst_ref[...] = cast_src_ref[...].astype(BF16)


def _proj(xb, w_in, layer, col_from, n_cols, act, name, cast):
    T, D = xb.shape
    first = col_from // PROJ_COLS
    grid = (T // PROJ_ROWS, n_cols // PROJ_COLS)
    src, dst, cast_shape = _cast_specs(*cast, grid[0] * grid[1], lambda i, j: i * grid[1] + j)
    return pl.pallas_call(
        functools.partial(_proj_kernel, act),
        grid=grid,
        in_specs=[pl.BlockSpec((PROJ_ROWS, D), lambda i, j: (i, 0)),
                  pl.BlockSpec((None, D, PROJ_COLS), lambda i, j: (layer, 0, first + j)), src],
        out_specs=[pl.BlockSpec((PROJ_ROWS, PROJ_COLS), lambda i, j: (i, j)), dst],
        out_shape=[jax.ShapeDtypeStruct((T, n_cols), BF16), cast_shape],
        compiler_params=_params(("parallel", "arbitrary"), 48),
        name=name,
    )(xb, w_in, cast[0])


def _lru_gates(first, xr_ref, cw_ref, cb_ref, wa_ref, ba_ref, wx_ref, bx_ref, lam_ref,
               xpad_ref, hc_ref, a_ref, u_ref):
    rows = xr_ref.shape[0]

    @pl.when(first)
    def _():
        xpad_ref[0:CONV_PAD, :] = jnp.zeros((CONV_PAD, D_RNN), F32)
        hc_ref[...] = jnp.zeros_like(hc_ref)

    x = xr_ref[...].astype(F32)
    xpad_ref[CONV_PAD:CONV_PAD + rows, :] = x
    xc = cb_ref[...] + cw_ref[CONV_WIDTH - 1:CONV_WIDTH, :] * x
    for k in range(CONV_WIDTH - 1):
        shift = CONV_WIDTH - 1 - k
        xc = xc + cw_ref[k:k + 1, :] * xpad_ref[CONV_PAD - shift:CONV_PAD - shift + rows, :]
    xpad_ref[0:CONV_PAD, :] = x[rows - CONV_PAD:rows, :]

    lam = lam_ref[...]
    sp = jnp.maximum(-lam, 0.0) + jnp.log1p(jnp.exp(-jnp.abs(lam)))
    for h in range(RNN_HEADS):
        cols = slice(h * HEAD_DIM, (h + 1) * HEAD_DIM)
        xh = xc[:, cols]
        xhb = xh.astype(BF16)
        r = _sigmoid(jnp.dot(xhb, wa_ref[h], preferred_element_type=F32) + ba_ref[:, cols])
        gi = _sigmoid(jnp.dot(xhb, wx_ref[h], preferred_element_type=F32) + bx_ref[:, cols])
        log_a = (-LRU_C) * r * sp[:, cols]
        a_ref[:, cols] = jnp.exp(log_a)
        th = jnp.tanh(log_a)
        u_ref[:, cols] = jnp.sqrt(-2.0 * th / (1.0 - th)) * (gi * xh)


def _lru_scan(gate_ref, o_ref, hc_ref, a_ref, u_ref, h_ref):
    def step(t, h):
        h = a_ref[pl.ds(t, 1), :] * h + u_ref[pl.ds(t, 1), :]
        h_ref[pl.ds(t, 1), :] = h
        return h

    hc_ref[0:1, :] = lax.fori_loop(0, a_ref.shape[0], step, hc_ref[0:1, :], unroll=8)
    o_ref[...] = (h_ref[...] * gate_ref[...].astype(F32)).astype(BF16)


def _proj_lru_kernel(act, tiles_per_seq, x_ref, w_ref, cast_src_ref, xr_ref, gate_ref, cw_ref, cb_ref,
                     wa_ref, ba_ref, wx_ref, bx_ref, lam_ref, o_ref, cast_dst_ref, a_out_ref,
                     xpad_ref, hc_ref, a_ref, u_ref, h_ref):
    step = pl.program_id(0) * pl.num_programs(1) + pl.program_id(1)
    _lru_gates(step % tiles_per_seq == 0, xr_ref, cw_ref, cb_ref, wa_ref, ba_ref, wx_ref, bx_ref,
               lam_ref, xpad_ref, hc_ref, a_ref, u_ref)
    acc = jnp.dot(x_ref[...], w_ref[...], preferred_element_type=F32)
    o_ref[...] = act(acc).astype(BF16)
    cast_dst_ref[...] = cast_src_ref[...].astype(BF16)
    _lru_scan(gate_ref, a_out_ref, hc_ref, a_ref, u_ref, h_ref)


def _proj_lru(xb, w_in, layer, col_from, n_cols, act, name, cast, x_rnn, gate, seq,
              conv_w, conv_b, w_a, b_a, w_x, b_x, lam):
    T, D = xb.shape
    first = col_from // PROJ_COLS
    grid = (T // PROJ_ROWS, n_cols // PROJ_COLS)
    n_steps = grid[0] * grid[1]
    step_of = lambda i, j: i * grid[1] + j
    lru_rows = T // n_steps
    src, dst, cast_shape = _cast_specs(*cast, n_steps, step_of)
    tile = pl.BlockSpec((lru_rows, D_RNN), lambda i, j: (step_of(i, j), 0))
    vec = _const_spec((1, D_RNN))
    gate_w = _const_spec((RNN_HEADS, HEAD_DIM, HEAD_DIM))
    tile_f32 = pltpu.VMEM((lru_rows, D_RNN), F32)
    return pl.pallas_call(
        functools.partial(_proj_lru_kernel, act, seq // lru_rows),
        grid=grid,
        in_specs=[pl.BlockSpec((PROJ_ROWS, D), lambda i, j: (i, 0)),
                  pl.BlockSpec((None, D, PROJ_COLS), lambda i, j: (layer, 0, first + j)),
                  src, tile, tile, _const_spec((CONV_WIDTH, D_RNN)), vec, gate_w, vec, gate_w, vec, vec],
        out_specs=[pl.BlockSpec((PROJ_ROWS, PROJ_COLS), lambda i, j: (i, j)), dst, tile],
        out_shape=[jax.ShapeDtypeStruct((T, n_cols), BF16), cast_shape,
                   jax.ShapeDtypeStruct((T, D_RNN), BF16)],
        scratch_shapes=[pltpu.VMEM((CONV_PAD + lru_rows, D_RNN), F32), pltpu.VMEM((8, D_RNN), F32),
                        tile_f32, tile_f32, tile_f32],
        compiler_params=_params(("arbitrary", "arbitrary"), 52),
        name=name,
    )(xb, w_in, cast[0], x_rnn, gate, conv_w, conv_b.reshape(1, D_RNN), w_a.astype(BF16),
      b_a.reshape(1, D_RNN), w_x.astype(BF16), b_x.reshape(1, D_RNN), lam.reshape(1, D_RNN))


def _spatial_gating(u_ref, v_ref, g_ref, b_ref, ws_ref, bs_ref, o_ref):
    rows = u_ref.shape[0]
    v = _layer_norm(v_ref[...].astype(F32), g_ref[...], b_ref[...]).astype(BF16)
    t_out = lax.broadcasted_iota(jnp.int32, (CHUNK, CHUNK), 0)
    t_in = lax.broadcasted_iota(jnp.int32, (CHUNK, CHUNK), 1)
    causal = t_in <= t_out
    for g in range(SGU_GROUPS):
        cols = slice(g * SGU_GROUP_DIM, (g + 1) * SGU_GROUP_DIM)
        ws = jnp.where(causal, ws_ref[g], 0.0).astype(BF16)
        bias = bs_ref[:, g:g + 1]
        for c in range(rows // CHUNK):
            rws = slice(c * CHUNK, (c + 1) * CHUNK)
            mixed = jnp.dot(ws, v[rws, cols], preferred_element_type=F32) + bias
            o_ref[rws, cols] = (u_ref[rws, cols].astype(F32) * mixed).astype(BF16)


def _mix_out_kernel(alpha, x_is_raw, a_ref, u_ref, v_ref, sga_ref, sgb_ref, x_ref, emb_g_ref,
                    emb_b_ref, sgu_g_ref, sgu_b_ref, ws_ref, bs_ref, woa_ref, wob_ref, wout_ref,
                    g_ref, beta_ref, cast_src_ref, o_ref, packed_ref, cast_dst_ref, b_ref):
    _spatial_gating(u_ref, v_ref, sgu_g_ref, sgu_b_ref, ws_ref, bs_ref, b_ref)
    ya = jnp.dot(a_ref[...], woa_ref[...], preferred_element_type=F32)
    yb = jnp.dot(b_ref[...], wob_ref[...], preferred_element_type=F32)
    merged = sga_ref[...].astype(F32) * ya + sgb_ref[...].astype(F32) * yb
    m = jnp.dot(merged.astype(BF16), wout_ref[...], preferred_element_type=F32)
    x = x_ref[...]
    if x_is_raw:
        x = _layer_norm(x, emb_g_ref[...], emb_b_ref[...])
    y = _layer_norm(alpha * x + m, g_ref[...], beta_ref[...])
    o_ref[...] = y
    packed_ref[...] = _pack_halves(y)
    cast_dst_ref[...] = cast_src_ref[...].astype(BF16)


def _mix_out(alpha, layer, a, uv_cols, sigmoid_cols, x, x_is_raw, emb_ln_g, emb_ln_b,
             sgu_ln_g, sgu_ln_b, w_s, b_s, w_o_rnn, w_o_sgu, w_out, ln_g, ln_b, cast):
    T, D = x.shape
    n_steps = T // MIX_ROWS
    row = lambda w, c: pl.BlockSpec((MIX_ROWS, w), lambda i: (i, c))
    cast_src, cast_dst, cast_shape = _cast_specs(*cast, n_steps, lambda i: i)
    return pl.pallas_call(
        functools.partial(_mix_out_kernel, alpha, x_is_raw),
        grid=(n_steps,),
        in_specs=[row(D_RNN, 0), row(D_SGU, 0), row(D_SGU, 1), row(D, 0), row(D, 1), row(D, 0),
                  _const_spec((1, D)), _const_spec((1, D)),
                  _const_spec((1, D_SGU)), _const_spec((1, D_SGU)),
                  _const_spec((SGU_GROUPS, CHUNK, CHUNK)), _const_spec((CHUNK, SGU_GROUPS)),
                  _layer_spec(layer, (D_RNN, D)), _layer_spec(layer, (D_SGU, D)),
                  _layer_spec(layer, (D, D)), _const_spec((1, D)), _const_spec((1, D)), cast_src],
        out_specs=[row(D, 0), row(D // 2, 0), cast_dst],
        out_shape=[jax.ShapeDtypeStruct((T, D), F32), jax.ShapeDtypeStruct((T, D // 2), jnp.uint32),
                   cast_shape],
        scratch_shapes=[pltpu.VMEM((MIX_ROWS, D_SGU), BF16)],
        compiler_params=_params(("parallel",), 56),
        name="mix_out",
    )(a, uv_cols, uv_cols, sigmoid_cols, sigmoid_cols, x,
      emb_ln_g.reshape(1, D), emb_ln_b.reshape(1, D),
      sgu_ln_g.reshape(1, D_SGU), sgu_ln_b.reshape(1, D_SGU), w_s, b_s.T,
      w_o_rnn, w_o_sgu, w_out, ln_g.reshape(1, D), ln_b.reshape(1, D), cast[0])


def _first_max(v, row, n):
    m = jnp.max(v, axis=0, keepdims=True)
    idx = jnp.min(jnp.where(v == m, row, float(n)), axis=0, keepdims=True)
    return m, idx


def _router_kernel(x_ref, wt_ref, b_ref, idx_ref, gate_ref, rank_ref, cnt_ref, run_ref):
    rows = x_ref.shape[0]

    @pl.when(pl.program_id(0) == 0)
    def _():
        run_ref[...] = jnp.zeros_like(run_ref)

    def nt_dot(w, x):
        return lax.dot_general(w, x, (((1,), (1,)), ((), ())), preferred_element_type=F32)

    x = x_ref[...]
    xh = x.astype(BF16)
    xl = (x - xh.astype(F32)).astype(BF16)
    w = wt_ref[...]
    wh = w.astype(BF16)
    wl = (w - wh.astype(F32)).astype(BF16)
    logits = nt_dot(wh, xh) + (nt_dot(wh, xl) + nt_dot(wl, xh))
    e = jnp.exp(logits - jnp.max(logits, axis=0, keepdims=True))
    scores = e / jnp.sum(e, axis=0, keepdims=True)
    sel = scores + b_ref[...]

    row = lax.broadcasted_iota(jnp.int32, (N_EXPERTS, rows), 0).astype(F32)
    grow = lax.broadcasted_iota(jnp.int32, (EXPERTS_PER_GROUP, rows), 0).astype(F32)
    neg_inf = float("-inf")
    best_score = None
    best_group = None
    for g in range(N_GROUPS):
        v = sel[g * EXPERTS_PER_GROUP:(g + 1) * EXPERTS_PER_GROUP, :]
        m1, i1 = _first_max(v, grow, EXPERTS_PER_GROUP)
        m2 = jnp.max(jnp.where(grow == i1, neg_inf, v), axis=0, keepdims=True)
        s = m1 + m2
        if g == 0:
            best_score, best_group = s, jnp.zeros_like(s)
        else:
            better = s > best_score
            best_group = jnp.where(better, float(g), best_group)
            best_score = jnp.where(better, s, best_score)

    lo = best_group * float(EXPERTS_PER_GROUP)
    in_group = jnp.logical_and(row >= lo, row < lo + float(EXPERTS_PER_GROUP))
    masked = jnp.where(in_group, sel, neg_inf)
    _, i1 = _first_max(masked, row, N_EXPERTS)
    pick1 = row == i1
    _, i2 = _first_max(jnp.where(pick1, neg_inf, masked), row, N_EXPERTS)
    pick2 = row == i2
    s1 = jnp.sum(jnp.where(pick1, scores, 0.0), axis=0, keepdims=True)
    s2 = jnp.sum(jnp.where(pick2, scores, 0.0), axis=0, keepdims=True)
    den = s1 + s2

    onehot = jnp.where(jnp.logical_or(pick1, pick2), 1.0, 0.0)
    before = (lax.broadcasted_iota(jnp.int32, (rows, rows), 0)
              < lax.broadcasted_iota(jnp.int32, (rows, rows), 1))
    prefix = jnp.dot(onehot.astype(BF16), jnp.where(before, 1.0, 0.0).astype(BF16),
                     preferred_element_type=F32)
    pos = prefix + run_ref[:, 0:1]
    r1 = jnp.sum(jnp.where(pick1, pos, 0.0), axis=0, keepdims=True)
    r2 = jnp.sum(jnp.where(pick2, pos, 0.0), axis=0, keepdims=True)
    run_ref[...] = run_ref[...] + jnp.sum(onehot, axis=1, keepdims=True)

    idx_ref[0:1, :] = i1.astype(jnp.int32)
    idx_ref[1:2, :] = i2.astype(jnp.int32)
    gate_ref[0:1, :] = s1 / den
    gate_ref[1:2, :] = s2 / den
    rank_ref[0:1, :] = r1.astype(jnp.int32)
    rank_ref[1:2, :] = r2.astype(jnp.int32)
    cnt_ref[...] = run_ref[...].astype(jnp.int32)


def _router(x, router_w, router_b):
    T, D = x.shape
    pair = pl.BlockSpec((TOP_K, ROUTER_ROWS), lambda i: (0, i))
    return pl.pallas_call(
        _router_kernel,
        grid=(T // ROUTER_ROWS,),
        in_specs=[pl.BlockSpec((ROUTER_ROWS, D), lambda i: (i, 0)),
                  _const_spec((N_EXPERTS, D)), _const_spec((N_EXPERTS, 1))],
        out_specs=[pair, pair, pair, pl.BlockSpec((N_EXPERTS, 128), lambda i: (0, 0))],
        out_shape=[jax.ShapeDtypeStruct((TOP_K, T), jnp.int32),
                   jax.ShapeDtypeStruct((TOP_K, T), F32),
                   jax.ShapeDtypeStruct((TOP_K, T), jnp.int32),
                   jax.ShapeDtypeStruct((N_EXPERTS, 128), jnp.int32)],
        scratch_shapes=[pltpu.VMEM((N_EXPERTS, 128), F32)],
        compiler_params=_params(("arbitrary",), 40),
        name="router",
    )(x, router_w.T, router_b.reshape(N_EXPERTS, 1))


def _dispatch_kernel(pad_from_ref, pad_n_ref, n_used_ref, dest_ref, x_ref, xs_hbm, zero_ref, sem, pad_sem):
    i = pl.program_id(0)
    rows = DISPATCH_ROWS
    n_blocks = xs_hbm.shape[0] // EXPERT_ROWS
    zero_rows = zero_ref.shape[0]

    def tail_copies(j):
        block = n_used_ref[0] + j
        off = pl.multiple_of(block * EXPERT_ROWS, EXPERT_ROWS)
        return block < n_blocks, [
            pltpu.make_async_copy(zero_ref, xs_hbm.at[pl.ds(off + part * zero_rows, zero_rows)], pad_sem)
            for part in range(EXPERT_ROWS // zero_rows)]

    def pad_copy(e, bit):
        n = pad_n_ref[e]
        if bit < ROW_TILE:
            off = pad_from_ref[e] + bit - 1
            return pltpu.make_async_copy(zero_ref.at[pl.ds(0, 1)], xs_hbm.at[pl.ds(off, 1)], pad_sem)
        done = (n & (ROW_TILE - 1)) + (n & ~(2 * bit - 1))
        off = pl.multiple_of(pad_from_ref[e] + done, ROW_TILE)
        return pltpu.make_async_copy(zero_ref.at[pl.ds(0, bit)], xs_hbm.at[pl.ds(off, bit)], pad_sem)

    def pad_needed(e, bit):
        n = pad_n_ref[e]
        if bit < ROW_TILE:
            return bit <= (n & (ROW_TILE - 1))
        return (n & bit) != 0

    bits = list(range(1, ROW_TILE)) + [
        1 << k for k in range(ROW_TILE.bit_length() - 1, EXPERT_ROWS.bit_length() - 1)]

    @pl.when(i == 0)
    def _():
        zero_ref[...] = jnp.zeros_like(zero_ref)
        for e in range(N_EXPERTS):
            for bit in bits:
                @pl.when(pad_needed(e, bit))
                def _():
                    pad_copy(e, bit).start()
        for j in range(N_EXPERTS):
            needed, copies = tail_copies(j)

            @pl.when(needed)
            def _():
                for c in copies:
                    c.start()

    for t in range(rows):
        src = x_ref.at[pl.ds(t, 1)]
        for k in range(TOP_K):
            pltpu.make_async_copy(src, xs_hbm.at[pl.ds(dest_ref[0, k * rows + t], 1)],
                                  sem).start(priority=k % N_DMA_PRIORITIES)
    for _ in range(TOP_K):
        pltpu.make_async_copy(x_ref, xs_hbm.at[pl.ds(0, rows)], sem).wait()

    @pl.when(i == 0)
    def _():
        for e in range(N_EXPERTS):
            for bit in bits:
                @pl.when(pad_needed(e, bit))
                def _():
                    pad_copy(e, bit).wait()
        for j in range(N_EXPERTS):
            needed, copies = tail_copies(j)

            @pl.when(needed)
            def _():
                for c in copies:
                    c.wait()


def _dispatch(x, dest_tiles, pad_from, pad_n, n_used, n_slots):
    T, D = x.shape
    grid_spec = pltpu.PrefetchScalarGridSpec(
        num_scalar_prefetch=3,
        grid=(T // DISPATCH_ROWS,),
        in_specs=[pl.BlockSpec((None, 1, TOP_K * DISPATCH_ROWS), lambda i, *_: (i, 0, 0),
                               memory_space=pltpu.SMEM),
                  pl.BlockSpec((DISPATCH_ROWS, D), lambda i, *_: (i, 0))],
        out_specs=pl.BlockSpec(memory_space=pl.ANY),
        scratch_shapes=[pltpu.VMEM((EXPERT_ROWS // 2, D), x.dtype),
                        pltpu.SemaphoreType.DMA(()), pltpu.SemaphoreType.DMA(())],
    )
    return pl.pallas_call(
        _dispatch_kernel,
        grid_spec=grid_spec,
        out_shape=jax.ShapeDtypeStruct((n_slots, D), x.dtype),
        compiler_params=_params(("arbitrary",), 16),
        name="dispatch",
    )(pad_from, pad_n, n_used, dest_tiles, x)


def _expert_kernel(block_e_ref, n_used_ref, xs_ref, w1_ref, w3_ref, w2_ref, *rest):
    n_casts = len(rest) // 2
    o_ref = rest[n_casts]
    used = pl.program_id(0) < n_used_ref[0]

    @pl.when(jnp.logical_not(used))
    def _():
        o_ref[...] = jnp.zeros_like(o_ref)

    @pl.when(used)
    def _():
        lo, hi = _unpack_halves(xs_ref[...])
        x = jnp.concatenate([lo.astype(BF16), hi.astype(BF16)], axis=1)
        h1 = jnp.dot(x, w1_ref[...], preferred_element_type=F32)
        h3 = jnp.dot(x, w3_ref[...], preferred_element_type=F32)
        hidden = (h1 * _sigmoid(h1)) * h3
        out = jnp.dot(hidden.astype(BF16), w2_ref[...], preferred_element_type=F32)
        o_ref[...] = _pack_halves(out)
        for src_ref, dst_ref in zip(rest[:n_casts], rest[n_casts + 1:]):
            dst_ref[...] = src_ref[...].astype(BF16)


def _experts(xs, block_e, n_used, w1, w3, w2, casts=()):
    P, packed = xs.shape
    D = 2 * packed
    n_blocks = P // EXPERT_ROWS
    used_rows = pl.BlockSpec((EXPERT_ROWS, packed),
                             lambda i, be, nu: (jnp.maximum(jnp.minimum(i, nu[0] - 1), 0), 0))
    rows = pl.BlockSpec((EXPERT_ROWS, packed), lambda i, be, nu: (i, 0))
    expert_slab = lambda i, be, nu: (be[i], 0, 0)
    cast_specs = [_cast_specs(stack, layer, n_blocks - N_EXPERTS, lambda i, be, nu: i)
                  for stack, layer in casts]
    grid_spec = pltpu.PrefetchScalarGridSpec(
        num_scalar_prefetch=2,
        grid=(n_blocks,),
        in_specs=[used_rows,
                  pl.BlockSpec((None, D, D_EXPERT), expert_slab),
                  pl.BlockSpec((None, D, D_EXPERT), expert_slab),
                  pl.BlockSpec((None, D_EXPERT, D), expert_slab)] + [s[0] for s in cast_specs],
        out_specs=[rows] + [s[1] for s in cast_specs],
    )
    out = pl.pallas_call(
        _expert_kernel,
        grid_spec=grid_spec,
        out_shape=[jax.ShapeDtypeStruct((P, packed), jnp.uint32)] + [s[2] for s in cast_specs],
        compiler_params=_params(("arbitrary",), 56),
        name="experts",
    )(block_e, n_used, xs, w1, w3, w2, *[stack for stack, _ in casts])
    return out[0], out[1:]


def _combine_kernel(alpha, project, dest_ref, dest_next_ref, x_ref, gate_ref, g_ref, b_ref, y_hbm,
                    *rest):
    if project:
        w_ref, of_ref, ob_ref, oxr_ref, ogate_ref, buf_a, buf_b, sem = rest
    else:
        of_ref, ob_ref, buf_a, buf_b, sem = rest
    rows = COMBINE_ROWS
    half = rows // 2
    i = pl.program_id(0)

    def copy(table_ref, h, t, k, buf, s):
        return pltpu.make_async_copy(y_hbm.at[pl.ds(table_ref[0, k * rows + h * half + t], 1)],
                                     buf.at[k, pl.ds(t, 1)], sem.at[s])

    def gather(table_ref, h, buf, s):
        for t in range(half):
            for k in range(TOP_K):
                copy(table_ref, h, t, k, buf, s).start(priority=k % N_DMA_PRIORITIES)

    def wait(buf, s):
        for k in range(TOP_K):
            pltpu.make_async_copy(y_hbm.at[pl.ds(0, half)], buf.at[k], sem.at[s]).wait()

    def reduce(buf, h):
        r = slice(h * half, (h + 1) * half)
        lo0, hi0 = _unpack_halves(buf[0])
        lo1, hi1 = _unpack_halves(buf[1])
        g0 = gate_ref[r, 0:1]
        g1 = gate_ref[r, 1:2]
        f = jnp.concatenate([g0 * lo0 + g1 * lo1, g0 * hi0 + g1 * hi1], axis=1)
        y = _layer_norm(alpha * x_ref[r, :] + f, g_ref[...], b_ref[...])
        of_ref[r, :] = y
        yb = y.astype(BF16)
        ob_ref[r, :] = yb
        if project:
            _project_branch_a(yb, w_ref, oxr_ref, ogate_ref, rows=r)

    @pl.when(i == 0)
    def _():
        def issue(t, carry):
            for k in range(TOP_K):
                copy(dest_ref, 0, t, k, buf_a, 0).start()
            return carry

        lax.fori_loop(0, half, issue, 0, unroll=8)

    wait(buf_a, 0)
    gather(dest_ref, 1, buf_b, 1)
    reduce(buf_a, 0)
    wait(buf_b, 1)
    gather(dest_next_ref, 0, buf_a, 0)
    reduce(buf_b, 1)

    @pl.when(i == pl.num_programs(0) - 1)
    def _():
        wait(buf_a, 0)


def _combine(alpha, dest_tiles, x, gate_t, ln_g, ln_b, yb, next_w_in=None):
    T, D = x.shape
    n_tiles = T // COMBINE_ROWS
    row = lambda w: pl.BlockSpec((COMBINE_ROWS, w), lambda i: (i, 0))
    table = lambda index: pl.BlockSpec((None, 1, TOP_K * COMBINE_ROWS), index, memory_space=pltpu.SMEM)
    half_buf = pltpu.VMEM((TOP_K, COMBINE_ROWS // 2, yb.shape[1]), yb.dtype)
    project = next_w_in is not None
    in_specs = [table(lambda i: (i, 0, 0)),
                table(lambda i: (jnp.minimum(i + 1, n_tiles - 1), 0, 0)),
                row(D), pl.BlockSpec((COMBINE_ROWS, TOP_K), lambda i: (i, 0)),
                _const_spec((1, D)), _const_spec((1, D)),
                pl.BlockSpec(memory_space=pl.ANY)]
    out_specs = [row(D), row(D)]
    out_shape = [jax.ShapeDtypeStruct((T, D), F32), jax.ShapeDtypeStruct((T, D), BF16)]
    operands = [dest_tiles, dest_tiles, x, gate_t, ln_g.reshape(1, D), ln_b.reshape(1, D), yb]
    if project:
        in_specs.append(_const_spec((D, 2 * D_RNN)))
        out_specs += [row(D_RNN), row(D_RNN)]
        out_shape += [jax.ShapeDtypeStruct((T, D_RNN), BF16)] * 2
        operands.append(next_w_in)
    return pl.pallas_call(
        functools.partial(_combine_kernel, alpha, project),
        grid=(n_tiles,),
        in_specs=in_specs,
        out_specs=out_specs,
        out_shape=out_shape,
        scratch_shapes=[half_buf, half_buf, pltpu.SemaphoreType.DMA((2,))],
        compiler_params=_params(("arbitrary",), 52),
        name="combine",
    )(*operands)


def _tile_pairs(dest, rows):
    T = dest.shape[1]
    return dest.reshape(TOP_K, T // rows, rows).transpose(1, 0, 2).reshape(T // rows, 1, TOP_K * rows)


def _moe(alpha, x, x_packed, router_w, router_b, w1, w3, w2, ln_g, ln_b, casts=()):
    T, D = x.shape
    n_blocks = (T * TOP_K) // EXPERT_ROWS + N_EXPERTS
    idx, gate, rank, cnt = _router(x, router_w, router_b)

    counts = cnt[:, 0]
    padded = (counts + EXPERT_ROWS - 1) // EXPERT_ROWS * EXPERT_ROWS
    pad_end = jnp.cumsum(padded)
    pad_start = pad_end - padded
    n_used = (pad_end[-1] // EXPERT_ROWS).astype(jnp.int32)
    block_start = jnp.minimum(jnp.arange(n_blocks, dtype=jnp.int32), n_used - 1) * EXPERT_ROWS
    block_e = jnp.minimum(jnp.sum(block_start[:, None] >= pad_end[None, :], axis=1),
                          N_EXPERTS - 1).astype(jnp.int32)
    expert_ids = jnp.arange(N_EXPERTS, dtype=jnp.int32)[:, None, None]
    dest = jnp.sum(jnp.where(idx[None] == expert_ids, pad_start[:, None, None], 0), axis=0) + rank

    n_used = n_used.reshape(1)
    xs = _dispatch(x_packed, _tile_pairs(dest, DISPATCH_ROWS), (pad_start + counts).astype(jnp.int32),
                   (padded - counts).astype(jnp.int32), n_used, n_blocks * EXPERT_ROWS)
    yb, casted = _experts(xs, block_e, n_used, w1, w3, w2, casts)
    new = _combine(alpha, _tile_pairs(dest, COMBINE_ROWS), x, gate.T, ln_g, ln_b, yb,
                   next_w_in=casted[0] if casted else None)
    return new, casted


def kernel(x, emb_ln_g, emb_ln_b, w_in, conv_w, conv_b, lru_w_a, lru_b_a, lru_w_x, lru_b_x, lru_lambda, w_o_rnn, sgu_ln_g, sgu_ln_b, sgu_w_s, sgu_b_s, w_o_sgu, w_out, ln1_g, ln1_b, router_w, router_b, expert_w1, expert_w3, expert_w2, ln2_g, ln2_b):
    batch, seq, D = x.shape
    depth = w_in.shape[0]
    alpha = float((2 * depth) ** 0.25)
    xf = x.reshape(batch * seq, D)
    mixer_f32 = (w_in, w_o_rnn, w_o_sgu, w_out)
    w_in, w_o_rnn, w_o_sgu, w_out = (w[0:1].astype(BF16) for w in mixer_f32)
    xb, x_rnn, gate = _emb_ln(xf, emb_ln_g, emb_ln_b, w_in)
    rows_of = lambda w: w.reshape(depth, N_EXPERTS * w.shape[2], w.shape[3])
    w1_rows, w3_rows, w2_rows = rows_of(expert_w1), rows_of(expert_w3), rows_of(expert_w2)
    uv_from = 2 * D_RNN
    sigmoid_from = 2 * D_RNN + 2 * D_SGU
    for l in range(depth):
        uv_cols, w3 = _proj(xb, w_in, 0, uv_from, sigmoid_from - uv_from, _gelu, "proj_gelu",
                            cast=(w3_rows, l))
        sigmoid_cols, w1, a = _proj_lru(
            xb, w_in, 0, sigmoid_from, D_IN - sigmoid_from, _sigmoid, "proj_sigmoid_lru",
            (w1_rows, l), x_rnn, gate, seq, conv_w[l], conv_b[l], lru_w_a[l], lru_b_a[l],
            lru_w_x[l], lru_b_x[l], lru_lambda[l])
        x1, x1_packed, w2 = _mix_out(alpha, 0, a, uv_cols, sigmoid_cols, xf, l == 0, emb_ln_g,
                                     emb_ln_b, sgu_ln_g[l], sgu_ln_b[l], sgu_w_s[l], sgu_b_s[l],
                                     w_o_rnn, w_o_sgu, w_out, ln1_g[l], ln1_b[l], cast=(w2_rows, l))
        next_mixer = [(w, l + 1) for w in mixer_f32] if l + 1 < depth else []
        (xf, xb, *branch_a), casted = _moe(
            alpha, x1, x1_packed, router_w, router_b, w1.reshape(expert_w1.shape[1:]),
            w3.reshape(expert_w3.shape[1:]), w2.reshape(expert_w2.shape[1:]), ln2_g[l], ln2_b[l],
            casts=next_mixer)
        if casted:
            x_rnn, gate = branch_a
            w_in, w_o_rnn, w_o_sgu, w_out = (w[None] for w in casted)
    return xf.reshape(batch, seq, D)
```

```python
import functools

import jax
import jax.numpy as jnp
import numpy as np
from jax import lax
from jax.experimental import pallas as pl
from jax.experimental.pallas import tpu as pltpu

F32 = jnp.float32
BF16 = jnp.bfloat16

D_MODEL = 2048
D_RNN = 1024
RNN_HEADS = 8
HEAD_DIM = D_RNN // RNN_HEADS
CONV_WIDTH = 4
LRU_C = 8.0
D_SGU = 1024
SGU_GROUPS = 8
SGU_GROUP_DIM = D_SGU // SGU_GROUPS
CHUNK = 128
N_EXPERTS = 16
N_GROUPS = 4
EXPERTS_PER_GROUP = N_EXPERTS // N_GROUPS
TOP_K = 2
D_EXPERT = 1408
D_IN = 2 * D_RNN + 2 * D_SGU + 2 * D_MODEL
LN_EPS = 1e-5
SQRT_2_OVER_PI = float(np.sqrt(2.0 / np.pi))

LN_ROWS = 512
PROJ_ROWS = 1024
PROJ_COLS = 1024
MIX_ROWS = 256
ROUTER_ROWS = 1024
DISPATCH_ROWS = 512
EXPERT_ROWS = 256
COMBINE_ROWS = 512
CONV_PAD = 8
N_DMA_PRIORITIES = 2
ROW_TILE = 8

MIB = 1024 * 1024


def _params(semantics, vmem_mib):
    return pltpu.CompilerParams(dimension_semantics=semantics,
                                vmem_limit_bytes=vmem_mib * MIB,
                                disable_bounds_checks=True)


def _const_spec(shape):
    zeros = (0,) * len(shape)
    return pl.BlockSpec(shape, lambda *_: zeros, pipeline_mode=pl.Buffered(1))


def _layer_spec(layer, shape):
    zeros = (0,) * len(shape)
    return pl.BlockSpec((None,) + tuple(shape), lambda *_: (layer,) + zeros,
                        pipeline_mode=pl.Buffered(1))


def _layer_norm(v, g, b):
    mu = jnp.mean(v, axis=-1, keepdims=True)
    d = v - mu
    var = jnp.mean(d * d, axis=-1, keepdims=True)
    return d * lax.rsqrt(var + LN_EPS) * g + b


def _gelu(x):
    inner = x * (SQRT_2_OVER_PI + (SQRT_2_OVER_PI * 0.044715) * (x * x))
    return x * (0.5 * jnp.tanh(inner) + 0.5)


def _pack_halves(y):
    n = y.shape[1] // 2
    bits = lambda v: lax.bitcast_convert_type(v.astype(BF16).astype(F32), jnp.uint32)
    return (bits(y[:, :n]) >> 16) | bits(y[:, n:])


def _unpack_halves(p):
    lo = lax.bitcast_convert_type(p << 16, F32)
    hi = lax.bitcast_convert_type(p & jnp.uint32(0xFFFF0000), F32)
    return lo, hi


def _sigmoid(x):
    return 0.5 * jnp.tanh(0.5 * x) + 0.5


def _project_branch_a(xb, w_ref, oxr_ref, ogate_ref, rows=slice(None)):
    acc = jnp.dot(xb, w_ref[...], preferred_element_type=F32)
    oxr_ref[rows, :] = acc[:, :D_RNN].astype(BF16)
    ogate_ref[rows, :] = _gelu(acc[:, D_RNN:]).astype(BF16)


def _emb_ln_kernel(x_ref, g_ref, b_ref, w_ref, ob_ref, oxr_ref, ogate_ref):
    xb = _layer_norm(x_ref[...], g_ref[...], b_ref[...]).astype(BF16)
    ob_ref[...] = xb
    _project_branch_a(xb, w_ref, oxr_ref, ogate_ref)


def _emb_ln(x, g, b, w_in):
    T, D = x.shape
    row = lambda w: pl.BlockSpec((LN_ROWS, w), lambda i: (i, 0))
    half = jax.ShapeDtypeStruct((T, D_RNN), BF16)
    return pl.pallas_call(
        _emb_ln_kernel,
        grid=(T // LN_ROWS,),
        in_specs=[row(D), _const_spec((1, D)), _const_spec((1, D)), _layer_spec(0, (D, 2 * D_RNN))],
        out_specs=[row(D), row(D_RNN), row(D_RNN)],
        out_shape=[jax.ShapeDtypeStruct((T, D), BF16), half, half],
        compiler_params=_params(("parallel",), 40),
        name="emb_ln",
    )(x, g.reshape(1, D), b.reshape(1, D), w_in)


def _cast_specs(stack, layer, n_steps, step_of):
    _, rows, cols = stack.shape
    slab = -(-rows // (n_steps * 16)) * 16
    last = -(-rows // slab) - 1
    src = pl.BlockSpec((None, slab, cols), lambda *g: (layer, jnp.minimum(step_of(*g), last), 0))
    dst = pl.BlockSpec((slab, cols), lambda *g: (jnp.minimum(step_of(*g), last), 0))
    return src, dst, jax.ShapeDtypeStruct((rows, cols), BF16)


def _proj_kernel(act, x_ref, w_ref, cast_src_ref, o_ref, cast_dst_ref):
    acc = jnp.dot(x_ref[...], w_ref[...], preferred_element_type=F32)
    o_ref[...] = act(acc).astype(BF16)
    cast_dst_ref[...] = cast_src_ref[...].astype(BF16)


def _proj(xb, w_in, layer, col_from, n_cols, act, name, cast):
    T, D = xb.shape
    first = col_from // PROJ_COLS
    grid = (T // PROJ_ROWS, n_cols // PROJ_COLS)
    src, dst, cast_shape = _cast_specs(*cast, grid[0] * grid[1], lambda i, j: i * grid[1] + j)
    return pl.pallas_call(
        functools.partial(_proj_kernel, act),
        grid=grid,
        in_specs=[pl.BlockSpec((PROJ_ROWS, D), lambda i, j: (i, 0)),
                  pl.BlockSpec((None, D, PROJ_COLS), lambda i, j: (layer, 0, first + j)), src],
        out_specs=[pl.BlockSpec((PROJ_ROWS, PROJ_COLS), lambda i, j: (i, j)), dst],
        out_shape=[jax.ShapeDtypeStruct((T, n_cols), BF16), cast_shape],
        compiler_params=_params(("parallel", "arbitrary"), 48),
        name=name,
    )(xb, w_in, cast[0])


def _lru_gates(first, xr_ref, cw_ref, cb_ref, wa_ref, ba_ref, wx_ref, bx_ref, lam_ref,
               xpad_ref, hc_ref, a_ref, u_ref):
    rows = xr_ref.shape[0]

    @pl.when(first)
    def _():
        xpad_ref[0:CONV_PAD, :] = jnp.zeros((CONV_PAD, D_RNN), F32)
        hc_ref[...] = jnp.zeros_like(hc_ref)

    x = xr_ref[...].astype(F32)
    xpad_ref[CONV_PAD:CONV_PAD + rows, :] = x
    xc = cb_ref[...] + cw_ref[CONV_WIDTH - 1:CONV_WIDTH, :] * x
    for k in range(CONV_WIDTH - 1):
        shift = CONV_WIDTH - 1 - k
        xc = xc + cw_ref[k:k + 1, :] * xpad_ref[CONV_PAD - shift:CONV_PAD - shift + rows, :]
    xpad_ref[0:CONV_PAD, :] = x[rows - CONV_PAD:rows, :]

    lam = lam_ref[...]
    sp = jnp.maximum(-lam, 0.0) + jnp.log1p(jnp.exp(-jnp.abs(lam)))
    decay = (-0.5 * LRU_C) * sp
    for h in range(RNN_HEADS):
        cols = slice(h * HEAD_DIM, (h + 1) * HEAD_DIM)
        xh = xc[:, cols]
        xhb = xh.astype(BF16)
        tanh_r = jnp.tanh(jnp.dot(xhb, wa_ref[h], preferred_element_type=F32) + ba_ref[:, cols])
        tanh_i = jnp.tanh(jnp.dot(xhb, wx_ref[h], preferred_element_type=F32) + bx_ref[:, cols])
        gi = 0.5 * tanh_i + 0.5
        log_a = decay[:, cols] * tanh_r + decay[:, cols]
        a_ref[:, cols] = jnp.exp(log_a)
        th = jnp.tanh(log_a)
        u_ref[:, cols] = jnp.sqrt(-2.0 * th / (1.0 - th)) * (gi * xh)


def _lru_scan(gate_ref, o_ref, hc_ref, a_ref, u_ref, h_ref):
    def step(t, h):
        h = a_ref[pl.ds(t, 1), :] * h + u_ref[pl.ds(t, 1), :]
        h_ref[pl.ds(t, 1), :] = h
        return h

    hc_ref[0:1, :] = lax.fori_loop(0, a_ref.shape[0], step, hc_ref[0:1, :], unroll=8)
    o_ref[...] = (h_ref[...] * gate_ref[...].astype(F32)).astype(BF16)


def _proj_lru_kernel(act, tiles_per_seq, x_ref, w_ref, cast_src_ref, xr_ref, gate_ref, cw_ref, cb_ref,
                     wa_ref, ba_ref, wx_ref, bx_ref, lam_ref, o_ref, cast_dst_ref, a_out_ref,
                     xpad_ref, hc_ref, a_ref, u_ref, h_ref):
    step = pl.program_id(0) * pl.num_programs(1) + pl.program_id(1)
    _lru_gates(step % tiles_per_seq == 0, xr_ref, cw_ref, cb_ref, wa_ref, ba_ref, wx_ref, bx_ref,
               lam_ref, xpad_ref, hc_ref, a_ref, u_ref)
    acc = jnp.dot(x_ref[...], w_ref[...], preferred_element_type=F32)
    o_ref[...] = act(acc).astype(BF16)
    cast_dst_ref[...] = cast_src_ref[...].astype(BF16)
    _lru_scan(gate_ref, a_out_ref, hc_ref, a_ref, u_ref, h_ref)


def _proj_lru(xb, w_in, layer, col_from, n_cols, act, name, cast, x_rnn, gate, seq,
              conv_w, conv_b, w_a, b_a, w_x, b_x, lam):
    T, D = xb.shape
    first = col_from // PROJ_COLS
    grid = (T // PROJ_ROWS, n_cols // PROJ_COLS)
    n_steps = grid[0] * grid[1]
    step_of = lambda i, j: i * grid[1] + j
    lru_rows = T // n_steps
    src, dst, cast_shape = _cast_specs(*cast, n_steps, step_of)
    tile = pl.BlockSpec((lru_rows, D_RNN), lambda i, j: (step_of(i, j), 0))
    vec = _const_spec((1, D_RNN))
    gate_w = _const_spec((RNN_HEADS, HEAD_DIM, HEAD_DIM))
    tile_f32 = pltpu.VMEM((lru_rows, D_RNN), F32)
    return pl.pallas_call(
        functools.partial(_proj_lru_kernel, act, seq // lru_rows),
        grid=grid,
        in_specs=[pl.BlockSpec((PROJ_ROWS, D), lambda i, j: (i, 0)),
                  pl.BlockSpec((None, D, PROJ_COLS), lambda i, j: (layer, 0, first + j)),
                  src, tile, tile, _const_spec((CONV_WIDTH, D_RNN)), vec, gate_w, vec, gate_w, vec, vec],
        out_specs=[pl.BlockSpec((PROJ_ROWS, PROJ_COLS), lambda i, j: (i, j)), dst, tile],
        out_shape=[jax.ShapeDtypeStruct((T, n_cols), BF16), cast_shape,
                   jax.ShapeDtypeStruct((T, D_RNN), BF16)],
        scratch_shapes=[pltpu.VMEM((CONV_PAD + lru_rows, D_RNN), F32), pltpu.VMEM((8, D_RNN), F32),
                        tile_f32, tile_f32, tile_f32],
        compiler_params=_params(("arbitrary", "arbitrary"), 52),
        name=name,
    )(xb, w_in, cast[0], x_rnn, gate, conv_w, conv_b.reshape(1, D_RNN), (0.5 * w_a).astype(BF16),
      0.5 * b_a.reshape(1, D_RNN), (0.5 * w_x).astype(BF16), 0.5 * b_x.reshape(1, D_RNN),
      lam.reshape(1, D_RNN))


def _spatial_gating(u_ref, v_ref, g_ref, b_ref, ws_ref, bs_ref, o_ref):
    rows = u_ref.shape[0]
    v = _layer_norm(v_ref[...].astype(F32), g_ref[...], b_ref[...]).astype(BF16)
    t_out = lax.broadcasted_iota(jnp.int32, (CHUNK, CHUNK), 0)
    t_in = lax.broadcasted_iota(jnp.int32, (CHUNK, CHUNK), 1)
    causal = t_in <= t_out
    for g in range(SGU_GROUPS):
        cols = slice(g * SGU_GROUP_DIM, (g + 1) * SGU_GROUP_DIM)
        ws = jnp.where(causal, ws_ref[g], 0.0).astype(BF16)
        bias = bs_ref[:, g:g + 1]
        for c in range(rows // CHUNK):
            rws = slice(c * CHUNK, (c + 1) * CHUNK)
            mixed = jnp.dot(ws, v[rws, cols], preferred_element_type=F32) + bias
            o_ref[rws, cols] = (u_ref[rws, cols].astype(F32) * mixed).astype(BF16)


def _mix_out_kernel(alpha, x_is_raw, a_ref, u_ref, v_ref, sga_ref, sgb_ref, x_ref, emb_g_ref,
                    emb_b_ref, sgu_g_ref, sgu_b_ref, ws_ref, bs_ref, woa_ref, wob_ref, wout_ref,
                    g_ref, beta_ref, cast_src_ref, o_ref, packed_ref, cast_dst_ref, b_ref):
    _spatial_gating(u_ref, v_ref, sgu_g_ref, sgu_b_ref, ws_ref, bs_ref, b_ref)
    ya = jnp.dot(a_ref[...], woa_ref[...], preferred_element_type=F32)
    yb = jnp.dot(b_ref[...], wob_ref[...], preferred_element_type=F32)
    merged = sga_ref[...].astype(F32) * ya + sgb_ref[...].astype(F32) * yb
    m = jnp.dot(merged.astype(BF16), wout_ref[...], preferred_element_type=F32)
    x = x_ref[...]
    if x_is_raw:
        x = _layer_norm(x, emb_g_ref[...], emb_b_ref[...])
    y = _layer_norm(alpha * x + m, g_ref[...], beta_ref[...])
    o_ref[...] = y
    packed_ref[...] = _pack_halves(y)
    cast_dst_ref[...] = cast_src_ref[...].astype(BF16)


def _mix_out(alpha, layer, a, uv_cols, sigmoid_cols, x, x_is_raw, emb_ln_g, emb_ln_b,
             sgu_ln_g, sgu_ln_b, w_s, b_s, w_o_rnn, w_o_sgu, w_out, ln_g, ln_b, cast):
    T, D = x.shape
    n_steps = T // MIX_ROWS
    row = lambda w, c: pl.BlockSpec((MIX_ROWS, w), lambda i: (i, c))
    cast_src, cast_dst, cast_shape = _cast_specs(*cast, n_steps, lambda i: i)
    return pl.pallas_call(
        functools.partial(_mix_out_kernel, alpha, x_is_raw),
        grid=(n_steps,),
        in_specs=[row(D_RNN, 0), row(D_SGU, 0), row(D_SGU, 1), row(D, 0), row(D, 1), row(D, 0),
                  _const_spec((1, D)), _const_spec((1, D)),
                  _const_spec((1, D_SGU)), _const_spec((1, D_SGU)),
                  _const_spec((SGU_GROUPS, CHUNK, CHUNK)), _const_spec((CHUNK, SGU_GROUPS)),
                  _layer_spec(layer, (D_RNN, D)), _layer_spec(layer, (D_SGU, D)),
                  _layer_spec(layer, (D, D)), _const_spec((1, D)), _const_spec((1, D)), cast_src],
        out_specs=[row(D, 0), row(D // 2, 0), cast_dst],
        out_shape=[jax.ShapeDtypeStruct((T, D), F32), jax.ShapeDtypeStruct((T, D // 2), jnp.uint32),
                   cast_shape],
        scratch_shapes=[pltpu.VMEM((MIX_ROWS, D_SGU), BF16)],
        compiler_params=_params(("parallel",), 56),
        name="mix_out",
    )(a, uv_cols, uv_cols, sigmoid_cols, sigmoid_cols, x,
      emb_ln_g.reshape(1, D), emb_ln_b.reshape(1, D),
      sgu_ln_g.reshape(1, D_SGU), sgu_ln_b.reshape(1, D_SGU), w_s, b_s.T,
      w_o_rnn, w_o_sgu, w_out, ln_g.reshape(1, D), ln_b.reshape(1, D), cast[0])


def _first_max(v, row, n):
    m = jnp.max(v, axis=0, keepdims=True)
    idx = jnp.min(jnp.where(v == m, row, float(n)), axis=0, keepdims=True)
    return m, idx


def _router_kernel(x_ref, wt_ref, b_ref, idx_ref, gate_ref, rank_ref, cnt_ref, run_ref):
    rows = x_ref.shape[0]

    @pl.when(pl.program_id(0) == 0)
    def _():
        run_ref[...] = jnp.zeros_like(run_ref)

    def nt_dot(w, x):
        return lax.dot_general(w, x, (((1,), (1,)), ((), ())), preferred_element_type=F32)

    x = x_ref[...]
    xh = x.astype(BF16)
    xl = (x - xh.astype(F32)).astype(BF16)
    w = wt_ref[...]
    wh = w.astype(BF16)
    wl = (w - wh.astype(F32)).astype(BF16)
    logits = nt_dot(wh, xh) + (nt_dot(wh, xl) + nt_dot(wl, xh))
    e = jnp.exp(logits - jnp.max(logits, axis=0, keepdims=True))
    scores = e / jnp.sum(e, axis=0, keepdims=True)
    sel = scores + b_ref[...]

    row = lax.broadcasted_iota(jnp.int32, (N_EXPERTS, rows), 0).astype(F32)
    grow = lax.broadcasted_iota(jnp.int32, (EXPERTS_PER_GROUP, rows), 0).astype(F32)
    neg_inf = float("-inf")
    best_score = None
    best_group = None
    for g in range(N_GROUPS):
        v = sel[g * EXPERTS_PER_GROUP:(g + 1) * EXPERTS_PER_GROUP, :]
        m1, i1 = _first_max(v, grow, EXPERTS_PER_GROUP)
        m2 = jnp.max(jnp.where(grow == i1, neg_inf, v), axis=0, keepdims=True)
        s = m1 + m2
        if g == 0:
            best_score, best_group = s, jnp.zeros_like(s)
        else:
            better = s > best_score
            best_group = jnp.where(better, float(g), best_group)
            best_score = jnp.where(better, s, best_score)

    lo = best_group * float(EXPERTS_PER_GROUP)
    in_group = jnp.logical_and(row >= lo, row < lo + float(EXPERTS_PER_GROUP))
    masked = jnp.where(in_group, sel, neg_inf)
    _, i1 = _first_max(masked, row, N_EXPERTS)
    pick1 = row == i1
    _, i2 = _first_max(jnp.where(pick1, neg_inf, masked), row, N_EXPERTS)
    pick2 = row == i2
    s1 = jnp.sum(jnp.where(pick1, scores, 0.0), axis=0, keepdims=True)
    s2 = jnp.sum(jnp.where(pick2, scores, 0.0), axis=0, keepdims=True)
    den = s1 + s2

    onehot = jnp.where(jnp.logical_or(pick1, pick2), 1.0, 0.0)
    before = (lax.broadcasted_iota(jnp.int32, (rows, rows), 0)
              < lax.broadcasted_iota(jnp.int32, (rows, rows), 1))
    prefix = jnp.dot(onehot.astype(BF16), jnp.where(before, 1.0, 0.0).astype(BF16),
                     preferred_element_type=F32)
    pos = prefix + run_ref[:, 0:1]
    r1 = jnp.sum(jnp.where(pick1, pos, 0.0), axis=0, keepdims=True)
    r2 = jnp.sum(jnp.where(pick2, pos, 0.0), axis=0, keepdims=True)
    run_ref[...] = run_ref[...] + jnp.sum(onehot, axis=1, keepdims=True)

    idx_ref[0:1, :] = i1.astype(jnp.int32)
    idx_ref[1:2, :] = i2.astype(jnp.int32)
    gate_ref[0:1, :] = s1 / den
    gate_ref[1:2, :] = s2 / den
    rank_ref[0:1, :] = r1.astype(jnp.int32)
    rank_ref[1:2, :] = r2.astype(jnp.int32)
    cnt_ref[...] = run_ref[...].astype(jnp.int32)


def _router(x, router_w, router_b):
    T, D = x.shape
    pair = pl.BlockSpec((TOP_K, ROUTER_ROWS), lambda i: (0, i))
    return pl.pallas_call(
        _router_kernel,
        grid=(T // ROUTER_ROWS,),
        in_specs=[pl.BlockSpec((ROUTER_ROWS, D), lambda i: (i, 0)),
                  _const_spec((N_EXPERTS, D)), _const_spec((N_EXPERTS, 1))],
        out_specs=[pair, pair, pair, pl.BlockSpec((N_EXPERTS, 128), lambda i: (0, 0))],
        out_shape=[jax.ShapeDtypeStruct((TOP_K, T), jnp.int32),
                   jax.ShapeDtypeStruct((TOP_K, T), F32),
                   jax.ShapeDtypeStruct((TOP_K, T), jnp.int32),
                   jax.ShapeDtypeStruct((N_EXPERTS, 128), jnp.int32)],
        scratch_shapes=[pltpu.VMEM((N_EXPERTS, 128), F32)],
        compiler_params=_params(("arbitrary",), 40),
        name="router",
    )(x, router_w.T, router_b.reshape(N_EXPERTS, 1))


def _dispatch_kernel(pad_from_ref, pad_n_ref, n_used_ref, dest_ref, x_ref, xs_hbm, zero_ref, sem, pad_sem):
    i = pl.program_id(0)
    rows = DISPATCH_ROWS
    n_blocks = xs_hbm.shape[0] // EXPERT_ROWS
    zero_rows = zero_ref.shape[0]

    def tail_copies(j):
        block = n_used_ref[0] + j
        off = pl.multiple_of(block * EXPERT_ROWS, EXPERT_ROWS)
        return block < n_blocks, [
            pltpu.make_async_copy(zero_ref, xs_hbm.at[pl.ds(off + part * zero_rows, zero_rows)], pad_sem)
            for part in range(EXPERT_ROWS // zero_rows)]

    def pad_copy(e, bit):
        n = pad_n_ref[e]
        if bit < ROW_TILE:
            off = pad_from_ref[e] + bit - 1
            return pltpu.make_async_copy(zero_ref.at[pl.ds(0, 1)], xs_hbm.at[pl.ds(off, 1)], pad_sem)
        done = (n & (ROW_TILE - 1)) + (n & ~(2 * bit - 1))
        off = pl.multiple_of(pad_from_ref[e] + done, ROW_TILE)
        return pltpu.make_async_copy(zero_ref.at[pl.ds(0, bit)], xs_hbm.at[pl.ds(off, bit)], pad_sem)

    def pad_needed(e, bit):
        n = pad_n_ref[e]
        if bit < ROW_TILE:
            return bit <= (n & (ROW_TILE - 1))
        return (n & bit) != 0

    bits = list(range(1, ROW_TILE)) + [
        1 << k for k in range(ROW_TILE.bit_length() - 1, EXPERT_ROWS.bit_length() - 1)]

    @pl.when(i == 0)
    def _():
        zero_ref[...] = jnp.zeros_like(zero_ref)
        for e in range(N_EXPERTS):
            for bit in bits:
                @pl.when(pad_needed(e, bit))
                def _():
                    pad_copy(e, bit).start()
        for j in range(N_EXPERTS):
            needed, copies = tail_copies(j)

            @pl.when(needed)
            def _():
                for c in copies:
                    c.start()

    for t in range(rows):
        src = x_ref.at[pl.ds(t, 1)]
        for k in range(TOP_K):
            pltpu.make_async_copy(src, xs_hbm.at[pl.ds(dest_ref[0, k * rows + t], 1)],
                                  sem).start(priority=k % N_DMA_PRIORITIES)
    for _ in range(TOP_K):
        pltpu.make_async_copy(x_ref, xs_hbm.at[pl.ds(0, rows)], sem).wait()

    @pl.when(i == 0)
    def _():
        for e in range(N_EXPERTS):
            for bit in bits:
                @pl.when(pad_needed(e, bit))
                def _():
                    pad_copy(e, bit).wait()
        for j in range(N_EXPERTS):
            needed, copies = tail_copies(j)

            @pl.when(needed)
            def _():
                for c in copies:
                    c.wait()


def _dispatch(x, dest_tiles, pad_from, pad_n, n_used, n_slots):
    T, D = x.shape
    grid_spec = pltpu.PrefetchScalarGridSpec(
        num_scalar_prefetch=3,
        grid=(T // DISPATCH_ROWS,),
        in_specs=[pl.BlockSpec((None, 1, TOP_K * DISPATCH_ROWS), lambda i, *_: (i, 0, 0),
                               memory_space=pltpu.SMEM),
                  pl.BlockSpec((DISPATCH_ROWS, D), lambda i, *_: (i, 0))],
        out_specs=pl.BlockSpec(memory_space=pl.ANY),
        scratch_shapes=[pltpu.VMEM((EXPERT_ROWS // 2, D), x.dtype),
                        pltpu.SemaphoreType.DMA(()), pltpu.SemaphoreType.DMA(())],
    )
    return pl.pallas_call(
        _dispatch_kernel,
        grid_spec=grid_spec,
        out_shape=jax.ShapeDtypeStruct((n_slots, D), x.dtype),
        compiler_params=_params(("arbitrary",), 16),
        name="dispatch",
    )(pad_from, pad_n, n_used, dest_tiles, x)


def _expert_kernel(block_e_ref, n_used_ref, xs_ref, w1_ref, w3_ref, w2_ref, *rest):
    n_casts = len(rest) // 2
    o_ref = rest[n_casts]
    used = pl.program_id(0) < n_used_ref[0]

    @pl.when(jnp.logical_not(used))
    def _():
        o_ref[...] = jnp.zeros_like(o_ref)

    @pl.when(used)
    def _():
        lo, hi = _unpack_halves(xs_ref[...])
        x = jnp.concatenate([lo.astype(BF16), hi.astype(BF16)], axis=1)
        h1 = jnp.dot(x, w1_ref[...], preferred_element_type=F32)
        h3 = jnp.dot(x, w3_ref[...], preferred_element_type=F32)
        hidden = (h1 * _sigmoid(h1)) * h3
        out = jnp.dot(hidden.astype(BF16), w2_ref[...], preferred_element_type=F32)
        o_ref[...] = _pack_halves(out)
        for src_ref, dst_ref in zip(rest[:n_casts], rest[n_casts + 1:]):
            dst_ref[...] = src_ref[...].astype(BF16)


def _experts(xs, block_e, n_used, w1, w3, w2, casts=()):
    P, packed = xs.shape
    D = 2 * packed
    n_blocks = P // EXPERT_ROWS
    used_rows = pl.BlockSpec((EXPERT_ROWS, packed),
                             lambda i, be, nu: (jnp.maximum(jnp.minimum(i, nu[0] - 1), 0), 0))
    rows = pl.BlockSpec((EXPERT_ROWS, packed), lambda i, be, nu: (i, 0))
    expert_slab = lambda i, be, nu: (be[i], 0, 0)
    cast_specs = [_cast_specs(stack, layer, n_blocks - N_EXPERTS, lambda i, be, nu: i)
                  for stack, layer in casts]
    grid_spec = pltpu.PrefetchScalarGridSpec(
        num_scalar_prefetch=2,
        grid=(n_blocks,),
        in_specs=[used_rows,
                  pl.BlockSpec((None, D, D_EXPERT), expert_slab),
                  pl.BlockSpec((None, D, D_EXPERT), expert_slab),
                  pl.BlockSpec((None, D_EXPERT, D), expert_slab)] + [s[0] for s in cast_specs],
        out_specs=[rows] + [s[1] for s in cast_specs],
    )
    out = pl.pallas_call(
        _expert_kernel,
        grid_spec=grid_spec,
        out_shape=[jax.ShapeDtypeStruct((P, packed), jnp.uint32)] + [s[2] for s in cast_specs],
        compiler_params=_params(("arbitrary",), 56),
        name="experts",
    )(block_e, n_used, xs, w1, w3, w2, *[stack for stack, _ in casts])
    return out[0], out[1:]


def _combine_kernel(alpha, project, dest_ref, dest_next_ref, x_ref, gate_ref, g_ref, b_ref, y_hbm,
                    *rest):
    if project:
        w_ref, of_ref, ob_ref, oxr_ref, ogate_ref, buf_a, buf_b, sem = rest
    else:
        of_ref, ob_ref, buf_a, buf_b, sem = rest
    rows = COMBINE_ROWS
    half = rows // 2
    i = pl.program_id(0)

    def copy(table_ref, h, t, k, buf, s):
        return pltpu.make_async_copy(y_hbm.at[pl.ds(table_ref[0, k * rows + h * half + t], 1)],
                                     buf.at[k, pl.ds(t, 1)], sem.at[s])

    def gather(table_ref, h, buf, s):
        for t in range(half):
            for k in range(TOP_K):
                copy(table_ref, h, t, k, buf, s).start(priority=k % N_DMA_PRIORITIES)

    def wait(buf, s):
        for k in range(TOP_K):
            pltpu.make_async_copy(y_hbm.at[pl.ds(0, half)], buf.at[k], sem.at[s]).wait()

    def reduce(buf, h):
        r = slice(h * half, (h + 1) * half)
        lo0, hi0 = _unpack_halves(buf[0])
        lo1, hi1 = _unpack_halves(buf[1])
        g0 = gate_ref[r, 0:1]
        g1 = gate_ref[r, 1:2]
        f = jnp.concatenate([g0 * lo0 + g1 * lo1, g0 * hi0 + g1 * hi1], axis=1)
        y = _layer_norm(alpha * x_ref[r, :] + f, g_ref[...], b_ref[...])
        of_ref[r, :] = y
        yb = y.astype(BF16)
        ob_ref[r, :] = yb
        if project:
            _project_branch_a(yb, w_ref, oxr_ref, ogate_ref, rows=r)

    @pl.when(i == 0)
    def _():
        def issue(t, carry):
            for k in range(TOP_K):
                copy(dest_ref, 0, t, k, buf_a, 0).start()
            return carry

        lax.fori_loop(0, half, issue, 0, unroll=8)

    wait(buf_a, 0)
    gather(dest_ref, 1, buf_b, 1)
    reduce(buf_a, 0)
    wait(buf_b, 1)
    gather(dest_next_ref, 0, buf_a, 0)
    reduce(buf_b, 1)

    @pl.when(i == pl.num_programs(0) - 1)
    def _():
        wait(buf_a, 0)


def _combine(alpha, dest_tiles, x, gate_t, ln_g, ln_b, yb, next_w_in=None):
    T, D = x.shape
    n_tiles = T // COMBINE_ROWS
    row = lambda w: pl.BlockSpec((COMBINE_ROWS, w), lambda i: (i, 0))
    table = lambda index: pl.BlockSpec((None, 1, TOP_K * COMBINE_ROWS), index, memory_space=pltpu.SMEM)
    half_buf = pltpu.VMEM((TOP_K, COMBINE_ROWS // 2, yb.shape[1]), yb.dtype)
    project = next_w_in is not None
    in_specs = [table(lambda i: (i, 0, 0)),
                table(lambda i: (jnp.minimum(i + 1, n_tiles - 1), 0, 0)),
                row(D), pl.BlockSpec((COMBINE_ROWS, TOP_K), lambda i: (i, 0)),
                _const_spec((1, D)), _const_spec((1, D)),
                pl.BlockSpec(memory_space=pl.ANY)]
    out_specs = [row(D), row(D)]
    out_shape = [jax.ShapeDtypeStruct((T, D), F32), jax.ShapeDtypeStruct((T, D), BF16)]
    operands = [dest_tiles, dest_tiles, x, gate_t, ln_g.reshape(1, D), ln_b.reshape(1, D), yb]
    if project:
        in_specs.append(_const_spec((D, 2 * D_RNN)))
        out_specs += [row(D_RNN), row(D_RNN)]
        out_shape += [jax.ShapeDtypeStruct((T, D_RNN), BF16)] * 2
        operands.append(next_w_in)
    return pl.pallas_call(
        functools.partial(_combine_kernel, alpha, project),
        grid=(n_tiles,),
        in_specs=in_specs,
        out_specs=out_specs,
        out_shape=out_shape,
        scratch_shapes=[half_buf, half_buf, pltpu.SemaphoreType.DMA((2,))],
        compiler_params=_params(("arbitrary",), 52),
        name="combine",
    )(*operands)


def _tile_pairs(dest, rows):
    T = dest.shape[1]
    return dest.reshape(TOP_K, T // rows, rows).transpose(1, 0, 2).reshape(T // rows, 1, TOP_K * rows)


def _moe(alpha, x, x_packed, router_w, router_b, w1, w3, w2, ln_g, ln_b, casts=()):
    T, D = x.shape
    n_blocks = (T * TOP_K) // EXPERT_ROWS + N_EXPERTS
    idx, gate, rank, cnt = _router(x, router_w, router_b)

    counts = cnt[:, 0]
    padded = (counts + EXPERT_ROWS - 1) // EXPERT_ROWS * EXPERT_ROWS
    pad_end = jnp.cumsum(padded)
    pad_start = pad_end - padded
    n_used = (pad_end[-1] // EXPERT_ROWS).astype(jnp.int32)
    block_start = jnp.minimum(jnp.arange(n_blocks, dtype=jnp.int32), n_used - 1) * EXPERT_ROWS
    block_e = jnp.minimum(jnp.sum(block_start[:, None] >= pad_end[None, :], axis=1),
                          N_EXPERTS - 1).astype(jnp.int32)
    expert_ids = jnp.arange(N_EXPERTS, dtype=jnp.int32)[:, None, None]
    dest = jnp.sum(jnp.where(idx[None] == expert_ids, pad_start[:, None, None], 0), axis=0) + rank

    n_used = n_used.reshape(1)
    xs = _dispatch(x_packed, _tile_pairs(dest, DISPATCH_ROWS), (pad_start + counts).astype(jnp.int32),
                   (padded - counts).astype(jnp.int32), n_used, n_blocks * EXPERT_ROWS)
    yb, casted = _experts(xs, block_e, n_used, w1, w3, w2, casts)
    new = _combine(alpha, _tile_pairs(dest, COMBINE_ROWS), x, gate.T, ln_g, ln_b, yb,
                   next_w_in=casted[0] if casted else None)
    return new, casted


def kernel(x, emb_ln_g, emb_ln_b, w_in, conv_w, conv_b, lru_w_a, lru_b_a, lru_w_x, lru_b_x, lru_lambda, w_o_rnn, sgu_ln_g, sgu_ln_b, sgu_w_s, sgu_b_s, w_o_sgu, w_out, ln1_g, ln1_b, router_w, router_b, expert_w1, expert_w3, expert_w2, ln2_g, ln2_b):
    batch, seq, D = x.shape
    depth = w_in.shape[0]
    alpha = float((2 * depth) ** 0.25)
    xf = x.reshape(batch * seq, D)
    mixer_f32 = (w_in, w_o_rnn, w_o_sgu, w_out)
    w_in, w_o_rnn, w_o_sgu, w_out = (w[0:1].astype(BF16) for w in mixer_f32)
    xb, x_rnn, gate = _emb_ln(xf, emb_ln_g, emb_ln_b, w_in)
    rows_of = lambda w: w.reshape(depth, N_EXPERTS * w.shape[2], w.shape[3])
    w1_rows, w3_rows, w2_rows = rows_of(expert_w1), rows_of(expert_w3), rows_of(expert_w2)
    uv_from = 2 * D_RNN
    sigmoid_from = 2 * D_RNN + 2 * D_SGU
    for l in range(depth):
        uv_cols, w3 = _proj(xb, w_in, 0, uv_from, sigmoid_from - uv_from, _gelu, "proj_gelu",
                            cast=(w3_rows, l))
        sigmoid_cols, w1, a = _proj_lru(
            xb, w_in, 0, sigmoid_from, D_IN - sigmoid_from, _sigmoid, "proj_sigmoid_lru",
            (w1_rows, l), x_rnn, gate, seq, conv_w[l], conv_b[l], lru_w_a[l], lru_b_a[l],
            lru_w_x[l], lru_b_x[l], lru_lambda[l])
        x1, x1_packed, w2 = _mix_out(alpha, 0, a, uv_cols, sigmoid_cols, xf, l == 0, emb_ln_g,
                                     emb_ln_b, sgu_ln_g[l], sgu_ln_b[l], sgu_w_s[l], sgu_b_s[l],
                                     w_o_rnn, w_o_sgu, w_out, ln1_g[l], ln1_b[l], cast=(w2_rows, l))
        next_mixer = [(w, l + 1) for w in mixer_f32] if l + 1 < depth else []
        (xf, xb, *branch_a), casted = _moe(
            alpha, x1, x1_packed, router_w, router_b, w1.reshape(expert_w1.shape[1:]),
            w3.reshape(expert_w3.shape[1:]), w2.reshape(expert_w2.shape[1:]), ln2_g[l], ln2_b[l],
            casts=next_mixer)
        if casted:
            x_rnn, gate = branch_a
            w_in, w_o_rnn, w_o_sgu, w_out = (w[None] for w in casted)
    return xf.reshape(batch, seq, D)
```

```python
import functools

import jax
import jax.numpy as jnp
import numpy as np
from jax import lax
from jax.experimental import pallas as pl
from jax.experimental.pallas import tpu as pltpu

F32 = jnp.float32
BF16 = jnp.bfloat16

D_MODEL = 2048
D_RNN = 1024
RNN_HEADS = 8
HEAD_DIM = D_RNN // RNN_HEADS
CONV_WIDTH = 4
LRU_C = 8.0
D_SGU = 1024
SGU_GROUPS = 8
SGU_GROUP_DIM = D_SGU // SGU_GROUPS
CHUNK = 128
N_EXPERTS = 16
N_GROUPS = 4
EXPERTS_PER_GROUP = N_EXPERTS // N_GROUPS
TOP_K = 2
D_EXPERT = 1408
D_IN = 2 * D_RNN + 2 * D_SGU + 2 * D_MODEL
LN_EPS = 1e-5
SQRT_2_OVER_PI = float(np.sqrt(2.0 / np.pi))

LN_ROWS = 512
PROJ_ROWS = 1024
PROJ_COLS = 1024
MIX_ROWS = 256
ROUTER_ROWS = 1024
DISPATCH_ROWS = 512
EXPERT_ROWS = 256
COMBINE_ROWS = 512
CONV_PAD = 8
N_DMA_PRIORITIES = 2
ROW_TILE = 8

MIB = 1024 * 1024


def _params(semantics, vmem_mib):
    return pltpu.CompilerParams(dimension_semantics=semantics,
                                vmem_limit_bytes=vmem_mib * MIB,
                                disable_bounds_checks=True)


def _const_spec(shape):
    zeros = (0,) * len(shape)
    return pl.BlockSpec(shape, lambda *_: zeros, pipeline_mode=pl.Buffered(1))


def _layer_spec(layer, shape):
    zeros = (0,) * len(shape)
    return pl.BlockSpec((None,) + tuple(shape), lambda *_: (layer,) + zeros,
                        pipeline_mode=pl.Buffered(1))


def _layer_norm(v, g, b):
    mu = jnp.mean(v, axis=-1, keepdims=True)
    d = v - mu
    var = jnp.mean(d * d, axis=-1, keepdims=True)
    return d * lax.rsqrt(var + LN_EPS) * g + b


def _gelu(x):
    inner = x * (SQRT_2_OVER_PI + (SQRT_2_OVER_PI * 0.044715) * (x * x))
    return x * (0.5 * jnp.tanh(inner) + 0.5)


def _pack_halves(y):
    n = y.shape[1] // 2
    bits = lambda v: lax.bitcast_convert_type(v.astype(BF16).astype(F32), jnp.uint32)
    return (bits(y[:, :n]) >> 16) | bits(y[:, n:])


def _unpack_halves(p):
    lo = lax.bitcast_convert_type(p << 16, F32)
    hi = lax.bitcast_convert_type(p & jnp.uint32(0xFFFF0000), F32)
    return lo, hi


def _sigmoid(x):
    return 0.5 * jnp.tanh(0.5 * x) + 0.5


def _project_branch_a(xb, w_ref, oxr_ref, ogate_ref, rows=slice(None)):
    acc = jnp.dot(xb, w_ref[...], preferred_element_type=F32)
    oxr_ref[rows, :] = acc[:, :D_RNN].astype(BF16)
    ogate_ref[rows, :] = _gelu(acc[:, D_RNN:]).astype(BF16)


def _emb_ln_kernel(x_ref, g_ref, b_ref, w_ref, ob_ref, oxr_ref, ogate_ref):
    xb = _layer_norm(x_ref[...], g_ref[...], b_ref[...]).astype(BF16)
    ob_ref[...] = xb
    _project_branch_a(xb, w_ref, oxr_ref, ogate_ref)


def _emb_ln(x, g, b, w_in):
    T, D = x.shape
    row = lambda w: pl.BlockSpec((LN_ROWS, w), lambda i: (i, 0))
    half = jax.ShapeDtypeStruct((T, D_RNN), BF16)
    return pl.pallas_call(
        _emb_ln_kernel,
        grid=(T // LN_ROWS,),
        in_specs=[row(D), _const_spec((1, D)), _const_spec((1, D)), _layer_spec(0, (D, 2 * D_RNN))],
        out_specs=[row(D), row(D_RNN), row(D_RNN)],
        out_shape=[jax.ShapeDtypeStruct((T, D), BF16), half, half],
        compiler_params=_params(("parallel",), 40),
        name="emb_ln",
    )(x, g.reshape(1, D), b.reshape(1, D), w_in)


def _cast_specs(stack, layer, n_steps, step_of):
    _, rows, cols = stack.shape
    slab = -(-rows // (n_steps * 16)) * 16
    last = -(-rows // slab) - 1
    src = pl.BlockSpec((None, slab, cols), lambda *g: (layer, jnp.minimum(step_of(*g), last), 0))
    dst = pl.BlockSpec((slab, cols), lambda *g: (jnp.minimum(step_of(*g), last), 0))
    return src, dst, jax.ShapeDtypeStruct((rows, cols), BF16)


def _proj_kernel(act, x_ref, w_ref, cast_src_ref, o_ref, cast_dst_ref):
    acc = jnp.dot(x_ref[...], w_ref[...], preferred_element_type=F32)
    o_ref[...] = act(acc).astype(BF16)
    cast_dst_ref[...] = cast_src_ref[...].astype(BF16)


def _proj(xb, w_in, layer, col_from, n_cols, act, name, cast):
    T, D = xb.shape
    first = col_from // PROJ_COLS
    grid = (T // PROJ_ROWS, n_cols // PROJ_COLS)
    src, dst, cast_shape = _cast_specs(*cast, grid[0] * grid[1], lambda i, j: i * grid[1] + j)
    return pl.pallas_call(
        functools.partial(_proj_kernel, act),
        grid=grid,
        in_specs=[pl.BlockSpec((PROJ_ROWS, D), lambda i, j: (i, 0)),
                  pl.BlockSpec((None, D, PROJ_COLS), lambda i, j: (layer, 0, first + j)), src],
        out_specs=[pl.BlockSpec((PROJ_ROWS, PROJ_COLS), lambda i, j: (i, j)), dst],
        out_shape=[jax.ShapeDtypeStruct((T, n_cols), BF16), cast_shape],
        compiler_params=_params(("parallel", "arbitrary"), 48),
        name=name,
    )(xb, w_in, cast[0])


def _lru_gates(first, xr_ref, cw_ref, cb_ref, wa_ref, ba_ref, wx_ref, bx_ref, lam_ref,
               xpad_ref, hc_ref, a_ref, u_ref):
    rows = xr_ref.shape[0]

    @pl.when(first)
    def _():
        xpad_ref[0:CONV_PAD, :] = jnp.zeros((CONV_PAD, D_RNN), F32)
        hc_ref[...] = jnp.zeros_like(hc_ref)

    x = xr_ref[...].astype(F32)
    xpad_ref[CONV_PAD:CONV_PAD + rows, :] = x
    xc = cb_ref[...] + cw_ref[CONV_WIDTH - 1:CONV_WIDTH, :] * x
    for k in range(CONV_WIDTH - 1):
        shift = CONV_WIDTH - 1 - k
        xc = xc + cw_ref[k:k + 1, :] * xpad_ref[CONV_PAD - shift:CONV_PAD - shift + rows, :]
    xpad_ref[0:CONV_PAD, :] = x[rows - CONV_PAD:rows, :]

    lam = lam_ref[...]
    sp = jnp.maximum(-lam, 0.0) + jnp.log1p(jnp.exp(-jnp.abs(lam)))
    decay = (-0.5 * LRU_C) * sp
    for h in range(RNN_HEADS):
        cols = slice(h * HEAD_DIM, (h + 1) * HEAD_DIM)
        xh = xc[:, cols]
        xhb = xh.astype(BF16)
        tanh_r = jnp.tanh(jnp.dot(xhb, wa_ref[h], preferred_element_type=F32) + ba_ref[:, cols])
        tanh_i = jnp.tanh(jnp.dot(xhb, wx_ref[h], preferred_element_type=F32) + bx_ref[:, cols])
        gi = 0.5 * tanh_i + 0.5
        log_a = decay[:, cols] * tanh_r + decay[:, cols]
        a_ref[:, cols] = jnp.exp(log_a)
        th = jnp.tanh(log_a)
        u_ref[:, cols] = jnp.sqrt(-2.0 * th / (1.0 - th)) * (gi * xh)


def _lru_scan(gate_ref, o_ref, hc_ref, a_ref, u_ref, h_ref):
    def step(t, h):
        h = a_ref[pl.ds(t, 1), :] * h + u_ref[pl.ds(t, 1), :]
        h_ref[pl.ds(t, 1), :] = h
        return h

    hc_ref[0:1, :] = lax.fori_loop(0, a_ref.shape[0], step, hc_ref[0:1, :], unroll=8)
    o_ref[...] = (h_ref[...] * gate_ref[...].astype(F32)).astype(BF16)


def _proj_lru_kernel(act, tiles_per_seq, x_ref, w_ref, cast_src_ref, xr_ref, gate_ref, cw_ref, cb_ref,
                     wa_ref, ba_ref, wx_ref, bx_ref, lam_ref, o_ref, cast_dst_ref, a_out_ref,
                     xpad_ref, hc_ref, a_ref, u_ref, h_ref):
    step = pl.program_id(0) * pl.num_programs(1) + pl.program_id(1)
    _lru_gates(step % tiles_per_seq == 0, xr_ref, cw_ref, cb_ref, wa_ref, ba_ref, wx_ref, bx_ref,
               lam_ref, xpad_ref, hc_ref, a_ref, u_ref)
    acc = jnp.dot(x_ref[...], w_ref[...], preferred_element_type=F32)
    o_ref[...] = act(acc).astype(BF16)
    cast_dst_ref[...] = cast_src_ref[...].astype(BF16)
    _lru_scan(gate_ref, a_out_ref, hc_ref, a_ref, u_ref, h_ref)


def _proj_lru(xb, w_in, layer, col_from, n_cols, act, name, cast, x_rnn, gate, seq,
              conv_w, conv_b, w_a, b_a, w_x, b_x, lam):
    T, D = xb.shape
    first = col_from // PROJ_COLS
    grid = (T // PROJ_ROWS, n_cols // PROJ_COLS)
    n_steps = grid[0] * grid[1]
    step_of = lambda i, j: i * grid[1] + j
    lru_rows = T // n_steps
    src, dst, cast_shape = _cast_specs(*cast, n_steps, step_of)
    tile = pl.BlockSpec((lru_rows, D_RNN), lambda i, j: (step_of(i, j), 0))
    vec = _const_spec((1, D_RNN))
    gate_w = _const_spec((RNN_HEADS, HEAD_DIM, HEAD_DIM))
    tile_f32 = pltpu.VMEM((lru_rows, D_RNN), F32)
    return pl.pallas_call(
        functools.partial(_proj_lru_kernel, act, seq // lru_rows),
        grid=grid,
        in_specs=[pl.BlockSpec((PROJ_ROWS, D), lambda i, j: (i, 0)),
                  pl.BlockSpec((None, D, PROJ_COLS), lambda i, j: (layer, 0, first + j)),
                  src, tile, tile, _const_spec((CONV_WIDTH, D_RNN)), vec, gate_w, vec, gate_w, vec, vec],
        out_specs=[pl.BlockSpec((PROJ_ROWS, PROJ_COLS), lambda i, j: (i, j)), dst, tile],
        out_shape=[jax.ShapeDtypeStruct((T, n_cols), BF16), cast_shape,
                   jax.ShapeDtypeStruct((T, D_RNN), BF16)],
        scratch_shapes=[pltpu.VMEM((CONV_PAD + lru_rows, D_RNN), F32), pltpu.VMEM((8, D_RNN), F32),
                        tile_f32, tile_f32, tile_f32],
        compiler_params=_params(("arbitrary", "arbitrary"), 52),
        name=name,
    )(xb, w_in, cast[0], x_rnn, gate, conv_w, conv_b.reshape(1, D_RNN), (0.5 * w_a).astype(BF16),
      0.5 * b_a.reshape(1, D_RNN), (0.5 * w_x).astype(BF16), 0.5 * b_x.reshape(1, D_RNN),
      lam.reshape(1, D_RNN))


def _spatial_gating(u_ref, v_ref, g_ref, b_ref, ws_ref, bs_ref, o_ref):
    rows = u_ref.shape[0]
    v = _layer_norm(v_ref[...].astype(F32), g_ref[...], b_ref[...]).astype(BF16)
    t_out = lax.broadcasted_iota(jnp.int32, (CHUNK, CHUNK), 0)
    t_in = lax.broadcasted_iota(jnp.int32, (CHUNK, CHUNK), 1)
    causal = t_in <= t_out
    for g in range(SGU_GROUPS):
        cols = slice(g * SGU_GROUP_DIM, (g + 1) * SGU_GROUP_DIM)
        ws = jnp.where(causal, ws_ref[g], 0.0).astype(BF16)
        bias = bs_ref[:, g:g + 1]
        for c in range(rows // CHUNK):
            rws = slice(c * CHUNK, (c + 1) * CHUNK)
            mixed = jnp.dot(ws, v[rws, cols], preferred_element_type=F32) + bias
            o_ref[rws, cols] = (u_ref[rws, cols].astype(F32) * mixed).astype(BF16)


def _mix_out_kernel(alpha, x_is_raw, a_ref, u_ref, v_ref, sga_ref, sgb_ref, x_ref, emb_g_ref,
                    emb_b_ref, sgu_g_ref, sgu_b_ref, ws_ref, bs_ref, woa_ref, wob_ref, wout_ref,
                    g_ref, beta_ref, cast_src_ref, o_ref, packed_ref, cast_dst_ref, b_ref):
    _spatial_gating(u_ref, v_ref, sgu_g_ref, sgu_b_ref, ws_ref, bs_ref, b_ref)
    ya = jnp.dot(a_ref[...], woa_ref[...], preferred_element_type=F32)
    yb = jnp.dot(b_ref[...], wob_ref[...], preferred_element_type=F32)
    merged = sga_ref[...].astype(F32) * ya + sgb_ref[...].astype(F32) * yb
    m = jnp.dot(merged.astype(BF16), wout_ref[...], preferred_element_type=F32)
    x = x_ref[...]
    if x_is_raw:
        x = _layer_norm(x, emb_g_ref[...], emb_b_ref[...])
    y = _layer_norm(alpha * x + m, g_ref[...], beta_ref[...])
    o_ref[...] = y
    packed_ref[...] = _pack_halves(y)
    cast_dst_ref[...] = cast_src_ref[...].astype(BF16)


def _mix_out(alpha, layer, a, uv_cols, sigmoid_cols, x, x_is_raw, emb_ln_g, emb_ln_b,
             sgu_ln_g, sgu_ln_b, w_s, b_s, w_o_rnn, w_o_sgu, w_out, ln_g, ln_b, cast):
    T, D = x.shape
    n_steps = T // MIX_ROWS
    row = lambda w, c: pl.BlockSpec((MIX_ROWS, w), lambda i: (i, c))
    cast_src, cast_dst, cast_shape = _cast_specs(*cast, n_steps, lambda i: i)
    return pl.pallas_call(
        functools.partial(_mix_out_kernel, alpha, x_is_raw),
        grid=(n_steps,),
        in_specs=[row(D_RNN, 0), row(D_SGU, 0), row(D_SGU, 1), row(D, 0), row(D, 1), row(D, 0),
                  _const_spec((1, D)), _const_spec((1, D)),
                  _const_spec((1, D_SGU)), _const_spec((1, D_SGU)),
                  _const_spec((SGU_GROUPS, CHUNK, CHUNK)), _const_spec((CHUNK, SGU_GROUPS)),
                  _layer_spec(layer, (D_RNN, D)), _layer_spec(layer, (D_SGU, D)),
                  _layer_spec(layer, (D, D)), _const_spec((1, D)), _const_spec((1, D)), cast_src],
        out_specs=[row(D, 0), row(D // 2, 0), cast_dst],
        out_shape=[jax.ShapeDtypeStruct((T, D), F32), jax.ShapeDtypeStruct((T, D // 2), jnp.uint32),
                   cast_shape],
        scratch_shapes=[pltpu.VMEM((MIX_ROWS, D_SGU), BF16)],
        compiler_params=_params(("parallel",), 56),
        name="mix_out",
    )(a, uv_cols, uv_cols, sigmoid_cols, sigmoid_cols, x,
      emb_ln_g.reshape(1, D), emb_ln_b.reshape(1, D),
      sgu_ln_g.reshape(1, D_SGU), sgu_ln_b.reshape(1, D_SGU), w_s, b_s.T,
      w_o_rnn, w_o_sgu, w_out, ln_g.reshape(1, D), ln_b.reshape(1, D), cast[0])


def _first_max(v, row, n):
    m = jnp.max(v, axis=0, keepdims=True)
    idx = jnp.min(jnp.where(v == m, row, float(n)), axis=0, keepdims=True)
    return m, idx


def _router_kernel(x_ref, wt_ref, b_ref, idx_ref, gate_ref, rank_ref, cnt_ref, run_ref):
    rows = x_ref.shape[0]

    @pl.when(pl.program_id(0) == 0)
    def _():
        run_ref[...] = jnp.zeros_like(run_ref)

    def nt_dot(w, x):
        return lax.dot_general(w, x, (((1,), (1,)), ((), ())), preferred_element_type=F32)

    x = x_ref[...]
    xh = x.astype(BF16)
    xl = (x - xh.astype(F32)).astype(BF16)
    w = wt_ref[...]
    wh = w.astype(BF16)
    wl = (w - wh.astype(F32)).astype(BF16)
    logits = nt_dot(wh, xh) + (nt_dot(wh, xl) + nt_dot(wl, xh))
    e = jnp.exp(logits - jnp.max(logits, axis=0, keepdims=True))
    scores = e / jnp.sum(e, axis=0, keepdims=True)
    sel = scores + b_ref[...]

    row = lax.broadcasted_iota(jnp.int32, (N_EXPERTS, rows), 0).astype(F32)
    grow = lax.broadcasted_iota(jnp.int32, (EXPERTS_PER_GROUP, rows), 0).astype(F32)
    neg_inf = float("-inf")
    best_score = None
    best_group = None
    for g in range(N_GROUPS):
        v = sel[g * EXPERTS_PER_GROUP:(g + 1) * EXPERTS_PER_GROUP, :]
        m1, i1 = _first_max(v, grow, EXPERTS_PER_GROUP)
        m2 = jnp.max(jnp.where(grow == i1, neg_inf, v), axis=0, keepdims=True)
        s = m1 + m2
        if g == 0:
            best_score, best_group = s, jnp.zeros_like(s)
        else:
            better = s > best_score
            best_group = jnp.where(better, float(g), best_group)
            best_score = jnp.where(better, s, best_score)

    lo = best_group * float(EXPERTS_PER_GROUP)
    in_group = jnp.logical_and(row >= lo, row < lo + float(EXPERTS_PER_GROUP))
    masked = jnp.where(in_group, sel, neg_inf)
    _, i1 = _first_max(masked, row, N_EXPERTS)
    pick1 = row == i1
    _, i2 = _first_max(jnp.where(pick1, neg_inf, masked), row, N_EXPERTS)
    pick2 = row == i2
    s1 = jnp.sum(jnp.where(pick1, scores, 0.0), axis=0, keepdims=True)
    s2 = jnp.sum(jnp.where(pick2, scores, 0.0), axis=0, keepdims=True)
    den = s1 + s2

    onehot = jnp.where(jnp.logical_or(pick1, pick2), 1.0, 0.0)
    before = (lax.broadcasted_iota(jnp.int32, (rows, rows), 0)
              < lax.broadcasted_iota(jnp.int32, (rows, rows), 1))
    prefix = jnp.dot(onehot.astype(BF16), jnp.where(before, 1.0, 0.0).astype(BF16),
                     preferred_element_type=F32)
    pos = prefix + run_ref[:, 0:1]
    r1 = jnp.sum(jnp.where(pick1, pos, 0.0), axis=0, keepdims=True)
    r2 = jnp.sum(jnp.where(pick2, pos, 0.0), axis=0, keepdims=True)
    run_ref[...] = run_ref[...] + jnp.sum(onehot, axis=1, keepdims=True)

    idx_ref[0:1, :] = i1.astype(jnp.int32)
    idx_ref[1:2, :] = i2.astype(jnp.int32)
    gate_ref[0:1, :] = s1 / den
    gate_ref[1:2, :] = s2 / den
    rank_ref[0:1, :] = r1.astype(jnp.int32)
    rank_ref[1:2, :] = r2.astype(jnp.int32)
    cnt_ref[...] = run_ref[...].astype(jnp.int32)


def _router(x, router_w, router_b):
    T, D = x.shape
    pair = pl.BlockSpec((TOP_K, ROUTER_ROWS), lambda i: (0, i))
    return pl.pallas_call(
        _router_kernel,
        grid=(T // ROUTER_ROWS,),
        in_specs=[pl.BlockSpec((ROUTER_ROWS, D), lambda i: (i, 0)),
                  _const_spec((N_EXPERTS, D)), _const_spec((N_EXPERTS, 1))],
        out_specs=[pair, pair, pair, pl.BlockSpec((N_EXPERTS, 128), lambda i: (0, 0))],
        out_shape=[jax.ShapeDtypeStruct((TOP_K, T), jnp.int32),
                   jax.ShapeDtypeStruct((TOP_K, T), F32),
                   jax.ShapeDtypeStruct((TOP_K, T), jnp.int32),
                   jax.ShapeDtypeStruct((N_EXPERTS, 128), jnp.int32)],
        scratch_shapes=[pltpu.VMEM((N_EXPERTS, 128), F32)],
        compiler_params=_params(("arbitrary",), 40),
        name="router",
    )(x, router_w.T, router_b.reshape(N_EXPERTS, 1))


def _dispatch_kernel(pad_from_ref, pad_n_ref, n_used_ref, dest_ref, x_ref, xs_hbm, zero_ref, sem, pad_sem):
    i = pl.program_id(0)
    rows = DISPATCH_ROWS
    n_blocks = xs_hbm.shape[0] // EXPERT_ROWS
    zero_rows = zero_ref.shape[0]

    def tail_copies(j):
        block = n_used_ref[0] + j
        off = pl.multiple_of(block * EXPERT_ROWS, EXPERT_ROWS)
        return block < n_blocks, [
            pltpu.make_async_copy(zero_ref, xs_hbm.at[pl.ds(off + part * zero_rows, zero_rows)], pad_sem)
            for part in range(EXPERT_ROWS // zero_rows)]

    def pad_copy(e, bit):
        n = pad_n_ref[e]
        if bit < ROW_TILE:
            off = pad_from_ref[e] + bit - 1
            return pltpu.make_async_copy(zero_ref.at[pl.ds(0, 1)], xs_hbm.at[pl.ds(off, 1)], pad_sem)
        done = (n & (ROW_TILE - 1)) + (n & ~(2 * bit - 1))
        off = pl.multiple_of(pad_from_ref[e] + done, ROW_TILE)
        return pltpu.make_async_copy(zero_ref.at[pl.ds(0, bit)], xs_hbm.at[pl.ds(off, bit)], pad_sem)

    def pad_needed(e, bit):
        n = pad_n_ref[e]
        if bit < ROW_TILE:
            return bit <= (n & (ROW_TILE - 1))
        return (n & bit) != 0

    bits = list(range(1, ROW_TILE)) + [
        1 << k for k in range(ROW_TILE.bit_length() - 1, EXPERT_ROWS.bit_length() - 1)]

    @pl.when(i == 0)
    def _():
        zero_ref[...] = jnp.zeros_like(zero_ref)
        for e in range(N_EXPERTS):
            for bit in bits:
                @pl.when(pad_needed(e, bit))
                def _():
                    pad_copy(e, bit).start()
        for j in range(N_EXPERTS):
            needed, copies = tail_copies(j)

            @pl.when(needed)
            def _():
                for c in copies:
                    c.start()

    for t in range(rows):
        src = x_ref.at[pl.ds(t, 1)]
        for k in range(TOP_K):
            pltpu.make_async_copy(src, xs_hbm.at[pl.ds(dest_ref[0, k * rows + t], 1)],
                                  sem).start(priority=k % N_DMA_PRIORITIES)
    for _ in range(TOP_K):
        pltpu.make_async_copy(x_ref, xs_hbm.at[pl.ds(0, rows)], sem).wait()

    @pl.when(i == 0)
    def _():
        for e in range(N_EXPERTS):
            for bit in bits:
                @pl.when(pad_needed(e, bit))
                def _():
                    pad_copy(e, bit).wait()
        for j in range(N_EXPERTS):
            needed, copies = tail_copies(j)

            @pl.when(needed)
            def _():
                for c in copies:
                    c.wait()


def _dispatch(x, dest_tiles, pad_from, pad_n, n_used, n_slots):
    T, D = x.shape
    grid_spec = pltpu.PrefetchScalarGridSpec(
        num_scalar_prefetch=3,
        grid=(T // DISPATCH_ROWS,),
        in_specs=[pl.BlockSpec((None, 1, TOP_K * DISPATCH_ROWS), lambda i, *_: (i, 0, 0),
                               memory_space=pltpu.SMEM),
                  pl.BlockSpec((DISPATCH_ROWS, D), lambda i, *_: (i, 0))],
        out_specs=pl.BlockSpec(memory_space=pl.ANY),
        scratch_shapes=[pltpu.VMEM((EXPERT_ROWS // 2, D), x.dtype),
                        pltpu.SemaphoreType.DMA(()), pltpu.SemaphoreType.DMA(())],
    )
    return pl.pallas_call(
        _dispatch_kernel,
        grid_spec=grid_spec,
        out_shape=jax.ShapeDtypeStruct((n_slots, D), x.dtype),
        compiler_params=_params(("arbitrary",), 16),
        name="dispatch",
    )(pad_from, pad_n, n_used, dest_tiles, x)


PLAN_EXPERT, PLAN_FIRST, PLAN_SLOT, PLAN_NEXT = range(4)


def _expert_kernel(plan_ref, n_used_ref, xs_ref, w1_hbm, w3_hbm, w2_hbm, *rest):
    *rest, w1_buf, w3_buf, w2_buf, sem = rest
    n_casts = len(rest) // 2
    o_ref = rest[n_casts]
    i = pl.program_id(0)
    used = i < n_used_ref[0]
    slot = plan_ref[PLAN_SLOT, i]

    def weight_copies(expert, into):
        return [pltpu.make_async_copy(hbm.at[expert], buf.at[into], sem.at[into, k])
                for k, (hbm, buf) in enumerate(((w1_hbm, w1_buf), (w3_hbm, w3_buf), (w2_hbm, w2_buf)))]

    @pl.when(i == 0)
    def _():
        for c in weight_copies(plan_ref[PLAN_EXPERT, 0], 0):
            c.start()

    @pl.when(plan_ref[PLAN_FIRST, i] == 1)
    def _():
        for c in weight_copies(plan_ref[PLAN_EXPERT, i], slot):
            c.wait()

        @pl.when(plan_ref[PLAN_NEXT, i] >= 0)
        def _():
            for c in weight_copies(plan_ref[PLAN_NEXT, i], 1 - slot):
                c.start()

    @pl.when(jnp.logical_not(used))
    def _():
        o_ref[...] = jnp.zeros_like(o_ref)

    @pl.when(used)
    def _():
        lo, hi = _unpack_halves(xs_ref[...])
        x = jnp.concatenate([lo.astype(BF16), hi.astype(BF16)], axis=1)
        h1 = jnp.dot(x, w1_buf[slot], preferred_element_type=F32)
        h3 = jnp.dot(x, w3_buf[slot], preferred_element_type=F32)
        hidden = (h1 * _sigmoid(h1)) * h3
        out = jnp.dot(hidden.astype(BF16), w2_buf[slot], preferred_element_type=F32)
        o_ref[...] = _pack_halves(out)
        for src_ref, dst_ref in zip(rest[:n_casts], rest[n_casts + 1:]):
            dst_ref[...] = src_ref[...].astype(BF16)


def _experts(xs, block_e, n_used, w1, w3, w2, casts=()):
    P, packed = xs.shape
    D = 2 * packed
    n_blocks = P // EXPERT_ROWS
    index = jnp.arange(n_blocks, dtype=jnp.int32)
    first = jnp.logical_and(index < n_used[0],
                            jnp.concatenate([jnp.ones((1,), bool), block_e[1:] != block_e[:-1]]))
    slot = (jnp.cumsum(first) - 1) % 2
    starts = jnp.where(first, index, n_blocks)
    next_start = jnp.concatenate([lax.cummin(starts, axis=0, reverse=True)[1:],
                                  jnp.full((1,), n_blocks, jnp.int32)])
    next_e = jnp.where(next_start < n_blocks, block_e[jnp.minimum(next_start, n_blocks - 1)], -1)
    plan = jnp.stack([block_e, first.astype(jnp.int32), slot.astype(jnp.int32),
                      next_e.astype(jnp.int32)])
    used_rows = pl.BlockSpec((EXPERT_ROWS, packed),
                             lambda i, plan, nu: (jnp.maximum(jnp.minimum(i, nu[0] - 1), 0), 0))
    rows = pl.BlockSpec((EXPERT_ROWS, packed), lambda i, plan, nu: (i, 0))
    cast_specs = [_cast_specs(stack, layer, n_blocks - N_EXPERTS, lambda i, plan, nu: i)
                  for stack, layer in casts]
    in_hbm = pl.BlockSpec(memory_space=pl.ANY)
    grid_spec = pltpu.PrefetchScalarGridSpec(
        num_scalar_prefetch=2,
        grid=(n_blocks,),
        in_specs=[used_rows, in_hbm, in_hbm, in_hbm] + [s[0] for s in cast_specs],
        out_specs=[rows] + [s[1] for s in cast_specs],
        scratch_shapes=[pltpu.VMEM((2, D, D_EXPERT), BF16), pltpu.VMEM((2, D, D_EXPERT), BF16),
                        pltpu.VMEM((2, D_EXPERT, D), BF16), pltpu.SemaphoreType.DMA((2, 3))],
    )
    out = pl.pallas_call(
        _expert_kernel,
        grid_spec=grid_spec,
        out_shape=[jax.ShapeDtypeStruct((P, packed), jnp.uint32)] + [s[2] for s in cast_specs],
        compiler_params=_params(("arbitrary",), 56),
        name="experts",
    )(plan, n_used, xs, w1, w3, w2, *[stack for stack, _ in casts])
    return out[0], out[1:]


def _combine_kernel(alpha, project, dest_ref, dest_next_ref, x_ref, gate_ref, g_ref, b_ref, y_hbm,
                    *rest):
    if project:
        w_ref, of_ref, ob_ref, oxr_ref, ogate_ref, buf_a, buf_b, sem = rest
    else:
        of_ref, ob_ref, buf_a, buf_b, sem = rest
    rows = COMBINE_ROWS
    half = rows // 2
    i = pl.program_id(0)

    def copy(table_ref, h, t, k, buf, s):
        return pltpu.make_async_copy(y_hbm.at[pl.ds(table_ref[0, k * rows + h * half + t], 1)],
                                     buf.at[k, pl.ds(t, 1)], sem.at[s])

    def gather(table_ref, h, buf, s):
        for t in range(half):
            for k in range(TOP_K):
                copy(table_ref, h, t, k, buf, s).start(priority=k % N_DMA_PRIORITIES)

    def wait(buf, s):
        for k in range(TOP_K):
            pltpu.make_async_copy(y_hbm.at[pl.ds(0, half)], buf.at[k], sem.at[s]).wait()

    def reduce(buf, h):
        r = slice(h * half, (h + 1) * half)
        lo0, hi0 = _unpack_halves(buf[0])
        lo1, hi1 = _unpack_halves(buf[1])
        g0 = gate_ref[r, 0:1]
        g1 = gate_ref[r, 1:2]
        f = jnp.concatenate([g0 * lo0 + g1 * lo1, g0 * hi0 + g1 * hi1], axis=1)
        y = _layer_norm(alpha * x_ref[r, :] + f, g_ref[...], b_ref[...])
        of_ref[r, :] = y
        yb = y.astype(BF16)
        ob_ref[r, :] = yb
        if project:
            _project_branch_a(yb, w_ref, oxr_ref, ogate_ref, rows=r)

    @pl.when(i == 0)
    def _():
        def issue(t, carry):
            for k in range(TOP_K):
                copy(dest_ref, 0, t, k, buf_a, 0).start()
            return carry

        lax.fori_loop(0, half, issue, 0, unroll=8)

    wait(buf_a, 0)
    gather(dest_ref, 1, buf_b, 1)
    reduce(buf_a, 0)
    wait(buf_b, 1)
    gather(dest_next_ref, 0, buf_a, 0)
    reduce(buf_b, 1)

    @pl.when(i == pl.num_programs(0) - 1)
    def _():
        wait(buf_a, 0)


def _combine(alpha, dest_tiles, x, gate_t, ln_g, ln_b, yb, next_w_in=None):
    T, D = x.shape
    n_tiles = T // COMBINE_ROWS
    row = lambda w: pl.BlockSpec((COMBINE_ROWS, w), lambda i: (i, 0))
    table = lambda index: pl.BlockSpec((None, 1, TOP_K * COMBINE_ROWS), index, memory_space=pltpu.SMEM)
    half_buf = pltpu.VMEM((TOP_K, COMBINE_ROWS // 2, yb.shape[1]), yb.dtype)
    project = next_w_in is not None
    in_specs = [table(lambda i: (i, 0, 0)),
                table(lambda i: (jnp.minimum(i + 1, n_tiles - 1), 0, 0)),
                row(D), pl.BlockSpec((COMBINE_ROWS, TOP_K), lambda i: (i, 0)),
                _const_spec((1, D)), _const_spec((1, D)),
                pl.BlockSpec(memory_space=pl.ANY)]
    out_specs = [row(D), row(D)]
    out_shape = [jax.ShapeDtypeStruct((T, D), F32), jax.ShapeDtypeStruct((T, D), BF16)]
    operands = [dest_tiles, dest_tiles, x, gate_t, ln_g.reshape(1, D), ln_b.reshape(1, D), yb]
    if project:
        in_specs.append(_const_spec((D, 2 * D_RNN)))
        out_specs += [row(D_RNN), row(D_RNN)]
        out_shape += [jax.ShapeDtypeStruct((T, D_RNN), BF16)] * 2
        operands.append(next_w_in)
    return pl.pallas_call(
        functools.partial(_combine_kernel, alpha, project),
        grid=(n_tiles,),
        in_specs=in_specs,
        out_specs=out_specs,
        out_shape=out_shape,
        scratch_shapes=[half_buf, half_buf, pltpu.SemaphoreType.DMA((2,))],
        compiler_params=_params(("arbitrary",), 52),
        name="combine",
    )(*operands)


def _tile_pairs(dest, rows):
    T = dest.shape[1]
    return dest.reshape(TOP_K, T // rows, rows).transpose(1, 0, 2).reshape(T // rows, 1, TOP_K * rows)


def _moe(alpha, x, x_packed, router_w, router_b, w1, w3, w2, ln_g, ln_b, casts=()):
    T, D = x.shape
    n_blocks = (T * TOP_K) // EXPERT_ROWS + N_EXPERTS
    idx, gate, rank, cnt = _router(x, router_w, router_b)

    counts = cnt[:, 0]
    padded = (counts + EXPERT_ROWS - 1) // EXPERT_ROWS * EXPERT_ROWS
    pad_end = jnp.cumsum(padded)
    pad_start = pad_end - padded
    n_used = (pad_end[-1] // EXPERT_ROWS).astype(jnp.int32)
    block_start = jnp.minimum(jnp.arange(n_blocks, dtype=jnp.int32), n_used - 1) * EXPERT_ROWS
    block_e = jnp.minimum(jnp.sum(block_start[:, None] >= pad_end[None, :], axis=1),
                          N_EXPERTS - 1).astype(jnp.int32)
    expert_ids = jnp.arange(N_EXPERTS, dtype=jnp.int32)[:, None, None]
    dest = jnp.sum(jnp.where(idx[None] == expert_ids, pad_start[:, None, None], 0), axis=0) + rank

    n_used = n_used.reshape(1)
    xs = _dispatch(x_packed, _tile_pairs(dest, DISPATCH_ROWS), (pad_start + counts).astype(jnp.int32),
                   (padded - counts).astype(jnp.int32), n_used, n_blocks * EXPERT_ROWS)
    yb, casted = _experts(xs, block_e, n_used, w1, w3, w2, casts)
    new = _combine(alpha, _tile_pairs(dest, COMBINE_ROWS), x, gate.T, ln_g, ln_b, yb,
                   next_w_in=casted[0] if casted else None)
    return new, casted


def kernel(x, emb_ln_g, emb_ln_b, w_in, conv_w, conv_b, lru_w_a, lru_b_a, lru_w_x, lru_b_x, lru_lambda, w_o_rnn, sgu_ln_g, sgu_ln_b, sgu_w_s, sgu_b_s, w_o_sgu, w_out, ln1_g, ln1_b, router_w, router_b, expert_w1, expert_w3, expert_w2, ln2_g, ln2_b):
    batch, seq, D = x.shape
    depth = w_in.shape[0]
    alpha = float((2 * depth) ** 0.25)
    xf = x.reshape(batch * seq, D)
    mixer_f32 = (w_in, w_o_rnn, w_o_sgu, w_out)
    w_in, w_o_rnn, w_o_sgu, w_out = (w[0:1].astype(BF16) for w in mixer_f32)
    xb, x_rnn, gate = _emb_ln(xf, emb_ln_g, emb_ln_b, w_in)
    rows_of = lambda w: w.reshape(depth, N_EXPERTS * w.shape[2], w.shape[3])
    w1_rows, w3_rows, w2_rows = rows_of(expert_w1), rows_of(expert_w3), rows_of(expert_w2)
    uv_from = 2 * D_RNN
    sigmoid_from = 2 * D_RNN + 2 * D_SGU
    for l in range(depth):
        uv_cols, w3 = _proj(xb, w_in, 0, uv_from, sigmoid_from - uv_from, _gelu, "proj_gelu",
                            cast=(w3_rows, l))
        sigmoid_cols, w1, a = _proj_lru(
            xb, w_in, 0, sigmoid_from, D_IN - sigmoid_from, _sigmoid, "proj_sigmoid_lru",
            (w1_rows, l), x_rnn, gate, seq, conv_w[l], conv_b[l], lru_w_a[l], lru_b_a[l],
            lru_w_x[l], lru_b_x[l], lru_lambda[l])
        x1, x1_packed, w2 = _mix_out(alpha, 0, a, uv_cols, sigmoid_cols, xf, l == 0, emb_ln_g,
                                     emb_ln_b, sgu_ln_g[l], sgu_ln_b[l], sgu_w_s[l], sgu_b_s[l],
                                     w_o_rnn, w_o_sgu, w_out, ln1_g[l], ln1_b[l], cast=(w2_rows, l))
        next_mixer = [(w, l + 1) for w in mixer_f32] if l + 1 < depth else []
        (xf, xb, *branch_a), casted = _moe(
            alpha, x1, x1_packed, router_w, router_b, w1.reshape(expert_w1.shape[1:]),
            w3.reshape(expert_w3.shape[1:]), w2.reshape(expert_w2.shape[1:]), ln2_g[l], ln2_b[l],
            casts=next_mixer)
        if casted:
            x_rnn, gate = branch_a
            w_in, w_o_rnn, w_o_sgu, w_out = (w[None] for w in casted)
    return xf.reshape(batch, seq, D)
```

```python
import functools

import jax
import jax.numpy as jnp
import numpy as np
from jax import lax
from jax.experimental import pallas as pl
from jax.experimental.pallas import tpu as pltpu

F32 = jnp.float32
BF16 = jnp.bfloat16

D_MODEL = 2048
D_RNN = 1024
RNN_HEADS = 8
HEAD_DIM = D_RNN // RNN_HEADS
CONV_WIDTH = 4
LRU_C = 8.0
D_SGU = 1024
SGU_GROUPS = 8
SGU_GROUP_DIM = D_SGU // SGU_GROUPS
CHUNK = 128
N_EXPERTS = 16
N_GROUPS = 4
EXPERTS_PER_GROUP = N_EXPERTS // N_GROUPS
TOP_K = 2
D_EXPERT = 1408
D_IN = 2 * D_RNN + 2 * D_SGU + 2 * D_MODEL
LN_EPS = 1e-5
SQRT_2_OVER_PI = float(np.sqrt(2.0 / np.pi))

LN_ROWS = 512
PROJ_ROWS = 1024
PROJ_COLS = 1024
MIX_ROWS = 256
ROUTER_ROWS = 1024
DISPATCH_ROWS = 512
EXPERT_ROWS = 256
COMBINE_ROWS = 512
CONV_PAD = 8
SCAN_SEGMENTS = 8
N_DMA_PRIORITIES = 2
ROW_TILE = 8

MIB = 1024 * 1024


def _params(semantics, vmem_mib):
    return pltpu.CompilerParams(dimension_semantics=semantics,
                                vmem_limit_bytes=vmem_mib * MIB,
                                disable_bounds_checks=True)


def _const_spec(shape):
    zeros = (0,) * len(shape)
    return pl.BlockSpec(shape, lambda *_: zeros, pipeline_mode=pl.Buffered(1))


def _layer_spec(layer, shape):
    zeros = (0,) * len(shape)
    return pl.BlockSpec((None,) + tuple(shape), lambda *_: (layer,) + zeros,
                        pipeline_mode=pl.Buffered(1))


def _layer_norm(v, g, b):
    mu = jnp.mean(v, axis=-1, keepdims=True)
    d = v - mu
    var = jnp.mean(d * d, axis=-1, keepdims=True)
    return d * lax.rsqrt(var + LN_EPS) * g + b


def _gelu(x):
    inner = x * (SQRT_2_OVER_PI + (SQRT_2_OVER_PI * 0.044715) * (x * x))
    return x * (0.5 * jnp.tanh(inner) + 0.5)


def _pack_halves(y):
    n = y.shape[1] // 2
    bits = lambda v: lax.bitcast_convert_type(v.astype(BF16).astype(F32), jnp.uint32)
    return (bits(y[:, :n]) >> 16) | bits(y[:, n:])


def _unpack_halves(p):
    lo = lax.bitcast_convert_type(p << 16, F32)
    hi = lax.bitcast_convert_type(p & jnp.uint32(0xFFFF0000), F32)
    return lo, hi


def _sigmoid(x):
    return 0.5 * jnp.tanh(0.5 * x) + 0.5


def _project_branch_a(xb, w_ref, oxr_ref, ogate_ref, rows=slice(None)):
    acc = jnp.dot(xb, w_ref[...], preferred_element_type=F32)
    oxr_ref[rows, :] = acc[:, :D_RNN].astype(BF16)
    ogate_ref[rows, :] = _gelu(acc[:, D_RNN:]).astype(BF16)


def _emb_ln_kernel(x_ref, g_ref, b_ref, w_ref, ob_ref, oxr_ref, ogate_ref):
    xb = _layer_norm(x_ref[...], g_ref[...], b_ref[...]).astype(BF16)
    ob_ref[...] = xb
    _project_branch_a(xb, w_ref, oxr_ref, ogate_ref)


def _emb_ln(x, g, b, w_in):
    T, D = x.shape
    row = lambda w: pl.BlockSpec((LN_ROWS, w), lambda i: (i, 0))
    half = jax.ShapeDtypeStruct((T, D_RNN), BF16)
    return pl.pallas_call(
        _emb_ln_kernel,
        grid=(T // LN_ROWS,),
        in_specs=[row(D), _const_spec((1, D)), _const_spec((1, D)), _layer_spec(0, (D, 2 * D_RNN))],
        out_specs=[row(D), row(D_RNN), row(D_RNN)],
        out_shape=[jax.ShapeDtypeStruct((T, D), BF16), half, half],
        compiler_params=_params(("parallel",), 40),
        name="emb_ln",
    )(x, g.reshape(1, D), b.reshape(1, D), w_in)


def _cast_specs(stack, layer, n_steps, step_of):
    _, rows, cols = stack.shape
    slab = -(-rows // (n_steps * 16)) * 16
    last = -(-rows // slab) - 1
    src = pl.BlockSpec((None, slab, cols), lambda *g: (layer, jnp.minimum(step_of(*g), last), 0))
    dst = pl.BlockSpec((slab, cols), lambda *g: (jnp.minimum(step_of(*g), last), 0))
    return src, dst, jax.ShapeDtypeStruct((rows, cols), BF16)


def _proj_kernel(act, x_ref, w_ref, cast_src_ref, o_ref, cast_dst_ref):
    acc = jnp.dot(x_ref[...], w_ref[...], preferred_element_type=F32)
    o_ref[...] = act(acc).astype(BF16)
    cast_dst_ref[...] = cast_src_ref[...].astype(BF16)


def _proj(xb, w_in, layer, col_from, n_cols, act, name, cast):
    T, D = xb.shape
    first = col_from // PROJ_COLS
    grid = (T // PROJ_ROWS, n_cols // PROJ_COLS)
    src, dst, cast_shape = _cast_specs(*cast, grid[0] * grid[1], lambda i, j: i * grid[1] + j)
    return pl.pallas_call(
        functools.partial(_proj_kernel, act),
        grid=grid,
        in_specs=[pl.BlockSpec((PROJ_ROWS, D), lambda i, j: (i, 0)),
                  pl.BlockSpec((None, D, PROJ_COLS), lambda i, j: (layer, 0, first + j)), src],
        out_specs=[pl.BlockSpec((PROJ_ROWS, PROJ_COLS), lambda i, j: (i, j)), dst],
        out_shape=[jax.ShapeDtypeStruct((T, n_cols), BF16), cast_shape],
        compiler_params=_params(("parallel", "arbitrary"), 48),
        name=name,
    )(xb, w_in, cast[0])


def _lru_gates(first, xr_ref, cw_ref, cb_ref, wa_ref, ba_ref, wx_ref, bx_ref, lam_ref,
               xpad_ref, hc_ref, a_ref, u_ref):
    rows = xr_ref.shape[0]

    @pl.when(first)
    def _():
        xpad_ref[0:CONV_PAD, :] = jnp.zeros((CONV_PAD, D_RNN), F32)
        hc_ref[...] = jnp.zeros_like(hc_ref)

    x = xr_ref[...].astype(F32)
    xpad_ref[CONV_PAD:CONV_PAD + rows, :] = x
    xc = cb_ref[...] + cw_ref[CONV_WIDTH - 1:CONV_WIDTH, :] * x
    for k in range(CONV_WIDTH - 1):
        shift = CONV_WIDTH - 1 - k
        xc = xc + cw_ref[k:k + 1, :] * xpad_ref[CONV_PAD - shift:CONV_PAD - shift + rows, :]
    xpad_ref[0:CONV_PAD, :] = x[rows - CONV_PAD:rows, :]

    lam = lam_ref[...]
    sp = jnp.maximum(-lam, 0.0) + jnp.log1p(jnp.exp(-jnp.abs(lam)))
    decay = (-0.5 * LRU_C) * sp
    for h in range(RNN_HEADS):
        cols = slice(h * HEAD_DIM, (h + 1) * HEAD_DIM)
        xh = xc[:, cols]
        xhb = xh.astype(BF16)
        tanh_r = jnp.tanh(jnp.dot(xhb, wa_ref[h], preferred_element_type=F32) + ba_ref[:, cols])
        tanh_i = jnp.tanh(jnp.dot(xhb, wx_ref[h], preferred_element_type=F32) + bx_ref[:, cols])
        gi = 0.5 * tanh_i + 0.5
        log_a = decay[:, cols] * tanh_r + decay[:, cols]
        a_ref[pl.ds(h, rows, stride=RNN_HEADS), :] = jnp.exp(log_a)
        th = jnp.tanh(log_a)
        u_ref[pl.ds(h, rows, stride=RNN_HEADS), :] = jnp.sqrt(-2.0 * th / (1.0 - th)) * (gi * xh)


def _lru_scan(gate_ref, o_ref, hc_ref, a_ref, u_ref, h_ref, p_ref):
    rows = gate_ref.shape[0]
    seg_len = rows // SCAN_SEGMENTS

    def step_rows(t):
        return pl.ds(pl.multiple_of(t * RNN_HEADS, RNN_HEADS), RNN_HEADS)

    def step(tau, carry):
        hs, ps = carry
        new_h, new_p = [], []
        for s in range(SCAN_SEGMENTS):
            at = step_rows(s * seg_len + tau)
            a = a_ref[at, :]
            h = a * hs[s] + u_ref[at, :]
            p = a * ps[s]
            h_ref[at, :] = h
            p_ref[at, :] = p
            new_h.append(h)
            new_p.append(p)
        return tuple(new_h), tuple(new_p)

    tile = (RNN_HEADS, HEAD_DIM)
    start = (tuple(jnp.zeros(tile, F32) for _ in range(SCAN_SEGMENTS)),
             tuple(jnp.ones(tile, F32) for _ in range(SCAN_SEGMENTS)))
    h_end, p_end = lax.fori_loop(0, seg_len, step, start, unroll=2)

    entering = []
    state = hc_ref[...]
    for s in range(SCAN_SEGMENTS):
        entering.append(state)
        state = p_end[s] * state + h_end[s]
    hc_ref[...] = state

    for head in range(RNN_HEADS):
        cols = slice(head * HEAD_DIM, (head + 1) * HEAD_DIM)
        of_head = pl.ds(head, rows, stride=RNN_HEADS)
        enter = jnp.concatenate([jnp.broadcast_to(e[head:head + 1, :], (seg_len, HEAD_DIM))
                                 for e in entering], axis=0)
        h = h_ref[of_head, :] + p_ref[of_head, :] * enter
        o_ref[:, cols] = (h * gate_ref[:, cols].astype(F32)).astype(BF16)


def _proj_lru_kernel(act, tiles_per_seq, x_ref, w_ref, cast_src_ref, xr_ref, gate_ref, cw_ref, cb_ref,
                     wa_ref, ba_ref, wx_ref, bx_ref, lam_ref, o_ref, cast_dst_ref, a_out_ref,
                     xpad_ref, hc_ref, a_ref, u_ref, h_ref, p_ref):
    step = pl.program_id(0) * pl.num_programs(1) + pl.program_id(1)
    _lru_gates(step % tiles_per_seq == 0, xr_ref, cw_ref, cb_ref, wa_ref, ba_ref, wx_ref, bx_ref,
               lam_ref, xpad_ref, hc_ref, a_ref, u_ref)
    acc = jnp.dot(x_ref[...], w_ref[...], preferred_element_type=F32)
    o_ref[...] = act(acc).astype(BF16)
    cast_dst_ref[...] = cast_src_ref[...].astype(BF16)
    _lru_scan(gate_ref, a_out_ref, hc_ref, a_ref, u_ref, h_ref, p_ref)


def _proj_lru(xb, w_in, layer, col_from, n_cols, act, name, cast, x_rnn, gate, seq,
              conv_w, conv_b, w_a, b_a, w_x, b_x, lam):
    T, D = xb.shape
    first = col_from // PROJ_COLS
    grid = (T // PROJ_ROWS, n_cols // PROJ_COLS)
    n_steps = grid[0] * grid[1]
    step_of = lambda i, j: i * grid[1] + j
    lru_rows = T // n_steps
    src, dst, cast_shape = _cast_specs(*cast, n_steps, step_of)
    tile = pl.BlockSpec((lru_rows, D_RNN), lambda i, j: (step_of(i, j), 0))
    vec = _const_spec((1, D_RNN))
    gate_w = _const_spec((RNN_HEADS, HEAD_DIM, HEAD_DIM))
    steps_f32 = pltpu.VMEM((lru_rows * RNN_HEADS, HEAD_DIM), F32)
    return pl.pallas_call(
        functools.partial(_proj_lru_kernel, act, seq // lru_rows),
        grid=grid,
        in_specs=[pl.BlockSpec((PROJ_ROWS, D), lambda i, j: (i, 0)),
                  pl.BlockSpec((None, D, PROJ_COLS), lambda i, j: (layer, 0, first + j)),
                  src, tile, tile, _const_spec((CONV_WIDTH, D_RNN)), vec, gate_w, vec, gate_w, vec, vec],
        out_specs=[pl.BlockSpec((PROJ_ROWS, PROJ_COLS), lambda i, j: (i, j)), dst, tile],
        out_shape=[jax.ShapeDtypeStruct((T, n_cols), BF16), cast_shape,
                   jax.ShapeDtypeStruct((T, D_RNN), BF16)],
        scratch_shapes=[pltpu.VMEM((CONV_PAD + lru_rows, D_RNN), F32),
                        pltpu.VMEM((RNN_HEADS, HEAD_DIM), F32),
                        steps_f32, steps_f32, steps_f32, steps_f32],
        compiler_params=_params(("arbitrary", "arbitrary"), 52),
        name=name,
    )(xb, w_in, cast[0], x_rnn, gate, conv_w, conv_b.reshape(1, D_RNN), (0.5 * w_a).astype(BF16),
      0.5 * b_a.reshape(1, D_RNN), (0.5 * w_x).astype(BF16), 0.5 * b_x.reshape(1, D_RNN),
      lam.reshape(1, D_RNN))


def _spatial_gating(u_ref, v_ref, g_ref, b_ref, ws_ref, bs_ref, o_ref):
    rows = u_ref.shape[0]
    v = _layer_norm(v_ref[...].astype(F32), g_ref[...], b_ref[...]).astype(BF16)
    t_out = lax.broadcasted_iota(jnp.int32, (CHUNK, CHUNK), 0)
    t_in = lax.broadcasted_iota(jnp.int32, (CHUNK, CHUNK), 1)
    causal = t_in <= t_out
    for g in range(SGU_GROUPS):
        cols = slice(g * SGU_GROUP_DIM, (g + 1) * SGU_GROUP_DIM)
        ws = jnp.where(causal, ws_ref[g], 0.0).astype(BF16)
        bias = bs_ref[:, g:g + 1]
        for c in range(rows // CHUNK):
            rws = slice(c * CHUNK, (c + 1) * CHUNK)
            mixed = jnp.dot(ws, v[rws, cols], preferred_element_type=F32) + bias
            o_ref[rws, cols] = (u_ref[rws, cols].astype(F32) * mixed).astype(BF16)


def _mix_out_kernel(alpha, x_is_raw, a_ref, u_ref, v_ref, sga_ref, sgb_ref, x_ref, emb_g_ref,
                    emb_b_ref, sgu_g_ref, sgu_b_ref, ws_ref, bs_ref, woa_ref, wob_ref, wout_ref,
                    g_ref, beta_ref, cast_src_ref, o_ref, packed_ref, cast_dst_ref, b_ref):
    _spatial_gating(u_ref, v_ref, sgu_g_ref, sgu_b_ref, ws_ref, bs_ref, b_ref)
    ya = jnp.dot(a_ref[...], woa_ref[...], preferred_element_type=F32)
    yb = jnp.dot(b_ref[...], wob_ref[...], preferred_element_type=F32)
    merged = sga_ref[...].astype(F32) * ya + sgb_ref[...].astype(F32) * yb
    m = jnp.dot(merged.astype(BF16), wout_ref[...], preferred_element_type=F32)
    x = x_ref[...]
    if x_is_raw:
        x = _layer_norm(x, emb_g_ref[...], emb_b_ref[...])
    y = _layer_norm(alpha * x + m, g_ref[...], beta_ref[...])
    o_ref[...] = y
    packed_ref[...] = _pack_halves(y)
    cast_dst_ref[...] = cast_src_ref[...].astype(BF16)


def _mix_out(alpha, layer, a, uv_cols, sigmoid_cols, x, x_is_raw, emb_ln_g, emb_ln_b,
             sgu_ln_g, sgu_ln_b, w_s, b_s, w_o_rnn, w_o_sgu, w_out, ln_g, ln_b, cast):
    T, D = x.shape
    n_steps = T // MIX_ROWS
    row = lambda w, c: pl.BlockSpec((MIX_ROWS, w), lambda i: (i, c))
    cast_src, cast_dst, cast_shape = _cast_specs(*cast, n_steps, lambda i: i)
    return pl.pallas_call(
        functools.partial(_mix_out_kernel, alpha, x_is_raw),
        grid=(n_steps,),
        in_specs=[row(D_RNN, 0), row(D_SGU, 0), row(D_SGU, 1), row(D, 0), row(D, 1), row(D, 0),
                  _const_spec((1, D)), _const_spec((1, D)),
                  _const_spec((1, D_SGU)), _const_spec((1, D_SGU)),
                  _const_spec((SGU_GROUPS, CHUNK, CHUNK)), _const_spec((CHUNK, SGU_GROUPS)),
                  _layer_spec(layer, (D_RNN, D)), _layer_spec(layer, (D_SGU, D)),
                  _layer_spec(layer, (D, D)), _const_spec((1, D)), _const_spec((1, D)), cast_src],
        out_specs=[row(D, 0), row(D // 2, 0), cast_dst],
        out_shape=[jax.ShapeDtypeStruct((T, D), F32), jax.ShapeDtypeStruct((T, D // 2), jnp.uint32),
                   cast_shape],
        scratch_shapes=[pltpu.VMEM((MIX_ROWS, D_SGU), BF16)],
        compiler_params=_params(("parallel",), 56),
        name="mix_out",
    )(a, uv_cols, uv_cols, sigmoid_cols, sigmoid_cols, x,
      emb_ln_g.reshape(1, D), emb_ln_b.reshape(1, D),
      sgu_ln_g.reshape(1, D_SGU), sgu_ln_b.reshape(1, D_SGU), w_s, b_s.T,
      w_o_rnn, w_o_sgu, w_out, ln_g.reshape(1, D), ln_b.reshape(1, D), cast[0])


def _first_max(v, row, n):
    m = jnp.max(v, axis=0, keepdims=True)
    idx = jnp.min(jnp.where(v == m, row, float(n)), axis=0, keepdims=True)
    return m, idx


def _router_kernel(x_ref, wt_ref, b_ref, idx_ref, gate_ref, rank_ref, cnt_ref, run_ref):
    rows = x_ref.shape[0]

    @pl.when(pl.program_id(0) == 0)
    def _():
        run_ref[...] = jnp.zeros_like(run_ref)

    def nt_dot(w, x):
        return lax.dot_general(w, x, (((1,), (1,)), ((), ())), preferred_element_type=F32)

    x = x_ref[...]
    xh = x.astype(BF16)
    xl = (x - xh.astype(F32)).astype(BF16)
    w = wt_ref[...]
    wh = w.astype(BF16)
    wl = (w - wh.astype(F32)).astype(BF16)
    logits = nt_dot(wh, xh) + (nt_dot(wh, xl) + nt_dot(wl, xh))
    e = jnp.exp(logits - jnp.max(logits, axis=0, keepdims=True))
    scores = e / jnp.sum(e, axis=0, keepdims=True)
    sel = scores + b_ref[...]

    row = lax.broadcasted_iota(jnp.int32, (N_EXPERTS, rows), 0).astype(F32)
    grow = lax.broadcasted_iota(jnp.int32, (EXPERTS_PER_GROUP, rows), 0).astype(F32)
    neg_inf = float("-inf")
    best_score = None
    best_group = None
    for g in range(N_GROUPS):
        v = sel[g * EXPERTS_PER_GROUP:(g + 1) * EXPERTS_PER_GROUP, :]
        m1, i1 = _first_max(v, grow, EXPERTS_PER_GROUP)
        m2 = jnp.max(jnp.where(grow == i1, neg_inf, v), axis=0, keepdims=True)
        s = m1 + m2
        if g == 0:
            best_score, best_group = s, jnp.zeros_like(s)
        else:
            better = s > best_score
            best_group = jnp.where(better, float(g), best_group)
            best_score = jnp.where(better, s, best_score)

    lo = best_group * float(EXPERTS_PER_GROUP)
    in_group = jnp.logical_and(row >= lo, row < lo + float(EXPERTS_PER_GROUP))
    masked = jnp.where(in_group, sel, neg_inf)
    _, i1 = _first_max(masked, row, N_EXPERTS)
    pick1 = row == i1
    _, i2 = _first_max(jnp.where(pick1, neg_inf, masked), row, N_EXPERTS)
    pick2 = row == i2
    s1 = jnp.sum(jnp.where(pick1, scores, 0.0), axis=0, keepdims=True)
    s2 = jnp.sum(jnp.where(pick2, scores, 0.0), axis=0, keepdims=True)
    den = s1 + s2

    onehot = jnp.where(jnp.logical_or(pick1, pick2), 1.0, 0.0)
    before = (lax.broadcasted_iota(jnp.int32, (rows, rows), 0)
              < lax.broadcasted_iota(jnp.int32, (rows, rows), 1))
    prefix = jnp.dot(onehot.astype(BF16), jnp.where(before, 1.0, 0.0).astype(BF16),
                     preferred_element_type=F32)
    pos = prefix + run_ref[:, 0:1]
    r1 = jnp.sum(jnp.where(pick1, pos, 0.0), axis=0, keepdims=True)
    r2 = jnp.sum(jnp.where(pick2, pos, 0.0), axis=0, keepdims=True)
    run_ref[...] = run_ref[...] + jnp.sum(onehot, axis=1, keepdims=True)

    idx_ref[0:1, :] = i1.astype(jnp.int32)
    idx_ref[1:2, :] = i2.astype(jnp.int32)
    gate_ref[0:1, :] = s1 / den
    gate_ref[1:2, :] = s2 / den
    rank_ref[0:1, :] = r1.astype(jnp.int32)
    rank_ref[1:2, :] = r2.astype(jnp.int32)
    cnt_ref[...] = run_ref[...].astype(jnp.int32)


def _router(x, router_w, router_b):
    T, D = x.shape
    pair = pl.BlockSpec((TOP_K, ROUTER_ROWS), lambda i: (0, i))
    return pl.pallas_call(
        _router_kernel,
        grid=(T // ROUTER_ROWS,),
        in_specs=[pl.BlockSpec((ROUTER_ROWS, D), lambda i: (i, 0)),
                  _const_spec((N_EXPERTS, D)), _const_spec((N_EXPERTS, 1))],
        out_specs=[pair, pair, pair, pl.BlockSpec((N_EXPERTS, 128), lambda i: (0, 0))],
        out_shape=[jax.ShapeDtypeStruct((TOP_K, T), jnp.int32),
                   jax.ShapeDtypeStruct((TOP_K, T), F32),
                   jax.ShapeDtypeStruct((TOP_K, T), jnp.int32),
                   jax.ShapeDtypeStruct((N_EXPERTS, 128), jnp.int32)],
        scratch_shapes=[pltpu.VMEM((N_EXPERTS, 128), F32)],
        compiler_params=_params(("arbitrary",), 40),
        name="router",
    )(x, router_w.T, router_b.reshape(N_EXPERTS, 1))


def _dispatch_kernel(pad_from_ref, pad_n_ref, n_used_ref, dest_ref, x_ref, xs_hbm, zero_ref, sem, pad_sem):
    i = pl.program_id(0)
    rows = DISPATCH_ROWS
    n_blocks = xs_hbm.shape[0] // EXPERT_ROWS
    zero_rows = zero_ref.shape[0]

    def tail_copies(j):
        block = n_used_ref[0] + j
        off = pl.multiple_of(block * EXPERT_ROWS, EXPERT_ROWS)
        return block < n_blocks, [
            pltpu.make_async_copy(zero_ref, xs_hbm.at[pl.ds(off + part * zero_rows, zero_rows)], pad_sem)
            for part in range(EXPERT_ROWS // zero_rows)]

    def pad_copy(e, bit):
        n = pad_n_ref[e]
        if bit < ROW_TILE:
            off = pad_from_ref[e] + bit - 1
            return pltpu.make_async_copy(zero_ref.at[pl.ds(0, 1)], xs_hbm.at[pl.ds(off, 1)], pad_sem)
        done = (n & (ROW_TILE - 1)) + (n & ~(2 * bit - 1))
        off = pl.multiple_of(pad_from_ref[e] + done, ROW_TILE)
        return pltpu.make_async_copy(zero_ref.at[pl.ds(0, bit)], xs_hbm.at[pl.ds(off, bit)], pad_sem)

    def pad_needed(e, bit):
        n = pad_n_ref[e]
        if bit < ROW_TILE:
            return bit <= (n & (ROW_TILE - 1))
        return (n & bit) != 0

    bits = list(range(1, ROW_TILE)) + [
        1 << k for k in range(ROW_TILE.bit_length() - 1, EXPERT_ROWS.bit_length() - 1)]

    @pl.when(i == 0)
    def _():
        zero_ref[...] = jnp.zeros_like(zero_ref)
        for e in range(N_EXPERTS):
            for bit in bits:
                @pl.when(pad_needed(e, bit))
                def _():
                    pad_copy(e, bit).start()
        for j in range(N_EXPERTS):
            needed, copies = tail_copies(j)

            @pl.when(needed)
            def _():
                for c in copies:
                    c.start()

    for t in range(rows):
        src = x_ref.at[pl.ds(t, 1)]
        for k in range(TOP_K):
            pltpu.make_async_copy(src, xs_hbm.at[pl.ds(dest_ref[0, k * rows + t], 1)],
                                  sem).start(priority=k % N_DMA_PRIORITIES)
    for _ in range(TOP_K):
        pltpu.make_async_copy(x_ref, xs_hbm.at[pl.ds(0, rows)], sem).wait()

    @pl.when(i == 0)
    def _():
        for e in range(N_EXPERTS):
            for bit in bits:
                @pl.when(pad_needed(e, bit))
                def _():
                    pad_copy(e, bit).wait()
        for j in range(N_EXPERTS):
            needed, copies = tail_copies(j)

            @pl.when(needed)
            def _():
                for c in copies:
                    c.wait()


def _dispatch(x, dest_tiles, pad_from, pad_n, n_used, n_slots):
    T, D = x.shape
    grid_spec = pltpu.PrefetchScalarGridSpec(
        num_scalar_prefetch=3,
        grid=(T // DISPATCH_ROWS,),
        in_specs=[pl.BlockSpec((None, 1, TOP_K * DISPATCH_ROWS), lambda i, *_: (i, 0, 0),
                               memory_space=pltpu.SMEM),
                  pl.BlockSpec((DISPATCH_ROWS, D), lambda i, *_: (i, 0))],
        out_specs=pl.BlockSpec(memory_space=pl.ANY),
        scratch_shapes=[pltpu.VMEM((EXPERT_ROWS // 2, D), x.dtype),
                        pltpu.SemaphoreType.DMA(()), pltpu.SemaphoreType.DMA(())],
    )
    return pl.pallas_call(
        _dispatch_kernel,
        grid_spec=grid_spec,
        out_shape=jax.ShapeDtypeStruct((n_slots, D), x.dtype),
        compiler_params=_params(("arbitrary",), 16),
        name="dispatch",
    )(pad_from, pad_n, n_used, dest_tiles, x)


PLAN_EXPERT, PLAN_FIRST, PLAN_SLOT, PLAN_NEXT = range(4)


def _expert_kernel(plan_ref, n_used_ref, xs_ref, w1_hbm, w3_hbm, w2_hbm, *rest):
    *rest, w1_buf, w3_buf, w2_buf, sem = rest
    n_casts = len(rest) // 2
    o_ref = rest[n_casts]
    i = pl.program_id(0)
    used = i < n_used_ref[0]
    slot = plan_ref[PLAN_SLOT, i]

    def weight_copies(expert, into):
        return [pltpu.make_async_copy(hbm.at[expert], buf.at[into], sem.at[into, k])
                for k, (hbm, buf) in enumerate(((w1_hbm, w1_buf), (w3_hbm, w3_buf), (w2_hbm, w2_buf)))]

    @pl.when(i == 0)
    def _():
        for c in weight_copies(plan_ref[PLAN_EXPERT, 0], 0):
            c.start()

    @pl.when(plan_ref[PLAN_FIRST, i] == 1)
    def _():
        for c in weight_copies(plan_ref[PLAN_EXPERT, i], slot):
            c.wait()

        @pl.when(plan_ref[PLAN_NEXT, i] >= 0)
        def _():
            for c in weight_copies(plan_ref[PLAN_NEXT, i], 1 - slot):
                c.start()

    @pl.when(jnp.logical_not(used))
    def _():
        o_ref[...] = jnp.zeros_like(o_ref)

    @pl.when(used)
    def _():
        lo, hi = _unpack_halves(xs_ref[...])
        x = jnp.concatenate([lo.astype(BF16), hi.astype(BF16)], axis=1)
        h1 = jnp.dot(x, w1_buf[slot], preferred_element_type=F32)
        h3 = jnp.dot(x, w3_buf[slot], preferred_element_type=F32)
        hidden = (h1 * _sigmoid(h1)) * h3
        out = jnp.dot(hidden.astype(BF16), w2_buf[slot], preferred_element_type=F32)
        o_ref[...] = _pack_halves(out)
        for src_ref, dst_ref in zip(rest[:n_casts], rest[n_casts + 1:]):
            dst_ref[...] = src_ref[...].astype(BF16)


def _experts(xs, block_e, n_used, w1, w3, w2, casts=()):
    P, packed = xs.shape
    D = 2 * packed
    n_blocks = P // EXPERT_ROWS
    index = jnp.arange(n_blocks, dtype=jnp.int32)
    first = jnp.logical_and(index < n_used[0],
                            jnp.concatenate([jnp.ones((1,), bool), block_e[1:] != block_e[:-1]]))
    slot = (jnp.cumsum(first) - 1) % 2
    starts = jnp.where(first, index, n_blocks)
    next_start = jnp.concatenate([lax.cummin(starts, axis=0, reverse=True)[1:],
                                  jnp.full((1,), n_blocks, jnp.int32)])
    next_e = jnp.where(next_start < n_blocks, block_e[jnp.minimum(next_start, n_blocks - 1)], -1)
    plan = jnp.stack([block_e, first.astype(jnp.int32), slot.astype(jnp.int32),
                      next_e.astype(jnp.int32)])
    used_rows = pl.BlockSpec((EXPERT_ROWS, packed),
                             lambda i, plan, nu: (jnp.maximum(jnp.minimum(i, nu[0] - 1), 0), 0))
    rows = pl.BlockSpec((EXPERT_ROWS, packed), lambda i, plan, nu: (i, 0))
    cast_specs = [_cast_specs(stack, layer, n_blocks - N_EXPERTS, lambda i, plan, nu: i)
                  for stack, layer in casts]
    in_hbm = pl.BlockSpec(memory_space=pl.ANY)
    grid_spec = pltpu.PrefetchScalarGridSpec(
        num_scalar_prefetch=2,
        grid=(n_blocks,),
        in_specs=[used_rows, in_hbm, in_hbm, in_hbm] + [s[0] for s in cast_specs],
        out_specs=[rows] + [s[1] for s in cast_specs],
        scratch_shapes=[pltpu.VMEM((2, D, D_EXPERT), BF16), pltpu.VMEM((2, D, D_EXPERT), BF16),
                        pltpu.VMEM((2, D_EXPERT, D), BF16), pltpu.SemaphoreType.DMA((2, 3))],
    )
    out = pl.pallas_call(
        _expert_kernel,
        grid_spec=grid_spec,
        out_shape=[jax.ShapeDtypeStruct((P, packed), jnp.uint32)] + [s[2] for s in cast_specs],
        compiler_params=_params(("arbitrary",), 56),
        name="experts",
    )(plan, n_used, xs, w1, w3, w2, *[stack for stack, _ in casts])
    return out[0], out[1:]


def _combine_kernel(alpha, project, dest_ref, dest_next_ref, x_ref, gate_ref, g_ref, b_ref, y_hbm,
                    *rest):
    if project:
        w_ref, of_ref, ob_ref, oxr_ref, ogate_ref, buf_a, buf_b, sem = rest
    else:
        of_ref, ob_ref, buf_a, buf_b, sem = rest
    rows = COMBINE_ROWS
    half = rows // 2
    i = pl.program_id(0)

    def copy(table_ref, h, t, k, buf, s):
        return pltpu.make_async_copy(y_hbm.at[pl.ds(table_ref[0, k * rows + h * half + t], 1)],
                                     buf.at[k, pl.ds(t, 1)], sem.at[s])

    def gather(table_ref, h, buf, s):
        for t in range(half):
            for k in range(TOP_K):
                copy(table_ref, h, t, k, buf, s).start(priority=k % N_DMA_PRIORITIES)

    def wait(buf, s):
        for k in range(TOP_K):
            pltpu.make_async_copy(y_hbm.at[pl.ds(0, half)], buf.at[k], sem.at[s]).wait()

    def reduce(buf, h):
        r = slice(h * half, (h + 1) * half)
        lo0, hi0 = _unpack_halves(buf[0])
        lo1, hi1 = _unpack_halves(buf[1])
        g0 = gate_ref[r, 0:1]
        g1 = gate_ref[r, 1:2]
        f = jnp.concatenate([g0 * lo0 + g1 * lo1, g0 * hi0 + g1 * hi1], axis=1)
        y = _layer_norm(alpha * x_ref[r, :] + f, g_ref[...], b_ref[...])
        of_ref[r, :] = y
        yb = y.astype(BF16)
        ob_ref[r, :] = yb
        if project:
            _project_branch_a(yb, w_ref, oxr_ref, ogate_ref, rows=r)

    @pl.when(i == 0)
    def _():
        def issue(t, carry):
            for k in range(TOP_K):
                copy(dest_ref, 0, t, k, buf_a, 0).start()
            return carry

        lax.fori_loop(0, half, issue, 0, unroll=8)

    wait(buf_a, 0)
    gather(dest_ref, 1, buf_b, 1)
    reduce(buf_a, 0)
    wait(buf_b, 1)
    gather(dest_next_ref, 0, buf_a, 0)
    reduce(buf_b, 1)

    @pl.when(i == pl.num_programs(0) - 1)
    def _():
        wait(buf_a, 0)


def _combine(alpha, dest_tiles, x, gate_t, ln_g, ln_b, yb, next_w_in=None):
    T, D = x.shape
    n_tiles = T // COMBINE_ROWS
    row = lambda w: pl.BlockSpec((COMBINE_ROWS, w), lambda i: (i, 0))
    table = lambda index: pl.BlockSpec((None, 1, TOP_K * COMBINE_ROWS), index, memory_space=pltpu.SMEM)
    half_buf = pltpu.VMEM((TOP_K, COMBINE_ROWS // 2, yb.shape[1]), yb.dtype)
    project = next_w_in is not None
    in_specs = [table(lambda i: (i, 0, 0)),
                table(lambda i: (jnp.minimum(i + 1, n_tiles - 1), 0, 0)),
                row(D), pl.BlockSpec((COMBINE_ROWS, TOP_K), lambda i: (i, 0)),
                _const_spec((1, D)), _const_spec((1, D)),
                pl.BlockSpec(memory_space=pl.ANY)]
    out_specs = [row(D), row(D)]
    out_shape = [jax.ShapeDtypeStruct((T, D), F32), jax.ShapeDtypeStruct((T, D), BF16)]
    operands = [dest_tiles, dest_tiles, x, gate_t, ln_g.reshape(1, D), ln_b.reshape(1, D), yb]
    if project:
        in_specs.append(_const_spec((D, 2 * D_RNN)))
        out_specs += [row(D_RNN), row(D_RNN)]
        out_shape += [jax.ShapeDtypeStruct((T, D_RNN), BF16)] * 2
        operands.append(next_w_in)
    return pl.pallas_call(
        functools.partial(_combine_kernel, alpha, project),
        grid=(n_tiles,),
        in_specs=in_specs,
        out_specs=out_specs,
        out_shape=out_shape,
        scratch_shapes=[half_buf, half_buf, pltpu.SemaphoreType.DMA((2,))],
        compiler_params=_params(("arbitrary",), 52),
        name="combine",
    )(*operands)


def _tile_pairs(dest, rows):
    T = dest.shape[1]
    return dest.reshape(TOP_K, T // rows, rows).transpose(1, 0, 2).reshape(T // rows, 1, TOP_K * rows)


def _moe(alpha, x, x_packed, router_w, router_b, w1, w3, w2, ln_g, ln_b, casts=()):
    T, D = x.shape
    n_blocks = (T * TOP_K) // EXPERT_ROWS + N_EXPERTS
    idx, gate, rank, cnt = _router(x, router_w, router_b)

    counts = cnt[:, 0]
    padded = (counts + EXPERT_ROWS - 1) // EXPERT_ROWS * EXPERT_ROWS
    pad_end = jnp.cumsum(padded)
    pad_start = pad_end - padded
    n_used = (pad_end[-1] // EXPERT_ROWS).astype(jnp.int32)
    block_start = jnp.minimum(jnp.arange(n_blocks, dtype=jnp.int32), n_used - 1) * EXPERT_ROWS
    block_e = jnp.minimum(jnp.sum(block_start[:, None] >= pad_end[None, :], axis=1),
                          N_EXPERTS - 1).astype(jnp.int32)
    expert_ids = jnp.arange(N_EXPERTS, dtype=jnp.int32)[:, None, None]
    dest = jnp.sum(jnp.where(idx[None] == expert_ids, pad_start[:, None, None], 0), axis=0) + rank

    n_used = n_used.reshape(1)
    xs = _dispatch(x_packed, _tile_pairs(dest, DISPATCH_ROWS), (pad_start + counts).astype(jnp.int32),
                   (padded - counts).astype(jnp.int32), n_used, n_blocks * EXPERT_ROWS)
    yb, casted = _experts(xs, block_e, n_used, w1, w3, w2, casts)
    new = _combine(alpha, _tile_pairs(dest, COMBINE_ROWS), x, gate.T, ln_g, ln_b, yb,
                   next_w_in=casted[0] if casted else None)
    return new, casted


def kernel(x, emb_ln_g, emb_ln_b, w_in, conv_w, conv_b, lru_w_a, lru_b_a, lru_w_x, lru_b_x, lru_lambda, w_o_rnn, sgu_ln_g, sgu_ln_b, sgu_w_s, sgu_b_s, w_o_sgu, w_out, ln1_g, ln1_b, router_w, router_b, expert_w1, expert_w3, expert_w2, ln2_g, ln2_b):
    batch, seq, D = x.shape
    depth = w_in.shape[0]
    alpha = float((2 * depth) ** 0.25)
    xf = x.reshape(batch * seq, D)
    mixer_f32 = (w_in, w_o_rnn, w_o_sgu, w_out)
    w_in, w_o_rnn, w_o_sgu, w_out = (w[0:1].astype(BF16) for w in mixer_f32)
    xb, x_rnn, gate = _emb_ln(xf, emb_ln_g, emb_ln_b, w_in)
    rows_of = lambda w: w.reshape(depth, N_EXPERTS * w.shape[2], w.shape[3])
    w1_rows, w3_rows, w2_rows = rows_of(expert_w1), rows_of(expert_w3), rows_of(expert_w2)
    uv_from = 2 * D_RNN
    sigmoid_from = 2 * D_RNN + 2 * D_SGU
    for l in range(depth):
        uv_cols, w3 = _proj(xb, w_in, 0, uv_from, sigmoid_from - uv_from, _gelu, "proj_gelu",
                            cast=(w3_rows, l))
        sigmoid_cols, w1, a = _proj_lru(
            xb, w_in, 0, sigmoid_from, D_IN - sigmoid_from, _sigmoid, "proj_sigmoid_lru",
            (w1_rows, l), x_rnn, gate, seq, conv_w[l], conv_b[l], lru_w_a[l], lru_b_a[l],
            lru_w_x[l], lru_b_x[l], lru_lambda[l])
        x1, x1_packed, w2 = _mix_out(alpha, 0, a, uv_cols, sigmoid_cols, xf, l == 0, emb_ln_g,
                                     emb_ln_b, sgu_ln_g[l], sgu_ln_b[l], sgu_w_s[l], sgu_b_s[l],
                                     w_o_rnn, w_o_sgu, w_out, ln1_g[l], ln1_b[l], cast=(w2_rows, l))
        next_mixer = [(w, l + 1) for w in mixer_f32] if l + 1 < depth else []
        (xf, xb, *branch_a), casted = _moe(
            alpha, x1, x1_packed, router_w, router_b, w1.reshape(expert_w1.shape[1:]),
            w3.reshape(expert_w3.shape[1:]), w2.reshape(expert_w2.shape[1:]), ln2_g[l], ln2_b[l],
            casts=next_mixer)
        if casted:
            x_rnn, gate = branch_a
            w_in, w_o_rnn, w_o_sgu, w_out = (w[None] for w in casted)
    return xf.reshape(batch, seq, D)
```

```python
import functools

import jax
import jax.numpy as jnp
import numpy as np
from jax import lax
from jax.experimental import pallas as pl
from jax.experimental.pallas import tpu as pltpu

F32 = jnp.float32
BF16 = jnp.bfloat16

D_MODEL = 2048
D_RNN = 1024
RNN_HEADS = 8
HEAD_DIM = D_RNN // RNN_HEADS
CONV_WIDTH = 4
LRU_C = 8.0
D_SGU = 1024
SGU_GROUPS = 8
SGU_GROUP_DIM = D_SGU // SGU_GROUPS
CHUNK = 128
N_EXPERTS = 16
N_GROUPS = 4
EXPERTS_PER_GROUP = N_EXPERTS // N_GROUPS
TOP_K = 2
D_EXPERT = 1408
D_IN = 2 * D_RNN + 2 * D_SGU + 2 * D_MODEL
LN_EPS = 1e-5
SQRT_2_OVER_PI = float(np.sqrt(2.0 / np.pi))

LN_ROWS = 512
PROJ_ROWS = 1024
PROJ_COLS = 1024
MIX_ROWS = 256
ROUTER_ROWS = 1024
DISPATCH_ROWS = 512
EXPERT_ROWS = 256
COMBINE_ROWS = 512
CONV_PAD = 8
N_DMA_PRIORITIES = 2
ROW_TILE = 8

MIB = 1024 * 1024


def _params(semantics, vmem_mib):
    return pltpu.CompilerParams(dimension_semantics=semantics,
                                vmem_limit_bytes=vmem_mib * MIB,
                                disable_bounds_checks=True)


def _const_spec(shape):
    zeros = (0,) * len(shape)
    return pl.BlockSpec(shape, lambda *_: zeros, pipeline_mode=pl.Buffered(1))


def _layer_spec(layer, shape):
    zeros = (0,) * len(shape)
    return pl.BlockSpec((None,) + tuple(shape), lambda *_: (layer,) + zeros,
                        pipeline_mode=pl.Buffered(1))


def _layer_norm(v, g, b):
    mu = jnp.mean(v, axis=-1, keepdims=True)
    d = v - mu
    var = jnp.mean(d * d, axis=-1, keepdims=True)
    return d * lax.rsqrt(var + LN_EPS) * g + b


def _gelu(x):
    inner = x * (SQRT_2_OVER_PI + (SQRT_2_OVER_PI * 0.044715) * (x * x))
    return x * (0.5 * jnp.tanh(inner) + 0.5)


def _pack_halves(y):
    n = y.shape[1] // 2
    bits = lambda v: lax.bitcast_convert_type(v.astype(BF16).astype(F32), jnp.uint32)
    return (bits(y[:, :n]) >> 16) | bits(y[:, n:])


def _unpack_halves(p):
    lo = lax.bitcast_convert_type(p << 16, F32)
    hi = lax.bitcast_convert_type(p & jnp.uint32(0xFFFF0000), F32)
    return lo, hi


def _sigmoid(x):
    return 0.5 * jnp.tanh(0.5 * x) + 0.5


def _project_branch_a(xb, w_ref, oxr_ref, ogate_ref, rows=slice(None)):
    acc = jnp.dot(xb, w_ref[...], preferred_element_type=F32)
    oxr_ref[rows, :] = acc[:, :D_RNN].astype(BF16)
    ogate_ref[rows, :] = _gelu(acc[:, D_RNN:]).astype(BF16)


def _emb_ln_kernel(x_ref, g_ref, b_ref, w_ref, ob_ref, oxr_ref, ogate_ref):
    xb = _layer_norm(x_ref[...], g_ref[...], b_ref[...]).astype(BF16)
    ob_ref[...] = xb
    _project_branch_a(xb, w_ref, oxr_ref, ogate_ref)


def _emb_ln(x, g, b, w_in):
    T, D = x.shape
    row = lambda w: pl.BlockSpec((LN_ROWS, w), lambda i: (i, 0))
    half = jax.ShapeDtypeStruct((T, D_RNN), BF16)
    return pl.pallas_call(
        _emb_ln_kernel,
        grid=(T // LN_ROWS,),
        in_specs=[row(D), _const_spec((1, D)), _const_spec((1, D)), _layer_spec(0, (D, 2 * D_RNN))],
        out_specs=[row(D), row(D_RNN), row(D_RNN)],
        out_shape=[jax.ShapeDtypeStruct((T, D), BF16), half, half],
        compiler_params=_params(("parallel",), 40),
        name="emb_ln",
    )(x, g.reshape(1, D), b.reshape(1, D), w_in)


def _cast_specs(stack, layer, n_steps, step_of):
    _, rows, cols = stack.shape
    slab = -(-rows // (n_steps * 16)) * 16
    last = -(-rows // slab) - 1
    src = pl.BlockSpec((None, slab, cols), lambda *g: (layer, jnp.minimum(step_of(*g), last), 0))
    dst = pl.BlockSpec((slab, cols), lambda *g: (jnp.minimum(step_of(*g), last), 0))
    return src, dst, jax.ShapeDtypeStruct((rows, cols), BF16)


def _row_tile(x_hbm, x_buf, sem):
    i = pl.program_id(0)
    j = pl.program_id(1)

    def copy(tile, slot):
        rows = pl.ds(pl.multiple_of(tile * PROJ_ROWS, PROJ_ROWS), PROJ_ROWS)
        return pltpu.make_async_copy(x_hbm.at[rows], x_buf.at[slot], sem.at[slot])

    @pl.when(jnp.logical_and(i == 0, j == 0))
    def _():
        copy(0, 0).start()

    @pl.when(j == 0)
    def _():
        copy(i, i % 2).wait()

        @pl.when(i + 1 < pl.num_programs(0))
        def _():
            copy(i + 1, (i + 1) % 2).start()

    return x_buf[i % 2]


def _row_tile_scratch(D):
    return [pltpu.VMEM((2, PROJ_ROWS, D), BF16), pltpu.SemaphoreType.DMA((2,))]


def _proj_kernel(act, x_hbm, w_ref, cast_src_ref, o_ref, cast_dst_ref, x_buf, sem):
    acc = jnp.dot(_row_tile(x_hbm, x_buf, sem), w_ref[...], preferred_element_type=F32)
    o_ref[...] = act(acc).astype(BF16)
    cast_dst_ref[...] = cast_src_ref[...].astype(BF16)


def _proj(xb, w_in, layer, col_from, n_cols, act, name, cast):
    T, D = xb.shape
    first = col_from // PROJ_COLS
    grid = (T // PROJ_ROWS, n_cols // PROJ_COLS)
    src, dst, cast_shape = _cast_specs(*cast, grid[0] * grid[1], lambda i, j: i * grid[1] + j)
    return pl.pallas_call(
        functools.partial(_proj_kernel, act),
        grid=grid,
        in_specs=[pl.BlockSpec(memory_space=pl.ANY),
                  pl.BlockSpec((None, D, PROJ_COLS), lambda i, j: (layer, 0, first + j)), src],
        out_specs=[pl.BlockSpec((PROJ_ROWS, PROJ_COLS), lambda i, j: (i, j)), dst],
        out_shape=[jax.ShapeDtypeStruct((T, n_cols), BF16), cast_shape],
        scratch_shapes=_row_tile_scratch(D),
        compiler_params=_params(("arbitrary", "arbitrary"), 48),
        name=name,
    )(xb, w_in, cast[0])


def _lru_gates(first, xr_ref, cw_ref, cb_ref, wa_ref, ba_ref, wx_ref, bx_ref, lam_ref,
               xpad_ref, hc_ref, a_ref, u_ref):
    rows = xr_ref.shape[0]

    @pl.when(first)
    def _():
        xpad_ref[0:CONV_PAD, :] = jnp.zeros((CONV_PAD, D_RNN), F32)
        hc_ref[...] = jnp.zeros_like(hc_ref)

    x = xr_ref[...].astype(F32)
    xpad_ref[CONV_PAD:CONV_PAD + rows, :] = x
    xc = cb_ref[...] + cw_ref[CONV_WIDTH - 1:CONV_WIDTH, :] * x
    for k in range(CONV_WIDTH - 1):
        shift = CONV_WIDTH - 1 - k
        xc = xc + cw_ref[k:k + 1, :] * xpad_ref[CONV_PAD - shift:CONV_PAD - shift + rows, :]
    xpad_ref[0:CONV_PAD, :] = x[rows - CONV_PAD:rows, :]

    lam = lam_ref[...]
    sp = jnp.maximum(-lam, 0.0) + jnp.log1p(jnp.exp(-jnp.abs(lam)))
    decay = (-0.5 * LRU_C) * sp
    for h in range(RNN_HEADS):
        cols = slice(h * HEAD_DIM, (h + 1) * HEAD_DIM)
        xh = xc[:, cols]
        xhb = xh.astype(BF16)
        tanh_r = jnp.tanh(jnp.dot(xhb, wa_ref[h], preferred_element_type=F32) + ba_ref[:, cols])
        tanh_i = jnp.tanh(jnp.dot(xhb, wx_ref[h], preferred_element_type=F32) + bx_ref[:, cols])
        gi = 0.5 * tanh_i + 0.5
        log_a = decay[:, cols] * tanh_r + decay[:, cols]
        a_ref[:, cols] = jnp.exp(log_a)
        th = jnp.tanh(log_a)
        u_ref[:, cols] = jnp.sqrt(-2.0 * th / (1.0 - th)) * (gi * xh)


def _lru_scan(gate_ref, o_ref, hc_ref, a_ref, u_ref, h_ref):
    def step(t, h):
        h = a_ref[pl.ds(t, 1), :] * h + u_ref[pl.ds(t, 1), :]
        h_ref[pl.ds(t, 1), :] = h
        return h

    hc_ref[0:1, :] = lax.fori_loop(0, a_ref.shape[0], step, hc_ref[0:1, :], unroll=8)
    o_ref[...] = (h_ref[...] * gate_ref[...].astype(F32)).astype(BF16)


def _proj_lru_kernel(act, tiles_per_seq, x_hbm, w_ref, cast_src_ref, xr_ref, gate_ref, cw_ref, cb_ref,
                     wa_ref, ba_ref, wx_ref, bx_ref, lam_ref, o_ref, cast_dst_ref, a_out_ref,
                     xpad_ref, hc_ref, a_ref, u_ref, h_ref, x_buf, sem):
    step = pl.program_id(0) * pl.num_programs(1) + pl.program_id(1)
    _lru_gates(step % tiles_per_seq == 0, xr_ref, cw_ref, cb_ref, wa_ref, ba_ref, wx_ref, bx_ref,
               lam_ref, xpad_ref, hc_ref, a_ref, u_ref)
    acc = jnp.dot(_row_tile(x_hbm, x_buf, sem), w_ref[...], preferred_element_type=F32)
    o_ref[...] = act(acc).astype(BF16)
    cast_dst_ref[...] = cast_src_ref[...].astype(BF16)
    _lru_scan(gate_ref, a_out_ref, hc_ref, a_ref, u_ref, h_ref)


def _proj_lru(xb, w_in, layer, col_from, n_cols, act, name, cast, x_rnn, gate, seq,
              conv_w, conv_b, w_a, b_a, w_x, b_x, lam):
    T, D = xb.shape
    first = col_from // PROJ_COLS
    grid = (T // PROJ_ROWS, n_cols // PROJ_COLS)
    n_steps = grid[0] * grid[1]
    step_of = lambda i, j: i * grid[1] + j
    lru_rows = T // n_steps
    src, dst, cast_shape = _cast_specs(*cast, n_steps, step_of)
    tile = pl.BlockSpec((lru_rows, D_RNN), lambda i, j: (step_of(i, j), 0))
    vec = _const_spec((1, D_RNN))
    gate_w = _const_spec((RNN_HEADS, HEAD_DIM, HEAD_DIM))
    tile_f32 = pltpu.VMEM((lru_rows, D_RNN), F32)
    return pl.pallas_call(
        functools.partial(_proj_lru_kernel, act, seq // lru_rows),
        grid=grid,
        in_specs=[pl.BlockSpec(memory_space=pl.ANY),
                  pl.BlockSpec((None, D, PROJ_COLS), lambda i, j: (layer, 0, first + j)),
                  src, tile, tile, _const_spec((CONV_WIDTH, D_RNN)), vec, gate_w, vec, gate_w, vec, vec],
        out_specs=[pl.BlockSpec((PROJ_ROWS, PROJ_COLS), lambda i, j: (i, j)), dst, tile],
        out_shape=[jax.ShapeDtypeStruct((T, n_cols), BF16), cast_shape,
                   jax.ShapeDtypeStruct((T, D_RNN), BF16)],
        scratch_shapes=[pltpu.VMEM((CONV_PAD + lru_rows, D_RNN), F32), pltpu.VMEM((8, D_RNN), F32),
                        tile_f32, tile_f32, tile_f32] + _row_tile_scratch(D),
        compiler_params=_params(("arbitrary", "arbitrary"), 52),
        name=name,
    )(xb, w_in, cast[0], x_rnn, gate, conv_w, conv_b.reshape(1, D_RNN), (0.5 * w_a).astype(BF16),
      0.5 * b_a.reshape(1, D_RNN), (0.5 * w_x).astype(BF16), 0.5 * b_x.reshape(1, D_RNN),
      lam.reshape(1, D_RNN))


def _spatial_gating(u_ref, v_ref, g_ref, b_ref, ws_ref, bs_ref, o_ref):
    rows = u_ref.shape[0]
    v = _layer_norm(v_ref[...].astype(F32), g_ref[...], b_ref[...]).astype(BF16)
    t_out = lax.broadcasted_iota(jnp.int32, (CHUNK, CHUNK), 0)
    t_in = lax.broadcasted_iota(jnp.int32, (CHUNK, CHUNK), 1)
    causal = t_in <= t_out
    for g in range(SGU_GROUPS):
        cols = slice(g * SGU_GROUP_DIM, (g + 1) * SGU_GROUP_DIM)
        ws = jnp.where(causal, ws_ref[g], 0.0).astype(BF16)
        bias = bs_ref[:, g:g + 1]
        for c in range(rows // CHUNK):
            rws = slice(c * CHUNK, (c + 1) * CHUNK)
            mixed = jnp.dot(ws, v[rws, cols], preferred_element_type=F32) + bias
            o_ref[rws, cols] = (u_ref[rws, cols].astype(F32) * mixed).astype(BF16)


def _mix_out_kernel(alpha, x_is_raw, a_ref, u_ref, v_ref, sga_ref, sgb_ref, x_ref, emb_g_ref,
                    emb_b_ref, sgu_g_ref, sgu_b_ref, ws_ref, bs_ref, woa_ref, wob_ref, wout_ref,
                    g_ref, beta_ref, cast_src_ref, o_ref, packed_ref, cast_dst_ref, b_ref):
    _spatial_gating(u_ref, v_ref, sgu_g_ref, sgu_b_ref, ws_ref, bs_ref, b_ref)
    ya = jnp.dot(a_ref[...], woa_ref[...], preferred_element_type=F32)
    yb = jnp.dot(b_ref[...], wob_ref[...], preferred_element_type=F32)
    merged = sga_ref[...].astype(F32) * ya + sgb_ref[...].astype(F32) * yb
    m = jnp.dot(merged.astype(BF16), wout_ref[...], preferred_element_type=F32)
    x = x_ref[...]
    if x_is_raw:
        x = _layer_norm(x, emb_g_ref[...], emb_b_ref[...])
    y = _layer_norm(alpha * x + m, g_ref[...], beta_ref[...])
    o_ref[...] = y
    packed_ref[...] = _pack_halves(y)
    cast_dst_ref[...] = cast_src_ref[...].astype(BF16)


def _mix_out(alpha, layer, a, uv_cols, sigmoid_cols, x, x_is_raw, emb_ln_g, emb_ln_b,
             sgu_ln_g, sgu_ln_b, w_s, b_s, w_o_rnn, w_o_sgu, w_out, ln_g, ln_b, cast):
    T, D = x.shape
    n_steps = T // MIX_ROWS
    row = lambda w, c: pl.BlockSpec((MIX_ROWS, w), lambda i: (i, c))
    cast_src, cast_dst, cast_shape = _cast_specs(*cast, n_steps, lambda i: i)
    return pl.pallas_call(
        functools.partial(_mix_out_kernel, alpha, x_is_raw),
        grid=(n_steps,),
        in_specs=[row(D_RNN, 0), row(D_SGU, 0), row(D_SGU, 1), row(D, 0), row(D, 1), row(D, 0),
                  _const_spec((1, D)), _const_spec((1, D)),
                  _const_spec((1, D_SGU)), _const_spec((1, D_SGU)),
                  _const_spec((SGU_GROUPS, CHUNK, CHUNK)), _const_spec((CHUNK, SGU_GROUPS)),
                  _layer_spec(layer, (D_RNN, D)), _layer_spec(layer, (D_SGU, D)),
                  _layer_spec(layer, (D, D)), _const_spec((1, D)), _const_spec((1, D)), cast_src],
        out_specs=[row(D, 0), row(D // 2, 0), cast_dst],
        out_shape=[jax.ShapeDtypeStruct((T, D), F32), jax.ShapeDtypeStruct((T, D // 2), jnp.uint32),
                   cast_shape],
        scratch_shapes=[pltpu.VMEM((MIX_ROWS, D_SGU), BF16)],
        compiler_params=_params(("parallel",), 56),
        name="mix_out",
    )(a, uv_cols, uv_cols, sigmoid_cols, sigmoid_cols, x,
      emb_ln_g.reshape(1, D), emb_ln_b.reshape(1, D),
      sgu_ln_g.reshape(1, D_SGU), sgu_ln_b.reshape(1, D_SGU), w_s, b_s.T,
      w_o_rnn, w_o_sgu, w_out, ln_g.reshape(1, D), ln_b.reshape(1, D), cast[0])


def _first_max(v, row, n):
    m = jnp.max(v, axis=0, keepdims=True)
    idx = jnp.min(jnp.where(v == m, row, float(n)), axis=0, keepdims=True)
    return m, idx


def _router_kernel(x_ref, wt_ref, b_ref, idx_ref, gate_ref, rank_ref, cnt_ref, run_ref):
    rows = x_ref.shape[0]

    @pl.when(pl.program_id(0) == 0)
    def _():
        run_ref[...] = jnp.zeros_like(run_ref)

    def nt_dot(w, x):
        return lax.dot_general(w, x, (((1,), (1,)), ((), ())), preferred_element_type=F32)

    x = x_ref[...]
    xh = x.astype(BF16)
    xl = (x - xh.astype(F32)).astype(BF16)
    w = wt_ref[...]
    wh = w.astype(BF16)
    wl = (w - wh.astype(F32)).astype(BF16)
    logits = nt_dot(wh, xh) + (nt_dot(wh, xl) + nt_dot(wl, xh))
    e = jnp.exp(logits - jnp.max(logits, axis=0, keepdims=True))
    scores = e / jnp.sum(e, axis=0, keepdims=True)
    sel = scores + b_ref[...]

    row = lax.broadcasted_iota(jnp.int32, (N_EXPERTS, rows), 0).astype(F32)
    grow = lax.broadcasted_iota(jnp.int32, (EXPERTS_PER_GROUP, rows), 0).astype(F32)
    neg_inf = float("-inf")
    best_score = None
    best_group = None
    for g in range(N_GROUPS):
        v = sel[g * EXPERTS_PER_GROUP:(g + 1) * EXPERTS_PER_GROUP, :]
        m1, i1 = _first_max(v, grow, EXPERTS_PER_GROUP)
        m2 = jnp.max(jnp.where(grow == i1, neg_inf, v), axis=0, keepdims=True)
        s = m1 + m2
        if g == 0:
            best_score, best_group = s, jnp.zeros_like(s)
        else:
            better = s > best_score
            best_group = jnp.where(better, float(g), best_group)
            best_score = jnp.where(better, s, best_score)

    lo = best_group * float(EXPERTS_PER_GROUP)
    in_group = jnp.logical_and(row >= lo, row < lo + float(EXPERTS_PER_GROUP))
    masked = jnp.where(in_group, sel, neg_inf)
    _, i1 = _first_max(masked, row, N_EXPERTS)
    pick1 = row == i1
    _, i2 = _first_max(jnp.where(pick1, neg_inf, masked), row, N_EXPERTS)
    pick2 = row == i2
    s1 = jnp.sum(jnp.where(pick1, scores, 0.0), axis=0, keepdims=True)
    s2 = jnp.sum(jnp.where(pick2, scores, 0.0), axis=0, keepdims=True)
    den = s1 + s2

    onehot = jnp.where(jnp.logical_or(pick1, pick2), 1.0, 0.0)
    before = (lax.broadcasted_iota(jnp.int32, (rows, rows), 0)
              < lax.broadcasted_iota(jnp.int32, (rows, rows), 1))
    prefix = jnp.dot(onehot.astype(BF16), jnp.where(before, 1.0, 0.0).astype(BF16),
                     preferred_element_type=F32)
    pos = prefix + run_ref[:, 0:1]
    r1 = jnp.sum(jnp.where(pick1, pos, 0.0), axis=0, keepdims=True)
    r2 = jnp.sum(jnp.where(pick2, pos, 0.0), axis=0, keepdims=True)
    run_ref[...] = run_ref[...] + jnp.sum(onehot, axis=1, keepdims=True)

    idx_ref[0:1, :] = i1.astype(jnp.int32)
    idx_ref[1:2, :] = i2.astype(jnp.int32)
    gate_ref[0:1, :] = s1 / den
    gate_ref[1:2, :] = s2 / den
    rank_ref[0:1, :] = r1.astype(jnp.int32)
    rank_ref[1:2, :] = r2.astype(jnp.int32)
    cnt_ref[...] = run_ref[...].astype(jnp.int32)


def _router(x, router_w, router_b):
    T, D = x.shape
    pair = pl.BlockSpec((TOP_K, ROUTER_ROWS), lambda i: (0, i))
    return pl.pallas_call(
        _router_kernel,
        grid=(T // ROUTER_ROWS,),
        in_specs=[pl.BlockSpec((ROUTER_ROWS, D), lambda i: (i, 0)),
                  _const_spec((N_EXPERTS, D)), _const_spec((N_EXPERTS, 1))],
        out_specs=[pair, pair, pair, pl.BlockSpec((N_EXPERTS, 128), lambda i: (0, 0))],
        out_shape=[jax.ShapeDtypeStruct((TOP_K, T), jnp.int32),
                   jax.ShapeDtypeStruct((TOP_K, T), F32),
                   jax.ShapeDtypeStruct((TOP_K, T), jnp.int32),
                   jax.ShapeDtypeStruct((N_EXPERTS, 128), jnp.int32)],
        scratch_shapes=[pltpu.VMEM((N_EXPERTS, 128), F32)],
        compiler_params=_params(("arbitrary",), 40),
        name="router",
    )(x, router_w.T, router_b.reshape(N_EXPERTS, 1))


def _dispatch_kernel(pad_from_ref, pad_n_ref, n_used_ref, dest_ref, x_ref, xs_hbm, zero_ref, sem, pad_sem):
    i = pl.program_id(0)
    rows = DISPATCH_ROWS
    n_blocks = xs_hbm.shape[0] // EXPERT_ROWS
    zero_rows = zero_ref.shape[0]

    def tail_copies(j):
        block = n_used_ref[0] + j
        off = pl.multiple_of(block * EXPERT_ROWS, EXPERT_ROWS)
        return block < n_blocks, [
            pltpu.make_async_copy(zero_ref, xs_hbm.at[pl.ds(off + part * zero_rows, zero_rows)], pad_sem)
            for part in range(EXPERT_ROWS // zero_rows)]

    def pad_copy(e, bit):
        n = pad_n_ref[e]
        if bit < ROW_TILE:
            off = pad_from_ref[e] + bit - 1
            return pltpu.make_async_copy(zero_ref.at[pl.ds(0, 1)], xs_hbm.at[pl.ds(off, 1)], pad_sem)
        done = (n & (ROW_TILE - 1)) + (n & ~(2 * bit - 1))
        off = pl.multiple_of(pad_from_ref[e] + done, ROW_TILE)
        return pltpu.make_async_copy(zero_ref.at[pl.ds(0, bit)], xs_hbm.at[pl.ds(off, bit)], pad_sem)

    def pad_needed(e, bit):
        n = pad_n_ref[e]
        if bit < ROW_TILE:
            return bit <= (n & (ROW_TILE - 1))
        return (n & bit) != 0

    bits = list(range(1, ROW_TILE)) + [
        1 << k for k in range(ROW_TILE.bit_length() - 1, EXPERT_ROWS.bit_length() - 1)]

    @pl.when(i == 0)
    def _():
        zero_ref[...] = jnp.zeros_like(zero_ref)
        for e in range(N_EXPERTS):
            for bit in bits:
                @pl.when(pad_needed(e, bit))
                def _():
                    pad_copy(e, bit).start()
        for j in range(N_EXPERTS):
            needed, copies = tail_copies(j)

            @pl.when(needed)
            def _():
                for c in copies:
                    c.start()

    for t in range(rows):
        src = x_ref.at[pl.ds(t, 1)]
        for k in range(TOP_K):
            pltpu.make_async_copy(src, xs_hbm.at[pl.ds(dest_ref[0, k * rows + t], 1)],
                                  sem).start(priority=k % N_DMA_PRIORITIES)
    for _ in range(TOP_K):
        pltpu.make_async_copy(x_ref, xs_hbm.at[pl.ds(0, rows)], sem).wait()

    @pl.when(i == 0)
    def _():
        for e in range(N_EXPERTS):
            for bit in bits:
                @pl.when(pad_needed(e, bit))
                def _():
                    pad_copy(e, bit).wait()
        for j in range(N_EXPERTS):
            needed, copies = tail_copies(j)

            @pl.when(needed)
            def _():
                for c in copies:
                    c.wait()


def _dispatch(x, dest_tiles, pad_from, pad_n, n_used, n_slots):
    T, D = x.shape
    grid_spec = pltpu.PrefetchScalarGridSpec(
        num_scalar_prefetch=3,
        grid=(T // DISPATCH_ROWS,),
        in_specs=[pl.BlockSpec((None, 1, TOP_K * DISPATCH_ROWS), lambda i, *_: (i, 0, 0),
                               memory_space=pltpu.SMEM),
                  pl.BlockSpec((DISPATCH_ROWS, D), lambda i, *_: (i, 0))],
        out_specs=pl.BlockSpec(memory_space=pl.ANY),
        scratch_shapes=[pltpu.VMEM((EXPERT_ROWS // 2, D), x.dtype),
                        pltpu.SemaphoreType.DMA(()), pltpu.SemaphoreType.DMA(())],
    )
    return pl.pallas_call(
        _dispatch_kernel,
        grid_spec=grid_spec,
        out_shape=jax.ShapeDtypeStruct((n_slots, D), x.dtype),
        compiler_params=_params(("arbitrary",), 16),
        name="dispatch",
    )(pad_from, pad_n, n_used, dest_tiles, x)


PLAN_EXPERT, PLAN_FIRST, PLAN_SLOT, PLAN_NEXT = range(4)


def _expert_kernel(plan_ref, n_used_ref, xs_ref, w1_hbm, w3_hbm, w2_hbm, *rest):
    *rest, w1_buf, w3_buf, w2_buf, sem = rest
    n_casts = len(rest) // 2
    o_ref = rest[n_casts]
    i = pl.program_id(0)
    used = i < n_used_ref[0]
    slot = plan_ref[PLAN_SLOT, i]

    def weight_copies(expert, into):
        return [pltpu.make_async_copy(hbm.at[expert], buf.at[into], sem.at[into, k])
                for k, (hbm, buf) in enumerate(((w1_hbm, w1_buf), (w3_hbm, w3_buf), (w2_hbm, w2_buf)))]

    @pl.when(i == 0)
    def _():
        for c in weight_copies(plan_ref[PLAN_EXPERT, 0], 0):
            c.start()

    @pl.when(plan_ref[PLAN_FIRST, i] == 1)
    def _():
        for c in weight_copies(plan_ref[PLAN_EXPERT, i], slot):
            c.wait()

        @pl.when(plan_ref[PLAN_NEXT, i] >= 0)
        def _():
            for c in weight_copies(plan_ref[PLAN_NEXT, i], 1 - slot):
                c.start()

    @pl.when(jnp.logical_not(used))
    def _():
        o_ref[...] = jnp.zeros_like(o_ref)

    @pl.when(used)
    def _():
        lo, hi = _unpack_halves(xs_ref[...])
        x = jnp.concatenate([lo.astype(BF16), hi.astype(BF16)], axis=1)
        h1 = jnp.dot(x, w1_buf[slot], preferred_element_type=F32)
        h3 = jnp.dot(x, w3_buf[slot], preferred_element_type=F32)
        hidden = (h1 * _sigmoid(h1)) * h3
        out = jnp.dot(hidden.astype(BF16), w2_buf[slot], preferred_element_type=F32)
        o_ref[...] = _pack_halves(out)
        for src_ref, dst_ref in zip(rest[:n_casts], rest[n_casts + 1:]):
            dst_ref[...] = src_ref[...].astype(BF16)


def _experts(xs, block_e, n_used, w1, w3, w2, casts=()):
    P, packed = xs.shape
    D = 2 * packed
    n_blocks = P // EXPERT_ROWS
    index = jnp.arange(n_blocks, dtype=jnp.int32)
    first = jnp.logical_and(index < n_used[0],
                            jnp.concatenate([jnp.ones((1,), bool), block_e[1:] != block_e[:-1]]))
    slot = (jnp.cumsum(first) - 1) % 2
    starts = jnp.where(first, index, n_blocks)
    next_start = jnp.concatenate([lax.cummin(starts, axis=0, reverse=True)[1:],
                                  jnp.full((1,), n_blocks, jnp.int32)])
    next_e = jnp.where(next_start < n_blocks, block_e[jnp.minimum(next_start, n_blocks - 1)], -1)
    plan = jnp.stack([block_e, first.astype(jnp.int32), slot.astype(jnp.int32),
                      next_e.astype(jnp.int32)])
    used_rows = pl.BlockSpec((EXPERT_ROWS, packed),
                             lambda i, plan, nu: (jnp.maximum(jnp.minimum(i, nu[0] - 1), 0), 0))
    rows = pl.BlockSpec((EXPERT_ROWS, packed), lambda i, plan, nu: (i, 0))
    cast_specs = [_cast_specs(stack, layer, n_blocks - N_EXPERTS, lambda i, plan, nu: i)
                  for stack, layer in casts]
    in_hbm = pl.BlockSpec(memory_space=pl.ANY)
    grid_spec = pltpu.PrefetchScalarGridSpec(
        num_scalar_prefetch=2,
        grid=(n_blocks,),
        in_specs=[used_rows, in_hbm, in_hbm, in_hbm] + [s[0] for s in cast_specs],
        out_specs=[rows] + [s[1] for s in cast_specs],
        scratch_shapes=[pltpu.VMEM((2, D, D_EXPERT), BF16), pltpu.VMEM((2, D, D_EXPERT), BF16),
                        pltpu.VMEM((2, D_EXPERT, D), BF16), pltpu.SemaphoreType.DMA((2, 3))],
    )
    out = pl.pallas_call(
        _expert_kernel,
        grid_spec=grid_spec,
        out_shape=[jax.ShapeDtypeStruct((P, packed), jnp.uint32)] + [s[2] for s in cast_specs],
        compiler_params=_params(("arbitrary",), 56),
        name="experts",
    )(plan, n_used, xs, w1, w3, w2, *[stack for stack, _ in casts])
    return out[0], out[1:]


def _combine_kernel(alpha, project, dest_ref, dest_next_ref, x_ref, gate_ref, g_ref, b_ref, y_hbm,
                    *rest):
    if project:
        w_ref, of_ref, ob_ref, oxr_ref, ogate_ref, buf_a, buf_b, sem = rest
    else:
        of_ref, ob_ref, buf_a, buf_b, sem = rest
    rows = COMBINE_ROWS
    half = rows // 2
    i = pl.program_id(0)

    def copy(table_ref, h, t, k, buf, s):
        return pltpu.make_async_copy(y_hbm.at[pl.ds(table_ref[0, k * rows + h * half + t], 1)],
                                     buf.at[k, pl.ds(t, 1)], sem.at[s])

    def gather(table_ref, h, buf, s):
        for t in range(half):
            for k in range(TOP_K):
                copy(table_ref, h, t, k, buf, s).start(priority=k % N_DMA_PRIORITIES)

    def wait(buf, s):
        for k in range(TOP_K):
            pltpu.make_async_copy(y_hbm.at[pl.ds(0, half)], buf.at[k], sem.at[s]).wait()

    def reduce(buf, h):
        r = slice(h * half, (h + 1) * half)
        lo0, hi0 = _unpack_halves(buf[0])
        lo1, hi1 = _unpack_halves(buf[1])
        g0 = gate_ref[r, 0:1]
        g1 = gate_ref[r, 1:2]
        f = jnp.concatenate([g0 * lo0 + g1 * lo1, g0 * hi0 + g1 * hi1], axis=1)
        y = _layer_norm(alpha * x_ref[r, :] + f, g_ref[...], b_ref[...])
        of_ref[r, :] = y
        yb = y.astype(BF16)
        ob_ref[r, :] = yb
        if project:
            _project_branch_a(yb, w_ref, oxr_ref, ogate_ref, rows=r)

    @pl.when(i == 0)
    def _():
        def issue(t, carry):
            for k in range(TOP_K):
                copy(dest_ref, 0, t, k, buf_a, 0).start()
            return carry

        lax.fori_loop(0, half, issue, 0, unroll=8)

    wait(buf_a, 0)
    gather(dest_ref, 1, buf_b, 1)
    reduce(buf_a, 0)
    wait(buf_b, 1)
    gather(dest_next_ref, 0, buf_a, 0)
    reduce(buf_b, 1)

    @pl.when(i == pl.num_programs(0) - 1)
    def _():
        wait(buf_a, 0)


def _combine(alpha, dest_tiles, x, gate_t, ln_g, ln_b, yb, next_w_in=None):
    T, D = x.shape
    n_tiles = T // COMBINE_ROWS
    row = lambda w: pl.BlockSpec((COMBINE_ROWS, w), lambda i: (i, 0))
    table = lambda index: pl.BlockSpec((None, 1, TOP_K * COMBINE_ROWS), index, memory_space=pltpu.SMEM)
    half_buf = pltpu.VMEM((TOP_K, COMBINE_ROWS // 2, yb.shape[1]), yb.dtype)
    project = next_w_in is not None
    in_specs = [table(lambda i: (i, 0, 0)),
                table(lambda i: (jnp.minimum(i + 1, n_tiles - 1), 0, 0)),
                row(D), pl.BlockSpec((COMBINE_ROWS, TOP_K), lambda i: (i, 0)),
                _const_spec((1, D)), _const_spec((1, D)),
                pl.BlockSpec(memory_space=pl.ANY)]
    out_specs = [row(D), row(D)]
    out_shape = [jax.ShapeDtypeStruct((T, D), F32), jax.ShapeDtypeStruct((T, D), BF16)]
    operands = [dest_tiles, dest_tiles, x, gate_t, ln_g.reshape(1, D), ln_b.reshape(1, D), yb]
    if project:
        in_specs.append(_const_spec((D, 2 * D_RNN)))
        out_specs += [row(D_RNN), row(D_RNN)]
        out_shape += [jax.ShapeDtypeStruct((T, D_RNN), BF16)] * 2
        operands.append(next_w_in)
    return pl.pallas_call(
        functools.partial(_combine_kernel, alpha, project),
        grid=(n_tiles,),
        in_specs=in_specs,
        out_specs=out_specs,
        out_shape=out_shape,
        scratch_shapes=[half_buf, half_buf, pltpu.SemaphoreType.DMA((2,))],
        compiler_params=_params(("arbitrary",), 52),
        name="combine",
    )(*operands)


def _tile_pairs(dest, rows):
    T = dest.shape[1]
    return dest.reshape(TOP_K, T // rows, rows).transpose(1, 0, 2).reshape(T // rows, 1, TOP_K * rows)


def _moe(alpha, x, x_packed, router_w, router_b, w1, w3, w2, ln_g, ln_b, casts=()):
    T, D = x.shape
    n_blocks = (T * TOP_K) // EXPERT_ROWS + N_EXPERTS
    idx, gate, rank, cnt = _router(x, router_w, router_b)

    counts = cnt[:, 0]
    padded = (counts + EXPERT_ROWS - 1) // EXPERT_ROWS * EXPERT_ROWS
    pad_end = jnp.cumsum(padded)
    pad_start = pad_end - padded
    n_used = (pad_end[-1] // EXPERT_ROWS).astype(jnp.int32)
    block_start = jnp.minimum(jnp.arange(n_blocks, dtype=jnp.int32), n_used - 1) * EXPERT_ROWS
    block_e = jnp.minimum(jnp.sum(block_start[:, None] >= pad_end[None, :], axis=1),
                          N_EXPERTS - 1).astype(jnp.int32)
    expert_ids = jnp.arange(N_EXPERTS, dtype=jnp.int32)[:, None, None]
    dest = jnp.sum(jnp.where(idx[None] == expert_ids, pad_start[:, None, None], 0), axis=0) + rank

    n_used = n_used.reshape(1)
    xs = _dispatch(x_packed, _tile_pairs(dest, DISPATCH_ROWS), (pad_start + counts).astype(jnp.int32),
                   (padded - counts).astype(jnp.int32), n_used, n_blocks * EXPERT_ROWS)
    yb, casted = _experts(xs, block_e, n_used, w1, w3, w2, casts)
    new = _combine(alpha, _tile_pairs(dest, COMBINE_ROWS), x, gate.T, ln_g, ln_b, yb,
                   next_w_in=casted[0] if casted else None)
    return new, casted


def kernel(x, emb_ln_g, emb_ln_b, w_in, conv_w, conv_b, lru_w_a, lru_b_a, lru_w_x, lru_b_x, lru_lambda, w_o_rnn, sgu_ln_g, sgu_ln_b, sgu_w_s, sgu_b_s, w_o_sgu, w_out, ln1_g, ln1_b, router_w, router_b, expert_w1, expert_w3, expert_w2, ln2_g, ln2_b):
    batch, seq, D = x.shape
    depth = w_in.shape[0]
    alpha = float((2 * depth) ** 0.25)
    xf = x.reshape(batch * seq, D)
    mixer_f32 = (w_in, w_o_rnn, w_o_sgu, w_out)
    w_in, w_o_rnn, w_o_sgu, w_out = (w[0:1].astype(BF16) for w in mixer_f32)
    xb, x_rnn, gate = _emb_ln(xf, emb_ln_g, emb_ln_b, w_in)
    rows_of = lambda w: w.reshape(depth, N_EXPERTS * w.shape[2], w.shape[3])
    w1_rows, w3_rows, w2_rows = rows_of(expert_w1), rows_of(expert_w3), rows_of(expert_w2)
    uv_from = 2 * D_RNN
    sigmoid_from = 2 * D_RNN + 2 * D_SGU
    for l in range(depth):
        uv_cols, w3 = _proj(xb, w_in, 0, uv_from, sigmoid_from - uv_from, _gelu, "proj_gelu",
                            cast=(w3_rows, l))
        sigmoid_cols, w1, a = _proj_lru(
            xb, w_in, 0, sigmoid_from, D_IN - sigmoid_from, _sigmoid, "proj_sigmoid_lru",
            (w1_rows, l), x_rnn, gate, seq, conv_w[l], conv_b[l], lru_w_a[l], lru_b_a[l],
            lru_w_x[l], lru_b_x[l], lru_lambda[l])
        x1, x1_packed, w2 = _mix_out(alpha, 0, a, uv_cols, sigmoid_cols, xf, l == 0, emb_ln_g,
                                     emb_ln_b, sgu_ln_g[l], sgu_ln_b[l], sgu_w_s[l], sgu_b_s[l],
                                     w_o_rnn, w_o_sgu, w_out, ln1_g[l], ln1_b[l], cast=(w2_rows, l))
        next_mixer = [(w, l + 1) for w in mixer_f32] if l + 1 < depth else []
        (xf, xb, *branch_a), casted = _moe(
            alpha, x1, x1_packed, router_w, router_b, w1.reshape(expert_w1.shape[1:]),
            w3.reshape(expert_w3.shape[1:]), w2.reshape(expert_w2.shape[1:]), ln2_g[l], ln2_b[l],
            casts=next_mixer)
        if casted:
            x_rnn, gate = branch_a
            w_in, w_o_rnn, w_o_sgu, w_out = (w[None] for w in casted)
    return xf.reshape(batch, seq, D)
```

```python
import functools

import jax
import jax.numpy as jnp
import numpy as np
from jax import lax
from jax.experimental import pallas as pl
from jax.experimental.pallas import tpu as pltpu

F32 = jnp.float32
BF16 = jnp.bfloat16

D_MODEL = 2048
D_RNN = 1024
RNN_HEADS = 8
HEAD_DIM = D_RNN // RNN_HEADS
CONV_WIDTH = 4
LRU_C = 8.0
D_SGU = 1024
SGU_GROUPS = 8
SGU_GROUP_DIM = D_SGU // SGU_GROUPS
CHUNK = 128
N_EXPERTS = 16
N_GROUPS = 4
EXPERTS_PER_GROUP = N_EXPERTS // N_GROUPS
TOP_K = 2
D_EXPERT = 1408
D_IN = 2 * D_RNN + 2 * D_SGU + 2 * D_MODEL
LN_EPS = 1e-5
SQRT_2_OVER_PI = float(np.sqrt(2.0 / np.pi))

LN_ROWS = 512
PROJ_ROWS = 1024
PROJ_COLS = 1024
MIX_ROWS = 256
ROUTER_ROWS = 1024
DISPATCH_ROWS = 512
EXPERT_ROWS = 256
COMBINE_ROWS = 512
CONV_PAD = 8
N_DMA_PRIORITIES = 2
ROW_TILE = 8

MIB = 1024 * 1024


def _params(semantics, vmem_mib):
    return pltpu.CompilerParams(dimension_semantics=semantics,
                                vmem_limit_bytes=vmem_mib * MIB,
                                disable_bounds_checks=True)


def _const_spec(shape):
    zeros = (0,) * len(shape)
    return pl.BlockSpec(shape, lambda *_: zeros, pipeline_mode=pl.Buffered(1))


def _layer_spec(layer, shape):
    zeros = (0,) * len(shape)
    return pl.BlockSpec((None,) + tuple(shape), lambda *_: (layer,) + zeros,
                        pipeline_mode=pl.Buffered(1))


def _layer_norm(v, g, b):
    mu = jnp.mean(v, axis=-1, keepdims=True)
    d = v - mu
    var = jnp.mean(d * d, axis=-1, keepdims=True)
    return d * lax.rsqrt(var + LN_EPS) * g + b


def _gelu(x):
    inner = x * (SQRT_2_OVER_PI + (SQRT_2_OVER_PI * 0.044715) * (x * x))
    return x * (0.5 * jnp.tanh(inner) + 0.5)


def _pack_halves(y):
    n = y.shape[1] // 2
    bits = lambda v: lax.bitcast_convert_type(v.astype(BF16).astype(F32), jnp.uint32)
    return (bits(y[:, :n]) >> 16) | bits(y[:, n:])


def _unpack_halves(p):
    lo = lax.bitcast_convert_type(p << 16, F32)
    hi = lax.bitcast_convert_type(p & jnp.uint32(0xFFFF0000), F32)
    return lo, hi


def _sigmoid(x):
    return 0.5 * jnp.tanh(0.5 * x) + 0.5


def _project_branch_a(xb, w_ref, oxr_ref, ogate_ref, rows=slice(None)):
    acc = jnp.dot(xb, w_ref[...], preferred_element_type=F32)
    oxr_ref[rows, :] = acc[:, :D_RNN].astype(BF16)
    ogate_ref[rows, :] = _gelu(acc[:, D_RNN:]).astype(BF16)


def _emb_ln_kernel(x_ref, g_ref, b_ref, w_ref, ob_ref, oxr_ref, ogate_ref):
    xb = _layer_norm(x_ref[...], g_ref[...], b_ref[...]).astype(BF16)
    ob_ref[...] = xb
    _project_branch_a(xb, w_ref, oxr_ref, ogate_ref)


def _emb_ln(x, g, b, w_in):
    T, D = x.shape
    row = lambda w: pl.BlockSpec((LN_ROWS, w), lambda i: (i, 0))
    half = jax.ShapeDtypeStruct((T, D_RNN), BF16)
    return pl.pallas_call(
        _emb_ln_kernel,
        grid=(T // LN_ROWS,),
        in_specs=[row(D), _const_spec((1, D)), _const_spec((1, D)), _layer_spec(0, (D, 2 * D_RNN))],
        out_specs=[row(D), row(D_RNN), row(D_RNN)],
        out_shape=[jax.ShapeDtypeStruct((T, D), BF16), half, half],
        compiler_params=_params(("parallel",), 40),
        name="emb_ln",
    )(x, g.reshape(1, D), b.reshape(1, D), w_in)


def _cast_specs(stack, layer, n_steps, step_of):
    _, rows, cols = stack.shape
    slab = -(-rows // (n_steps * 16)) * 16
    last = -(-rows // slab) - 1
    src = pl.BlockSpec((None, slab, cols), lambda *g: (layer, jnp.minimum(step_of(*g), last), 0))
    dst = pl.BlockSpec((slab, cols), lambda *g: (jnp.minimum(step_of(*g), last), 0))
    return src, dst, jax.ShapeDtypeStruct((rows, cols), BF16)


def _row_tile(x_hbm, x_buf, sem):
    i = pl.program_id(0)
    j = pl.program_id(1)

    def copy(tile, slot):
        rows = pl.ds(pl.multiple_of(tile * PROJ_ROWS, PROJ_ROWS), PROJ_ROWS)
        return pltpu.make_async_copy(x_hbm.at[rows], x_buf.at[slot], sem.at[slot])

    @pl.when(jnp.logical_and(i == 0, j == 0))
    def _():
        copy(0, 0).start()

    @pl.when(j == 0)
    def _():
        copy(i, i % 2).wait()

        @pl.when(i + 1 < pl.num_programs(0))
        def _():
            copy(i + 1, (i + 1) % 2).start()

    return x_buf[i % 2]


def _row_tile_scratch(D):
    return [pltpu.VMEM((2, PROJ_ROWS, D), BF16), pltpu.SemaphoreType.DMA((2,))]


def _proj_kernel(act, x_hbm, w_ref, cast_src_ref, o_ref, cast_dst_ref, x_buf, sem):
    acc = jnp.dot(_row_tile(x_hbm, x_buf, sem), w_ref[...], preferred_element_type=F32)
    o_ref[...] = act(acc).astype(BF16)
    cast_dst_ref[...] = cast_src_ref[...].astype(BF16)


def _proj(xb, w_in, layer, col_from, n_cols, act, name, cast):
    T, D = xb.shape
    first = col_from // PROJ_COLS
    grid = (T // PROJ_ROWS, n_cols // PROJ_COLS)
    src, dst, cast_shape = _cast_specs(*cast, grid[0] * grid[1], lambda i, j: i * grid[1] + j)
    return pl.pallas_call(
        functools.partial(_proj_kernel, act),
        grid=grid,
        in_specs=[pl.BlockSpec(memory_space=pl.ANY),
                  pl.BlockSpec((None, D, PROJ_COLS), lambda i, j: (layer, 0, first + j)), src],
        out_specs=[pl.BlockSpec((PROJ_ROWS, PROJ_COLS), lambda i, j: (i, j)), dst],
        out_shape=[jax.ShapeDtypeStruct((T, n_cols), BF16), cast_shape],
        scratch_shapes=_row_tile_scratch(D),
        compiler_params=_params(("arbitrary", "arbitrary"), 48),
        name=name,
    )(xb, w_in, cast[0])


def _lru_gates(first, xr_ref, cw_ref, cb_ref, wa_ref, ba_ref, wx_ref, bx_ref, lam_ref,
               xpad_ref, hc_ref, a_ref, u_ref):
    rows = xr_ref.shape[0]

    @pl.when(first)
    def _():
        xpad_ref[0:CONV_PAD, :] = jnp.zeros((CONV_PAD, D_RNN), F32)
        hc_ref[...] = jnp.zeros_like(hc_ref)

    x = xr_ref[...].astype(F32)
    xpad_ref[CONV_PAD:CONV_PAD + rows, :] = x
    xc = cb_ref[...] + cw_ref[CONV_WIDTH - 1:CONV_WIDTH, :] * x
    for k in range(CONV_WIDTH - 1):
        shift = CONV_WIDTH - 1 - k
        xc = xc + cw_ref[k:k + 1, :] * xpad_ref[CONV_PAD - shift:CONV_PAD - shift + rows, :]
    xpad_ref[0:CONV_PAD, :] = x[rows - CONV_PAD:rows, :]

    lam = lam_ref[...]
    sp = jnp.maximum(-lam, 0.0) + jnp.log1p(jnp.exp(-jnp.abs(lam)))
    decay = (-0.5 * LRU_C) * sp
    for h in range(RNN_HEADS):
        cols = slice(h * HEAD_DIM, (h + 1) * HEAD_DIM)
        xh = xc[:, cols]
        xhb = xh.astype(BF16)
        tanh_r = jnp.tanh(jnp.dot(xhb, wa_ref[h], preferred_element_type=F32) + ba_ref[:, cols])
        tanh_i = jnp.tanh(jnp.dot(xhb, wx_ref[h], preferred_element_type=F32) + bx_ref[:, cols])
        gi = 0.5 * tanh_i + 0.5
        log_a = decay[:, cols] * tanh_r + decay[:, cols]
        a_ref[:, cols] = jnp.exp(log_a)
        th = jnp.tanh(log_a)
        u_ref[:, cols] = jnp.sqrt(-2.0 * th / (1.0 - th)) * (gi * xh)


def _lru_scan(gate_ref, o_ref, hc_ref, a_ref, u_ref, h_ref):
    def step(t, h):
        h = a_ref[pl.ds(t, 1), :] * h + u_ref[pl.ds(t, 1), :]
        h_ref[pl.ds(t, 1), :] = h
        return h

    hc_ref[0:1, :] = lax.fori_loop(0, a_ref.shape[0], step, hc_ref[0:1, :], unroll=8)
    o_ref[...] = (h_ref[...] * gate_ref[...].astype(F32)).astype(BF16)


def _proj_lru_kernel(act, tiles_per_seq, x_ref, w_ref, cast_src_ref, xr_ref, gate_ref, cw_ref, cb_ref,
                     wa_ref, ba_ref, wx_ref, bx_ref, lam_ref, o_ref, cast_dst_ref, a_out_ref,
                     xpad_ref, hc_ref, a_ref, u_ref, h_ref):
    step = pl.program_id(0) * pl.num_programs(1) + pl.program_id(1)
    _lru_gates(step % tiles_per_seq == 0, xr_ref, cw_ref, cb_ref, wa_ref, ba_ref, wx_ref, bx_ref,
               lam_ref, xpad_ref, hc_ref, a_ref, u_ref)
    acc = jnp.dot(x_ref[...], w_ref[...], preferred_element_type=F32)
    o_ref[...] = act(acc).astype(BF16)
    cast_dst_ref[...] = cast_src_ref[...].astype(BF16)
    _lru_scan(gate_ref, a_out_ref, hc_ref, a_ref, u_ref, h_ref)


def _proj_lru(xb, w_in, layer, col_from, n_cols, act, name, cast, x_rnn, gate, seq,
              conv_w, conv_b, w_a, b_a, w_x, b_x, lam):
    T, D = xb.shape
    first = col_from // PROJ_COLS
    grid = (T // PROJ_ROWS, n_cols // PROJ_COLS)
    n_steps = grid[0] * grid[1]
    step_of = lambda i, j: i * grid[1] + j
    lru_rows = T // n_steps
    src, dst, cast_shape = _cast_specs(*cast, n_steps, step_of)
    tile = pl.BlockSpec((lru_rows, D_RNN), lambda i, j: (step_of(i, j), 0))
    vec = _const_spec((1, D_RNN))
    gate_w = _const_spec((RNN_HEADS, HEAD_DIM, HEAD_DIM))
    tile_f32 = pltpu.VMEM((lru_rows, D_RNN), F32)
    return pl.pallas_call(
        functools.partial(_proj_lru_kernel, act, seq // lru_rows),
        grid=grid,
        in_specs=[pl.BlockSpec((PROJ_ROWS, D), lambda i, j: (i, 0)),
                  pl.BlockSpec((None, D, PROJ_COLS), lambda i, j: (layer, 0, first + j)),
                  src, tile, tile, _const_spec((CONV_WIDTH, D_RNN)), vec, gate_w, vec, gate_w, vec, vec],
        out_specs=[pl.BlockSpec((PROJ_ROWS, PROJ_COLS), lambda i, j: (i, j)), dst, tile],
        out_shape=[jax.ShapeDtypeStruct((T, n_cols), BF16), cast_shape,
                   jax.ShapeDtypeStruct((T, D_RNN), BF16)],
        scratch_shapes=[pltpu.VMEM((CONV_PAD + lru_rows, D_RNN), F32), pltpu.VMEM((8, D_RNN), F32),
                        tile_f32, tile_f32, tile_f32],
        compiler_params=_params(("arbitrary", "arbitrary"), 52),
        name=name,
    )(xb, w_in, cast[0], x_rnn, gate, conv_w, conv_b.reshape(1, D_RNN), (0.5 * w_a).astype(BF16),
      0.5 * b_a.reshape(1, D_RNN), (0.5 * w_x).astype(BF16), 0.5 * b_x.reshape(1, D_RNN),
      lam.reshape(1, D_RNN))


def _spatial_gating(u_ref, v_ref, g_ref, b_ref, ws_ref, bs_ref, o_ref):
    rows = u_ref.shape[0]
    v = _layer_norm(v_ref[...].astype(F32), g_ref[...], b_ref[...]).astype(BF16)
    t_out = lax.broadcasted_iota(jnp.int32, (CHUNK, CHUNK), 0)
    t_in = lax.broadcasted_iota(jnp.int32, (CHUNK, CHUNK), 1)
    causal = t_in <= t_out
    for g in range(SGU_GROUPS):
        cols = slice(g * SGU_GROUP_DIM, (g + 1) * SGU_GROUP_DIM)
        ws = jnp.where(causal, ws_ref[g], 0.0).astype(BF16)
        bias = bs_ref[:, g:g + 1]
        for c in range(rows // CHUNK):
            rws = slice(c * CHUNK, (c + 1) * CHUNK)
            mixed = jnp.dot(ws, v[rws, cols], preferred_element_type=F32) + bias
            o_ref[rws, cols] = (u_ref[rws, cols].astype(F32) * mixed).astype(BF16)


def _mix_out_kernel(alpha, x_is_raw, a_ref, u_ref, v_ref, sga_ref, sgb_ref, x_ref, emb_g_ref,
                    emb_b_ref, sgu_g_ref, sgu_b_ref, ws_ref, bs_ref, woa_ref, wob_ref, wout_ref,
                    g_ref, beta_ref, cast_src_ref, o_ref, packed_ref, cast_dst_ref, b_ref):
    _spatial_gating(u_ref, v_ref, sgu_g_ref, sgu_b_ref, ws_ref, bs_ref, b_ref)
    ya = jnp.dot(a_ref[...], woa_ref[...], preferred_element_type=F32)
    yb = jnp.dot(b_ref[...], wob_ref[...], preferred_element_type=F32)
    merged = sga_ref[...].astype(F32) * ya + sgb_ref[...].astype(F32) * yb
    m = jnp.dot(merged.astype(BF16), wout_ref[...], preferred_element_type=F32)
    x = x_ref[...]
    if x_is_raw:
        x = _layer_norm(x, emb_g_ref[...], emb_b_ref[...])
    y = _layer_norm(alpha * x + m, g_ref[...], beta_ref[...])
    o_ref[...] = y
    packed_ref[...] = _pack_halves(y)
    cast_dst_ref[...] = cast_src_ref[...].astype(BF16)


def _mix_out(alpha, layer, a, uv_cols, sigmoid_cols, x, x_is_raw, emb_ln_g, emb_ln_b,
             sgu_ln_g, sgu_ln_b, w_s, b_s, w_o_rnn, w_o_sgu, w_out, ln_g, ln_b, cast):
    T, D = x.shape
    n_steps = T // MIX_ROWS
    row = lambda w, c: pl.BlockSpec((MIX_ROWS, w), lambda i: (i, c))
    cast_src, cast_dst, cast_shape = _cast_specs(*cast, n_steps, lambda i: i)
    return pl.pallas_call(
        functools.partial(_mix_out_kernel, alpha, x_is_raw),
        grid=(n_steps,),
        in_specs=[row(D_RNN, 0), row(D_SGU, 0), row(D_SGU, 1), row(D, 0), row(D, 1), row(D, 0),
                  _const_spec((1, D)), _const_spec((1, D)),
                  _const_spec((1, D_SGU)), _const_spec((1, D_SGU)),
                  _const_spec((SGU_GROUPS, CHUNK, CHUNK)), _const_spec((CHUNK, SGU_GROUPS)),
                  _layer_spec(layer, (D_RNN, D)), _layer_spec(layer, (D_SGU, D)),
                  _layer_spec(layer, (D, D)), _const_spec((1, D)), _const_spec((1, D)), cast_src],
        out_specs=[row(D, 0), row(D // 2, 0), cast_dst],
        out_shape=[jax.ShapeDtypeStruct((T, D), F32), jax.ShapeDtypeStruct((T, D // 2), jnp.uint32),
                   cast_shape],
        scratch_shapes=[pltpu.VMEM((MIX_ROWS, D_SGU), BF16)],
        compiler_params=_params(("parallel",), 56),
        name="mix_out",
    )(a, uv_cols, uv_cols, sigmoid_cols, sigmoid_cols, x,
      emb_ln_g.reshape(1, D), emb_ln_b.reshape(1, D),
      sgu_ln_g.reshape(1, D_SGU), sgu_ln_b.reshape(1, D_SGU), w_s, b_s.T,
      w_o_rnn, w_o_sgu, w_out, ln_g.reshape(1, D), ln_b.reshape(1, D), cast[0])


def _first_max(v, row, n):
    m = jnp.max(v, axis=0, keepdims=True)
    idx = jnp.min(jnp.where(v == m, row, float(n)), axis=0, keepdims=True)
    return m, idx


def _router_kernel(x_ref, wt_ref, b_ref, idx_ref, gate_ref, rank_ref, cnt_ref, run_ref):
    rows = x_ref.shape[0]

    @pl.when(pl.program_id(0) == 0)
    def _():
        run_ref[...] = jnp.zeros_like(run_ref)

    def nt_dot(w, x):
        return lax.dot_general(w, x, (((1,), (1,)), ((), ())), preferred_element_type=F32)

    x = x_ref[...]
    xh = x.astype(BF16)
    xl = (x - xh.astype(F32)).astype(BF16)
    w = wt_ref[...]
    wh = w.astype(BF16)
    wl = (w - wh.astype(F32)).astype(BF16)
    logits = nt_dot(wh, xh) + (nt_dot(wh, xl) + nt_dot(wl, xh))
    e = jnp.exp(logits - jnp.max(logits, axis=0, keepdims=True))
    scores = e / jnp.sum(e, axis=0, keepdims=True)
    sel = scores + b_ref[...]

    row = lax.broadcasted_iota(jnp.int32, (N_EXPERTS, rows), 0).astype(F32)
    grow = lax.broadcasted_iota(jnp.int32, (EXPERTS_PER_GROUP, rows), 0).astype(F32)
    neg_inf = float("-inf")
    best_score = None
    best_group = None
    for g in range(N_GROUPS):
        v = sel[g * EXPERTS_PER_GROUP:(g + 1) * EXPERTS_PER_GROUP, :]
        m1, i1 = _first_max(v, grow, EXPERTS_PER_GROUP)
        m2 = jnp.max(jnp.where(grow == i1, neg_inf, v), axis=0, keepdims=True)
        s = m1 + m2
        if g == 0:
            best_score, best_group = s, jnp.zeros_like(s)
        else:
            better = s > best_score
            best_group = jnp.where(better, float(g), best_group)
            best_score = jnp.where(better, s, best_score)

    lo = best_group * float(EXPERTS_PER_GROUP)
    in_group = jnp.logical_and(row >= lo, row < lo + float(EXPERTS_PER_GROUP))
    masked = jnp.where(in_group, sel, neg_inf)
    _, i1 = _first_max(masked, row, N_EXPERTS)
    pick1 = row == i1
    _, i2 = _first_max(jnp.where(pick1, neg_inf, masked), row, N_EXPERTS)
    pick2 = row == i2
    s1 = jnp.sum(jnp.where(pick1, scores, 0.0), axis=0, keepdims=True)
    s2 = jnp.sum(jnp.where(pick2, scores, 0.0), axis=0, keepdims=True)
    den = s1 + s2

    onehot = jnp.where(jnp.logical_or(pick1, pick2), 1.0, 0.0)
    before = (lax.broadcasted_iota(jnp.int32, (rows, rows), 0)
              < lax.broadcasted_iota(jnp.int32, (rows, rows), 1))
    prefix = jnp.dot(onehot.astype(BF16), jnp.where(before, 1.0, 0.0).astype(BF16),
                     preferred_element_type=F32)
    pos = prefix + run_ref[:, 0:1]
    r1 = jnp.sum(jnp.where(pick1, pos, 0.0), axis=0, keepdims=True)
    r2 = jnp.sum(jnp.where(pick2, pos, 0.0), axis=0, keepdims=True)
    run_ref[...] = run_ref[...] + jnp.sum(onehot, axis=1, keepdims=True)

    idx_ref[0:1, :] = i1.astype(jnp.int32)
    idx_ref[1:2, :] = i2.astype(jnp.int32)
    gate_ref[0:1, :] = s1 / den
    gate_ref[1:2, :] = s2 / den
    rank_ref[0:1, :] = r1.astype(jnp.int32)
    rank_ref[1:2, :] = r2.astype(jnp.int32)
    cnt_ref[...] = run_ref[...].astype(jnp.int32)


def _router(x, router_w, router_b):
    T, D = x.shape
    pair = pl.BlockSpec((TOP_K, ROUTER_ROWS), lambda i: (0, i))
    return pl.pallas_call(
        _router_kernel,
        grid=(T // ROUTER_ROWS,),
        in_specs=[pl.BlockSpec((ROUTER_ROWS, D), lambda i: (i, 0)),
                  _const_spec((N_EXPERTS, D)), _const_spec((N_EXPERTS, 1))],
        out_specs=[pair, pair, pair, pl.BlockSpec((N_EXPERTS, 128), lambda i: (0, 0))],
        out_shape=[jax.ShapeDtypeStruct((TOP_K, T), jnp.int32),
                   jax.ShapeDtypeStruct((TOP_K, T), F32),
                   jax.ShapeDtypeStruct((TOP_K, T), jnp.int32),
                   jax.ShapeDtypeStruct((N_EXPERTS, 128), jnp.int32)],
        scratch_shapes=[pltpu.VMEM((N_EXPERTS, 128), F32)],
        compiler_params=_params(("arbitrary",), 40),
        name="router",
    )(x, router_w.T, router_b.reshape(N_EXPERTS, 1))


def _dispatch_kernel(pad_from_ref, pad_n_ref, n_used_ref, dest_ref, x_ref, xs_hbm, zero_ref, sem, pad_sem):
    i = pl.program_id(0)
    rows = DISPATCH_ROWS
    n_blocks = xs_hbm.shape[0] // EXPERT_ROWS
    zero_rows = zero_ref.shape[0]

    def tail_copies(j):
        block = n_used_ref[0] + j
        off = pl.multiple_of(block * EXPERT_ROWS, EXPERT_ROWS)
        return block < n_blocks, [
            pltpu.make_async_copy(zero_ref, xs_hbm.at[pl.ds(off + part * zero_rows, zero_rows)], pad_sem)
            for part in range(EXPERT_ROWS // zero_rows)]

    def pad_copy(e, bit):
        n = pad_n_ref[e]
        if bit < ROW_TILE:
            off = pad_from_ref[e] + bit - 1
            return pltpu.make_async_copy(zero_ref.at[pl.ds(0, 1)], xs_hbm.at[pl.ds(off, 1)], pad_sem)
        done = (n & (ROW_TILE - 1)) + (n & ~(2 * bit - 1))
        off = pl.multiple_of(pad_from_ref[e] + done, ROW_TILE)
        return pltpu.make_async_copy(zero_ref.at[pl.ds(0, bit)], xs_hbm.at[pl.ds(off, bit)], pad_sem)

    def pad_needed(e, bit):
        n = pad_n_ref[e]
        if bit < ROW_TILE:
            return bit <= (n & (ROW_TILE - 1))
        return (n & bit) != 0

    bits = list(range(1, ROW_TILE)) + [
        1 << k for k in range(ROW_TILE.bit_length() - 1, EXPERT_ROWS.bit_length() - 1)]

    @pl.when(i == 0)
    def _():
        zero_ref[...] = jnp.zeros_like(zero_ref)
        for e in range(N_EXPERTS):
            for bit in bits:
                @pl.when(pad_needed(e, bit))
                def _():
                    pad_copy(e, bit).start()
        for j in range(N_EXPERTS):
            needed, copies = tail_copies(j)

            @pl.when(needed)
            def _():
                for c in copies:
                    c.start()

    for t in range(rows):
        src = x_ref.at[pl.ds(t, 1)]
        for k in range(TOP_K):
            pltpu.make_async_copy(src, xs_hbm.at[pl.ds(dest_ref[0, k * rows + t], 1)],
                                  sem).start(priority=k % N_DMA_PRIORITIES)
    for _ in range(TOP_K):
        pltpu.make_async_copy(x_ref, xs_hbm.at[pl.ds(0, rows)], sem).wait()

    @pl.when(i == 0)
    def _():
        for e in range(N_EXPERTS):
            for bit in bits:
                @pl.when(pad_needed(e, bit))
                def _():
                    pad_copy(e, bit).wait()
        for j in range(N_EXPERTS):
            needed, copies = tail_copies(j)

            @pl.when(needed)
            def _():
                for c in copies:
                    c.wait()


def _dispatch(x, dest_tiles, pad_from, pad_n, n_used, n_slots):
    T, D = x.shape
    grid_spec = pltpu.PrefetchScalarGridSpec(
        num_scalar_prefetch=3,
        grid=(T // DISPATCH_ROWS,),
        in_specs=[pl.BlockSpec((None, 1, TOP_K * DISPATCH_ROWS), lambda i, *_: (i, 0, 0),
                               memory_space=pltpu.SMEM),
                  pl.BlockSpec((DISPATCH_ROWS, D), lambda i, *_: (i, 0))],
        out_specs=pl.BlockSpec(memory_space=pl.ANY),
        scratch_shapes=[pltpu.VMEM((EXPERT_ROWS // 2, D), x.dtype),
                        pltpu.SemaphoreType.DMA(()), pltpu.SemaphoreType.DMA(())],
    )
    return pl.pallas_call(
        _dispatch_kernel,
        grid_spec=grid_spec,
        out_shape=jax.ShapeDtypeStruct((n_slots, D), x.dtype),
        compiler_params=_params(("arbitrary",), 16),
        name="dispatch",
    )(pad_from, pad_n, n_used, dest_tiles, x)


PLAN_EXPERT, PLAN_FIRST, PLAN_SLOT, PLAN_NEXT = range(4)


def _expert_kernel(plan_ref, n_used_ref, xs_ref, w1_hbm, w3_hbm, w2_hbm, *rest):
    *rest, w1_buf, w3_buf, w2_buf, sem = rest
    n_casts = len(rest) // 2
    o_ref = rest[n_casts]
    i = pl.program_id(0)
    used = i < n_used_ref[0]
    slot = plan_ref[PLAN_SLOT, i]

    def weight_copies(expert, into):
        return [pltpu.make_async_copy(hbm.at[expert], buf.at[into], sem.at[into, k])
                for k, (hbm, buf) in enumerate(((w1_hbm, w1_buf), (w3_hbm, w3_buf), (w2_hbm, w2_buf)))]

    @pl.when(i == 0)
    def _():
        for c in weight_copies(plan_ref[PLAN_EXPERT, 0], 0):
            c.start()

    @pl.when(plan_ref[PLAN_FIRST, i] == 1)
    def _():
        for c in weight_copies(plan_ref[PLAN_EXPERT, i], slot):
            c.wait()

        @pl.when(plan_ref[PLAN_NEXT, i] >= 0)
        def _():
            for c in weight_copies(plan_ref[PLAN_NEXT, i], 1 - slot):
                c.start()

    @pl.when(jnp.logical_not(used))
    def _():
        o_ref[...] = jnp.zeros_like(o_ref)

    @pl.when(used)
    def _():
        lo, hi = _unpack_halves(xs_ref[...])
        x = jnp.concatenate([lo.astype(BF16), hi.astype(BF16)], axis=1)
        h1 = jnp.dot(x, w1_buf[slot], preferred_element_type=F32)
        h3 = jnp.dot(x, w3_buf[slot], preferred_element_type=F32)
        hidden = (h1 * _sigmoid(h1)) * h3
        out = jnp.dot(hidden.astype(BF16), w2_buf[slot], preferred_element_type=F32)
        o_ref[...] = _pack_halves(out)
        for src_ref, dst_ref in zip(rest[:n_casts], rest[n_casts + 1:]):
            dst_ref[...] = src_ref[...].astype(BF16)


def _experts(xs, block_e, n_used, w1, w3, w2, casts=()):
    P, packed = xs.shape
    D = 2 * packed
    n_blocks = P // EXPERT_ROWS
    index = jnp.arange(n_blocks, dtype=jnp.int32)
    first = jnp.logical_and(index < n_used[0],
                            jnp.concatenate([jnp.ones((1,), bool), block_e[1:] != block_e[:-1]]))
    slot = (jnp.cumsum(first) - 1) % 2
    starts = jnp.where(first, index, n_blocks)
    next_start = jnp.concatenate([lax.cummin(starts, axis=0, reverse=True)[1:],
                                  jnp.full((1,), n_blocks, jnp.int32)])
    next_e = jnp.where(next_start < n_blocks, block_e[jnp.minimum(next_start, n_blocks - 1)], -1)
    plan = jnp.stack([block_e, first.astype(jnp.int32), slot.astype(jnp.int32),
                      next_e.astype(jnp.int32)])
    used_rows = pl.BlockSpec((EXPERT_ROWS, packed),
                             lambda i, plan, nu: (jnp.maximum(jnp.minimum(i, nu[0] - 1), 0), 0))
    rows = pl.BlockSpec((EXPERT_ROWS, packed), lambda i, plan, nu: (i, 0))
    cast_specs = [_cast_specs(stack, layer, n_blocks - N_EXPERTS, lambda i, plan, nu: i)
                  for stack, layer in casts]
    in_hbm = pl.BlockSpec(memory_space=pl.ANY)
    grid_spec = pltpu.PrefetchScalarGridSpec(
        num_scalar_prefetch=2,
        grid=(n_blocks,),
        in_specs=[used_rows, in_hbm, in_hbm, in_hbm] + [s[0] for s in cast_specs],
        out_specs=[rows] + [s[1] for s in cast_specs],
        scratch_shapes=[pltpu.VMEM((2, D, D_EXPERT), BF16), pltpu.VMEM((2, D, D_EXPERT), BF16),
                        pltpu.VMEM((2, D_EXPERT, D), BF16), pltpu.SemaphoreType.DMA((2, 3))],
    )
    out = pl.pallas_call(
        _expert_kernel,
        grid_spec=grid_spec,
        out_shape=[jax.ShapeDtypeStruct((P, packed), jnp.uint32)] + [s[2] for s in cast_specs],
        compiler_params=_params(("arbitrary",), 56),
        name="experts",
    )(plan, n_used, xs, w1, w3, w2, *[stack for stack, _ in casts])
    return out[0], out[1:]


def _combine_kernel(alpha, project, dest_ref, dest_next_ref, x_ref, gate_ref, g_ref, b_ref, y_hbm,
                    *rest):
    if project:
        w_ref, of_ref, ob_ref, oxr_ref, ogate_ref, buf_a, buf_b, sem = rest
    else:
        of_ref, ob_ref, buf_a, buf_b, sem = rest
    rows = COMBINE_ROWS
    half = rows // 2
    i = pl.program_id(0)

    def copy(table_ref, h, t, k, buf, s):
        return pltpu.make_async_copy(y_hbm.at[pl.ds(table_ref[0, k * rows + h * half + t], 1)],
                                     buf.at[k, pl.ds(t, 1)], sem.at[s])

    def gather(table_ref, h, buf, s):
        for t in range(half):
            for k in range(TOP_K):
                copy(table_ref, h, t, k, buf, s).start(priority=k % N_DMA_PRIORITIES)

    def wait(buf, s):
        for k in range(TOP_K):
            pltpu.make_async_copy(y_hbm.at[pl.ds(0, half)], buf.at[k], sem.at[s]).wait()

    def reduce(buf, h):
        r = slice(h * half, (h + 1) * half)
        lo0, hi0 = _unpack_halves(buf[0])
        lo1, hi1 = _unpack_halves(buf[1])
        g0 = gate_ref[r, 0:1]
        g1 = gate_ref[r, 1:2]
        f = jnp.concatenate([g0 * lo0 + g1 * lo1, g0 * hi0 + g1 * hi1], axis=1)
        y = _layer_norm(alpha * x_ref[r, :] + f, g_ref[...], b_ref[...])
        of_ref[r, :] = y
        yb = y.astype(BF16)
        ob_ref[r, :] = yb
        if project:
            _project_branch_a(yb, w_ref, oxr_ref, ogate_ref, rows=r)

    @pl.when(i == 0)
    def _():
        def issue(t, carry):
            for k in range(TOP_K):
                copy(dest_ref, 0, t, k, buf_a, 0).start()
            return carry

        lax.fori_loop(0, half, issue, 0, unroll=8)

    wait(buf_a, 0)
    gather(dest_ref, 1, buf_b, 1)
    reduce(buf_a, 0)
    wait(buf_b, 1)
    gather(dest_next_ref, 0, buf_a, 0)
    reduce(buf_b, 1)

    @pl.when(i == pl.num_programs(0) - 1)
    def _():
        wait(buf_a, 0)


def _combine(alpha, dest_tiles, x, gate_t, ln_g, ln_b, yb, next_w_in=None):
    T, D = x.shape
    n_tiles = T // COMBINE_ROWS
    row = lambda w: pl.BlockSpec((COMBINE_ROWS, w), lambda i: (i, 0))
    table = lambda index: pl.BlockSpec((None, 1, TOP_K * COMBINE_ROWS), index, memory_space=pltpu.SMEM)
    half_buf = pltpu.VMEM((TOP_K, COMBINE_ROWS // 2, yb.shape[1]), yb.dtype)
    project = next_w_in is not None
    in_specs = [table(lambda i: (i, 0, 0)),
                table(lambda i: (jnp.minimum(i + 1, n_tiles - 1), 0, 0)),
                row(D), pl.BlockSpec((COMBINE_ROWS, TOP_K), lambda i: (i, 0)),
                _const_spec((1, D)), _const_spec((1, D)),
                pl.BlockSpec(memory_space=pl.ANY)]
    out_specs = [row(D), row(D)]
    out_shape = [jax.ShapeDtypeStruct((T, D), F32), jax.ShapeDtypeStruct((T, D), BF16)]
    operands = [dest_tiles, dest_tiles, x, gate_t, ln_g.reshape(1, D), ln_b.reshape(1, D), yb]
    if project:
        in_specs.append(_const_spec((D, 2 * D_RNN)))
        out_specs += [row(D_RNN), row(D_RNN)]
        out_shape += [jax.ShapeDtypeStruct((T, D_RNN), BF16)] * 2
        operands.append(next_w_in)
    return pl.pallas_call(
        functools.partial(_combine_kernel, alpha, project),
        grid=(n_tiles,),
        in_specs=in_specs,
        out_specs=out_specs,
        out_shape=out_shape,
        scratch_shapes=[half_buf, half_buf, pltpu.SemaphoreType.DMA((2,))],
        compiler_params=_params(("arbitrary",), 52),
        name="combine",
    )(*operands)


def _tile_pairs(dest, rows):
    T = dest.shape[1]
    return dest.reshape(TOP_K, T // rows, rows).transpose(1, 0, 2).reshape(T // rows, 1, TOP_K * rows)


def _moe(alpha, x, x_packed, router_w, router_b, w1, w3, w2, ln_g, ln_b, casts=()):
    T, D = x.shape
    n_blocks = (T * TOP_K) // EXPERT_ROWS + N_EXPERTS
    idx, gate, rank, cnt = _router(x, router_w, router_b)

    counts = cnt[:, 0]
    padded = (counts + EXPERT_ROWS - 1) // EXPERT_ROWS * EXPERT_ROWS
    pad_end = jnp.cumsum(padded)
    pad_start = pad_end - padded
    n_used = (pad_end[-1] // EXPERT_ROWS).astype(jnp.int32)
    block_start = jnp.minimum(jnp.arange(n_blocks, dtype=jnp.int32), n_used - 1) * EXPERT_ROWS
    block_e = jnp.minimum(jnp.sum(block_start[:, None] >= pad_end[None, :], axis=1),
                          N_EXPERTS - 1).astype(jnp.int32)
    expert_ids = jnp.arange(N_EXPERTS, dtype=jnp.int32)[:, None, None]
    dest = jnp.sum(jnp.where(idx[None] == expert_ids, pad_start[:, None, None], 0), axis=0) + rank

    n_used = n_used.reshape(1)
    xs = _dispatch(x_packed, _tile_pairs(dest, DISPATCH_ROWS), (pad_start + counts).astype(jnp.int32),
                   (padded - counts).astype(jnp.int32), n_used, n_blocks * EXPERT_ROWS)
    yb, casted = _experts(xs, block_e, n_used, w1, w3, w2, casts)
    new = _combine(alpha, _tile_pairs(dest, COMBINE_ROWS), x, gate.T, ln_g, ln_b, yb,
                   next_w_in=casted[0] if casted else None)
    return new, casted


def kernel(x, emb_ln_g, emb_ln_b, w_in, conv_w, conv_b, lru_w_a, lru_b_a, lru_w_x, lru_b_x, lru_lambda, w_o_rnn, sgu_ln_g, sgu_ln_b, sgu_w_s, sgu_b_s, w_o_sgu, w_out, ln1_g, ln1_b, router_w, router_b, expert_w1, expert_w3, expert_w2, ln2_g, ln2_b):
    batch, seq, D = x.shape
    depth = w_in.shape[0]
    alpha = float((2 * depth) ** 0.25)
    xf = x.reshape(batch * seq, D)
    mixer_f32 = (w_in, w_o_rnn, w_o_sgu, w_out)
    w_in, w_o_rnn, w_o_sgu, w_out = (w[0:1].astype(BF16) for w in mixer_f32)
    xb, x_rnn, gate = _emb_ln(xf, emb_ln_g, emb_ln_b, w_in)
    rows_of = lambda w: w.reshape(depth, N_EXPERTS * w.shape[2], w.shape[3])
    w1_rows, w3_rows, w2_rows = rows_of(expert_w1), rows_of(expert_w3), rows_of(expert_w2)
    uv_from = 2 * D_RNN
    sigmoid_from = 2 * D_RNN + 2 * D_SGU
    for l in range(depth):
        uv_cols, w3 = _proj(xb, w_in, 0, uv_from, sigmoid_from - uv_from, _gelu, "proj_gelu",
                            cast=(w3_rows, l))
        sigmoid_cols, w1, a = _proj_lru(
            xb, w_in, 0, sigmoid_from, D_IN - sigmoid_from, _sigmoid, "proj_sigmoid_lru",
            (w1_rows, l), x_rnn, gate, seq, conv_w[l], conv_b[l], lru_w_a[l], lru_b_a[l],
            lru_w_x[l], lru_b_x[l], lru_lambda[l])
        x1, x1_packed, w2 = _mix_out(alpha, 0, a, uv_cols, sigmoid_cols, xf, l == 0, emb_ln_g,
                                     emb_ln_b, sgu_ln_g[l], sgu_ln_b[l], sgu_w_s[l], sgu_b_s[l],
                                     w_o_rnn, w_o_sgu, w_out, ln1_g[l], ln1_b[l], cast=(w2_rows, l))
        next_mixer = [(w, l + 1) for w in mixer_f32] if l + 1 < depth else []
        (xf, xb, *branch_a), casted = _moe(
            alpha, x1, x1_packed, router_w, router_b, w1.reshape(expert_w1.shape[1:]),
            w3.reshape(expert_w3.shape[1:]), w2.reshape(expert_w2.shape[1:]), ln2_g[l], ln2_b[l],
            casts=next_mixer)
        if casted:
            x_rnn, gate = branch_a
            w_in, w_o_rnn, w_o_sgu, w_out = (w[None] for w in casted)
    return xf.reshape(batch, seq, D)
```
